```python
import math
import jax, jax.numpy as jnp
from jax import lax
import numpy as np

D_MODEL = 1024
BATCH = 8
SEQ = 8192
DEPTH = 1

EPS = 1e-6
GDN_HEADS = 8
GDN_DK = 128
GDN_DV = 128
GDN_CHUNK = 64
CONV_WIDTH = 5
DIFF_HEADS = 8
DIFF_DH = 64
DIFF_DV = 2 * DIFF_DH
Q_BLOCK = 128
ROPE_THETA = 10000.0
N_EXPERTS = 32
TOP_K = 4
D_FF_EXPERT = 1024
SWIGLU_ALPHA = 1.702
SWIGLU_LIMIT = 7.0
MOE_BLOCK = 512

GDN_QK = GDN_HEADS * GDN_DK
GDN_V = GDN_HEADS * GDN_DV
DIFF_QK = DIFF_HEADS * 2 * DIFF_DH
DIFF_V = DIFF_HEADS * DIFF_DV
IN_SIZES = (GDN_QK, GDN_QK, GDN_V, GDN_V, 2 * GDN_HEADS, 2 * GDN_HEADS, DIFF_QK, DIFF_QK, DIFF_V, 2 * D_MODEL)
IN_TOTAL = sum(IN_SIZES)
IN_SPLITS = tuple(int(s) for s in np.cumsum(IN_SIZES)[:-1])

kernel_name = 'hybrid_gdn_diffattn_moe_adaln_encoder'


def rmsnorm(x, g):
    xf = x.astype(jnp.float32)
    y = xf * lax.rsqrt(jnp.mean(xf * xf, axis=-1, keepdims=True) + EPS)
    return (y * g.astype(jnp.float32)).astype(x.dtype)


def l2norm(x):
    xf = x.astype(jnp.float32)
    return xf * lax.rsqrt(jnp.sum(xf * xf, axis=-1, keepdims=True) + EPS)


def modulate(h, shift, scale):
    return h * (1.0 + scale[:, None, :]) + shift[:, None, :]


def diff_lambda_init(layer_idx):
    return 0.8 - 0.6 * math.exp(-0.3 * layer_idx)


def centred_depthwise_conv(u, w):
    pad = (CONV_WIDTH - 1) // 2
    return lax.conv_general_dilated(u, w[:, None, :].astype(u.dtype), window_strides=(1,),
                                    padding=[(pad, pad)], dimension_numbers=('NWC', 'WIO', 'NWC'),
                                    feature_group_count=u.shape[-1])


def rope(t, pos):
    half = t.shape[-1] // 2
    inv_freq = ROPE_THETA ** (-jnp.arange(half, dtype=jnp.float32) / half)
    ang = pos.astype(jnp.float32)[:, None] * inv_freq[None, :]
    cos = jnp.cos(ang)[None, :, None, None, :]
    sin = jnp.sin(ang)[None, :, None, None, :]
    t1, t2 = t[..., :half], t[..., half:]
    return jnp.concatenate([t1 * cos - t2 * sin, t2 * cos + t1 * sin], axis=-1)


def gated_delta_chunked(q, k, v, beta, g):
    Bb, H, S, dk = q.shape
    dv = v.shape[-1]
    C = GDN_CHUNK
    n = S // C
    f32 = jnp.float32
    q = q.astype(f32).reshape(Bb, H, n, C, dk)
    k = k.astype(f32).reshape(Bb, H, n, C, dk)
    v = v.astype(f32).reshape(Bb, H, n, C, dv)
    beta = beta.astype(f32).reshape(Bb, H, n, C)
    G = jnp.cumsum(g.astype(f32).reshape(Bb, H, n, C), axis=-1)
    idx = jnp.arange(C)
    lower_incl = idx[:, None] >= idx[None, :]
    strict = idx[:, None] > idx[None, :]
    decay = jnp.exp(jnp.where(lower_incl, G[..., :, None] - G[..., None, :], -jnp.inf))
    kbeta = k * beta[..., None]
    L = jnp.where(strict, jnp.einsum('bhnid,bhnjd->bhnij', kbeta, k) * decay, 0.0)
    A = L + jnp.eye(C, dtype=f32)
    rhs = jnp.concatenate([v * beta[..., None], kbeta * jnp.exp(G)[..., None]], axis=-1)
    sol = lax.linalg.triangular_solve(A, rhs, left_side=True, lower=True, unit_diagonal=True)
    u, w = sol[..., :dv], sol[..., dv:]
    attn_qk = jnp.einsum('bhnid,bhnjd->bhnij', q, k) * decay
    q_dec = q * jnp.exp(G)[..., None]
    G_last = G[..., -1:]
    k_dec = k * jnp.exp(G_last - G)[..., None]
    chunk_decay = jnp.exp(G_last[..., 0])
    xs = tuple(jnp.moveaxis(t, 2, 0) for t in (u, w, attn_qk, q_dec, k_dec, chunk_decay))

    def step(state, xc):
        u_c, w_c, a_c, qd_c, kd_c, cd_c = xc
        v_new = u_c - jnp.einsum('bhcd,bhde->bhce', w_c, state)
        o_c = jnp.einsum('bhcd,bhde->bhce', qd_c, state) + jnp.einsum('bhcj,bhje->bhce', a_c, v_new)
        state = state * cd_c[..., None, None] + jnp.einsum('bhcd,bhce->bhde', kd_c, v_new)
        return state, o_c

    state0 = jnp.zeros((Bb, H, dk, dv), f32)
    _, o = lax.scan(step, state0, xs)
    return jnp.moveaxis(o, 0, 2).reshape(Bb, H, S, dv)


def differential_attention(q, k, v, lam):
    Bb, S, H, _, d = q.shape
    dv = v.shape[-1]
    scale = d ** -0.5
    qh = jnp.transpose(q, (0, 2, 3, 1, 4))
    kh = jnp.transpose(k, (0, 2, 3, 1, 4))
    vh = jnp.transpose(v, (0, 2, 1, 3))
    nb = S // Q_BLOCK
    qb = jnp.moveaxis(qh.reshape(Bb, H, 2, nb, Q_BLOCK, d), 3, 0)

    def block(q_blk):
        s = jnp.einsum('bhcqd,bhckd->bhcqk', q_blk, kh).astype(jnp.float32) * scale
        p = jax.nn.softmax(s, axis=-1)
        p_diff = p[:, :, 0] - lam * p[:, :, 1]
        return jnp.einsum('bhqk,bhkv->bhqv', p_diff.astype(vh.dtype), vh)

    o = lax.map(block, qb)
    return jnp.moveaxis(o, 0, 2).reshape(Bb, H, S, dv)


def hybrid_mixer(h, pos, w_in, b_gate, conv_w, a_log, dt_bias, gdn_norm_g, w_branch_a,
                 diff_lambda, diff_norm_g, w_branch_b, w_out, lam_init):
    Bb, S, _ = h.shape
    f32 = jnp.float32
    proj = h @ w_in
    qa, ka, va, za, ba, aa, qb, kb, vb, gl = jnp.split(proj, IN_SPLITS, axis=-1)

    qkv = jax.nn.silu(centred_depthwise_conv(jnp.concatenate([qa, ka, va], axis=-1), conv_w))
    qa, ka, va = jnp.split(qkv, (GDN_QK, 2 * GDN_QK), axis=-1)
    qa = l2norm(qa.reshape(Bb, S, GDN_HEADS, GDN_DK)).transpose(0, 2, 1, 3) * (GDN_DK ** -0.5)
    ka = l2norm(ka.reshape(Bb, S, GDN_HEADS, GDN_DK)).transpose(0, 2, 1, 3)
    va = va.reshape(Bb, S, GDN_HEADS, GDN_DV).transpose(0, 2, 1, 3).astype(f32)
    beta = jax.nn.sigmoid(ba.astype(f32)).reshape(Bb, S, 2, GDN_HEADS).transpose(2, 0, 3, 1)
    a_in = aa.astype(f32).reshape(Bb, S, 2, GDN_HEADS).transpose(2, 0, 3, 1)
    log_decay = -jnp.exp(a_log.astype(f32))[:, None, :, None] * jax.nn.softplus(a_in + dt_bias.astype(f32)[:, None, :, None])
    o_fwd = gated_delta_chunked(qa, ka, va, beta[0], log_decay[0])
    flip = lambda t: jnp.flip(t, axis=2)
    o_bwd = flip(gated_delta_chunked(flip(qa), flip(ka), flip(va), flip(beta[1]), flip(log_decay[1])))
    o_a = (o_fwd + o_bwd).transpose(0, 2, 1, 3)
    o_a = rmsnorm(o_a, gdn_norm_g) * jax.nn.silu(za.reshape(Bb, S, GDN_HEADS, GDN_DV).astype(f32))
    y_a = o_a.reshape(Bb, S, GDN_V).astype(h.dtype) @ w_branch_a

    qb = rope(qb.reshape(Bb, S, DIFF_HEADS, 2, DIFF_DH), pos)
    kb = rope(kb.reshape(Bb, S, DIFF_HEADS, 2, DIFF_DH), pos)
    vb = vb.reshape(Bb, S, DIFF_HEADS, DIFF_DV)
    lamf = diff_lambda.astype(f32)
    lam = jnp.exp(jnp.sum(lamf[0] * lamf[1])) - jnp.exp(jnp.sum(lamf[2] * lamf[3])) + lam_init
    o_b = differential_attention(qb, kb, vb, lam)
    o_b = rmsnorm(o_b, diff_norm_g) * (1.0 - lam_init)
    y_b = o_b.transpose(0, 2, 1, 3).reshape(Bb, S, DIFF_V).astype(h.dtype) @ w_branch_b

    gate_a, gate_b = jnp.split(jax.nn.sigmoid(gl + b_gate), 2, axis=-1)
    return (gate_a * y_a + gate_b * y_b) @ w_out


def clamped_swiglu(glu, lin):
    glu = jnp.minimum(glu, SWIGLU_LIMIT)
    lin = jnp.clip(lin, -SWIGLU_LIMIT, SWIGLU_LIMIT)
    return glu * jax.nn.sigmoid(SWIGLU_ALPHA * glu) * (lin + 1.0)


def moe_ffn(h, router_w, router_b, w_glu, b_glu, w_lin, b_lin, w_down, b_down):
    Bb, S, D = h.shape
    n_tok = Bb * S
    n_assign = n_tok * TOP_K
    xt = h.reshape(n_tok, D)
    logits = (xt @ router_w + router_b).astype(jnp.float32)
    top_vals, top_idx = lax.top_k(logits, TOP_K)
    gates = jax.nn.softmax(top_vals, axis=-1)
    flat_e = top_idx.reshape(-1).astype(jnp.int32)
    flat_g = gates.reshape(-1)
    flat_tok = jnp.arange(n_assign, dtype=jnp.int32) // TOP_K
    order = jnp.argsort(flat_e)
    sorted_e = flat_e[order]
    counts = jnp.bincount(flat_e, length=N_EXPERTS).astype(jnp.int32)
    padded = (counts + MOE_BLOCK - 1) // MOE_BLOCK * MOE_BLOCK
    starts = jnp.cumsum(counts) - counts
    pad_ends = jnp.cumsum(padded)
    pad_starts = pad_ends - padded
    dest = pad_starts[sorted_e] + jnp.arange(n_assign, dtype=jnp.int32) - starts[sorted_e]
    n_blocks = -(-n_assign // MOE_BLOCK) + N_EXPERTS
    n_rows = n_blocks * MOE_BLOCK
    row_tok = jnp.full((n_rows,), n_tok, jnp.int32).at[dest].set(flat_tok[order])
    row_gate = jnp.zeros((n_rows,), jnp.float32).at[dest].set(flat_g[order])
    block_start = jnp.arange(n_blocks, dtype=jnp.int32) * MOE_BLOCK
    block_e = jnp.minimum(jnp.searchsorted(pad_ends, block_start, side='right'), N_EXPERTS - 1)
    x_pad = jnp.concatenate([xt, jnp.zeros((1, D), xt.dtype)], axis=0)

    def expert_block(args):
        rows, e, g = args
        xb = x_pad[rows]
        act = clamped_swiglu(xb @ w_glu[e] + b_glu[e], xb @ w_lin[e] + b_lin[e])
        yb = act @ w_down[e] + b_down[e]
        return yb * g[:, None].astype(yb.dtype)

    y = lax.map(expert_block, (row_tok.reshape(n_blocks, MOE_BLOCK), block_e,
                               row_gate.reshape(n_blocks, MOE_BLOCK)))
    out = jax.ops.segment_sum(y.reshape(n_rows, D), row_tok, num_segments=n_tok + 1)[:n_tok]
    return out.reshape(Bb, S, D)


def setup_inputs(seed: int = 0) -> dict:
    key = jax.random.key(seed)
    ks = jax.random.split(key, 32)
    f32 = jnp.float32
    D, E, F = D_MODEL, N_EXPERTS, D_FF_EXPERT

    def nrm(k, shape, fan_in):
        return jax.random.normal(k, shape, f32) * (fan_in ** -0.5)

    def small(k, shape, s):
        return jax.random.normal(k, shape, f32) * s

    dt = jnp.exp(jax.random.uniform(ks[10], (DEPTH, 2, GDN_HEADS), f32, math.log(1e-3), math.log(1e-1)))
    return {
        'x': jax.random.normal(ks[0], (BATCH, SEQ, D), f32),
        'c': jax.random.normal(ks[1], (BATCH, D), f32),
        'ada_w': nrm(ks[2], (DEPTH, D, 6 * D), D),
        'ada_b': small(ks[3], (DEPTH, 6 * D), 0.02),
        'norm1_g': 1.0 + small(ks[4], (DEPTH, D), 0.05),
        'norm2_g': 1.0 + small(ks[5], (DEPTH, D), 0.05),
        'w_in': nrm(ks[6], (DEPTH, D, IN_TOTAL), D),
        'b_gate': small(ks[7], (DEPTH, 2 * D), 0.02),
        'conv_w': nrm(ks[8], (DEPTH, CONV_WIDTH, 2 * GDN_QK + GDN_V), CONV_WIDTH),
        'a_log': jnp.log(jax.random.uniform(ks[9], (DEPTH, 2, GDN_HEADS), f32, 1.0, 16.0)),
        'dt_bias': dt + jnp.log(-jnp.expm1(-dt)),
        'gdn_norm_g': 1.0 + small(ks[11], (DEPTH, GDN_DV), 0.05),
        'w_branch_a': nrm(ks[12], (DEPTH, GDN_V, D), GDN_V),
        'diff_lambda': small(ks[13], (DEPTH, 4, DIFF_DH), 0.1),
        'diff_norm_g': 1.0 + small(ks[14], (DEPTH, DIFF_DV), 0.05),
        'w_branch_b': nrm(ks[15], (DEPTH, DIFF_V, D), DIFF_V),
        'w_out': nrm(ks[16], (DEPTH, D, D), D),
        'router_w': nrm(ks[17], (DEPTH, D, E), D),
        'router_b': small(ks[18], (DEPTH, E), 0.01),
        'w_glu': nrm(ks[19], (DEPTH, E, D, F), D),
        'b_glu': small(ks[20], (DEPTH, E, F), 0.02),
        'w_lin': nrm(ks[21], (DEPTH, E, D, F), D),
        'b_lin': small(ks[22], (DEPTH, E, F), 0.02),
        'w_down': nrm(ks[23], (DEPTH, E, F, D), F),
        'b_down': small(ks[24], (DEPTH, E, D), 0.02),
        'final_g': 1.0 + small(ks[25], (D,), 0.05),
    }


def reference(x, c, ada_w, ada_b, norm1_g, norm2_g, w_in, b_gate, conv_w, a_log, dt_bias,
              gdn_norm_g, w_branch_a, diff_lambda, diff_norm_g, w_branch_b, w_out,
              router_w, router_b, w_glu, b_glu, w_lin, b_lin, w_down, b_down, final_g):
    S = x.shape[1]
    pos = jnp.arange(S, dtype=jnp.int32)
    cond = jax.nn.silu(c)
    for l in range(DEPTH):
        mod = cond @ ada_w[l] + ada_b[l]
        shift1, scale1, gate1, shift2, scale2, gate2 = jnp.split(mod, 6, axis=-1)
        h = modulate(rmsnorm(x, norm1_g[l]), shift1, scale1)
        mix = hybrid_mixer(h, pos, w_in[l], b_gate[l], conv_w[l], a_log[l], dt_bias[l], gdn_norm_g[l],
                           w_branch_a[l], diff_lambda[l], diff_norm_g[l], w_branch_b[l], w_out[l],
                           diff_lambda_init(l))
        x = x + gate1[:, None, :] * mix
        h = modulate(rmsnorm(x, norm2_g[l]), shift2, scale2)
        x = x + gate2[:, None, :] * moe_ffn(h, router_w[l], router_b[l], w_glu[l], b_glu[l],
                                            w_lin[l], b_lin[l], w_down[l], b_down[l])
    return rmsnorm(x, final_g)
```

```python
import functools
import math

import jax
import jax.numpy as jnp
from jax import lax
from jax.experimental import pallas as pl
from jax.experimental.pallas import tpu as pltpu

F32 = jnp.float32
BF16 = jnp.bfloat16

D_MODEL = 1024
EPS = 1e-6
N_HEADS = 8
HEAD_W = 128
GDN_CHUNK = 64
CONV_WIDTH = 5
DIFF_DH = 64
ROPE_THETA = 10000.0
LAM_INIT = 0.8 - 0.6 * math.exp(-0.3 * 0)
N_EXPERTS = 32
TOP_K = 4
SWIGLU_ALPHA = 1.702
SWIGLU_LIMIT = 7.0

LANES = 128
GROUP = 256
CHUNKS_PER_GROUP = GROUP // GDN_CHUNK
CHUNK_SHIFT = GDN_CHUNK.bit_length() - 1
MOE_TILE = 512
ROW_MOVE_TILE = 256
NEG_INF = float("-inf")

COL_QA, COL_KA, COL_VA, COL_ZA, COL_QB, COL_KB, COL_VB, COL_GA, COL_GB = range(9)
N_COL_BLOCKS = 9


def _params(sem, vmem_mb=48):
    return pltpu.CompilerParams(dimension_semantics=sem, vmem_limit_bytes=vmem_mb * 1024 * 1024)


def _dot(a, b):
    return jnp.dot(a, b, preferred_element_type=F32)


def _dot_nt(a, b):
    return lax.dot_general(a, b, (((1,), (1,)), ((), ())), preferred_element_type=F32)


def _sigmoid(x):
    return 1.0 / (1.0 + jnp.exp(-x))


def _split3(x):
    a = x.astype(BF16)
    r = x - a.astype(F32)
    b = r.astype(BF16)
    c = (r - b.astype(F32)).astype(BF16)
    return a, b, c


def _adaln_kernel(c_ref, w_ref, b_ref, o_ref):
    c = c_ref[...]
    cond = c * _sigmoid(c)
    c0, c1, c2 = _split3(cond)
    w0, w1, w2 = _split3(w_ref[...])
    acc = _dot(c0, w0) + (_dot(c0, w1) + _dot(c1, w0)) + (_dot(c0, w2) + _dot(c1, w1) + _dot(c2, w0))
    o_ref[...] = acc + b_ref[...]


def _adaln(c, ada_w, ada_b):
    B = c.shape[0]
    n = ada_w.shape[1] // D_MODEL
    return pl.pallas_call(
        _adaln_kernel,
        grid=(n,),
        in_specs=[pl.BlockSpec((B, D_MODEL), lambda j: (0, 0)),
                  pl.BlockSpec((D_MODEL, D_MODEL), lambda j: (0, j)),
                  pl.BlockSpec((1, D_MODEL), lambda j: (0, j))],
        out_specs=pl.BlockSpec((B, D_MODEL), lambda j: (0, j)),
        out_shape=jax.ShapeDtypeStruct((B, n * D_MODEL), F32),
        compiler_params=_params(("parallel",)),
        name="adaln",
    )(c, ada_w, ada_b.reshape(1, -1))


def _inproj_kernel(x_ref, sh_ref, sc_ref, g_ref, w_ref, ws_ref, o_ref, os_ref, h_ref):
    @pl.when(pl.program_id(1) == 0)
    def _():
        x = x_ref[...]
        ms = jnp.mean(x * x, axis=-1, keepdims=True)
        y = x * lax.rsqrt(ms + EPS) * g_ref[...]
        h = (y * (1.0 + sc_ref[0]) + sh_ref[0]).astype(BF16)
        h_ref[...] = h
        os_ref[...] = _dot(h, ws_ref[...])

    o_ref[...] = _dot(h_ref[...], w_ref[...]).astype(o_ref.dtype)


def _inproj(x2, shift, scale, g, w_big, w_small, S):
    N = x2.shape[0]
    TM = min(1024, S)
    tpb = S // TM
    return pl.pallas_call(
        _inproj_kernel,
        grid=(N // TM, N_COL_BLOCKS),
        in_specs=[pl.BlockSpec((TM, D_MODEL), lambda i, j: (i, 0)),
                  pl.BlockSpec((1, 1, D_MODEL), lambda i, j: (i // tpb, 0, 0)),
                  pl.BlockSpec((1, 1, D_MODEL), lambda i, j: (i // tpb, 0, 0)),
                  pl.BlockSpec((1, D_MODEL), lambda i, j: (0, 0)),
                  pl.BlockSpec((D_MODEL, D_MODEL), lambda i, j: (0, j)),
                  pl.BlockSpec((D_MODEL, LANES), lambda i, j: (0, 0))],
        out_specs=[pl.BlockSpec((TM, D_MODEL), lambda i, j: (i, j)),
                   pl.BlockSpec((TM, LANES), lambda i, j: (i, 0))],
        out_shape=[jax.ShapeDtypeStruct((N, N_COL_BLOCKS * D_MODEL), BF16),
                   jax.ShapeDtypeStruct((N, LANES), F32)],
        scratch_shapes=[pltpu.VMEM((TM, D_MODEL), BF16)],
        compiler_params=_params(("parallel", "arbitrary")),
        name="inproj",
    )(x2, shift, scale, g, w_big, w_small)


HALO = 16


def _conv_kernel(cur_ref, prev_ref, next_ref, w_ref, o_ref, ext_ref, *, TR):
    i = pl.program_id(1)
    g = pl.program_id(2)
    last = pl.num_programs(1) - 1
    ext_ref[8:8 + TR, :] = cur_ref[0].astype(F32)
    pv = prev_ref[0].astype(F32)[HALO - 8:HALO]
    nx = next_ref[0].astype(F32)[0:8]
    ext_ref[0:8, :] = jnp.where(i > 0, pv, 0.0)
    ext_ref[TR + 8:TR + 16, :] = jnp.where(i < last, nx, 0.0)
    pad = (CONV_WIDTH - 1) // 2
    acc = ext_ref[8 - pad:8 - pad + TR, :] * w_ref[0:1, :]
    for j in range(1, CONV_WIDTH):
        acc = acc + ext_ref[8 - pad + j:8 - pad + j + TR, :] * w_ref[j:j + 1, :]
    y = acc * _sigmoid(acc)
    ones = jnp.ones((HEAD_W, HEAD_W), BF16)
    qscale = jnp.where(g == 0, HEAD_W ** -0.5, 1.0)
    for h in range(N_HEADS):
        yh = y[:, h * HEAD_W:(h + 1) * HEAD_W]
        ss = _dot((yh * yh).astype(BF16), ones)
        normed = yh * (lax.rsqrt(ss + EPS) * qscale)
        o_ref[0, :, h * HEAD_W:(h + 1) * HEAD_W] = jnp.where(g < 2, normed, yh).astype(o_ref.dtype)


def _gdn_conv(proj3, conv_w):
    B, S, _ = proj3.shape
    TR = min(512, S)
    nT = S // TR
    rb = TR // HALO
    nH = S // HALO
    return pl.pallas_call(
        functools.partial(_conv_kernel, TR=TR),
        grid=(B, nT, 3),
        in_specs=[pl.BlockSpec((1, TR, D_MODEL), lambda b, i, g: (b, i, g)),
                  pl.BlockSpec((1, HALO, D_MODEL), lambda b, i, g: (b, jnp.maximum(i * rb - 1, 0), g)),
                  pl.BlockSpec((1, HALO, D_MODEL), lambda b, i, g: (b, jnp.minimum((i + 1) * rb, nH - 1), g)),
                  pl.BlockSpec((CONV_WIDTH, D_MODEL), lambda b, i, g: (0, g))],
        out_specs=pl.BlockSpec((1, TR, D_MODEL), lambda b, i, g: (b, i, g)),
        out_shape=jax.ShapeDtypeStruct((B, S, 3 * D_MODEL), BF16),
        scratch_shapes=[pltpu.VMEM((TR + 16, D_MODEL), F32)],
        compiler_params=_params(("parallel", "parallel", "parallel")),
        name="gdn_conv",
    )(proj3, proj3, proj3, conv_w)


GATE_LANE0 = 16


def _gates_kernel(x_ref, alog_ref, dt_ref, beta_ref, g_ref, eg_ref, egl_ref, cd_ref):
    x = x_ref[...]
    R = x.shape[0]
    lane = lax.broadcasted_iota(jnp.int32, x.shape, 1)
    beta_ref[...] = _sigmoid(x)
    z = x + dt_ref[...]
    softplus = jnp.maximum(z, 0.0) + jnp.log(1.0 + jnp.exp(-jnp.abs(z)))
    gd = -jnp.exp(alog_ref[...]) * softplus
    gd = jnp.where((lane >= GATE_LANE0) & (lane < GATE_LANE0 + 2 * N_HEADS), gd, 0.0)
    r = lax.broadcasted_iota(jnp.int32, (R, R), 0)
    c = lax.broadcasted_iota(jnp.int32, (R, R), 1)
    same = (r >> CHUNK_SHIFT) == (c >> CHUNK_SHIFT)
    lower = jnp.where(same & (c <= r), 1.0, 0.0).astype(BF16)
    upper = jnp.where(same & (c >= r), 1.0, 0.0).astype(BF16)
    block = jnp.where(same, 1.0, 0.0).astype(BF16)
    p0, p1, p2 = _split3(gd)
    g_fwd = _dot(lower, p0) + _dot(lower, p1) + _dot(lower, p2)
    g_bwd = _dot(upper, p0) + _dot(upper, p1) + _dot(upper, p2)
    tot = _dot(block, p0) + _dot(block, p1) + _dot(block, p2)
    G = jnp.where(lane < GATE_LANE0 + N_HEADS, g_fwd, g_bwd)
    g_ref[...] = G
    eg_ref[...] = jnp.exp(G)
    egl_ref[...] = jnp.exp(tot - G)
    cd_ref[...] = jnp.exp(tot)


def _gdn_gates(small, alog_row, dt_row):
    N = small.shape[0]
    spec = pl.BlockSpec((GROUP, LANES), lambda i: (i, 0))
    row = pl.BlockSpec((1, LANES), lambda i: (0, 0))
    return pl.pallas_call(
        _gates_kernel,
        grid=(N // GROUP,),
        in_specs=[spec, row, row],
        out_specs=[spec] * 5,
        out_shape=[jax.ShapeDtypeStruct((N, LANES), F32)] * 5,
        compiler_params=_params(("parallel",)),
        name="gdn_gates",
    )(small, alog_row, dt_row)


def _col(x, l, lane):
    return jnp.sum(jnp.where(lane == l, x, 0.0), axis=1, keepdims=True)


def _prep_kernel(q_ref, k_ref, v_ref, beta_ref, g_ref, eg_ref, egl_ref, gtf_ref, gtb_ref,
                 u_ref, w_ref, qd_ref, at_ref, kdt_ref):
    h = pl.program_id(1)
    q = q_ref[0]
    k = k_ref[0]
    qf = q.astype(F32)
    kf = k.astype(F32)
    vf = v_ref[0].astype(F32)
    kk = _dot_nt(k, k)
    qk = _dot_nt(q, k)
    lane = lax.broadcasted_iota(jnp.int32, (GROUP, LANES), 1)
    r = lax.broadcasted_iota(jnp.int32, (GROUP, GROUP), 0)
    c = lax.broadcasted_iota(jnp.int32, (GROUP, GROUP), 1)
    same = (r >> CHUNK_SHIFT) == (c >> CHUNK_SHIFT)
    eye = jnp.where(r == c, 1.0, 0.0)
    for d in range(2):
        lb = d * N_HEADS + h
        lg = GATE_LANE0 + lb
        beta_c = _col(beta_ref[0], lb, lane)
        g_c = _col(g_ref[0], lg, lane)
        eg_c = _col(eg_ref[0], lg, lane)
        egl_c = _col(egl_ref[0], lg, lane)
        g_r = (gtf_ref if d == 0 else gtb_ref)[0, 0]
        strict = same & ((c < r) if d == 0 else (c > r))
        incl = same & ((c <= r) if d == 0 else (c >= r))
        dec = jnp.exp(jnp.minimum(g_c - g_r, 0.0))
        p = jnp.where(strict, -(kk * beta_c) * dec, 0.0)
        att = jnp.where(incl, qk * dec, 0.0)
        p4 = jnp.where((r >> 2) == (c >> 2), p, 0.0).astype(BF16)
        t = eye + p4.astype(F32)
        t = t + _dot(t.astype(BF16), _dot(p4, p4).astype(BF16))
        for shift in range(2, CHUNK_SHIFT):
            off = ((r >> shift) != (c >> shift)) & ((r >> (shift + 1)) == (c >> (shift + 1)))
            tb = t.astype(BF16)
            t = t + _dot(_dot(tb, jnp.where(off, p, 0.0).astype(BF16)).astype(BF16), tb)
        rhs = jnp.concatenate([vf * beta_c, kf * (beta_c * eg_c)], axis=1).astype(BF16)
        uw = _dot(t.astype(BF16), rhs)
        u_ref[0, d] = uw[:, :HEAD_W].astype(u_ref.dtype)
        w_ref[0, d] = uw[:, HEAD_W:].astype(w_ref.dtype)
        qd_ref[0, d] = (qf * eg_c).astype(qd_ref.dtype)
        kdt = (kf * egl_c).T
        for ci in range(CHUNKS_PER_GROUP):
            sl = slice(ci * GDN_CHUNK, (ci + 1) * GDN_CHUNK)
            at_ref[0, d, 0, ci] = att[sl, sl].astype(at_ref.dtype)
            kdt_ref[0, d, 0, ci] = kdt[:, sl].astype(kdt_ref.dtype)


def _gdn_prep(qkv, beta, G, eG, eGl, GT):
    B, S, _ = qkv.shape
    nG = S // GROUP
    nC = S // GDN_CHUNK
    sm = pl.BlockSpec((1, GROUP, LANES), lambda b, h, g: (b, g, 0))
    big = pl.BlockSpec((1, 2, GROUP, HEAD_W), lambda b, h, g: (b, 0, g, h))
    return pl.pallas_call(
        _prep_kernel,
        grid=(B, N_HEADS, nG),
        in_specs=[pl.BlockSpec((1, GROUP, HEAD_W), lambda b, h, g: (b, g, h)),
                  pl.BlockSpec((1, GROUP, HEAD_W), lambda b, h, g: (b, g, N_HEADS + h)),
                  pl.BlockSpec((1, GROUP, HEAD_W), lambda b, h, g: (b, g, 2 * N_HEADS + h)),
                  sm, sm, sm, sm,
                  pl.BlockSpec((1, 1, 1, GROUP), lambda b, h, g: (b, h, 0, g)),
                  pl.BlockSpec((1, 1, 1, GROUP), lambda b, h, g: (b, N_HEADS + h, 0, g))],
        out_specs=[big, big, big,
                   pl.BlockSpec((1, 2, 1, CHUNKS_PER_GROUP, GDN_CHUNK, GDN_CHUNK), lambda b, h, g: (b, 0, h, g, 0, 0)),
                   pl.BlockSpec((1, 2, 1, CHUNKS_PER_GROUP, HEAD_W, GDN_CHUNK), lambda b, h, g: (b, 0, h, g, 0, 0))],
        out_shape=[jax.ShapeDtypeStruct((B, 2, S, D_MODEL), BF16)] * 3
                  + [jax.ShapeDtypeStruct((B, 2, N_HEADS, nC, GDN_CHUNK, GDN_CHUNK), BF16),
                     jax.ShapeDtypeStruct((B, 2, N_HEADS, nC, HEAD_W, GDN_CHUNK), BF16)],
        compiler_params=_params(("parallel", "parallel", "parallel")),
        name="gdn_prep",
    )(qkv, qkv, qkv, beta, G, eG, eGl, GT, GT)


def _scan_kernel(cd_ref, u_ref, w_ref, qd_ref, at_ref, kdt_ref, o_ref, state_ref, *, nc, nC):
    b = pl.program_id(0)
    d = pl.program_id(1)
    t = pl.program_id(2)
    nT = pl.num_programs(2)
    tt = jnp.where(d == 0, t, nT - 1 - t)

    @pl.when(t == 0)
    def _():
        state_ref[...] = jnp.zeros_like(state_ref)

    def chunk(ci, carry):
        c = jnp.where(d == 0, ci, nc - 1 - ci)
        row = pl.multiple_of(c * GDN_CHUNK, GDN_CHUNK)
        gc = tt * nc + c
        for h in range(N_HEADS):
            cols = slice(h * HEAD_W, (h + 1) * HEAD_W)
            s = state_ref[h]
            sb = s.astype(BF16)
            w_c = w_ref[0, 0, pl.ds(row, GDN_CHUNK), cols]
            u_c = u_ref[0, 0, pl.ds(row, GDN_CHUNK), cols]
            qd_c = qd_ref[0, 0, pl.ds(row, GDN_CHUNK), cols]
            v_new = u_c.astype(F32) - _dot(w_c, sb)
            vb = v_new.astype(BF16)
            o = _dot(qd_c, sb) + _dot(at_ref[0, 0, h, c], vb)
            cdv = cd_ref[((b * 2 + d) * N_HEADS + h) * nC + gc]
            state_ref[h] = s * cdv + _dot(kdt_ref[0, 0, h, c], vb)
            o_ref[0, 0, pl.ds(row, GDN_CHUNK), cols] = o.astype(o_ref.dtype)
        return carry

    lax.fori_loop(0, nc, chunk, 0)


def _gdn_scan(cd, u, w, qd, att, kdt):
    B, _, S, _ = u.shape
    TC = min(512, S)
    nT = S // TC
    nc = TC // GDN_CHUNK
    nC = S // GDN_CHUNK

    def tmap(b, d, t):
        return jnp.where(d == 0, t, nT - 1 - t)

    big = pl.BlockSpec((1, 1, TC, D_MODEL), lambda b, d, t: (b, d, tmap(b, d, t), 0))
    return pl.pallas_call(
        functools.partial(_scan_kernel, nc=nc, nC=nC),
        grid=(B, 2, nT),
        in_specs=[pl.BlockSpec(memory_space=pltpu.SMEM),
                  big, big, big,
                  pl.BlockSpec((1, 1, N_HEADS, nc, GDN_CHUNK, GDN_CHUNK), lambda b, d, t: (b, d, 0, tmap(b, d, t), 0, 0)),
                  pl.BlockSpec((1, 1, N_HEADS, nc, HEAD_W, GDN_CHUNK), lambda b, d, t: (b, d, 0, tmap(b, d, t), 0, 0))],
        out_specs=big,
        out_shape=jax.ShapeDtypeStruct((B, 2, S, D_MODEL), BF16),
        scratch_shapes=[pltpu.VMEM((N_HEADS, HEAD_W, HEAD_W), F32)],
        compiler_params=_params(("parallel", "parallel", "arbitrary")),
        name="gdn_scan",
    )(cd, u, w, qd, att, kdt)


def _rope_kernel(x_ref, cos_ref, sin_ref, o_ref):
    g = pl.program_id(2)
    scale = jnp.where(g == 0, DIFF_DH ** -0.5, 1.0)
    cs = cos_ref[...] * scale
    sn = sin_ref[...] * scale
    lane = lax.broadcasted_iota(jnp.int32, cs.shape, 1)
    first_half = (lane & (DIFF_DH - 1)) < (DIFF_DH // 2)
    for h in range(N_HEADS):
        cols = slice(h * HEAD_W, (h + 1) * HEAD_W)
        x = x_ref[0, :, cols].astype(F32)
        partner = jnp.where(first_half, pltpu.roll(x, HEAD_W - DIFF_DH // 2, 1), pltpu.roll(x, DIFF_DH // 2, 1))
        o_ref[0, 0, :, cols] = (x * cs + partner * sn).astype(o_ref.dtype)


def _rope(proj3, cos_t, sin_t):
    B, S, _ = proj3.shape
    TR = min(512, S)
    tab = pl.BlockSpec((TR, HEAD_W), lambda b, i, g: (i, 0))
    return pl.pallas_call(
        _rope_kernel,
        grid=(B, S // TR, 2),
        in_specs=[pl.BlockSpec((1, TR, D_MODEL), lambda b, i, g: (b, i, COL_QB + g)), tab, tab],
        out_specs=pl.BlockSpec((1, 1, TR, D_MODEL), lambda b, i, g: (g, b, i, 0)),
        out_shape=jax.ShapeDtypeStruct((2, B, S, D_MODEL), BF16),
        compiler_params=_params(("parallel", "parallel", "parallel")),
        name="rope",
    )(proj3, cos_t, sin_t)


def _attn_kernel(q_ref, k_ref, v_ref, la_ref, lb_ref, g_ref, o_ref, m_ref, l_ref, acc_ref, *, TK):
    ki = pl.program_id(3)

    @pl.when(ki == 0)
    def _():
        m_ref[...] = jnp.full_like(m_ref, NEG_INF)
        l_ref[...] = jnp.zeros_like(l_ref)
        acc_ref[...] = jnp.zeros_like(acc_ref)

    q = q_ref[0, 0]
    k = k_ref[0, 0]
    v = v_ref[0]
    lane = lax.broadcasted_iota(jnp.int32, q.shape, 1)
    zero = jnp.zeros_like(q)
    for comp in range(2):
        in_comp = (lane < DIFF_DH) if comp == 0 else (lane >= DIFF_DH)
        s = _dot_nt(jnp.where(in_comp, q, zero), k)
        m_prev = m_ref[comp]
        m_new = jnp.maximum(m_prev, jnp.max(s, axis=1, keepdims=True))
        alpha = jnp.exp(m_prev - m_new)
        p = jnp.exp(s - jnp.concatenate([m_new] * (TK // LANES), axis=1))
        l_ref[comp] = alpha * l_ref[comp] + jnp.sum(p, axis=1, keepdims=True)
        acc_ref[comp] = alpha * acc_ref[comp] + _dot(p.astype(BF16), v)
        m_ref[comp] = m_new

    @pl.when(ki == pl.num_programs(3) - 1)
    def _():
        row = lax.broadcasted_iota(jnp.int32, la_ref.shape, 0)
        sums = jnp.sum(la_ref[...] * lb_ref[...], axis=1, keepdims=True)
        sign = jnp.where(row == 0, 1.0, jnp.where(row == 1, -1.0, 0.0))
        lam = jnp.sum(sign * jnp.exp(sums), axis=0, keepdims=True) + LAM_INIT
        o = acc_ref[0] / l_ref[0] - lam * (acc_ref[1] / l_ref[1])
        ms = jnp.mean(o * o, axis=-1, keepdims=True)
        y = o * lax.rsqrt(ms + EPS) * g_ref[...] * (1.0 - LAM_INIT)
        o_ref[0] = y.astype(o_ref.dtype)


def _diff_attn(qk_rot, proj3, lam_a, lam_b, norm_g):
    _, B, S, _ = qk_rot.shape
    TQ = min(512, S)
    TK = min(512, S)
    lam_spec = pl.BlockSpec((8, LANES), lambda b, h, qi, ki: (0, 0))
    return pl.pallas_call(
        functools.partial(_attn_kernel, TK=TK),
        grid=(B, N_HEADS, S // TQ, S // TK),
        in_specs=[pl.BlockSpec((1, 1, TQ, HEAD_W), lambda b, h, qi, ki: (0, b, qi, h)),
                  pl.BlockSpec((1, 1, TK, HEAD_W), lambda b, h, qi, ki: (1, b, ki, h)),
                  pl.BlockSpec((1, TK, HEAD_W), lambda b, h, qi, ki: (b, ki, COL_VB * N_HEADS + h)),
                  lam_spec, lam_spec,
                  pl.BlockSpec((1, HEAD_W), lambda b, h, qi, ki: (0, 0))],
        out_specs=pl.BlockSpec((1, TQ, HEAD_W), lambda b, h, qi, ki: (b, qi, h)),
        out_shape=jax.ShapeDtypeStruct((B, S, D_MODEL), BF16),
        scratch_shapes=[pltpu.VMEM((2, TQ, LANES), F32), pltpu.VMEM((2, TQ, LANES), F32),
                        pltpu.VMEM((2, TQ, HEAD_W), F32)],
        compiler_params=_params(("parallel", "parallel", "parallel", "arbitrary")),
        name="diff_attn",
    )(qk_rot, qk_rot, proj3, lam_a, lam_b, norm_g)


def _merge_kernel(of_ref, ob_ref, z_ref, oB_ref, ga_ref, gb_ref, bga_ref, bgb_ref, x_ref, g1_ref, gn_ref,
                  wa_ref, wb_ref, wo_ref, o_ref, ya_ref):
    oa = of_ref[0, 0].astype(F32) + ob_ref[0, 0].astype(F32)
    z = z_ref[...].astype(F32)
    gate = z * _sigmoid(z)
    for h in range(N_HEADS):
        cols = slice(h * HEAD_W, (h + 1) * HEAD_W)
        oh = oa[:, cols]
        ms = jnp.mean(oh * oh, axis=-1, keepdims=True)
        ya_ref[:, cols] = (oh * lax.rsqrt(ms + EPS) * gn_ref[...] * gate[:, cols]).astype(BF16)
    y_a = _dot(ya_ref[...], wa_ref[...])
    y_b = _dot(oB_ref[...], wb_ref[...])
    gate_a = _sigmoid(ga_ref[...].astype(F32) + bga_ref[...])
    gate_b = _sigmoid(gb_ref[...].astype(F32) + bgb_ref[...])
    mix = _dot((gate_a * y_a + gate_b * y_b).astype(BF16), wo_ref[...])
    o_ref[...] = x_ref[...] + g1_ref[0] * mix


def _merge(oA, proj, oB, b_gate, x2, gate1, gn, wa, wb, wo, S):
    N = x2.shape[0]
    TM = min(512, S)
    tpb = S // TM
    row = lambda i: (i, 0)
    full = pl.BlockSpec((D_MODEL, D_MODEL), lambda i: (0, 0))
    return pl.pallas_call(
        _merge_kernel,
        grid=(N // TM,),
        in_specs=[pl.BlockSpec((1, 1, TM, D_MODEL), lambda i: (i // tpb, 0, i % tpb, 0)),
                  pl.BlockSpec((1, 1, TM, D_MODEL), lambda i: (i // tpb, 1, i % tpb, 0)),
                  pl.BlockSpec((TM, D_MODEL), lambda i: (i, COL_ZA)),
                  pl.BlockSpec((TM, D_MODEL), row),
                  pl.BlockSpec((TM, D_MODEL), lambda i: (i, COL_GA)),
                  pl.BlockSpec((TM, D_MODEL), lambda i: (i, COL_GB)),
                  pl.BlockSpec((1, D_MODEL), lambda i: (0, 0)),
                  pl.BlockSpec((1, D_MODEL), lambda i: (0, 1)),
                  pl.BlockSpec((TM, D_MODEL), row),
                  pl.BlockSpec((1, 1, D_MODEL), lambda i: (i // tpb, 0, 0)),
                  pl.BlockSpec((1, HEAD_W), lambda i: (0, 0)),
                  full, full, full],
        out_specs=pl.BlockSpec((TM, D_MODEL), row),
        out_shape=jax.ShapeDtypeStruct((N, D_MODEL), F32),
        scratch_shapes=[pltpu.VMEM((TM, D_MODEL), BF16)],
        compiler_params=_params(("parallel",)),
        name="merge",
    )(oA, oA, proj, oB, proj, proj, b_gate, b_gate, x2, gate1, gn, wa, wb, wo)


def _router_kernel(x_ref, sh_ref, sc_ref, g_ref, rw0_ref, rw1_ref, rb_ref, tri_ref,
                   h_ref, idx_ref, gate_ref, rank_ref, cnt_ref, base_ref):
    i = pl.program_id(0)

    @pl.when(i == 0)
    def _():
        base_ref[...] = jnp.zeros_like(base_ref)

    x = x_ref[...]
    ms = jnp.mean(x * x, axis=-1, keepdims=True)
    h = x * lax.rsqrt(ms + EPS) * g_ref[...] * (1.0 + sc_ref[0]) + sh_ref[0]
    h_ref[...] = h
    h0 = h.astype(BF16)
    h1 = (h - h0.astype(F32)).astype(BF16)
    logits = _dot(h0, rw0_ref[...]) + (_dot(h0, rw1_ref[...]) + _dot(h1, rw0_ref[...])) + rb_ref[...]
    lane = lax.broadcasted_iota(jnp.int32, logits.shape, 1)
    lane_f = lane.astype(F32)
    cur = jnp.where(lane < N_EXPERTS, logits, NEG_INF)
    vals, sel = [], []
    for _ in range(TOP_K):
        m = jnp.max(cur, axis=1, keepdims=True)
        ix = jnp.min(jnp.where(cur == m, lane_f, float(LANES)), axis=1, keepdims=True)
        hit = lane_f == ix
        vals.append(m)
        sel.append(hit)
        cur = jnp.where(hit, NEG_INF, cur)
    exps = [jnp.exp(v - vals[0]) for v in vals]
    den = exps[0] + exps[1] + exps[2] + exps[3]
    onehot = jnp.zeros(logits.shape, F32)
    for hit in sel:
        onehot = onehot + jnp.where(hit, 1.0, 0.0)
    before = _dot(tri_ref[...], onehot.astype(BF16)) + base_ref[...]
    idx_out = jnp.zeros(logits.shape, F32)
    gate_out = jnp.zeros(logits.shape, F32)
    rank_out = jnp.zeros(logits.shape, F32)
    for kk in range(TOP_K):
        slot = lane == kk
        e_id = jnp.sum(jnp.where(sel[kk], lane_f, 0.0), axis=1, keepdims=True)
        rk = jnp.sum(jnp.where(sel[kk], before, 0.0), axis=1, keepdims=True)
        idx_out = jnp.where(slot, e_id, idx_out)
        gate_out = jnp.where(slot, exps[kk] / den, gate_out)
        rank_out = jnp.where(slot, rk, rank_out)
    idx_ref[...] = idx_out.astype(jnp.int32)
    gate_ref[...] = gate_out
    rank_ref[...] = rank_out.astype(jnp.int32)
    base_ref[...] = base_ref[...] + jnp.sum(onehot, axis=0, keepdims=True)
    cnt_ref[...] = base_ref[...]


def _router(x1, shift, scale, g, rw0, rw1, rb, S):
    N = x1.shape[0]
    TM = min(512, S)
    tpb = S // TM
    r = jnp.arange(TM)
    tri = (r[None, :] < r[:, None]).astype(BF16)
    row = lambda i: (i, 0)
    const = lambda i: (0, 0)
    lanes = pl.BlockSpec((TM, LANES), row)
    return pl.pallas_call(
        _router_kernel,
        grid=(N // TM,),
        in_specs=[pl.BlockSpec((TM, D_MODEL), row),
                  pl.BlockSpec((1, 1, D_MODEL), lambda i: (i // tpb, 0, 0)),
                  pl.BlockSpec((1, 1, D_MODEL), lambda i: (i // tpb, 0, 0)),
                  pl.BlockSpec((1, D_MODEL), const),
                  pl.BlockSpec((D_MODEL, LANES), const),
                  pl.BlockSpec((D_MODEL, LANES), const),
                  pl.BlockSpec((1, LANES), const),
                  pl.BlockSpec((TM, TM), const)],
        out_specs=[pl.BlockSpec((TM, D_MODEL), row), lanes, lanes, lanes, pl.BlockSpec((1, LANES), const)],
        out_shape=[jax.ShapeDtypeStruct((N, D_MODEL), F32),
                   jax.ShapeDtypeStruct((N, LANES), jnp.int32),
                   jax.ShapeDtypeStruct((N, LANES), F32),
                   jax.ShapeDtypeStruct((N, LANES), jnp.int32),
                   jax.ShapeDtypeStruct((1, LANES), F32)],
        scratch_shapes=[pltpu.VMEM((1, LANES), F32)],
        compiler_params=_params(("arbitrary",)),
        name="router",
    )(x1, shift, scale, g, rw0, rw1, rb, tri)


INDEX_SLICE = ROW_MOVE_TILE * TOP_K


def _row_copy_out(h_ref, xs_hbm, sem, r, dst):
    return pltpu.make_async_copy(h_ref.at[pl.ds(r, 1)], xs_hbm.at[pl.ds(dst, 1)], sem)


def _dispatch_kernel(dest_hbm, h_ref, xs_in_hbm, xs_hbm, idx_smem, sem_idx, sem_rows):
    del xs_in_hbm
    i = pl.program_id(0)
    fetch = pltpu.make_async_copy(dest_hbm.at[pl.ds(i * INDEX_SLICE, INDEX_SLICE)], idx_smem, sem_idx)
    fetch.start()
    fetch.wait()

    def start(r, carry):
        for kk in range(TOP_K):
            _row_copy_out(h_ref, xs_hbm, sem_rows, r, idx_smem[r * TOP_K + kk]).start()
        return carry

    lax.fori_loop(0, ROW_MOVE_TILE, start, 0)

    def drain(r, carry):
        for kk in range(TOP_K):
            _row_copy_out(h_ref, xs_hbm, sem_rows, 0, 0).wait()
        return carry

    lax.fori_loop(0, ROW_MOVE_TILE, drain, 0)


def _dispatch(dest_flat, h2, n_rows):
    N = h2.shape[0]
    xs0 = jnp.zeros((n_rows, D_MODEL), F32)
    return pl.pallas_call(
        _dispatch_kernel,
        grid=(N // ROW_MOVE_TILE,),
        in_specs=[pl.BlockSpec(memory_space=pl.ANY),
                  pl.BlockSpec((ROW_MOVE_TILE, D_MODEL), lambda i: (i, 0)),
                  pl.BlockSpec(memory_space=pl.ANY)],
        out_specs=pl.BlockSpec(memory_space=pl.ANY),
        out_shape=jax.ShapeDtypeStruct((n_rows, D_MODEL), F32),
        scratch_shapes=[pltpu.SMEM((INDEX_SLICE,), jnp.int32), pltpu.SemaphoreType.DMA, pltpu.SemaphoreType.DMA],
        input_output_aliases={2: 0},
        compiler_params=_params(("arbitrary",)),
        name="moe_dispatch",
    )(dest_flat, h2, xs0)


def _expert_kernel(te_ref, xs_ref, wg_ref, bg_ref, wl_ref, bl_ref, wd_ref, bd_ref, ys_ref):
    del te_ref
    xb = xs_ref[...].astype(BF16)
    glu = jnp.minimum(_dot(xb, wg_ref[0]) + bg_ref[0], SWIGLU_LIMIT)
    lin = jnp.clip(_dot(xb, wl_ref[0]) + bl_ref[0], -SWIGLU_LIMIT, SWIGLU_LIMIT)
    act = glu * _sigmoid(SWIGLU_ALPHA * glu) * (lin + 1.0)
    ys_ref[...] = _dot(act.astype(BF16), wd_ref[0]) + bd_ref[0]


def _experts(tile_expert, xs, wg, bg, wl, bl, wd, bd):
    n_rows = xs.shape[0]
    n_tiles = n_rows // MOE_TILE
    wspec = pl.BlockSpec((1, D_MODEL, D_MODEL), lambda i, te: (te[i], 0, 0))
    bspec = pl.BlockSpec((1, 1, D_MODEL), lambda i, te: (te[i], 0, 0))
    rows = pl.BlockSpec((MOE_TILE, D_MODEL), lambda i, te: (i, 0))
    return pl.pallas_call(
        _expert_kernel,
        grid_spec=pltpu.PrefetchScalarGridSpec(
            num_scalar_prefetch=1,
            grid=(n_tiles,),
            in_specs=[rows, wspec, bspec, wspec, bspec, wspec, bspec],
            out_specs=rows),
        out_shape=jax.ShapeDtypeStruct((n_rows, D_MODEL), F32),
        compiler_params=_params(("arbitrary",)),
        name="moe_experts",
    )(tile_expert, xs, wg, bg, wl, bl, wd, bd)


def _row_copy_in(ys_hbm, buf_ref, sem, src, kk, r):
    return pltpu.make_async_copy(ys_hbm.at[pl.ds(src, 1)], buf_ref.at[kk, pl.ds(r, 1)], sem)


def _combine_kernel(dest_hbm, ys_hbm, gate_ref, x_ref, g2_ref, fg_ref, o_ref, idx_smem, buf_ref, sem_idx, sem_rows):
    i = pl.program_id(0)
    fetch = pltpu.make_async_copy(dest_hbm.at[pl.ds(i * INDEX_SLICE, INDEX_SLICE)], idx_smem, sem_idx)
    fetch.start()
    fetch.wait()

    def start(r, carry):
        for kk in range(TOP_K):
            _row_copy_in(ys_hbm, buf_ref, sem_rows, idx_smem[r * TOP_K + kk], kk, r).start()
        return carry

    lax.fori_loop(0, ROW_MOVE_TILE, start, 0)

    def drain(r, carry):
        for kk in range(TOP_K):
            _row_copy_in(ys_hbm, buf_ref, sem_rows, 0, 0, 0).wait()
        return carry

    lax.fori_loop(0, ROW_MOVE_TILE, drain, 0)

    gates = gate_ref[...]
    moe = gates[:, 0:1] * buf_ref[0]
    for kk in range(1, TOP_K):
        moe = moe + gates[:, kk:kk + 1] * buf_ref[kk]
    x = x_ref[...] + g2_ref[0] * moe
    ms = jnp.mean(x * x, axis=-1, keepdims=True)
    o_ref[...] = x * lax.rsqrt(ms + EPS) * fg_ref[...]


def _combine(dest_flat, ys, gates, x1, gate2, final_g, S):
    N = x1.shape[0]
    TM = ROW_MOVE_TILE
    tpb = S // TM
    row = lambda i: (i, 0)
    return pl.pallas_call(
        _combine_kernel,
        grid=(N // TM,),
        in_specs=[pl.BlockSpec(memory_space=pl.ANY),
                  pl.BlockSpec(memory_space=pl.ANY),
                  pl.BlockSpec((TM, LANES), row),
                  pl.BlockSpec((TM, D_MODEL), row),
                  pl.BlockSpec((1, 1, D_MODEL), lambda i: (i // tpb, 0, 0)),
                  pl.BlockSpec((1, D_MODEL), lambda i: (0, 0))],
        out_specs=pl.BlockSpec((TM, D_MODEL), row),
        out_shape=jax.ShapeDtypeStruct((N, D_MODEL), F32),
        scratch_shapes=[pltpu.SMEM((INDEX_SLICE,), jnp.int32), pltpu.VMEM((TOP_K, TM, D_MODEL), F32),
                        pltpu.SemaphoreType.DMA, pltpu.SemaphoreType.DMA],
        compiler_params=_params(("arbitrary",)),
        name="moe_combine",
    )(dest_flat, ys, gates, x1, gate2, final_g)


def _pad_lanes(a, offset=0):
    return jnp.pad(a, ((0, 0), (offset, LANES - offset - a.shape[1])))


def kernel(x, c, ada_w, ada_b, norm1_g, norm2_g, w_in, b_gate, conv_w, a_log, dt_bias, gdn_norm_g, w_branch_a,
           diff_lambda, diff_norm_g, w_branch_b, w_out, router_w, router_b, w_glu, b_glu, w_lin, b_lin, w_down,
           b_down, final_g):
    B, S, D = x.shape
    N = B * S
    assert D == D_MODEL and S % GROUP == 0 and ada_w.shape[0] == 1
    x2 = x.reshape(N, D)

    mod = _adaln(c, ada_w[0], ada_b[0])
    shift1, scale1, gate1, shift2, scale2, gate2 = [m.reshape(B, 1, D) for m in jnp.split(mod, 6, axis=-1)]

    wi = w_in[0]
    n_a = 4 * D
    n_small = 4 * N_HEADS
    w_big = jnp.concatenate([wi[:, :n_a], wi[:, n_a + n_small:]], axis=1).astype(BF16)
    w_small = _pad_lanes(wi[:, n_a:n_a + n_small]).astype(BF16)
    proj, small = _inproj(x2, shift1, scale1, norm1_g, w_big, w_small, S)
    proj3 = proj.reshape(B, S, N_COL_BLOCKS * D)

    qkv = _gdn_conv(proj3, conv_w[0])
    alog_row = _pad_lanes(a_log[0].reshape(1, -1), GATE_LANE0)
    dt_row = _pad_lanes(dt_bias[0].reshape(1, -1), GATE_LANE0)
    beta, G, eG, eGl, cdl = _gdn_gates(small, alog_row, dt_row)
    r3 = lambda a: a.reshape(B, S, LANES)
    nC = S // GDN_CHUNK
    GT = jnp.transpose(r3(G)[:, :, GATE_LANE0:GATE_LANE0 + 2 * N_HEADS], (0, 2, 1)).reshape(B, 2 * N_HEADS, 1, S)
    cd = r3(cdl).reshape(B, nC, GDN_CHUNK, LANES)[:, :, 0, GATE_LANE0:GATE_LANE0 + 2 * N_HEADS]
    cd = jnp.transpose(cd.reshape(B, nC, 2, N_HEADS), (0, 2, 3, 1)).reshape(-1)
    u, w, qd, att, kdt = _gdn_prep(qkv, r3(beta), r3(G), r3(eG), r3(eGl), GT)
    oA = _gdn_scan(cd, u, w, qd, att, kdt)

    half = DIFF_DH // 2
    inv_freq = ROPE_THETA ** (-jnp.arange(half, dtype=F32) / half)
    ang = jnp.arange(S, dtype=F32)[:, None] * inv_freq[None, :]
    cos_t = jnp.tile(jnp.cos(ang), (1, 4))
    sin_h = jnp.sin(ang)
    sin_t = jnp.tile(jnp.concatenate([-sin_h, sin_h], axis=1), (1, 2))
    qk_rot = _rope(proj3, cos_t, sin_t)
    lam_a = jnp.pad(_pad_lanes(diff_lambda[0][0::2]), ((0, 6), (0, 0)))
    lam_b = jnp.pad(_pad_lanes(diff_lambda[0][1::2]), ((0, 6), (0, 0)))
    oB = _diff_attn(qk_rot, proj3, lam_a, lam_b, diff_norm_g)

    x1 = _merge(oA, proj, oB.reshape(N, D), b_gate, x2, gate1, gdn_norm_g,
                w_branch_a[0].astype(BF16), w_branch_b[0].astype(BF16), w_out[0].astype(BF16), S)

    rw = _pad_lanes(router_w[0])
    rw0 = rw.astype(BF16)
    rw1 = (rw - rw0.astype(F32)).astype(BF16)
    h2, idx, gates, rank, counts = _router(x1, shift2, scale2, norm2_g, rw0, rw1, _pad_lanes(router_b), S)
    cnt = counts[0, :N_EXPERTS].astype(jnp.int32)
    padded = (cnt + MOE_TILE - 1) // MOE_TILE * MOE_TILE
    pad_ends = jnp.cumsum(padded)
    pad_starts = pad_ends - padded
    dest = (pad_starts[idx[:, :TOP_K]] + rank[:, :TOP_K]).reshape(-1)
    n_tiles = -(-(N * TOP_K) // MOE_TILE) + N_EXPERTS
    tile_start = jnp.arange(n_tiles, dtype=jnp.int32) * MOE_TILE
    tile_expert = jnp.minimum(jnp.searchsorted(pad_ends, tile_start, side="right"), N_EXPERTS - 1).astype(jnp.int32)
    xs = _dispatch(dest, h2, n_tiles * MOE_TILE)
    ys = _experts(tile_expert, xs, w_glu[0].astype(BF16), b_glu[0][:, None, :], w_lin[0].astype(BF16),
                  b_lin[0][:, None, :], w_down[0].astype(BF16), b_down[0][:, None, :])
    out = _combine(dest, ys, gates, x1, gate2, final_g.reshape(1, D), S)
    return out.reshape(B, S, D)
```

```python
import functools
import math

import jax
import jax.numpy as jnp
from jax import lax
from jax.experimental import pallas as pl
from jax.experimental.pallas import tpu as pltpu

F32 = jnp.float32
BF16 = jnp.bfloat16

D_MODEL = 1024
EPS = 1e-6
N_HEADS = 8
HEAD_W = 128
GDN_CHUNK = 64
CONV_WIDTH = 5
DIFF_DH = 64
ROPE_THETA = 10000.0
LAM_INIT = 0.8 - 0.6 * math.exp(-0.3 * 0)
N_EXPERTS = 32
TOP_K = 4
SWIGLU_ALPHA = 1.702
SWIGLU_LIMIT = 7.0

LANES = 128
GROUP = 256
CHUNKS_PER_GROUP = GROUP // GDN_CHUNK
CHUNK_SHIFT = GDN_CHUNK.bit_length() - 1
MOE_TILE = 512
ROW_MOVE_TILE = 256
NEG_INF = float("-inf")

COL_QA, COL_KA, COL_VA, COL_ZA, COL_QB, COL_KB, COL_VB, COL_GA, COL_GB = range(9)
N_COL_BLOCKS = 9


def _params(sem, vmem_mb=48):
    return pltpu.CompilerParams(dimension_semantics=sem, vmem_limit_bytes=vmem_mb * 1024 * 1024)


def _dot(a, b):
    return jnp.dot(a, b, preferred_element_type=F32)


def _dot_nt(a, b):
    return lax.dot_general(a, b, (((1,), (1,)), ((), ())), preferred_element_type=F32)


def _sigmoid(x):
    return 1.0 / (1.0 + jnp.exp(-x))


def _split3(x):
    a = x.astype(BF16)
    r = x - a.astype(F32)
    b = r.astype(BF16)
    c = (r - b.astype(F32)).astype(BF16)
    return a, b, c


def _adaln_kernel(c_ref, w_ref, b_ref, o_ref):
    c = c_ref[...]
    cond = c * _sigmoid(c)
    c0, c1, c2 = _split3(cond)
    w0, w1, w2 = _split3(w_ref[...])
    acc = _dot(c0, w0) + (_dot(c0, w1) + _dot(c1, w0)) + (_dot(c0, w2) + _dot(c1, w1) + _dot(c2, w0))
    o_ref[...] = acc + b_ref[...]


def _adaln(c, ada_w, ada_b):
    B = c.shape[0]
    n = ada_w.shape[1] // D_MODEL
    return pl.pallas_call(
        _adaln_kernel,
        grid=(n,),
        in_specs=[pl.BlockSpec((B, D_MODEL), lambda j: (0, 0)),
                  pl.BlockSpec((D_MODEL, D_MODEL), lambda j: (0, j)),
                  pl.BlockSpec((1, D_MODEL), lambda j: (0, j))],
        out_specs=pl.BlockSpec((B, D_MODEL), lambda j: (0, j)),
        out_shape=jax.ShapeDtypeStruct((B, n * D_MODEL), F32),
        compiler_params=_params(("parallel",)),
        name="adaln",
    )(c, ada_w, ada_b.reshape(1, -1))


def _inproj_kernel(x_ref, sh_ref, sc_ref, g_ref, w_ref, ws_ref, o_ref, os_ref, h_ref):
    @pl.when(pl.program_id(1) == 0)
    def _():
        x = x_ref[...]
        ms = jnp.mean(x * x, axis=-1, keepdims=True)
        y = x * lax.rsqrt(ms + EPS) * g_ref[...]
        h = (y * (1.0 + sc_ref[0]) + sh_ref[0]).astype(BF16)
        h_ref[...] = h
        os_ref[...] = _dot(h, ws_ref[...])

    o_ref[...] = _dot(h_ref[...], w_ref[...]).astype(o_ref.dtype)


def _inproj(x2, shift, scale, g, w_big, w_small, S):
    N = x2.shape[0]
    TM = min(1024, S)
    tpb = S // TM
    return pl.pallas_call(
        _inproj_kernel,
        grid=(N // TM, N_COL_BLOCKS),
        in_specs=[pl.BlockSpec((TM, D_MODEL), lambda i, j: (i, 0)),
                  pl.BlockSpec((1, 1, D_MODEL), lambda i, j: (i // tpb, 0, 0)),
                  pl.BlockSpec((1, 1, D_MODEL), lambda i, j: (i // tpb, 0, 0)),
                  pl.BlockSpec((1, D_MODEL), lambda i, j: (0, 0)),
                  pl.BlockSpec((D_MODEL, D_MODEL), lambda i, j: (0, j)),
                  pl.BlockSpec((D_MODEL, LANES), lambda i, j: (0, 0))],
        out_specs=[pl.BlockSpec((TM, D_MODEL), lambda i, j: (i, j)),
                   pl.BlockSpec((TM, LANES), lambda i, j: (i, 0))],
        out_shape=[jax.ShapeDtypeStruct((N, N_COL_BLOCKS * D_MODEL), BF16),
                   jax.ShapeDtypeStruct((N, LANES), F32)],
        scratch_shapes=[pltpu.VMEM((TM, D_MODEL), BF16)],
        compiler_params=_params(("parallel", "arbitrary")),
        name="inproj",
    )(x2, shift, scale, g, w_big, w_small)


HALO = 16


def _conv_kernel(cur_ref, prev_ref, next_ref, w_ref, o_ref, ext_ref, *, TR):
    i = pl.program_id(1)
    g = pl.program_id(2)
    last = pl.num_programs(1) - 1
    ext_ref[8:8 + TR, :] = cur_ref[0].astype(F32)
    pv = prev_ref[0].astype(F32)[HALO - 8:HALO]
    nx = next_ref[0].astype(F32)[0:8]
    ext_ref[0:8, :] = jnp.where(i > 0, pv, 0.0)
    ext_ref[TR + 8:TR + 16, :] = jnp.where(i < last, nx, 0.0)
    pad = (CONV_WIDTH - 1) // 2
    acc = ext_ref[8 - pad:8 - pad + TR, :] * w_ref[0:1, :]
    for j in range(1, CONV_WIDTH):
        acc = acc + ext_ref[8 - pad + j:8 - pad + j + TR, :] * w_ref[j:j + 1, :]
    y = acc * _sigmoid(acc)
    ones = jnp.ones((HEAD_W, HEAD_W), BF16)
    qscale = jnp.where(g == 0, HEAD_W ** -0.5, 1.0)
    for h in range(N_HEADS):
        yh = y[:, h * HEAD_W:(h + 1) * HEAD_W]
        ss = _dot((yh * yh).astype(BF16), ones)
        normed = yh * (lax.rsqrt(ss + EPS) * qscale)
        o_ref[0, :, h * HEAD_W:(h + 1) * HEAD_W] = jnp.where(g < 2, normed, yh).astype(o_ref.dtype)


def _gdn_conv(proj3, conv_w):
    B, S, _ = proj3.shape
    TR = min(512, S)
    nT = S // TR
    rb = TR // HALO
    nH = S // HALO
    return pl.pallas_call(
        functools.partial(_conv_kernel, TR=TR),
        grid=(B, nT, 3),
        in_specs=[pl.BlockSpec((1, TR, D_MODEL), lambda b, i, g: (b, i, g)),
                  pl.BlockSpec((1, HALO, D_MODEL), lambda b, i, g: (b, jnp.maximum(i * rb - 1, 0), g)),
                  pl.BlockSpec((1, HALO, D_MODEL), lambda b, i, g: (b, jnp.minimum((i + 1) * rb, nH - 1), g)),
                  pl.BlockSpec((CONV_WIDTH, D_MODEL), lambda b, i, g: (0, g))],
        out_specs=pl.BlockSpec((1, TR, D_MODEL), lambda b, i, g: (b, i, g)),
        out_shape=jax.ShapeDtypeStruct((B, S, 3 * D_MODEL), BF16),
        scratch_shapes=[pltpu.VMEM((TR + 16, D_MODEL), F32)],
        compiler_params=_params(("parallel", "parallel", "parallel")),
        name="gdn_conv",
    )(proj3, proj3, proj3, conv_w)


GATE_LANE0 = 16


def _gates_kernel(x_ref, alog_ref, dt_ref, beta_ref, g_ref, eg_ref, egl_ref, cd_ref):
    x = x_ref[...]
    R = x.shape[0]
    lane = lax.broadcasted_iota(jnp.int32, x.shape, 1)
    beta_ref[...] = _sigmoid(x)
    z = x + dt_ref[...]
    softplus = jnp.maximum(z, 0.0) + jnp.log(1.0 + jnp.exp(-jnp.abs(z)))
    gd = -jnp.exp(alog_ref[...]) * softplus
    gd = jnp.where((lane >= GATE_LANE0) & (lane < GATE_LANE0 + 2 * N_HEADS), gd, 0.0)
    r = lax.broadcasted_iota(jnp.int32, (R, R), 0)
    c = lax.broadcasted_iota(jnp.int32, (R, R), 1)
    same = (r >> CHUNK_SHIFT) == (c >> CHUNK_SHIFT)
    lower = jnp.where(same & (c <= r), 1.0, 0.0).astype(BF16)
    upper = jnp.where(same & (c >= r), 1.0, 0.0).astype(BF16)
    block = jnp.where(same, 1.0, 0.0).astype(BF16)
    p0, p1, p2 = _split3(gd)
    g_fwd = _dot(lower, p0) + _dot(lower, p1) + _dot(lower, p2)
    g_bwd = _dot(upper, p0) + _dot(upper, p1) + _dot(upper, p2)
    tot = _dot(block, p0) + _dot(block, p1) + _dot(block, p2)
    G = jnp.where(lane < GATE_LANE0 + N_HEADS, g_fwd, g_bwd)
    g_ref[...] = G
    eg_ref[...] = jnp.exp(G)
    egl_ref[...] = jnp.exp(tot - G)
    cd_ref[...] = jnp.exp(tot)


def _gdn_gates(small, alog_row, dt_row):
    N = small.shape[0]
    spec = pl.BlockSpec((GROUP, LANES), lambda i: (i, 0))
    row = pl.BlockSpec((1, LANES), lambda i: (0, 0))
    return pl.pallas_call(
        _gates_kernel,
        grid=(N // GROUP,),
        in_specs=[spec, row, row],
        out_specs=[spec] * 5,
        out_shape=[jax.ShapeDtypeStruct((N, LANES), F32)] * 5,
        compiler_params=_params(("parallel",)),
        name="gdn_gates",
    )(small, alog_row, dt_row)


def _col(x, l, lane):
    return jnp.sum(jnp.where(lane == l, x, 0.0), axis=1, keepdims=True)


def _prep_kernel(q_ref, k_ref, v_ref, beta_ref, g_ref, eg_ref, egl_ref, gtf_ref, gtb_ref,
                 u_ref, w_ref, qd_ref, at_ref, kdt_ref):
    h = pl.program_id(1)
    q = q_ref[0]
    k = k_ref[0]
    qf = q.astype(F32)
    kf = k.astype(F32)
    vf = v_ref[0].astype(F32)
    kk = _dot_nt(k, k)
    qk = _dot_nt(q, k)
    lane = lax.broadcasted_iota(jnp.int32, (GROUP, LANES), 1)
    r = lax.broadcasted_iota(jnp.int32, (GROUP, GROUP), 0)
    c = lax.broadcasted_iota(jnp.int32, (GROUP, GROUP), 1)
    same = (r >> CHUNK_SHIFT) == (c >> CHUNK_SHIFT)
    eye = jnp.where(r == c, 1.0, 0.0)
    for d in range(2):
        lb = d * N_HEADS + h
        lg = GATE_LANE0 + lb
        beta_c = _col(beta_ref[0], lb, lane)
        g_c = _col(g_ref[0], lg, lane)
        eg_c = _col(eg_ref[0], lg, lane)
        egl_c = _col(egl_ref[0], lg, lane)
        g_r = (gtf_ref if d == 0 else gtb_ref)[0, 0]
        strict = same & ((c < r) if d == 0 else (c > r))
        incl = same & ((c <= r) if d == 0 else (c >= r))
        dec = jnp.exp(jnp.minimum(g_c - g_r, 0.0))
        p = jnp.where(strict, -(kk * beta_c) * dec, 0.0)
        att = jnp.where(incl, qk * dec, 0.0)
        p4 = jnp.where((r >> 2) == (c >> 2), p, 0.0).astype(BF16)
        t = eye + p4.astype(F32)
        t = t + _dot(t.astype(BF16), _dot(p4, p4).astype(BF16))
        for shift in range(2, CHUNK_SHIFT):
            off = ((r >> shift) != (c >> shift)) & ((r >> (shift + 1)) == (c >> (shift + 1)))
            tb = t.astype(BF16)
            t = t + _dot(_dot(tb, jnp.where(off, p, 0.0).astype(BF16)).astype(BF16), tb)
        rhs = jnp.concatenate([vf * beta_c, kf * (beta_c * eg_c)], axis=1).astype(BF16)
        uw = _dot(t.astype(BF16), rhs)
        u_ref[0, d] = uw[:, :HEAD_W].astype(u_ref.dtype)
        w_ref[0, d] = uw[:, HEAD_W:].astype(w_ref.dtype)
        qd_ref[0, d] = (qf * eg_c).astype(qd_ref.dtype)
        kdt = (kf * egl_c).T
        for ci in range(CHUNKS_PER_GROUP):
            sl = slice(ci * GDN_CHUNK, (ci + 1) * GDN_CHUNK)
            at_ref[0, d, 0, ci] = att[sl, sl].astype(at_ref.dtype)
            kdt_ref[0, d, 0, ci] = kdt[:, sl].astype(kdt_ref.dtype)


def _gdn_prep(qkv, beta, G, eG, eGl, GT):
    B, S, _ = qkv.shape
    nG = S // GROUP
    nC = S // GDN_CHUNK
    sm = pl.BlockSpec((1, GROUP, LANES), lambda b, h, g: (b, g, 0))
    big = pl.BlockSpec((1, 2, GROUP, HEAD_W), lambda b, h, g: (b, 0, g, h))
    return pl.pallas_call(
        _prep_kernel,
        grid=(B, N_HEADS, nG),
        in_specs=[pl.BlockSpec((1, GROUP, HEAD_W), lambda b, h, g: (b, g, h)),
                  pl.BlockSpec((1, GROUP, HEAD_W), lambda b, h, g: (b, g, N_HEADS + h)),
                  pl.BlockSpec((1, GROUP, HEAD_W), lambda b, h, g: (b, g, 2 * N_HEADS + h)),
                  sm, sm, sm, sm,
                  pl.BlockSpec((1, 1, 1, GROUP), lambda b, h, g: (b, h, 0, g)),
                  pl.BlockSpec((1, 1, 1, GROUP), lambda b, h, g: (b, N_HEADS + h, 0, g))],
        out_specs=[big, big, big,
                   pl.BlockSpec((1, 2, 1, CHUNKS_PER_GROUP, GDN_CHUNK, GDN_CHUNK), lambda b, h, g: (b, 0, h, g, 0, 0)),
                   pl.BlockSpec((1, 2, 1, CHUNKS_PER_GROUP, HEAD_W, GDN_CHUNK), lambda b, h, g: (b, 0, h, g, 0, 0))],
        out_shape=[jax.ShapeDtypeStruct((B, 2, S, D_MODEL), BF16)] * 3
                  + [jax.ShapeDtypeStruct((B, 2, N_HEADS, nC, GDN_CHUNK, GDN_CHUNK), BF16),
                     jax.ShapeDtypeStruct((B, 2, N_HEADS, nC, HEAD_W, GDN_CHUNK), BF16)],
        compiler_params=_params(("parallel", "parallel", "parallel")),
        name="gdn_prep",
    )(qkv, qkv, qkv, beta, G, eG, eGl, GT, GT)


def _scan_kernel(cd_ref, u_ref, w_ref, qd_ref, at_ref, kdt_ref, o_ref, state_ref, *, nc, nC):
    b = pl.program_id(0)
    d = pl.program_id(1)
    t = pl.program_id(2)
    nT = pl.num_programs(2)
    tt = jnp.where(d == 0, t, nT - 1 - t)

    @pl.when(t == 0)
    def _():
        state_ref[...] = jnp.zeros_like(state_ref)

    def chunk(ci, carry):
        c = jnp.where(d == 0, ci, nc - 1 - ci)
        row = pl.multiple_of(c * GDN_CHUNK, GDN_CHUNK)
        gc = tt * nc + c
        for h in range(N_HEADS):
            cols = slice(h * HEAD_W, (h + 1) * HEAD_W)
            s = state_ref[h]
            sb = s.astype(BF16)
            w_c = w_ref[0, 0, pl.ds(row, GDN_CHUNK), cols]
            u_c = u_ref[0, 0, pl.ds(row, GDN_CHUNK), cols]
            qd_c = qd_ref[0, 0, pl.ds(row, GDN_CHUNK), cols]
            v_new = u_c.astype(F32) - _dot(w_c, sb)
            vb = v_new.astype(BF16)
            o = _dot(qd_c, sb) + _dot(at_ref[0, 0, h, c], vb)
            cdv = cd_ref[((b * 2 + d) * N_HEADS + h) * nC + gc]
            state_ref[h] = s * cdv + _dot(kdt_ref[0, 0, h, c], vb)
            o_ref[0, 0, pl.ds(row, GDN_CHUNK), cols] = o.astype(o_ref.dtype)
        return carry

    lax.fori_loop(0, nc, chunk, 0)


def _gdn_scan(cd, u, w, qd, att, kdt):
    B, _, S, _ = u.shape
    TC = min(512, S)
    nT = S // TC
    nc = TC // GDN_CHUNK
    nC = S // GDN_CHUNK

    def tmap(b, d, t):
        return jnp.where(d == 0, t, nT - 1 - t)

    big = pl.BlockSpec((1, 1, TC, D_MODEL), lambda b, d, t: (b, d, tmap(b, d, t), 0))
    return pl.pallas_call(
        functools.partial(_scan_kernel, nc=nc, nC=nC),
        grid=(B, 2, nT),
        in_specs=[pl.BlockSpec(memory_space=pltpu.SMEM),
                  big, big, big,
                  pl.BlockSpec((1, 1, N_HEADS, nc, GDN_CHUNK, GDN_CHUNK), lambda b, d, t: (b, d, 0, tmap(b, d, t), 0, 0)),
                  pl.BlockSpec((1, 1, N_HEADS, nc, HEAD_W, GDN_CHUNK), lambda b, d, t: (b, d, 0, tmap(b, d, t), 0, 0))],
        out_specs=big,
        out_shape=jax.ShapeDtypeStruct((B, 2, S, D_MODEL), BF16),
        scratch_shapes=[pltpu.VMEM((N_HEADS, HEAD_W, HEAD_W), F32)],
        compiler_params=_params(("parallel", "parallel", "arbitrary")),
        name="gdn_scan",
    )(cd, u, w, qd, att, kdt)


ATT_TQ = 256
ATT_TK = 256
ATT_NQ = 4
LOG2E = 1.4426950408889634


def _rope_kernel(q_ref, k_ref, v_ref, cos_ref, sin_ref, qt_ref, kr_ref, vt_ref, *, TR):
    cs = cos_ref[...]
    sn = sin_ref[...]
    lane = lax.broadcasted_iota(jnp.int32, cs.shape, 1)
    first_half = (lane & (DIFF_DH - 1)) < (DIFF_DH // 2)
    qscale = DIFF_DH ** -0.5 * LOG2E

    def rot(x):
        partner = jnp.where(first_half, pltpu.roll(x, HEAD_W - DIFF_DH // 2, 1), pltpu.roll(x, DIFF_DH // 2, 1))
        return x * cs + partner * sn

    for h in range(N_HEADS):
        cols = slice(h * HEAD_W, (h + 1) * HEAD_W)
        qr = rot(q_ref[0, :, cols].astype(F32)) * qscale
        kr_ref[0, :, cols] = rot(k_ref[0, :, cols].astype(F32)).astype(kr_ref.dtype)
        vf = v_ref[0, :, cols].astype(F32)
        for ci in range(TR // ATT_TK):
            rows = slice(ci * ATT_TK, (ci + 1) * ATT_TK)
            vt_ref[0, h, ci] = vf[rows].T.astype(vt_ref.dtype)
        for ci in range(TR // ATT_TQ):
            rows = slice(ci * ATT_TQ, (ci + 1) * ATT_TQ)
            qt_ref[0, h, ci] = qr[rows].T.astype(qt_ref.dtype)


def _rope(proj3, cos_t, sin_t):
    B, S, _ = proj3.shape
    TR = min(512, S)
    tab = pl.BlockSpec((TR, HEAD_W), lambda b, i: (i, 0))
    col = lambda cb: pl.BlockSpec((1, TR, D_MODEL), lambda b, i: (b, i, cb))
    return pl.pallas_call(
        functools.partial(_rope_kernel, TR=TR),
        grid=(B, S // TR),
        in_specs=[col(COL_QB), col(COL_KB), col(COL_VB), tab, tab],
        out_specs=[pl.BlockSpec((1, N_HEADS, TR // ATT_TQ, HEAD_W, ATT_TQ), lambda b, i: (b, 0, i, 0, 0)),
                   pl.BlockSpec((1, TR, D_MODEL), lambda b, i: (b, i, 0)),
                   pl.BlockSpec((1, N_HEADS, TR // ATT_TK, HEAD_W, ATT_TK), lambda b, i: (b, 0, i, 0, 0))],
        out_shape=[jax.ShapeDtypeStruct((B, N_HEADS, S // ATT_TQ, HEAD_W, ATT_TQ), BF16),
                   jax.ShapeDtypeStruct((B, S, D_MODEL), BF16),
                   jax.ShapeDtypeStruct((B, N_HEADS, S // ATT_TK, HEAD_W, ATT_TK), BF16)],
        compiler_params=_params(("parallel", "parallel")),
        name="rope",
    )(proj3, proj3, proj3, cos_t, sin_t)


def _attn_kernel(qt_ref, k_ref, vt_ref, la_ref, lb_ref, g_ref, o_ref, s_ref, acc_ref, *, n_chunks):
    row = lax.broadcasted_iota(jnp.int32, (HEAD_W, ATT_TQ), 0)
    qw = []
    for qb in range(ATT_NQ):
        qt = qt_ref[0, 0, qb]
        zero = jnp.zeros_like(qt)
        qw.append((jnp.where(row < DIFF_DH, qt, zero), jnp.where(row >= DIFF_DH, qt, zero)))
    chains = [(qb, comp) for qb in range(ATT_NQ) for comp in range(2)]

    def scores(j):
        kc = k_ref[0, pl.ds(pl.multiple_of(j * ATT_TK, ATT_TK), ATT_TK), :]
        return [_dot(kc, qw[qb][comp]) for qb, comp in chains]

    acc_ref[...] = jnp.zeros_like(acc_ref)
    for (qb, comp), s0 in zip(chains, scores(0)):
        s_ref[qb, comp] = s0

    def chunk(j, carry):
        s_next = scores(jnp.minimum(j + 1, n_chunks - 1))
        vt = vt_ref[0, 0, j]
        out = []
        for ci, (qb, comp) in enumerate(chains):
            m_prev, l_prev = carry[2 * ci], carry[2 * ci + 1]
            s = s_ref[qb, comp]
            m_new = jnp.maximum(m_prev, jnp.max(s, axis=0, keepdims=True))
            alpha = jnp.exp2(m_prev - m_new)
            p = jnp.exp2(s - m_new)
            out.append(m_new)
            out.append(alpha * l_prev + jnp.sum(p, axis=0, keepdims=True))
            acc_ref[qb, comp] = alpha * acc_ref[qb, comp] + _dot(vt, p.astype(BF16))
        for (qb, comp), sn in zip(chains, s_next):
            s_ref[qb, comp] = sn
        return tuple(out)

    neg = jnp.full((1, ATT_TQ), NEG_INF, F32)
    zer = jnp.zeros((1, ATT_TQ), F32)
    stats = lax.fori_loop(0, n_chunks, chunk, (neg, zer) * len(chains))

    sums = jnp.sum(la_ref[...] * lb_ref[...], axis=1, keepdims=True)
    lrow = lax.broadcasted_iota(jnp.int32, sums.shape, 0)
    sign = jnp.where(lrow == 0, 1.0, jnp.where(lrow == 1, -1.0, 0.0))
    lam = jnp.sum(sign * jnp.exp(sums), axis=0, keepdims=True) + LAM_INIT
    for qb in range(ATT_NQ):
        l0, l1 = stats[4 * qb + 1], stats[4 * qb + 3]
        ot = acc_ref[qb, 0] / l0 - lam * (acc_ref[qb, 1] / l1)
        ms = jnp.mean(ot * ot, axis=0, keepdims=True)
        y = (ot * lax.rsqrt(ms + EPS)).T * g_ref[...] * (1.0 - LAM_INIT)
        o_ref[0, qb * ATT_TQ:(qb + 1) * ATT_TQ, :] = y.astype(o_ref.dtype)


def _diff_attn(qt, kr, vt, lam_a, lam_b, norm_g):
    B, S, _ = kr.shape
    lam_spec = pl.BlockSpec((8, LANES), lambda b, h, qi: (0, 0))
    return pl.pallas_call(
        functools.partial(_attn_kernel, n_chunks=S // ATT_TK),
        grid=(B, N_HEADS, S // (ATT_NQ * ATT_TQ)),
        in_specs=[pl.BlockSpec((1, 1, ATT_NQ, HEAD_W, ATT_TQ), lambda b, h, qi: (b, h, qi, 0, 0)),
                  pl.BlockSpec((1, S, HEAD_W), lambda b, h, qi: (b, 0, h)),
                  pl.BlockSpec((1, 1, S // ATT_TK, HEAD_W, ATT_TK), lambda b, h, qi: (b, h, 0, 0, 0)),
                  lam_spec, lam_spec,
                  pl.BlockSpec((1, HEAD_W), lambda b, h, qi: (0, 0))],
        out_specs=pl.BlockSpec((1, ATT_NQ * ATT_TQ, HEAD_W), lambda b, h, qi: (b, qi, h)),
        out_shape=jax.ShapeDtypeStruct((B, S, D_MODEL), BF16),
        scratch_shapes=[pltpu.VMEM((ATT_NQ, 2, ATT_TK, ATT_TQ), F32), pltpu.VMEM((ATT_NQ, 2, HEAD_W, ATT_TQ), F32)],
        compiler_params=_params(("parallel", "parallel", "parallel")),
        name="diff_attn",
    )(qt, kr, vt, lam_a, lam_b, norm_g)


def _merge_kernel(of_ref, ob_ref, z_ref, oB_ref, ga_ref, gb_ref, bga_ref, bgb_ref, x_ref, g1_ref, gn_ref,
                  wa_ref, wb_ref, wo_ref, o_ref, ya_ref):
    oa = of_ref[0, 0].astype(F32) + ob_ref[0, 0].astype(F32)
    z = z_ref[...].astype(F32)
    gate = z * _sigmoid(z)
    for h in range(N_HEADS):
        cols = slice(h * HEAD_W, (h + 1) * HEAD_W)
        oh = oa[:, cols]
        ms = jnp.mean(oh * oh, axis=-1, keepdims=True)
        ya_ref[:, cols] = (oh * lax.rsqrt(ms + EPS) * gn_ref[...] * gate[:, cols]).astype(BF16)
    y_a = _dot(ya_ref[...], wa_ref[...])
    y_b = _dot(oB_ref[...], wb_ref[...])
    gate_a = _sigmoid(ga_ref[...].astype(F32) + bga_ref[...])
    gate_b = _sigmoid(gb_ref[...].astype(F32) + bgb_ref[...])
    mix = _dot((gate_a * y_a + gate_b * y_b).astype(BF16), wo_ref[...])
    o_ref[...] = x_ref[...] + g1_ref[0] * mix


def _merge(oA, proj, oB, b_gate, x2, gate1, gn, wa, wb, wo, S):
    N = x2.shape[0]
    TM = min(512, S)
    tpb = S // TM
    row = lambda i: (i, 0)
    full = pl.BlockSpec((D_MODEL, D_MODEL), lambda i: (0, 0))
    return pl.pallas_call(
        _merge_kernel,
        grid=(N // TM,),
        in_specs=[pl.BlockSpec((1, 1, TM, D_MODEL), lambda i: (i // tpb, 0, i % tpb, 0)),
                  pl.BlockSpec((1, 1, TM, D_MODEL), lambda i: (i // tpb, 1, i % tpb, 0)),
                  pl.BlockSpec((TM, D_MODEL), lambda i: (i, COL_ZA)),
                  pl.BlockSpec((TM, D_MODEL), row),
                  pl.BlockSpec((TM, D_MODEL), lambda i: (i, COL_GA)),
                  pl.BlockSpec((TM, D_MODEL), lambda i: (i, COL_GB)),
                  pl.BlockSpec((1, D_MODEL), lambda i: (0, 0)),
                  pl.BlockSpec((1, D_MODEL), lambda i: (0, 1)),
                  pl.BlockSpec((TM, D_MODEL), row),
                  pl.BlockSpec((1, 1, D_MODEL), lambda i: (i // tpb, 0, 0)),
                  pl.BlockSpec((1, HEAD_W), lambda i: (0, 0)),
                  full, full, full],
        out_specs=pl.BlockSpec((TM, D_MODEL), row),
        out_shape=jax.ShapeDtypeStruct((N, D_MODEL), F32),
        scratch_shapes=[pltpu.VMEM((TM, D_MODEL), BF16)],
        compiler_params=_params(("parallel",)),
        name="merge",
    )(oA, oA, proj, oB, proj, proj, b_gate, b_gate, x2, gate1, gn, wa, wb, wo)


def _router_kernel(x_ref, sh_ref, sc_ref, g_ref, rw0_ref, rw1_ref, rb_ref, tri_ref,
                   h_ref, idx_ref, gate_ref, rank_ref, cnt_ref, base_ref):
    i = pl.program_id(0)

    @pl.when(i == 0)
    def _():
        base_ref[...] = jnp.zeros_like(base_ref)

    x = x_ref[...]
    ms = jnp.mean(x * x, axis=-1, keepdims=True)
    h = x * lax.rsqrt(ms + EPS) * g_ref[...] * (1.0 + sc_ref[0]) + sh_ref[0]
    h_ref[...] = h
    h0 = h.astype(BF16)
    h1 = (h - h0.astype(F32)).astype(BF16)
    logits = _dot(h0, rw0_ref[...]) + (_dot(h0, rw1_ref[...]) + _dot(h1, rw0_ref[...])) + rb_ref[...]
    lane = lax.broadcasted_iota(jnp.int32, logits.shape, 1)
    lane_f = lane.astype(F32)
    cur = jnp.where(lane < N_EXPERTS, logits, NEG_INF)
    vals, sel = [], []
    for _ in range(TOP_K):
        m = jnp.max(cur, axis=1, keepdims=True)
        ix = jnp.min(jnp.where(cur == m, lane_f, float(LANES)), axis=1, keepdims=True)
        hit = lane_f == ix
        vals.append(m)
        sel.append(hit)
        cur = jnp.where(hit, NEG_INF, cur)
    exps = [jnp.exp(v - vals[0]) for v in vals]
    den = exps[0] + exps[1] + exps[2] + exps[3]
    onehot = jnp.zeros(logits.shape, F32)
    for hit in sel:
        onehot = onehot + jnp.where(hit, 1.0, 0.0)
    before = _dot(tri_ref[...], onehot.astype(BF16)) + base_ref[...]
    idx_out = jnp.zeros(logits.shape, F32)
    gate_out = jnp.zeros(logits.shape, F32)
    rank_out = jnp.zeros(logits.shape, F32)
    for kk in range(TOP_K):
        slot = lane == kk
        e_id = jnp.sum(jnp.where(sel[kk], lane_f, 0.0), axis=1, keepdims=True)
        rk = jnp.sum(jnp.where(sel[kk], before, 0.0), axis=1, keepdims=True)
        idx_out = jnp.where(slot, e_id, idx_out)
        gate_out = jnp.where(slot, exps[kk] / den, gate_out)
        rank_out = jnp.where(slot, rk, rank_out)
    idx_ref[...] = idx_out.astype(jnp.int32)
    gate_ref[...] = gate_out
    rank_ref[...] = rank_out.astype(jnp.int32)
    base_ref[...] = base_ref[...] + jnp.sum(onehot, axis=0, keepdims=True)
    cnt_ref[...] = base_ref[...]


def _router(x1, shift, scale, g, rw0, rw1, rb, S):
    N = x1.shape[0]
    TM = min(512, S)
    tpb = S // TM
    r = jnp.arange(TM)
    tri = (r[None, :] < r[:, None]).astype(BF16)
    row = lambda i: (i, 0)
    const = lambda i: (0, 0)
    lanes = pl.BlockSpec((TM, LANES), row)
    return pl.pallas_call(
        _router_kernel,
        grid=(N // TM,),
        in_specs=[pl.BlockSpec((TM, D_MODEL), row),
                  pl.BlockSpec((1, 1, D_MODEL), lambda i: (i // tpb, 0, 0)),
                  pl.BlockSpec((1, 1, D_MODEL), lambda i: (i // tpb, 0, 0)),
                  pl.BlockSpec((1, D_MODEL), const),
                  pl.BlockSpec((D_MODEL, LANES), const),
                  pl.BlockSpec((D_MODEL, LANES), const),
                  pl.BlockSpec((1, LANES), const),
                  pl.BlockSpec((TM, TM), const)],
        out_specs=[pl.BlockSpec((TM, D_MODEL), row), lanes, lanes, lanes, pl.BlockSpec((1, LANES), const)],
        out_shape=[jax.ShapeDtypeStruct((N, D_MODEL), F32),
                   jax.ShapeDtypeStruct((N, LANES), jnp.int32),
                   jax.ShapeDtypeStruct((N, LANES), F32),
                   jax.ShapeDtypeStruct((N, LANES), jnp.int32),
                   jax.ShapeDtypeStruct((1, LANES), F32)],
        scratch_shapes=[pltpu.VMEM((1, LANES), F32)],
        compiler_params=_params(("arbitrary",)),
        name="router",
    )(x1, shift, scale, g, rw0, rw1, rb, tri)


INDEX_SLICE = ROW_MOVE_TILE * TOP_K


def _row_copy_out(h_ref, xs_hbm, sem, r, dst):
    return pltpu.make_async_copy(h_ref.at[pl.ds(r, 1)], xs_hbm.at[pl.ds(dst, 1)], sem)


def _dispatch_kernel(dest_hbm, h_ref, xs_in_hbm, xs_hbm, idx_smem, sem_idx, sem_rows):
    del xs_in_hbm
    i = pl.program_id(0)
    fetch = pltpu.make_async_copy(dest_hbm.at[pl.ds(i * INDEX_SLICE, INDEX_SLICE)], idx_smem, sem_idx)
    fetch.start()
    fetch.wait()

    def start(r, carry):
        for kk in range(TOP_K):
            _row_copy_out(h_ref, xs_hbm, sem_rows, r, idx_smem[r * TOP_K + kk]).start()
        return carry

    lax.fori_loop(0, ROW_MOVE_TILE, start, 0)

    def drain(r, carry):
        for kk in range(TOP_K):
            _row_copy_out(h_ref, xs_hbm, sem_rows, 0, 0).wait()
        return carry

    lax.fori_loop(0, ROW_MOVE_TILE, drain, 0)


def _dispatch(dest_flat, h2, n_rows):
    N = h2.shape[0]
    xs0 = jnp.zeros((n_rows, D_MODEL), F32)
    return pl.pallas_call(
        _dispatch_kernel,
        grid=(N // ROW_MOVE_TILE,),
        in_specs=[pl.BlockSpec(memory_space=pl.ANY),
                  pl.BlockSpec((ROW_MOVE_TILE, D_MODEL), lambda i: (i, 0)),
                  pl.BlockSpec(memory_space=pl.ANY)],
        out_specs=pl.BlockSpec(memory_space=pl.ANY),
        out_shape=jax.ShapeDtypeStruct((n_rows, D_MODEL), F32),
        scratch_shapes=[pltpu.SMEM((INDEX_SLICE,), jnp.int32), pltpu.SemaphoreType.DMA, pltpu.SemaphoreType.DMA],
        input_output_aliases={2: 0},
        compiler_params=_params(("arbitrary",)),
        name="moe_dispatch",
    )(dest_flat, h2, xs0)


def _expert_kernel(te_ref, xs_ref, wg_ref, bg_ref, wl_ref, bl_ref, wd_ref, bd_ref, ys_ref):
    del te_ref
    xb = xs_ref[...].astype(BF16)
    glu = jnp.minimum(_dot(xb, wg_ref[0]) + bg_ref[0], SWIGLU_LIMIT)
    lin = jnp.clip(_dot(xb, wl_ref[0]) + bl_ref[0], -SWIGLU_LIMIT, SWIGLU_LIMIT)
    act = glu * _sigmoid(SWIGLU_ALPHA * glu) * (lin + 1.0)
    ys_ref[...] = _dot(act.astype(BF16), wd_ref[0]) + bd_ref[0]


def _experts(tile_expert, xs, wg, bg, wl, bl, wd, bd):
    n_rows = xs.shape[0]
    n_tiles = n_rows // MOE_TILE
    wspec = pl.BlockSpec((1, D_MODEL, D_MODEL), lambda i, te: (te[i], 0, 0))
    bspec = pl.BlockSpec((1, 1, D_MODEL), lambda i, te: (te[i], 0, 0))
    rows = pl.BlockSpec((MOE_TILE, D_MODEL), lambda i, te: (i, 0))
    return pl.pallas_call(
        _expert_kernel,
        grid_spec=pltpu.PrefetchScalarGridSpec(
            num_scalar_prefetch=1,
            grid=(n_tiles,),
            in_specs=[rows, wspec, bspec, wspec, bspec, wspec, bspec],
            out_specs=rows),
        out_shape=jax.ShapeDtypeStruct((n_rows, D_MODEL), F32),
        compiler_params=_params(("arbitrary",)),
        name="moe_experts",
    )(tile_expert, xs, wg, bg, wl, bl, wd, bd)


def _row_copy_in(ys_hbm, buf_ref, sem, src, kk, r):
    return pltpu.make_async_copy(ys_hbm.at[pl.ds(src, 1)], buf_ref.at[kk, pl.ds(r, 1)], sem)


def _combine_kernel(dest_hbm, ys_hbm, gate_ref, x_ref, g2_ref, fg_ref, o_ref, idx_smem, buf_ref, sem_idx, sem_rows):
    i = pl.program_id(0)
    fetch = pltpu.make_async_copy(dest_hbm.at[pl.ds(i * INDEX_SLICE, INDEX_SLICE)], idx_smem, sem_idx)
    fetch.start()
    fetch.wait()

    def start(r, carry):
        for kk in range(TOP_K):
            _row_copy_in(ys_hbm, buf_ref, sem_rows, idx_smem[r * TOP_K + kk], kk, r).start()
        return carry

    lax.fori_loop(0, ROW_MOVE_TILE, start, 0)

    def drain(r, carry):
        for kk in range(TOP_K):
            _row_copy_in(ys_hbm, buf_ref, sem_rows, 0, 0, 0).wait()
        return carry

    lax.fori_loop(0, ROW_MOVE_TILE, drain, 0)

    gates = gate_ref[...]
    moe = gates[:, 0:1] * buf_ref[0]
    for kk in range(1, TOP_K):
        moe = moe + gates[:, kk:kk + 1] * buf_ref[kk]
    x = x_ref[...] + g2_ref[0] * moe
    ms = jnp.mean(x * x, axis=-1, keepdims=True)
    o_ref[...] = x * lax.rsqrt(ms + EPS) * fg_ref[...]


def _combine(dest_flat, ys, gates, x1, gate2, final_g, S):
    N = x1.shape[0]
    TM = ROW_MOVE_TILE
    tpb = S // TM
    row = lambda i: (i, 0)
    return pl.pallas_call(
        _combine_kernel,
        grid=(N // TM,),
        in_specs=[pl.BlockSpec(memory_space=pl.ANY),
                  pl.BlockSpec(memory_space=pl.ANY),
                  pl.BlockSpec((TM, LANES), row),
                  pl.BlockSpec((TM, D_MODEL), row),
                  pl.BlockSpec((1, 1, D_MODEL), lambda i: (i // tpb, 0, 0)),
                  pl.BlockSpec((1, D_MODEL), lambda i: (0, 0))],
        out_specs=pl.BlockSpec((TM, D_MODEL), row),
        out_shape=jax.ShapeDtypeStruct((N, D_MODEL), F32),
        scratch_shapes=[pltpu.SMEM((INDEX_SLICE,), jnp.int32), pltpu.VMEM((TOP_K, TM, D_MODEL), F32),
                        pltpu.SemaphoreType.DMA, pltpu.SemaphoreType.DMA],
        compiler_params=_params(("arbitrary",)),
        name="moe_combine",
    )(dest_flat, ys, gates, x1, gate2, final_g)


def _pad_lanes(a, offset=0):
    return jnp.pad(a, ((0, 0), (offset, LANES - offset - a.shape[1])))


def kernel(x, c, ada_w, ada_b, norm1_g, norm2_g, w_in, b_gate, conv_w, a_log, dt_bias, gdn_norm_g, w_branch_a,
           diff_lambda, diff_norm_g, w_branch_b, w_out, router_w, router_b, w_glu, b_glu, w_lin, b_lin, w_down,
           b_down, final_g):
    B, S, D = x.shape
    N = B * S
    assert D == D_MODEL and S % GROUP == 0 and ada_w.shape[0] == 1
    x2 = x.reshape(N, D)

    mod = _adaln(c, ada_w[0], ada_b[0])
    shift1, scale1, gate1, shift2, scale2, gate2 = [m.reshape(B, 1, D) for m in jnp.split(mod, 6, axis=-1)]

    wi = w_in[0]
    n_a = 4 * D
    n_small = 4 * N_HEADS
    w_big = jnp.concatenate([wi[:, :n_a], wi[:, n_a + n_small:]], axis=1).astype(BF16)
    w_small = _pad_lanes(wi[:, n_a:n_a + n_small]).astype(BF16)
    proj, small = _inproj(x2, shift1, scale1, norm1_g, w_big, w_small, S)
    proj3 = proj.reshape(B, S, N_COL_BLOCKS * D)

    qkv = _gdn_conv(proj3, conv_w[0])
    alog_row = _pad_lanes(a_log[0].reshape(1, -1), GATE_LANE0)
    dt_row = _pad_lanes(dt_bias[0].reshape(1, -1), GATE_LANE0)
    beta, G, eG, eGl, cdl = _gdn_gates(small, alog_row, dt_row)
    r3 = lambda a: a.reshape(B, S, LANES)
    nC = S // GDN_CHUNK
    GT = jnp.transpose(r3(G)[:, :, GATE_LANE0:GATE_LANE0 + 2 * N_HEADS], (0, 2, 1)).reshape(B, 2 * N_HEADS, 1, S)
    cd = r3(cdl).reshape(B, nC, GDN_CHUNK, LANES)[:, :, 0, GATE_LANE0:GATE_LANE0 + 2 * N_HEADS]
    cd = jnp.transpose(cd.reshape(B, nC, 2, N_HEADS), (0, 2, 3, 1)).reshape(-1)
    u, w, qd, att, kdt = _gdn_prep(qkv, r3(beta), r3(G), r3(eG), r3(eGl), GT)
    oA = _gdn_scan(cd, u, w, qd, att, kdt)

    half = DIFF_DH // 2
    inv_freq = ROPE_THETA ** (-jnp.arange(half, dtype=F32) / half)
    ang = jnp.arange(S, dtype=F32)[:, None] * inv_freq[None, :]
    cos_t = jnp.tile(jnp.cos(ang), (1, 4))
    sin_h = jnp.sin(ang)
    sin_t = jnp.tile(jnp.concatenate([-sin_h, sin_h], axis=1), (1, 2))
    qt, kr, vt = _rope(proj3, cos_t, sin_t)
    lam_a = jnp.pad(_pad_lanes(diff_lambda[0][0::2]), ((0, 6), (0, 0)))
    lam_b = jnp.pad(_pad_lanes(diff_lambda[0][1::2]), ((0, 6), (0, 0)))
    oB = _diff_attn(qt, kr, vt, lam_a, lam_b, diff_norm_g)

    x1 = _merge(oA, proj, oB.reshape(N, D), b_gate, x2, gate1, gdn_norm_g,
                w_branch_a[0].astype(BF16), w_branch_b[0].astype(BF16), w_out[0].astype(BF16), S)

    rw = _pad_lanes(router_w[0])
    rw0 = rw.astype(BF16)
    rw1 = (rw - rw0.astype(F32)).astype(BF16)
    h2, idx, gates, rank, counts = _router(x1, shift2, scale2, norm2_g, rw0, rw1, _pad_lanes(router_b), S)
    cnt = counts[0, :N_EXPERTS].astype(jnp.int32)
    padded = (cnt + MOE_TILE - 1) // MOE_TILE * MOE_TILE
    pad_ends = jnp.cumsum(padded)
    pad_starts = pad_ends - padded
    dest = (pad_starts[idx[:, :TOP_K]] + rank[:, :TOP_K]).reshape(-1)
    n_tiles = -(-(N * TOP_K) // MOE_TILE) + N_EXPERTS
    tile_start = jnp.arange(n_tiles, dtype=jnp.int32) * MOE_TILE
    tile_expert = jnp.minimum(jnp.searchsorted(pad_ends, tile_start, side="right"), N_EXPERTS - 1).astype(jnp.int32)
    xs = _dispatch(dest, h2, n_tiles * MOE_TILE)
    ys = _experts(tile_expert, xs, w_glu[0].astype(BF16), b_glu[0][:, None, :], w_lin[0].astype(BF16),
                  b_lin[0][:, None, :], w_down[0].astype(BF16), b_down[0][:, None, :])
    out = _combine(dest, ys, gates, x1, gate2, final_g.reshape(1, D), S)
    return out.reshape(B, S, D)
```

```python
import functools
import math

import jax
import jax.numpy as jnp
from jax import lax
from jax.experimental import pallas as pl
from jax.experimental.pallas import tpu as pltpu

F32 = jnp.float32
BF16 = jnp.bfloat16

D_MODEL = 1024
EPS = 1e-6
N_HEADS = 8
HEAD_W = 128
GDN_CHUNK = 64
CONV_WIDTH = 5
DIFF_DH = 64
ROPE_THETA = 10000.0
LAM_INIT = 0.8 - 0.6 * math.exp(-0.3 * 0)
N_EXPERTS = 32
TOP_K = 4
SWIGLU_ALPHA = 1.702
SWIGLU_LIMIT = 7.0

LANES = 128
GROUP = 256
CHUNKS_PER_GROUP = GROUP // GDN_CHUNK
CHUNK_SHIFT = GDN_CHUNK.bit_length() - 1
MOE_TILE = 512
ROW_MOVE_TILE = 256
NEG_INF = float("-inf")

COL_QA, COL_KA, COL_VA, COL_ZA, COL_QB, COL_KB, COL_VB, COL_GA, COL_GB = range(9)
N_COL_BLOCKS = 9


def _params(sem, vmem_mb=48):
    return pltpu.CompilerParams(dimension_semantics=sem, vmem_limit_bytes=vmem_mb * 1024 * 1024)


def _dot(a, b):
    return jnp.dot(a, b, preferred_element_type=F32)


def _dot_nt(a, b):
    return lax.dot_general(a, b, (((1,), (1,)), ((), ())), preferred_element_type=F32)


def _sigmoid(x):
    return 1.0 / (1.0 + jnp.exp(-x))


def _split3(x):
    a = x.astype(BF16)
    r = x - a.astype(F32)
    b = r.astype(BF16)
    c = (r - b.astype(F32)).astype(BF16)
    return a, b, c


def _adaln_kernel(c_ref, w_ref, b_ref, o_ref):
    c = c_ref[...]
    cond = c * _sigmoid(c)
    c0, c1, c2 = _split3(cond)
    w0, w1, w2 = _split3(w_ref[...])
    acc = _dot(c0, w0) + (_dot(c0, w1) + _dot(c1, w0)) + (_dot(c0, w2) + _dot(c1, w1) + _dot(c2, w0))
    o_ref[...] = acc + b_ref[...]


def _adaln(c, ada_w, ada_b):
    B = c.shape[0]
    n = ada_w.shape[1] // D_MODEL
    return pl.pallas_call(
        _adaln_kernel,
        grid=(n,),
        in_specs=[pl.BlockSpec((B, D_MODEL), lambda j: (0, 0)),
                  pl.BlockSpec((D_MODEL, D_MODEL), lambda j: (0, j)),
                  pl.BlockSpec((1, D_MODEL), lambda j: (0, j))],
        out_specs=pl.BlockSpec((B, D_MODEL), lambda j: (0, j)),
        out_shape=jax.ShapeDtypeStruct((B, n * D_MODEL), F32),
        compiler_params=_params(("parallel",)),
        name="adaln",
    )(c, ada_w, ada_b.reshape(1, -1))


def _inproj_kernel(x_ref, sh_ref, sc_ref, g_ref, w_ref, ws_ref, o_ref, os_ref, h_ref):
    @pl.when(pl.program_id(1) == 0)
    def _():
        x = x_ref[...]
        ms = jnp.mean(x * x, axis=-1, keepdims=True)
        y = x * lax.rsqrt(ms + EPS) * g_ref[...]
        h = (y * (1.0 + sc_ref[0]) + sh_ref[0]).astype(BF16)
        h_ref[...] = h
        os_ref[...] = _dot(h, ws_ref[...])

    o_ref[...] = _dot(h_ref[...], w_ref[...]).astype(o_ref.dtype)


def _inproj(x2, shift, scale, g, w_big, w_small, S):
    N = x2.shape[0]
    TM = min(1024, S)
    tpb = S // TM
    return pl.pallas_call(
        _inproj_kernel,
        grid=(N // TM, N_COL_BLOCKS),
        in_specs=[pl.BlockSpec((TM, D_MODEL), lambda i, j: (i, 0)),
                  pl.BlockSpec((1, 1, D_MODEL), lambda i, j: (i // tpb, 0, 0)),
                  pl.BlockSpec((1, 1, D_MODEL), lambda i, j: (i // tpb, 0, 0)),
                  pl.BlockSpec((1, D_MODEL), lambda i, j: (0, 0)),
                  pl.BlockSpec((D_MODEL, D_MODEL), lambda i, j: (0, j)),
                  pl.BlockSpec((D_MODEL, LANES), lambda i, j: (0, 0))],
        out_specs=[pl.BlockSpec((TM, D_MODEL), lambda i, j: (i, j)),
                   pl.BlockSpec((TM, LANES), lambda i, j: (i, 0))],
        out_shape=[jax.ShapeDtypeStruct((N, N_COL_BLOCKS * D_MODEL), BF16),
                   jax.ShapeDtypeStruct((N, LANES), F32)],
        scratch_shapes=[pltpu.VMEM((TM, D_MODEL), BF16)],
        compiler_params=_params(("parallel", "arbitrary")),
        name="inproj",
    )(x2, shift, scale, g, w_big, w_small)


HALO = 16


def _conv_kernel(cur_ref, prev_ref, next_ref, w_ref, o_ref, ext_ref, *, TR):
    i = pl.program_id(1)
    g = pl.program_id(2)
    last = pl.num_programs(1) - 1
    ext_ref[8:8 + TR, :] = cur_ref[0].astype(F32)
    pv = prev_ref[0].astype(F32)[HALO - 8:HALO]
    nx = next_ref[0].astype(F32)[0:8]
    ext_ref[0:8, :] = jnp.where(i > 0, pv, 0.0)
    ext_ref[TR + 8:TR + 16, :] = jnp.where(i < last, nx, 0.0)
    pad = (CONV_WIDTH - 1) // 2
    acc = ext_ref[8 - pad:8 - pad + TR, :] * w_ref[0:1, :]
    for j in range(1, CONV_WIDTH):
        acc = acc + ext_ref[8 - pad + j:8 - pad + j + TR, :] * w_ref[j:j + 1, :]
    y = acc * _sigmoid(acc)
    ones = jnp.ones((HEAD_W, HEAD_W), BF16)
    qscale = jnp.where(g == 0, HEAD_W ** -0.5, 1.0)
    for h in range(N_HEADS):
        yh = y[:, h * HEAD_W:(h + 1) * HEAD_W]
        ss = _dot((yh * yh).astype(BF16), ones)
        normed = yh * (lax.rsqrt(ss + EPS) * qscale)
        o_ref[0, :, h * HEAD_W:(h + 1) * HEAD_W] = jnp.where(g < 2, normed, yh).astype(o_ref.dtype)


def _gdn_conv(proj3, conv_w):
    B, S, _ = proj3.shape
    TR = min(512, S)
    nT = S // TR
    rb = TR // HALO
    nH = S // HALO
    return pl.pallas_call(
        functools.partial(_conv_kernel, TR=TR),
        grid=(B, nT, 3),
        in_specs=[pl.BlockSpec((1, TR, D_MODEL), lambda b, i, g: (b, i, g)),
                  pl.BlockSpec((1, HALO, D_MODEL), lambda b, i, g: (b, jnp.maximum(i * rb - 1, 0), g)),
                  pl.BlockSpec((1, HALO, D_MODEL), lambda b, i, g: (b, jnp.minimum((i + 1) * rb, nH - 1), g)),
                  pl.BlockSpec((CONV_WIDTH, D_MODEL), lambda b, i, g: (0, g))],
        out_specs=pl.BlockSpec((1, TR, D_MODEL), lambda b, i, g: (b, i, g)),
        out_shape=jax.ShapeDtypeStruct((B, S, 3 * D_MODEL), BF16),
        scratch_shapes=[pltpu.VMEM((TR + 16, D_MODEL), F32)],
        compiler_params=_params(("parallel", "parallel", "parallel")),
        name="gdn_conv",
    )(proj3, proj3, proj3, conv_w)


GATE_LANE0 = 16


def _gates_kernel(x_ref, alog_ref, dt_ref, beta_ref, g_ref, gl_ref, cd_ref):
    x = x_ref[...]
    R = x.shape[0]
    lane = lax.broadcasted_iota(jnp.int32, x.shape, 1)
    beta_ref[...] = _sigmoid(x)
    z = x + dt_ref[...]
    softplus = jnp.maximum(z, 0.0) + jnp.log(1.0 + jnp.exp(-jnp.abs(z)))
    gd = -jnp.exp(alog_ref[...]) * softplus
    gd = jnp.where((lane >= GATE_LANE0) & (lane < GATE_LANE0 + 2 * N_HEADS), gd, 0.0)
    r = lax.broadcasted_iota(jnp.int32, (R, R), 0)
    c = lax.broadcasted_iota(jnp.int32, (R, R), 1)
    same = (r >> CHUNK_SHIFT) == (c >> CHUNK_SHIFT)
    lower = jnp.where(same & (c <= r), 1.0, 0.0).astype(BF16)
    upper = jnp.where(same & (c >= r), 1.0, 0.0).astype(BF16)
    block = jnp.where(same, 1.0, 0.0).astype(BF16)
    p0, p1, p2 = _split3(gd)
    g_fwd = _dot(lower, p0) + _dot(lower, p1) + _dot(lower, p2)
    g_bwd = _dot(upper, p0) + _dot(upper, p1) + _dot(upper, p2)
    tot = _dot(block, p0) + _dot(block, p1) + _dot(block, p2)
    G = jnp.where(lane < GATE_LANE0 + N_HEADS, g_fwd, g_bwd)
    g_ref[...] = G
    gl_ref[...] = tot - G
    cd_ref[...] = jnp.exp(tot)


def _gdn_gates(small, alog_row, dt_row):
    N = small.shape[0]
    spec = pl.BlockSpec((GROUP, LANES), lambda i: (i, 0))
    row = pl.BlockSpec((1, LANES), lambda i: (0, 0))
    return pl.pallas_call(
        _gates_kernel,
        grid=(N // GROUP,),
        in_specs=[spec, row, row],
        out_specs=[spec] * 4,
        out_shape=[jax.ShapeDtypeStruct((N, LANES), F32)] * 4,
        compiler_params=_params(("parallel",)),
        name="gdn_gates",
    )(small, alog_row, dt_row)


PREP_HEADS = 2
(MASK_STRICT_LO, MASK_STRICT_UP, MASK_INCL_LO, MASK_INCL_UP, MASK_EYE, MASK_BLK4, MASK_OFF0) = range(7)
N_MASKS = MASK_OFF0 + (CHUNK_SHIFT - 2)


def _prep_masks():
    r = jnp.arange(GROUP)[:, None]
    c = jnp.arange(GROUP)[None, :]
    same = (r >> CHUNK_SHIFT) == (c >> CHUNK_SHIFT)
    masks = [same & (c < r), same & (c > r), same & (c <= r), same & (c >= r), r == c, (r >> 2) == (c >> 2)]
    for shift in range(2, CHUNK_SHIFT):
        masks.append(((r >> shift) != (c >> shift)) & ((r >> (shift + 1)) == (c >> (shift + 1))))
    return jnp.stack(masks).astype(F32)


def _col(x, l, lane):
    return jnp.broadcast_to(jnp.sum(jnp.where(lane == l, x, 0.0), axis=1, keepdims=True), x.shape)


def _prep_kernel(q_ref, k_ref, v_ref, beta_ref, g_ref, gl_ref, gtf_ref, gtb_ref, mask_ref,
                 u_ref, w_ref, qd_ref, at_ref, kdt_ref):
    hp = pl.program_id(1)
    lane = lax.broadcasted_iota(jnp.int32, (GROUP, LANES), 1)
    wide = lambda a: jnp.concatenate([a, a], axis=1)
    chains = [(hh, d) for hh in range(PREP_HEADS) for d in range(2)]
    p, rhs = {}, {}
    for hh in range(PREP_HEADS):
        cols = slice(hh * HEAD_W, (hh + 1) * HEAD_W)
        q = q_ref[0, :, cols]
        k = k_ref[0, :, cols]
        qf = q.astype(F32)
        kf = k.astype(F32)
        vf = v_ref[0, :, cols].astype(F32)
        kk = _dot_nt(k, k)
        qk = _dot_nt(q, k)
        for d in range(2):
            lb = d * N_HEADS + hp * PREP_HEADS + hh
            beta_c = _col(beta_ref[0], lb, lane)
            g_c = _col(g_ref[0], GATE_LANE0 + lb, lane)
            eg_c = jnp.exp(g_c)
            egl_c = jnp.exp(_col(gl_ref[0], GATE_LANE0 + lb, lane))
            g_r = (gtf_ref if d == 0 else gtb_ref)[0, hh]
            dec = jnp.exp(jnp.minimum(wide(g_c) - g_r, 0.0))
            p[hh, d] = -(kk * wide(beta_c)) * dec * mask_ref[MASK_STRICT_LO + d]
            att = qk * dec * mask_ref[MASK_INCL_LO + d]
            rhs[hh, d] = jnp.concatenate([vf * beta_c, kf * (beta_c * eg_c)], axis=1).astype(BF16)
            qd_ref[0, d, :, cols] = (qf * eg_c).astype(qd_ref.dtype)
            kdt = (kf * egl_c).T
            for ci in range(CHUNKS_PER_GROUP):
                sl = slice(ci * GDN_CHUNK, (ci + 1) * GDN_CHUNK)
                at_ref[0, d, hh, ci] = att[sl, sl].astype(at_ref.dtype)
                kdt_ref[0, d, hh, ci] = kdt[:, sl].astype(kdt_ref.dtype)
    p4 = {ch: (p[ch] * mask_ref[MASK_BLK4]).astype(BF16) for ch in chains}
    sq = {ch: _dot(p4[ch], p4[ch]).astype(BF16) for ch in chains}
    t = {ch: mask_ref[MASK_EYE] + p4[ch].astype(F32) for ch in chains}
    t = {ch: t[ch] + _dot(t[ch].astype(BF16), sq[ch]) for ch in chains}
    for lvl in range(CHUNK_SHIFT - 2):
        tb = {ch: t[ch].astype(BF16) for ch in chains}
        x = {ch: _dot(tb[ch], (p[ch] * mask_ref[MASK_OFF0 + lvl]).astype(BF16)).astype(BF16) for ch in chains}
        t = {ch: t[ch] + _dot(x[ch], tb[ch]) for ch in chains}
    uw = {ch: _dot(t[ch].astype(BF16), rhs[ch]) for ch in chains}
    for hh, d in chains:
        cols = slice(hh * HEAD_W, (hh + 1) * HEAD_W)
        u_ref[0, d, :, cols] = uw[hh, d][:, :HEAD_W].astype(u_ref.dtype)
        w_ref[0, d, :, cols] = uw[hh, d][:, HEAD_W:].astype(w_ref.dtype)


def _gdn_prep(qkv, beta, G, Gl, GT):
    B, S, _ = qkv.shape
    nG = S // GROUP
    nC = S // GDN_CHUNK
    PW = PREP_HEADS * HEAD_W
    nP = N_HEADS // PREP_HEADS
    sm = pl.BlockSpec((1, GROUP, LANES), lambda b, h, g: (b, g, 0))
    big = pl.BlockSpec((1, 2, GROUP, PW), lambda b, h, g: (b, 0, g, h))
    return pl.pallas_call(
        _prep_kernel,
        grid=(B, nP, nG),
        in_specs=[pl.BlockSpec((1, GROUP, PW), lambda b, h, g: (b, g, h)),
                  pl.BlockSpec((1, GROUP, PW), lambda b, h, g: (b, g, nP + h)),
                  pl.BlockSpec((1, GROUP, PW), lambda b, h, g: (b, g, 2 * nP + h)),
                  sm, sm, sm,
                  pl.BlockSpec((1, PREP_HEADS, 1, GROUP), lambda b, h, g: (b, h, 0, g)),
                  pl.BlockSpec((1, PREP_HEADS, 1, GROUP), lambda b, h, g: (b, nP + h, 0, g)),
                  pl.BlockSpec((N_MASKS, GROUP, GROUP), lambda b, h, g: (0, 0, 0))],
        out_specs=[big, big, big,
                   pl.BlockSpec((1, 2, PREP_HEADS, CHUNKS_PER_GROUP, GDN_CHUNK, GDN_CHUNK),
                                lambda b, h, g: (b, 0, h, g, 0, 0)),
                   pl.BlockSpec((1, 2, PREP_HEADS, CHUNKS_PER_GROUP, HEAD_W, GDN_CHUNK),
                                lambda b, h, g: (b, 0, h, g, 0, 0))],
        out_shape=[jax.ShapeDtypeStruct((B, 2, S, D_MODEL), BF16)] * 3
                  + [jax.ShapeDtypeStruct((B, 2, N_HEADS, nC, GDN_CHUNK, GDN_CHUNK), BF16),
                     jax.ShapeDtypeStruct((B, 2, N_HEADS, nC, HEAD_W, GDN_CHUNK), BF16)],
        compiler_params=_params(("parallel", "parallel", "parallel")),
        name="gdn_prep",
    )(qkv, qkv, qkv, beta, G, Gl, GT, GT, _prep_masks())


def _scan_kernel(cd_ref, uf_ref, wf_ref, qdf_ref, atf_ref, kdtf_ref, ub_ref, wb_ref, qdb_ref, atb_ref, kdtb_ref,
                 of_ref, ob_ref, state_ref, *, nc, nC):
    b = pl.program_id(0)
    t = pl.program_id(1)
    nT = pl.num_programs(1)

    @pl.when(t == 0)
    def _():
        state_ref[...] = jnp.zeros_like(state_ref)

    dirs = ((uf_ref, wf_ref, qdf_ref, atf_ref, kdtf_ref, of_ref), (ub_ref, wb_ref, qdb_ref, atb_ref, kdtb_ref, ob_ref))

    def chunk(ci, carry):
        work = []
        for d, refs in enumerate(dirs):
            c = ci if d == 0 else nc - 1 - ci
            tt = t if d == 0 else nT - 1 - t
            row = pl.multiple_of(c * GDN_CHUNK, GDN_CHUNK)
            for h in range(N_HEADS):
                work.append((d, h, c, row, ((b * 2 + d) * N_HEADS + h) * nC + tt * nc + c, refs))
        s_old = [state_ref[d, h] for d, h, *_ in work]
        sb = [s.astype(BF16) for s in s_old]
        tile = lambda ref, row, h: ref[0, 0, pl.ds(row, GDN_CHUNK), h * HEAD_W:(h + 1) * HEAD_W]
        ws = [_dot(tile(refs[1], row, h), sb[i]) for i, (d, h, c, row, gi, refs) in enumerate(work)]
        qs = [_dot(tile(refs[2], row, h), sb[i]) for i, (d, h, c, row, gi, refs) in enumerate(work)]
        vb = [(tile(refs[0], row, h).astype(F32) - ws[i]).astype(BF16)
              for i, (d, h, c, row, gi, refs) in enumerate(work)]
        o = [qs[i] + _dot(refs[3][0, 0, h, c], vb[i]) for i, (d, h, c, row, gi, refs) in enumerate(work)]
        upd = [_dot(refs[4][0, 0, h, c], vb[i]) for i, (d, h, c, row, gi, refs) in enumerate(work)]
        for i, (d, h, c, row, gi, refs) in enumerate(work):
            state_ref[d, h] = s_old[i] * cd_ref[gi] + upd[i]
            refs[5][0, pl.ds(row, GDN_CHUNK), h * HEAD_W:(h + 1) * HEAD_W] = o[i].astype(of_ref.dtype)
        return carry

    lax.fori_loop(0, nc, chunk, 0)


def _gdn_scan(cd, u, w, qd, att, kdt):
    B, _, S, _ = u.shape
    TC = min(512, S)
    nT = S // TC
    nc = TC // GDN_CHUNK
    nC = S // GDN_CHUNK
    fwd = lambda b, t: t
    bwd = lambda b, t: nT - 1 - t

    def specs(d, tm):
        big = pl.BlockSpec((1, 1, TC, D_MODEL), lambda b, t: (b, d, tm(b, t), 0))
        return [big, big, big,
                pl.BlockSpec((1, 1, N_HEADS, nc, GDN_CHUNK, GDN_CHUNK), lambda b, t: (b, d, 0, tm(b, t), 0, 0)),
                pl.BlockSpec((1, 1, N_HEADS, nc, HEAD_W, GDN_CHUNK), lambda b, t: (b, d, 0, tm(b, t), 0, 0))]

    return pl.pallas_call(
        functools.partial(_scan_kernel, nc=nc, nC=nC),
        grid=(B, nT),
        in_specs=[pl.BlockSpec(memory_space=pltpu.SMEM)] + specs(0, fwd) + specs(1, bwd),
        out_specs=[pl.BlockSpec((1, TC, D_MODEL), lambda b, t: (b, t, 0)),
                   pl.BlockSpec((1, TC, D_MODEL), lambda b, t: (b, nT - 1 - t, 0))],
        out_shape=[jax.ShapeDtypeStruct((B, S, D_MODEL), BF16)] * 2,
        scratch_shapes=[pltpu.VMEM((2, N_HEADS, HEAD_W, HEAD_W), F32)],
        compiler_params=_params(("parallel", "arbitrary")),
        name="gdn_scan",
    )(cd, u, w, qd, att, kdt, u, w, qd, att, kdt)


ATT_TQ = 256
ATT_TK = 256
ATT_NQ = 4
ATT_VROWS = HEAD_W + 16
LOG2E = 1.4426950408889634


def _rope_kernel(q_ref, k_ref, v_ref, cos_ref, sin_ref, qt_ref, kr_ref, vt_ref, *, TR):
    cs = cos_ref[...]
    sn = sin_ref[...]
    lane = lax.broadcasted_iota(jnp.int32, cs.shape, 1)
    first_half = (lane & (DIFF_DH - 1)) < (DIFF_DH // 2)
    qscale = DIFF_DH ** -0.5 * LOG2E

    def rot(x):
        partner = jnp.where(first_half, pltpu.roll(x, HEAD_W - DIFF_DH // 2, 1), pltpu.roll(x, DIFF_DH // 2, 1))
        return x * cs + partner * sn

    for h in range(N_HEADS):
        cols = slice(h * HEAD_W, (h + 1) * HEAD_W)
        qr = rot(q_ref[0, :, cols].astype(F32)) * qscale
        kr_ref[0, :, cols] = rot(k_ref[0, :, cols].astype(F32)).astype(kr_ref.dtype)
        vf = v_ref[0, :, cols].astype(F32)
        for ci in range(TR // ATT_TK):
            rows = slice(ci * ATT_TK, (ci + 1) * ATT_TK)
            vt_ref[0, h, ci, 0:HEAD_W, :] = vf[rows].T.astype(vt_ref.dtype)
            vt_ref[0, h, ci, HEAD_W:ATT_VROWS, :] = jnp.ones((ATT_VROWS - HEAD_W, ATT_TK), vt_ref.dtype)
        for ci in range(TR // ATT_TQ):
            rows = slice(ci * ATT_TQ, (ci + 1) * ATT_TQ)
            qt_ref[0, h, ci] = qr[rows].T.astype(qt_ref.dtype)


def _rope(proj3, cos_t, sin_t):
    B, S, _ = proj3.shape
    TR = min(512, S)
    tab = pl.BlockSpec((TR, HEAD_W), lambda b, i: (i, 0))
    col = lambda cb: pl.BlockSpec((1, TR, D_MODEL), lambda b, i: (b, i, cb))
    return pl.pallas_call(
        functools.partial(_rope_kernel, TR=TR),
        grid=(B, S // TR),
        in_specs=[col(COL_QB), col(COL_KB), col(COL_VB), tab, tab],
        out_specs=[pl.BlockSpec((1, N_HEADS, TR // ATT_TQ, HEAD_W, ATT_TQ), lambda b, i: (b, 0, i, 0, 0)),
                   pl.BlockSpec((1, TR, D_MODEL), lambda b, i: (b, i, 0)),
                   pl.BlockSpec((1, N_HEADS, TR // ATT_TK, ATT_VROWS, ATT_TK), lambda b, i: (b, 0, i, 0, 0))],
        out_shape=[jax.ShapeDtypeStruct((B, N_HEADS, S // ATT_TQ, HEAD_W, ATT_TQ), BF16),
                   jax.ShapeDtypeStruct((B, S, D_MODEL), BF16),
                   jax.ShapeDtypeStruct((B, N_HEADS, S // ATT_TK, ATT_VROWS, ATT_TK), BF16)],
        compiler_params=_params(("parallel", "parallel")),
        name="rope",
    )(proj3, proj3, proj3, cos_t, sin_t)


def _attn_kernel(qt_ref, k_ref, vt_ref, la_ref, lb_ref, g_ref, o_ref, s_ref, acc_ref, *, n_chunks):
    row = lax.broadcasted_iota(jnp.int32, (HEAD_W, ATT_TQ), 0)
    qw = []
    for qb in range(ATT_NQ):
        qt = qt_ref[0, 0, qb]
        zero = jnp.zeros_like(qt)
        qw.append((jnp.where(row < DIFF_DH, qt, zero), jnp.where(row >= DIFF_DH, qt, zero)))
    chains = [(qb, comp) for qb in range(ATT_NQ) for comp in range(2)]

    def scores(j):
        kc = k_ref[0, pl.ds(pl.multiple_of(j * ATT_TK, ATT_TK), ATT_TK), :]
        return [_dot(kc, qw[qb][comp]) for qb, comp in chains]

    acc_ref[...] = jnp.zeros_like(acc_ref)
    for (qb, comp), s0 in zip(chains, scores(0)):
        s_ref[qb, comp] = s0

    def chunk(j, carry):
        s_next = scores(jnp.minimum(j + 1, n_chunks - 1))
        vt = vt_ref[0, 0, j]
        out = []
        for ci, (qb, comp) in enumerate(chains):
            m_prev = carry[ci]
            s = s_ref[qb, comp]
            m_new = jnp.maximum(m_prev, jnp.max(s, axis=0, keepdims=True))
            alpha = jnp.exp2(m_prev - m_new)
            p = jnp.exp2(s - m_new)
            out.append(m_new)
            acc_ref[qb, comp] = alpha * acc_ref[qb, comp] + _dot(vt, p.astype(BF16))
        for (qb, comp), sn in zip(chains, s_next):
            s_ref[qb, comp] = sn
        return tuple(out)

    neg = jnp.full((1, ATT_TQ), NEG_INF, F32)
    lax.fori_loop(0, n_chunks, chunk, (neg,) * len(chains))

    sums = jnp.sum(la_ref[...] * lb_ref[...], axis=1, keepdims=True)
    lrow = lax.broadcasted_iota(jnp.int32, sums.shape, 0)
    sign = jnp.where(lrow == 0, 1.0, jnp.where(lrow == 1, -1.0, 0.0))
    lam = jnp.sum(sign * jnp.exp(sums), axis=0, keepdims=True) + LAM_INIT
    for qb in range(ATT_NQ):
        l0 = acc_ref[qb, 0, HEAD_W:HEAD_W + 1, :]
        l1 = acc_ref[qb, 1, HEAD_W:HEAD_W + 1, :]
        ot = acc_ref[qb, 0, 0:HEAD_W, :] / l0 - lam * (acc_ref[qb, 1, 0:HEAD_W, :] / l1)
        ms = jnp.mean(ot * ot, axis=0, keepdims=True)
        y = (ot * lax.rsqrt(ms + EPS)).T * g_ref[...] * (1.0 - LAM_INIT)
        o_ref[0, qb * ATT_TQ:(qb + 1) * ATT_TQ, :] = y.astype(o_ref.dtype)


def _diff_attn(qt, kr, vt, lam_a, lam_b, norm_g):
    B, S, _ = kr.shape
    lam_spec = pl.BlockSpec((8, LANES), lambda b, h, qi: (0, 0))
    return pl.pallas_call(
        functools.partial(_attn_kernel, n_chunks=S // ATT_TK),
        grid=(B, N_HEADS, S // (ATT_NQ * ATT_TQ)),
        in_specs=[pl.BlockSpec((1, 1, ATT_NQ, HEAD_W, ATT_TQ), lambda b, h, qi: (b, h, qi, 0, 0)),
                  pl.BlockSpec((1, S, HEAD_W), lambda b, h, qi: (b, 0, h)),
                  pl.BlockSpec((1, 1, S // ATT_TK, ATT_VROWS, ATT_TK), lambda b, h, qi: (b, h, 0, 0, 0)),
                  lam_spec, lam_spec,
                  pl.BlockSpec((1, HEAD_W), lambda b, h, qi: (0, 0))],
        out_specs=pl.BlockSpec((1, ATT_NQ * ATT_TQ, HEAD_W), lambda b, h, qi: (b, qi, h)),
        out_shape=jax.ShapeDtypeStruct((B, S, D_MODEL), BF16),
        scratch_shapes=[pltpu.VMEM((ATT_NQ, 2, ATT_TK, ATT_TQ), F32), pltpu.VMEM((ATT_NQ, 2, ATT_VROWS, ATT_TQ), F32)],
        compiler_params=_params(("parallel", "parallel", "parallel")),
        name="diff_attn",
    )(qt, kr, vt, lam_a, lam_b, norm_g)


def _merge_kernel(of_ref, ob_ref, z_ref, oB_ref, ga_ref, gb_ref, bga_ref, bgb_ref, x_ref, g1_ref, gn_ref,
                  wa_ref, wb_ref, wo_ref, o_ref, ya_ref):
    oa = of_ref[...].astype(F32) + ob_ref[...].astype(F32)
    z = z_ref[...].astype(F32)
    gate = z * _sigmoid(z)
    for h in range(N_HEADS):
        cols = slice(h * HEAD_W, (h + 1) * HEAD_W)
        oh = oa[:, cols]
        ms = jnp.mean(oh * oh, axis=-1, keepdims=True)
        ya_ref[:, cols] = (oh * lax.rsqrt(ms + EPS) * gn_ref[...] * gate[:, cols]).astype(BF16)
    y_a = _dot(ya_ref[...], wa_ref[...])
    y_b = _dot(oB_ref[...], wb_ref[...])
    gate_a = _sigmoid(ga_ref[...].astype(F32) + bga_ref[...])
    gate_b = _sigmoid(gb_ref[...].astype(F32) + bgb_ref[...])
    mix = _dot((gate_a * y_a + gate_b * y_b).astype(BF16), wo_ref[...])
    o_ref[...] = x_ref[...] + g1_ref[0] * mix


def _merge(oAf, oAb, proj, oB, b_gate, x2, gate1, gn, wa, wb, wo, S):
    N = x2.shape[0]
    TM = min(512, S)
    tpb = S // TM
    row = lambda i: (i, 0)
    full = pl.BlockSpec((D_MODEL, D_MODEL), lambda i: (0, 0))
    return pl.pallas_call(
        _merge_kernel,
        grid=(N // TM,),
        in_specs=[pl.BlockSpec((TM, D_MODEL), row),
                  pl.BlockSpec((TM, D_MODEL), row),
                  pl.BlockSpec((TM, D_MODEL), lambda i: (i, COL_ZA)),
                  pl.BlockSpec((TM, D_MODEL), row),
                  pl.BlockSpec((TM, D_MODEL), lambda i: (i, COL_GA)),
                  pl.BlockSpec((TM, D_MODEL), lambda i: (i, COL_GB)),
                  pl.BlockSpec((1, D_MODEL), lambda i: (0, 0)),
                  pl.BlockSpec((1, D_MODEL), lambda i: (0, 1)),
                  pl.BlockSpec((TM, D_MODEL), row),
                  pl.BlockSpec((1, 1, D_MODEL), lambda i: (i // tpb, 0, 0)),
                  pl.BlockSpec((1, HEAD_W), lambda i: (0, 0)),
                  full, full, full],
        out_specs=pl.BlockSpec((TM, D_MODEL), row),
        out_shape=jax.ShapeDtypeStruct((N, D_MODEL), F32),
        scratch_shapes=[pltpu.VMEM((TM, D_MODEL), BF16)],
        compiler_params=_params(("parallel",)),
        name="merge",
    )(oAf, oAb, proj, oB, proj, proj, b_gate, b_gate, x2, gate1, gn, wa, wb, wo)


def _router_kernel(x_ref, sh_ref, sc_ref, g_ref, rw0_ref, rw1_ref, rb_ref, tri_ref,
                   h_ref, idx_ref, gate_ref, rank_ref, cnt_ref, base_ref):
    i = pl.program_id(0)

    @pl.when(i == 0)
    def _():
        base_ref[...] = jnp.zeros_like(base_ref)

    x = x_ref[...]
    ms = jnp.mean(x * x, axis=-1, keepdims=True)
    h = x * lax.rsqrt(ms + EPS) * g_ref[...] * (1.0 + sc_ref[0]) + sh_ref[0]
    h_ref[...] = h
    h0 = h.astype(BF16)
    h1 = (h - h0.astype(F32)).astype(BF16)
    logits = _dot(h0, rw0_ref[...]) + (_dot(h0, rw1_ref[...]) + _dot(h1, rw0_ref[...])) + rb_ref[...]
    lane = lax.broadcasted_iota(jnp.int32, logits.shape, 1)
    lane_f = lane.astype(F32)
    cur = jnp.where(lane < N_EXPERTS, logits, NEG_INF)
    vals, sel = [], []
    for _ in range(TOP_K):
        m = jnp.max(cur, axis=1, keepdims=True)
        ix = jnp.min(jnp.where(cur == m, lane_f, float(LANES)), axis=1, keepdims=True)
        hit = lane_f == ix
        vals.append(m)
        sel.append(hit)
        cur = jnp.where(hit, NEG_INF, cur)
    exps = [jnp.exp(v - vals[0]) for v in vals]
    den = exps[0] + exps[1] + exps[2] + exps[3]
    onehot = jnp.zeros(logits.shape, F32)
    for hit in sel:
        onehot = onehot + jnp.where(hit, 1.0, 0.0)
    before = _dot(tri_ref[...], onehot.astype(BF16)) + base_ref[...]
    idx_out = jnp.zeros(logits.shape, F32)
    gate_out = jnp.zeros(logits.shape, F32)
    rank_out = jnp.zeros(logits.shape, F32)
    for kk in range(TOP_K):
        slot = lane == kk
        e_id = jnp.sum(jnp.where(sel[kk], lane_f, 0.0), axis=1, keepdims=True)
        rk = jnp.sum(jnp.where(sel[kk], before, 0.0), axis=1, keepdims=True)
        idx_out = jnp.where(slot, e_id, idx_out)
        gate_out = jnp.where(slot, exps[kk] / den, gate_out)
        rank_out = jnp.where(slot, rk, rank_out)
    idx_ref[...] = idx_out.astype(jnp.int32)
    gate_ref[...] = gate_out
    rank_ref[...] = rank_out.astype(jnp.int32)
    base_ref[...] = base_ref[...] + jnp.sum(onehot, axis=0, keepdims=True)
    cnt_ref[...] = base_ref[...]


def _router(x1, shift, scale, g, rw0, rw1, rb, S):
    N = x1.shape[0]
    TM = min(512, S)
    tpb = S // TM
    r = jnp.arange(TM)
    tri = (r[None, :] < r[:, None]).astype(BF16)
    row = lambda i: (i, 0)
    const = lambda i: (0, 0)
    lanes = pl.BlockSpec((TM, LANES), row)
    return pl.pallas_call(
        _router_kernel,
        grid=(N // TM,),
        in_specs=[pl.BlockSpec((TM, D_MODEL), row),
                  pl.BlockSpec((1, 1, D_MODEL), lambda i: (i // tpb, 0, 0)),
                  pl.BlockSpec((1, 1, D_MODEL), lambda i: (i // tpb, 0, 0)),
                  pl.BlockSpec((1, D_MODEL), const),
                  pl.BlockSpec((D_MODEL, LANES), const),
                  pl.BlockSpec((D_MODEL, LANES), const),
                  pl.BlockSpec((1, LANES), const),
                  pl.BlockSpec((TM, TM), const)],
        out_specs=[pl.BlockSpec((TM, D_MODEL), row), lanes, lanes, lanes, pl.BlockSpec((1, LANES), const)],
        out_shape=[jax.ShapeDtypeStruct((N, D_MODEL), F32),
                   jax.ShapeDtypeStruct((N, LANES), jnp.int32),
                   jax.ShapeDtypeStruct((N, LANES), F32),
                   jax.ShapeDtypeStruct((N, LANES), jnp.int32),
                   jax.ShapeDtypeStruct((1, LANES), F32)],
        scratch_shapes=[pltpu.VMEM((1, LANES), F32)],
        compiler_params=_params(("arbitrary",)),
        name="router",
    )(x1, shift, scale, g, rw0, rw1, rb, tri)


INDEX_SLICE = ROW_MOVE_TILE * TOP_K


def _row_copy_out(h_ref, xs_hbm, sem, r, dst):
    return pltpu.make_async_copy(h_ref.at[pl.ds(r, 1)], xs_hbm.at[pl.ds(dst, 1)], sem)


def _dispatch_kernel(dest_hbm, h_ref, xs_in_hbm, xs_hbm, idx_smem, sem_idx, sem_rows):
    del xs_in_hbm
    i = pl.program_id(0)
    fetch = pltpu.make_async_copy(dest_hbm.at[pl.ds(i * INDEX_SLICE, INDEX_SLICE)], idx_smem, sem_idx)
    fetch.start()
    fetch.wait()

    def start(r, carry):
        for kk in range(TOP_K):
            _row_copy_out(h_ref, xs_hbm, sem_rows, r, idx_smem[r * TOP_K + kk]).start()
        return carry

    lax.fori_loop(0, ROW_MOVE_TILE, start, 0)

    def drain(r, carry):
        for kk in range(TOP_K):
            _row_copy_out(h_ref, xs_hbm, sem_rows, 0, 0).wait()
        return carry

    lax.fori_loop(0, ROW_MOVE_TILE, drain, 0)


def _dispatch(dest_flat, h2, n_rows):
    N = h2.shape[0]
    xs0 = jnp.zeros((n_rows, D_MODEL), F32)
    return pl.pallas_call(
        _dispatch_kernel,
        grid=(N // ROW_MOVE_TILE,),
        in_specs=[pl.BlockSpec(memory_space=pl.ANY),
                  pl.BlockSpec((ROW_MOVE_TILE, D_MODEL), lambda i: (i, 0)),
                  pl.BlockSpec(memory_space=pl.ANY)],
        out_specs=pl.BlockSpec(memory_space=pl.ANY),
        out_shape=jax.ShapeDtypeStruct((n_rows, D_MODEL), F32),
        scratch_shapes=[pltpu.SMEM((INDEX_SLICE,), jnp.int32), pltpu.SemaphoreType.DMA, pltpu.SemaphoreType.DMA],
        input_output_aliases={2: 0},
        compiler_params=_params(("arbitrary",)),
        name="moe_dispatch",
    )(dest_flat, h2, xs0)


def _expert_kernel(te_ref, xs_ref, wg_ref, bg_ref, wl_ref, bl_ref, wd_ref, bd_ref, ys_ref):
    del te_ref
    xb = xs_ref[...].astype(BF16)
    glu = jnp.minimum(_dot(xb, wg_ref[0]) + bg_ref[0], SWIGLU_LIMIT)
    lin = jnp.clip(_dot(xb, wl_ref[0]) + bl_ref[0], -SWIGLU_LIMIT, SWIGLU_LIMIT)
    act = glu * _sigmoid(SWIGLU_ALPHA * glu) * (lin + 1.0)
    ys_ref[...] = _dot(act.astype(BF16), wd_ref[0]) + bd_ref[0]


def _experts(tile_expert, xs, wg, bg, wl, bl, wd, bd):
    n_rows = xs.shape[0]
    n_tiles = n_rows // MOE_TILE
    wspec = pl.BlockSpec((1, D_MODEL, D_MODEL), lambda i, te: (te[i], 0, 0))
    bspec = pl.BlockSpec((1, 1, D_MODEL), lambda i, te: (te[i], 0, 0))
    rows = pl.BlockSpec((MOE_TILE, D_MODEL), lambda i, te: (i, 0))
    return pl.pallas_call(
        _expert_kernel,
        grid_spec=pltpu.PrefetchScalarGridSpec(
            num_scalar_prefetch=1,
            grid=(n_tiles,),
            in_specs=[rows, wspec, bspec, wspec, bspec, wspec, bspec],
            out_specs=rows),
        out_shape=jax.ShapeDtypeStruct((n_rows, D_MODEL), F32),
        compiler_params=_params(("arbitrary",)),
        name="moe_experts",
    )(tile_expert, xs, wg, bg, wl, bl, wd, bd)


def _row_copy_in(ys_hbm, buf_ref, sem, src, kk, r):
    return pltpu.make_async_copy(ys_hbm.at[pl.ds(src, 1)], buf_ref.at[kk, pl.ds(r, 1)], sem)


def _combine_kernel(dest_hbm, ys_hbm, gate_ref, x_ref, g2_ref, fg_ref, o_ref, idx_smem, buf_ref, sem_idx, sem_rows):
    i = pl.program_id(0)
    fetch = pltpu.make_async_copy(dest_hbm.at[pl.ds(i * INDEX_SLICE, INDEX_SLICE)], idx_smem, sem_idx)
    fetch.start()
    fetch.wait()

    def start(r, carry):
        for kk in range(TOP_K):
            _row_copy_in(ys_hbm, buf_ref, sem_rows, idx_smem[r * TOP_K + kk], kk, r).start()
        return carry

    lax.fori_loop(0, ROW_MOVE_TILE, start, 0)

    def drain(r, carry):
        for kk in range(TOP_K):
            _row_copy_in(ys_hbm, buf_ref, sem_rows, 0, 0, 0).wait()
        return carry

    lax.fori_loop(0, ROW_MOVE_TILE, drain, 0)

    gates = gate_ref[...]
    moe = gates[:, 0:1] * buf_ref[0]
    for kk in range(1, TOP_K):
        moe = moe + gates[:, kk:kk + 1] * buf_ref[kk]
    x = x_ref[...] + g2_ref[0] * moe
    ms = jnp.mean(x * x, axis=-1, keepdims=True)
    o_ref[...] = x * lax.rsqrt(ms + EPS) * fg_ref[...]


def _combine(dest_flat, ys, gates, x1, gate2, final_g, S):
    N = x1.shape[0]
    TM = ROW_MOVE_TILE
    tpb = S // TM
    row = lambda i: (i, 0)
    return pl.pallas_call(
        _combine_kernel,
        grid=(N // TM,),
        in_specs=[pl.BlockSpec(memory_space=pl.ANY),
                  pl.BlockSpec(memory_space=pl.ANY),
                  pl.BlockSpec((TM, LANES), row),
                  pl.BlockSpec((TM, D_MODEL), row),
                  pl.BlockSpec((1, 1, D_MODEL), lambda i: (i // tpb, 0, 0)),
                  pl.BlockSpec((1, D_MODEL), lambda i: (0, 0))],
        out_specs=pl.BlockSpec((TM, D_MODEL), row),
        out_shape=jax.ShapeDtypeStruct((N, D_MODEL), F32),
        scratch_shapes=[pltpu.SMEM((INDEX_SLICE,), jnp.int32), pltpu.VMEM((TOP_K, TM, D_MODEL), F32),
                        pltpu.SemaphoreType.DMA, pltpu.SemaphoreType.DMA],
        compiler_params=_params(("arbitrary",)),
        name="moe_combine",
    )(dest_flat, ys, gates, x1, gate2, final_g)


def _pad_lanes(a, offset=0):
    return jnp.pad(a, ((0, 0), (offset, LANES - offset - a.shape[1])))


def kernel(x, c, ada_w, ada_b, norm1_g, norm2_g, w_in, b_gate, conv_w, a_log, dt_bias, gdn_norm_g, w_branch_a,
           diff_lambda, diff_norm_g, w_branch_b, w_out, router_w, router_b, w_glu, b_glu, w_lin, b_lin, w_down,
           b_down, final_g):
    B, S, D = x.shape
    N = B * S
    assert D == D_MODEL and S % GROUP == 0 and ada_w.shape[0] == 1
    x2 = x.reshape(N, D)

    mod = _adaln(c, ada_w[0], ada_b[0])
    shift1, scale1, gate1, shift2, scale2, gate2 = [m.reshape(B, 1, D) for m in jnp.split(mod, 6, axis=-1)]

    wi = w_in[0]
    n_a = 4 * D
    n_small = 4 * N_HEADS
    w_big = jnp.concatenate([wi[:, :n_a], wi[:, n_a + n_small:]], axis=1).astype(BF16)
    w_small = _pad_lanes(wi[:, n_a:n_a + n_small]).astype(BF16)
    proj, small = _inproj(x2, shift1, scale1, norm1_g, w_big, w_small, S)
    proj3 = proj.reshape(B, S, N_COL_BLOCKS * D)

    qkv = _gdn_conv(proj3, conv_w[0])
    alog_row = _pad_lanes(a_log[0].reshape(1, -1), GATE_LANE0)
    dt_row = _pad_lanes(dt_bias[0].reshape(1, -1), GATE_LANE0)
    beta, G, Gl, cdl = _gdn_gates(small, alog_row, dt_row)
    r3 = lambda a: a.reshape(B, S, LANES)
    nC = S // GDN_CHUNK
    GT = jnp.transpose(r3(G)[:, :, GATE_LANE0:GATE_LANE0 + 2 * N_HEADS], (0, 2, 1)).reshape(B, 2 * N_HEADS, 1, S)
    cd = r3(cdl).reshape(B, nC, GDN_CHUNK, LANES)[:, :, 0, GATE_LANE0:GATE_LANE0 + 2 * N_HEADS]
    cd = jnp.transpose(cd.reshape(B, nC, 2, N_HEADS), (0, 2, 3, 1)).reshape(-1)
    u, w, qd, att, kdt = _gdn_prep(qkv, r3(beta), r3(G), r3(Gl), GT)
    oAf, oAb = _gdn_scan(cd, u, w, qd, att, kdt)

    half = DIFF_DH // 2
    inv_freq = ROPE_THETA ** (-jnp.arange(half, dtype=F32) / half)
    ang = jnp.arange(S, dtype=F32)[:, None] * inv_freq[None, :]
    cos_t = jnp.tile(jnp.cos(ang), (1, 4))
    sin_h = jnp.sin(ang)
    sin_t = jnp.tile(jnp.concatenate([-sin_h, sin_h], axis=1), (1, 2))
    qt, kr, vt = _rope(proj3, cos_t, sin_t)
    lam_a = jnp.pad(_pad_lanes(diff_lambda[0][0::2]), ((0, 6), (0, 0)))
    lam_b = jnp.pad(_pad_lanes(diff_lambda[0][1::2]), ((0, 6), (0, 0)))
    oB = _diff_attn(qt, kr, vt, lam_a, lam_b, diff_norm_g)

    x1 = _merge(oAf.reshape(N, D), oAb.reshape(N, D), proj, oB.reshape(N, D), b_gate, x2, gate1, gdn_norm_g,
                w_branch_a[0].astype(BF16), w_branch_b[0].astype(BF16), w_out[0].astype(BF16), S)

    rw = _pad_lanes(router_w[0])
    rw0 = rw.astype(BF16)
    rw1 = (rw - rw0.astype(F32)).astype(BF16)
    h2, idx, gates, rank, counts = _router(x1, shift2, scale2, norm2_g, rw0, rw1, _pad_lanes(router_b), S)
    cnt = counts[0, :N_EXPERTS].astype(jnp.int32)
    padded = (cnt + MOE_TILE - 1) // MOE_TILE * MOE_TILE
    pad_ends = jnp.cumsum(padded)
    pad_starts = pad_ends - padded
    dest = (pad_starts[idx[:, :TOP_K]] + rank[:, :TOP_K]).reshape(-1)
    n_tiles = -(-(N * TOP_K) // MOE_TILE) + N_EXPERTS
    tile_start = jnp.arange(n_tiles, dtype=jnp.int32) * MOE_TILE
    tile_expert = jnp.minimum(jnp.searchsorted(pad_ends, tile_start, side="right"), N_EXPERTS - 1).astype(jnp.int32)
    xs = _dispatch(dest, h2, n_tiles * MOE_TILE)
    ys = _experts(tile_expert, xs, w_glu[0].astype(BF16), b_glu[0][:, None, :], w_lin[0].astype(BF16),
                  b_lin[0][:, None, :], w_down[0].astype(BF16), b_down[0][:, None, :])
    out = _combine(dest, ys, gates, x1, gate2, final_g.reshape(1, D), S)
    return out.reshape(B, S, D)
```

```python
import functools
import math

import jax
import jax.numpy as jnp
from jax import lax
from jax.experimental import pallas as pl
from jax.experimental.pallas import tpu as pltpu

F32 = jnp.float32
BF16 = jnp.bfloat16

D_MODEL = 1024
EPS = 1e-6
N_HEADS = 8
HEAD_W = 128
GDN_CHUNK = 64
CONV_WIDTH = 5
DIFF_DH = 64
ROPE_THETA = 10000.0
LAM_INIT = 0.8 - 0.6 * math.exp(-0.3 * 0)
N_EXPERTS = 32
TOP_K = 4
SWIGLU_ALPHA = 1.702
SWIGLU_LIMIT = 7.0

LANES = 128
GROUP = 256
CHUNKS_PER_GROUP = GROUP // GDN_CHUNK
CHUNK_SHIFT = GDN_CHUNK.bit_length() - 1
MOE_TILE = 512
ROW_MOVE_TILE = 256
NEG_INF = float("-inf")

COL_QA, COL_KA, COL_VA, COL_ZA, COL_QB, COL_KB, COL_VB, COL_GA, COL_GB = range(9)
N_COL_BLOCKS = 9


def _params(sem, vmem_mb=48):
    return pltpu.CompilerParams(dimension_semantics=sem, vmem_limit_bytes=vmem_mb * 1024 * 1024)


def _dot(a, b):
    return jnp.dot(a, b, preferred_element_type=F32)


def _dot_nt(a, b):
    return lax.dot_general(a, b, (((1,), (1,)), ((), ())), preferred_element_type=F32)


def _sigmoid(x):
    return 1.0 / (1.0 + jnp.exp(-x))


def _split3(x):
    a = x.astype(BF16)
    r = x - a.astype(F32)
    b = r.astype(BF16)
    c = (r - b.astype(F32)).astype(BF16)
    return a, b, c


def _adaln_kernel(c_ref, w_ref, b_ref, o_ref):
    c = c_ref[...]
    cond = c * _sigmoid(c)
    c0, c1, c2 = _split3(cond)
    w0, w1, w2 = _split3(w_ref[...])
    acc = _dot(c0, w0) + (_dot(c0, w1) + _dot(c1, w0)) + (_dot(c0, w2) + _dot(c1, w1) + _dot(c2, w0))
    o_ref[...] = acc + b_ref[...]


def _adaln(c, ada_w, ada_b):
    B = c.shape[0]
    n = ada_w.shape[1] // D_MODEL
    return pl.pallas_call(
        _adaln_kernel,
        grid=(n,),
        in_specs=[pl.BlockSpec((B, D_MODEL), lambda j: (0, 0)),
                  pl.BlockSpec((D_MODEL, D_MODEL), lambda j: (0, j)),
                  pl.BlockSpec((1, D_MODEL), lambda j: (0, j))],
        out_specs=pl.BlockSpec((B, D_MODEL), lambda j: (0, j)),
        out_shape=jax.ShapeDtypeStruct((B, n * D_MODEL), F32),
        compiler_params=_params(("parallel",)),
        name="adaln",
    )(c, ada_w, ada_b.reshape(1, -1))


def _inproj_kernel(x_ref, sh_ref, sc_ref, g_ref, w_ref, ws_ref, o_ref, os_ref, h_ref):
    @pl.when(pl.program_id(1) == 0)
    def _():
        x = x_ref[...]
        ms = jnp.mean(x * x, axis=-1, keepdims=True)
        y = x * lax.rsqrt(ms + EPS) * g_ref[...]
        h = (y * (1.0 + sc_ref[0]) + sh_ref[0]).astype(BF16)
        h_ref[...] = h
        os_ref[...] = _dot(h, ws_ref[...])

    o_ref[...] = _dot(h_ref[...], w_ref[...]).astype(o_ref.dtype)


def _inproj(x2, shift, scale, g, w_big, w_small, S):
    N = x2.shape[0]
    TM = min(1024, S)
    tpb = S // TM
    return pl.pallas_call(
        _inproj_kernel,
        grid=(N // TM, N_COL_BLOCKS),
        in_specs=[pl.BlockSpec((TM, D_MODEL), lambda i, j: (i, 0)),
                  pl.BlockSpec((1, 1, D_MODEL), lambda i, j: (i // tpb, 0, 0)),
                  pl.BlockSpec((1, 1, D_MODEL), lambda i, j: (i // tpb, 0, 0)),
                  pl.BlockSpec((1, D_MODEL), lambda i, j: (0, 0)),
                  pl.BlockSpec((D_MODEL, D_MODEL), lambda i, j: (0, j)),
                  pl.BlockSpec((D_MODEL, LANES), lambda i, j: (0, 0))],
        out_specs=[pl.BlockSpec((TM, D_MODEL), lambda i, j: (i, j)),
                   pl.BlockSpec((TM, LANES), lambda i, j: (i, 0))],
        out_shape=[jax.ShapeDtypeStruct((N, N_COL_BLOCKS * D_MODEL), BF16),
                   jax.ShapeDtypeStruct((N, LANES), F32)],
        scratch_shapes=[pltpu.VMEM((TM, D_MODEL), BF16)],
        compiler_params=_params(("parallel", "arbitrary")),
        name="inproj",
    )(x2, shift, scale, g, w_big, w_small)


HALO = 16


def _conv_kernel(cur_ref, prev_ref, next_ref, w_ref, o_ref, ext_ref, *, TR):
    i = pl.program_id(1)
    g = pl.program_id(2)
    last = pl.num_programs(1) - 1
    ext_ref[8:8 + TR, :] = cur_ref[0].astype(F32)
    pv = prev_ref[0].astype(F32)[HALO - 8:HALO]
    nx = next_ref[0].astype(F32)[0:8]
    ext_ref[0:8, :] = jnp.where(i > 0, pv, 0.0)
    ext_ref[TR + 8:TR + 16, :] = jnp.where(i < last, nx, 0.0)
    pad = (CONV_WIDTH - 1) // 2
    acc = ext_ref[8 - pad:8 - pad + TR, :] * w_ref[0:1, :]
    for j in range(1, CONV_WIDTH):
        acc = acc + ext_ref[8 - pad + j:8 - pad + j + TR, :] * w_ref[j:j + 1, :]
    y = acc * _sigmoid(acc)
    ones = jnp.ones((HEAD_W, HEAD_W), BF16)
    qscale = jnp.where(g == 0, HEAD_W ** -0.5, 1.0)
    for h in range(N_HEADS):
        yh = y[:, h * HEAD_W:(h + 1) * HEAD_W]
        ss = _dot((yh * yh).astype(BF16), ones)
        normed = yh * (lax.rsqrt(ss + EPS) * qscale)
        o_ref[0, :, h * HEAD_W:(h + 1) * HEAD_W] = jnp.where(g < 2, normed, yh).astype(o_ref.dtype)


def _gdn_conv(proj3, conv_w):
    B, S, _ = proj3.shape
    TR = min(512, S)
    nT = S // TR
    rb = TR // HALO
    nH = S // HALO
    return pl.pallas_call(
        functools.partial(_conv_kernel, TR=TR),
        grid=(B, nT, 3),
        in_specs=[pl.BlockSpec((1, TR, D_MODEL), lambda b, i, g: (b, i, g)),
                  pl.BlockSpec((1, HALO, D_MODEL), lambda b, i, g: (b, jnp.maximum(i * rb - 1, 0), g)),
                  pl.BlockSpec((1, HALO, D_MODEL), lambda b, i, g: (b, jnp.minimum((i + 1) * rb, nH - 1), g)),
                  pl.BlockSpec((CONV_WIDTH, D_MODEL), lambda b, i, g: (0, g))],
        out_specs=pl.BlockSpec((1, TR, D_MODEL), lambda b, i, g: (b, i, g)),
        out_shape=jax.ShapeDtypeStruct((B, S, 3 * D_MODEL), BF16),
        scratch_shapes=[pltpu.VMEM((TR + 16, D_MODEL), F32)],
        compiler_params=_params(("parallel", "parallel", "parallel")),
        name="gdn_conv",
    )(proj3, proj3, proj3, conv_w)


GATE_LANE0 = 16


def _gates_kernel(x_ref, alog_ref, dt_ref, beta_ref, g_ref, gl_ref, cd_ref):
    x = x_ref[...]
    R = x.shape[0]
    lane = lax.broadcasted_iota(jnp.int32, x.shape, 1)
    beta_ref[...] = _sigmoid(x)
    z = x + dt_ref[...]
    softplus = jnp.maximum(z, 0.0) + jnp.log(1.0 + jnp.exp(-jnp.abs(z)))
    gd = -jnp.exp(alog_ref[...]) * softplus
    gd = jnp.where((lane >= GATE_LANE0) & (lane < GATE_LANE0 + 2 * N_HEADS), gd, 0.0)
    r = lax.broadcasted_iota(jnp.int32, (R, R), 0)
    c = lax.broadcasted_iota(jnp.int32, (R, R), 1)
    same = (r >> CHUNK_SHIFT) == (c >> CHUNK_SHIFT)
    lower = jnp.where(same & (c <= r), 1.0, 0.0).astype(BF16)
    upper = jnp.where(same & (c >= r), 1.0, 0.0).astype(BF16)
    block = jnp.where(same, 1.0, 0.0).astype(BF16)
    p0, p1, p2 = _split3(gd)
    g_fwd = _dot(lower, p0) + _dot(lower, p1) + _dot(lower, p2)
    g_bwd = _dot(upper, p0) + _dot(upper, p1) + _dot(upper, p2)
    tot = _dot(block, p0) + _dot(block, p1) + _dot(block, p2)
    G = jnp.where(lane < GATE_LANE0 + N_HEADS, g_fwd, g_bwd)
    g_ref[...] = G
    gl_ref[...] = tot - G
    cd_ref[...] = jnp.exp(tot)


def _gdn_gates(small, alog_row, dt_row):
    N = small.shape[0]
    spec = pl.BlockSpec((GROUP, LANES), lambda i: (i, 0))
    row = pl.BlockSpec((1, LANES), lambda i: (0, 0))
    return pl.pallas_call(
        _gates_kernel,
        grid=(N // GROUP,),
        in_specs=[spec, row, row],
        out_specs=[spec] * 4,
        out_shape=[jax.ShapeDtypeStruct((N, LANES), F32)] * 4,
        compiler_params=_params(("parallel",)),
        name="gdn_gates",
    )(small, alog_row, dt_row)


PREP_HEADS = 2
(MASK_STRICT_LO, MASK_STRICT_UP, MASK_INCL_LO, MASK_INCL_UP, MASK_EYE, MASK_BLK4, MASK_OFF0) = range(7)
N_MASKS = MASK_OFF0 + (CHUNK_SHIFT - 2)


def _prep_masks():
    r = jnp.arange(GROUP)[:, None]
    c = jnp.arange(GROUP)[None, :]
    same = (r >> CHUNK_SHIFT) == (c >> CHUNK_SHIFT)
    masks = [same & (c < r), same & (c > r), same & (c <= r), same & (c >= r), r == c, (r >> 2) == (c >> 2)]
    for shift in range(2, CHUNK_SHIFT):
        masks.append(((r >> shift) != (c >> shift)) & ((r >> (shift + 1)) == (c >> (shift + 1))))
    return jnp.stack(masks).astype(F32)


def _col(x, l, lane):
    return jnp.broadcast_to(jnp.sum(jnp.where(lane == l, x, 0.0), axis=1, keepdims=True), x.shape)


def _prep_kernel(q_ref, k_ref, v_ref, beta_ref, g_ref, gl_ref, gtf_ref, gtb_ref, mask_ref,
                 u_ref, w_ref, qd_ref, at_ref, kdt_ref):
    hp = pl.program_id(1)
    lane = lax.broadcasted_iota(jnp.int32, (GROUP, LANES), 1)
    wide = lambda a: jnp.concatenate([a, a], axis=1)
    chains = [(hh, d) for hh in range(PREP_HEADS) for d in range(2)]
    p, rhs = {}, {}
    for hh in range(PREP_HEADS):
        cols = slice(hh * HEAD_W, (hh + 1) * HEAD_W)
        q = q_ref[0, :, cols]
        k = k_ref[0, :, cols]
        qf = q.astype(F32)
        kf = k.astype(F32)
        vf = v_ref[0, :, cols].astype(F32)
        kk = _dot_nt(k, k)
        qk = _dot_nt(q, k)
        for d in range(2):
            lb = d * N_HEADS + hp * PREP_HEADS + hh
            beta_c = _col(beta_ref[0], lb, lane)
            g_c = _col(g_ref[0], GATE_LANE0 + lb, lane)
            eg_c = jnp.exp(g_c)
            egl_c = jnp.exp(_col(gl_ref[0], GATE_LANE0 + lb, lane))
            g_r = (gtf_ref if d == 0 else gtb_ref)[0, hh]
            dec = jnp.exp(jnp.minimum(wide(g_c) - g_r, 0.0))
            p[hh, d] = -(kk * wide(beta_c)) * dec * mask_ref[MASK_STRICT_LO + d]
            att = qk * dec * mask_ref[MASK_INCL_LO + d]
            rhs[hh, d] = jnp.concatenate([vf * beta_c, kf * (beta_c * eg_c)], axis=1).astype(BF16)
            qd_ref[0, d, :, cols] = (qf * eg_c).astype(qd_ref.dtype)
            kdt = (kf * egl_c).T
            for ci in range(CHUNKS_PER_GROUP):
                sl = slice(ci * GDN_CHUNK, (ci + 1) * GDN_CHUNK)
                at_ref[0, d, hh, ci] = att[sl, sl].astype(at_ref.dtype)
                kdt_ref[0, d, hh, ci] = kdt[:, sl].astype(kdt_ref.dtype)
    p4 = {ch: (p[ch] * mask_ref[MASK_BLK4]).astype(BF16) for ch in chains}
    sq = {ch: _dot(p4[ch], p4[ch]).astype(BF16) for ch in chains}
    t = {ch: mask_ref[MASK_EYE] + p4[ch].astype(F32) for ch in chains}
    t = {ch: t[ch] + _dot(t[ch].astype(BF16), sq[ch]) for ch in chains}
    for lvl in range(CHUNK_SHIFT - 2):
        tb = {ch: t[ch].astype(BF16) for ch in chains}
        x = {ch: _dot(tb[ch], (p[ch] * mask_ref[MASK_OFF0 + lvl]).astype(BF16)).astype(BF16) for ch in chains}
        t = {ch: t[ch] + _dot(x[ch], tb[ch]) for ch in chains}
    uw = {ch: _dot(t[ch].astype(BF16), rhs[ch]) for ch in chains}
    for hh, d in chains:
        cols = slice(hh * HEAD_W, (hh + 1) * HEAD_W)
        u_ref[0, d, :, cols] = uw[hh, d][:, :HEAD_W].astype(u_ref.dtype)
        w_ref[0, d, :, cols] = uw[hh, d][:, HEAD_W:].astype(w_ref.dtype)


def _gdn_prep(qkv, beta, G, Gl, GT):
    B, S, _ = qkv.shape
    nG = S // GROUP
    nC = S // GDN_CHUNK
    PW = PREP_HEADS * HEAD_W
    nP = N_HEADS // PREP_HEADS
    sm = pl.BlockSpec((1, GROUP, LANES), lambda b, h, g: (b, g, 0))
    big = pl.BlockSpec((1, 2, GROUP, PW), lambda b, h, g: (b, 0, g, h))
    return pl.pallas_call(
        _prep_kernel,
        grid=(B, nP, nG),
        in_specs=[pl.BlockSpec((1, GROUP, PW), lambda b, h, g: (b, g, h)),
                  pl.BlockSpec((1, GROUP, PW), lambda b, h, g: (b, g, nP + h)),
                  pl.BlockSpec((1, GROUP, PW), lambda b, h, g: (b, g, 2 * nP + h)),
                  sm, sm, sm,
                  pl.BlockSpec((1, PREP_HEADS, 1, GROUP), lambda b, h, g: (b, h, 0, g)),
                  pl.BlockSpec((1, PREP_HEADS, 1, GROUP), lambda b, h, g: (b, nP + h, 0, g)),
                  pl.BlockSpec((N_MASKS, GROUP, GROUP), lambda b, h, g: (0, 0, 0))],
        out_specs=[big, big, big,
                   pl.BlockSpec((1, 2, PREP_HEADS, CHUNKS_PER_GROUP, GDN_CHUNK, GDN_CHUNK),
                                lambda b, h, g: (b, 0, h, g, 0, 0)),
                   pl.BlockSpec((1, 2, PREP_HEADS, CHUNKS_PER_GROUP, HEAD_W, GDN_CHUNK),
                                lambda b, h, g: (b, 0, h, g, 0, 0))],
        out_shape=[jax.ShapeDtypeStruct((B, 2, S, D_MODEL), BF16)] * 3
                  + [jax.ShapeDtypeStruct((B, 2, N_HEADS, nC, GDN_CHUNK, GDN_CHUNK), BF16),
                     jax.ShapeDtypeStruct((B, 2, N_HEADS, nC, HEAD_W, GDN_CHUNK), BF16)],
        compiler_params=_params(("parallel", "parallel", "parallel")),
        name="gdn_prep",
    )(qkv, qkv, qkv, beta, G, Gl, GT, GT, _prep_masks())


def _scan_kernel(cd_ref, uf_ref, wf_ref, qdf_ref, atf_ref, kdtf_ref, ub_ref, wb_ref, qdb_ref, atb_ref, kdtb_ref,
                 of_ref, ob_ref, state_ref, *, nc, nC):
    b = pl.program_id(0)
    t = pl.program_id(1)
    nT = pl.num_programs(1)

    @pl.when(t == 0)
    def _():
        state_ref[...] = jnp.zeros_like(state_ref)

    dirs = ((uf_ref, wf_ref, qdf_ref, atf_ref, kdtf_ref, of_ref), (ub_ref, wb_ref, qdb_ref, atb_ref, kdtb_ref, ob_ref))

    def chunk(ci, carry):
        work = []
        for d, refs in enumerate(dirs):
            c = ci if d == 0 else nc - 1 - ci
            tt = t if d == 0 else nT - 1 - t
            row = pl.multiple_of(c * GDN_CHUNK, GDN_CHUNK)
            for h in range(N_HEADS):
                work.append((d, h, c, row, ((b * 2 + d) * N_HEADS + h) * nC + tt * nc + c, refs))
        s_old = [state_ref[d, h] for d, h, *_ in work]
        sb = [s.astype(BF16) for s in s_old]
        tile = lambda ref, row, h: ref[0, 0, pl.ds(row, GDN_CHUNK), h * HEAD_W:(h + 1) * HEAD_W]
        ws = [_dot(tile(refs[1], row, h), sb[i]) for i, (d, h, c, row, gi, refs) in enumerate(work)]
        qs = [_dot(tile(refs[2], row, h), sb[i]) for i, (d, h, c, row, gi, refs) in enumerate(work)]
        vb = [(tile(refs[0], row, h).astype(F32) - ws[i]).astype(BF16)
              for i, (d, h, c, row, gi, refs) in enumerate(work)]
        o = [qs[i] + _dot(refs[3][0, 0, h, c], vb[i]) for i, (d, h, c, row, gi, refs) in enumerate(work)]
        upd = [_dot(refs[4][0, 0, h, c], vb[i]) for i, (d, h, c, row, gi, refs) in enumerate(work)]
        for i, (d, h, c, row, gi, refs) in enumerate(work):
            state_ref[d, h] = s_old[i] * cd_ref[gi] + upd[i]
            refs[5][0, pl.ds(row, GDN_CHUNK), h * HEAD_W:(h + 1) * HEAD_W] = o[i].astype(of_ref.dtype)
        return carry

    lax.fori_loop(0, nc, chunk, 0)


def _gdn_scan(cd, u, w, qd, att, kdt):
    B, _, S, _ = u.shape
    TC = min(512, S)
    nT = S // TC
    nc = TC // GDN_CHUNK
    nC = S // GDN_CHUNK
    fwd = lambda b, t: t
    bwd = lambda b, t: nT - 1 - t

    def specs(d, tm):
        big = pl.BlockSpec((1, 1, TC, D_MODEL), lambda b, t: (b, d, tm(b, t), 0))
        return [big, big, big,
                pl.BlockSpec((1, 1, N_HEADS, nc, GDN_CHUNK, GDN_CHUNK), lambda b, t: (b, d, 0, tm(b, t), 0, 0)),
                pl.BlockSpec((1, 1, N_HEADS, nc, HEAD_W, GDN_CHUNK), lambda b, t: (b, d, 0, tm(b, t), 0, 0))]

    return pl.pallas_call(
        functools.partial(_scan_kernel, nc=nc, nC=nC),
        grid=(B, nT),
        in_specs=[pl.BlockSpec(memory_space=pltpu.SMEM)] + specs(0, fwd) + specs(1, bwd),
        out_specs=[pl.BlockSpec((1, TC, D_MODEL), lambda b, t: (b, t, 0)),
                   pl.BlockSpec((1, TC, D_MODEL), lambda b, t: (b, nT - 1 - t, 0))],
        out_shape=[jax.ShapeDtypeStruct((B, S, D_MODEL), BF16)] * 2,
        scratch_shapes=[pltpu.VMEM((2, N_HEADS, HEAD_W, HEAD_W), F32)],
        compiler_params=_params(("parallel", "arbitrary")),
        name="gdn_scan",
    )(cd, u, w, qd, att, kdt, u, w, qd, att, kdt)


ATT_TQ = 256
ATT_TK = 256
ATT_NQ = 4
ATT_VROWS = HEAD_W + 16
LOG2E = 1.4426950408889634


def _rope_kernel(q_ref, k_ref, v_ref, cos_ref, sin_ref, qt_ref, kr_ref, vt_ref, *, TR):
    cs = cos_ref[...]
    sn = sin_ref[...]
    lane = lax.broadcasted_iota(jnp.int32, cs.shape, 1)
    first_half = (lane & (DIFF_DH - 1)) < (DIFF_DH // 2)
    qscale = DIFF_DH ** -0.5 * LOG2E

    def rot(x):
        partner = jnp.where(first_half, pltpu.roll(x, HEAD_W - DIFF_DH // 2, 1), pltpu.roll(x, DIFF_DH // 2, 1))
        return x * cs + partner * sn

    for h in range(N_HEADS):
        cols = slice(h * HEAD_W, (h + 1) * HEAD_W)
        qr = rot(q_ref[0, :, cols].astype(F32)) * qscale
        kr_ref[0, :, cols] = rot(k_ref[0, :, cols].astype(F32)).astype(kr_ref.dtype)
        vf = v_ref[0, :, cols].astype(F32)
        for ci in range(TR // ATT_TK):
            rows = slice(ci * ATT_TK, (ci + 1) * ATT_TK)
            vt_ref[0, h, ci, 0:HEAD_W, :] = vf[rows].T.astype(vt_ref.dtype)
            vt_ref[0, h, ci, HEAD_W:ATT_VROWS, :] = jnp.ones((ATT_VROWS - HEAD_W, ATT_TK), vt_ref.dtype)
        for ci in range(TR // ATT_TQ):
            rows = slice(ci * ATT_TQ, (ci + 1) * ATT_TQ)
            qt_ref[0, h, ci] = qr[rows].T.astype(qt_ref.dtype)


def _rope(proj3, cos_t, sin_t):
    B, S, _ = proj3.shape
    TR = min(512, S)
    tab = pl.BlockSpec((TR, HEAD_W), lambda b, i: (i, 0))
    col = lambda cb: pl.BlockSpec((1, TR, D_MODEL), lambda b, i: (b, i, cb))
    return pl.pallas_call(
        functools.partial(_rope_kernel, TR=TR),
        grid=(B, S // TR),
        in_specs=[col(COL_QB), col(COL_KB), col(COL_VB), tab, tab],
        out_specs=[pl.BlockSpec((1, N_HEADS, TR // ATT_TQ, HEAD_W, ATT_TQ), lambda b, i: (b, 0, i, 0, 0)),
                   pl.BlockSpec((1, TR, D_MODEL), lambda b, i: (b, i, 0)),
                   pl.BlockSpec((1, N_HEADS, TR // ATT_TK, ATT_VROWS, ATT_TK), lambda b, i: (b, 0, i, 0, 0))],
        out_shape=[jax.ShapeDtypeStruct((B, N_HEADS, S // ATT_TQ, HEAD_W, ATT_TQ), BF16),
                   jax.ShapeDtypeStruct((B, S, D_MODEL), BF16),
                   jax.ShapeDtypeStruct((B, N_HEADS, S // ATT_TK, ATT_VROWS, ATT_TK), BF16)],
        compiler_params=_params(("parallel", "parallel")),
        name="rope",
    )(proj3, proj3, proj3, cos_t, sin_t)


def _attn_kernel(qt_ref, k_ref, vt_ref, la_ref, lb_ref, g_ref, o_ref, s_ref, acc_ref, *, n_chunks):
    row = lax.broadcasted_iota(jnp.int32, (HEAD_W, ATT_TQ), 0)
    qw = []
    for qb in range(ATT_NQ):
        qt = qt_ref[0, 0, qb]
        zero = jnp.zeros_like(qt)
        qw.append((jnp.where(row < DIFF_DH, qt, zero), jnp.where(row >= DIFF_DH, qt, zero)))
    chains = [(qb, comp) for qb in range(ATT_NQ) for comp in range(2)]

    def scores(j):
        kc = k_ref[0, pl.ds(pl.multiple_of(j * ATT_TK, ATT_TK), ATT_TK), :]
        return [_dot(kc, qw[qb][comp]) for qb, comp in chains]

    acc_ref[...] = jnp.zeros_like(acc_ref)
    for (qb, comp), s0 in zip(chains, scores(0)):
        s_ref[qb, comp] = s0

    def chunk(j, carry):
        s_next = scores(jnp.minimum(j + 1, n_chunks - 1))
        vt = vt_ref[0, 0, j]
        out = []
        for ci, (qb, comp) in enumerate(chains):
            m_prev = carry[ci]
            s = s_ref[qb, comp]
            m_new = jnp.maximum(m_prev, jnp.max(s, axis=0, keepdims=True))
            alpha = jnp.exp2(m_prev - m_new)
            p = jnp.exp2(s - m_new)
            out.append(m_new)
            acc_ref[qb, comp] = alpha * acc_ref[qb, comp] + _dot(vt, p.astype(BF16))
        for (qb, comp), sn in zip(chains, s_next):
            s_ref[qb, comp] = sn
        return tuple(out)

    neg = jnp.full((1, ATT_TQ), NEG_INF, F32)
    lax.fori_loop(0, n_chunks, chunk, (neg,) * len(chains), unroll=4)

    sums = jnp.sum(la_ref[...] * lb_ref[...], axis=1, keepdims=True)
    lrow = lax.broadcasted_iota(jnp.int32, sums.shape, 0)
    sign = jnp.where(lrow == 0, 1.0, jnp.where(lrow == 1, -1.0, 0.0))
    lam = jnp.sum(sign * jnp.exp(sums), axis=0, keepdims=True) + LAM_INIT
    for qb in range(ATT_NQ):
        l0 = acc_ref[qb, 0, HEAD_W:HEAD_W + 1, :]
        l1 = acc_ref[qb, 1, HEAD_W:HEAD_W + 1, :]
        ot = acc_ref[qb, 0, 0:HEAD_W, :] / l0 - lam * (acc_ref[qb, 1, 0:HEAD_W, :] / l1)
        ms = jnp.mean(ot * ot, axis=0, keepdims=True)
        y = (ot * lax.rsqrt(ms + EPS)).T * g_ref[...] * (1.0 - LAM_INIT)
        o_ref[0, qb * ATT_TQ:(qb + 1) * ATT_TQ, :] = y.astype(o_ref.dtype)


def _diff_attn(qt, kr, vt, lam_a, lam_b, norm_g):
    B, S, _ = kr.shape
    lam_spec = pl.BlockSpec((8, LANES), lambda b, h, qi: (0, 0))
    return pl.pallas_call(
        functools.partial(_attn_kernel, n_chunks=S // ATT_TK),
        grid=(B, N_HEADS, S // (ATT_NQ * ATT_TQ)),
        in_specs=[pl.BlockSpec((1, 1, ATT_NQ, HEAD_W, ATT_TQ), lambda b, h, qi: (b, h, qi, 0, 0)),
                  pl.BlockSpec((1, S, HEAD_W), lambda b, h, qi: (b, 0, h)),
                  pl.BlockSpec((1, 1, S // ATT_TK, ATT_VROWS, ATT_TK), lambda b, h, qi: (b, h, 0, 0, 0)),
                  lam_spec, lam_spec,
                  pl.BlockSpec((1, HEAD_W), lambda b, h, qi: (0, 0))],
        out_specs=pl.BlockSpec((1, ATT_NQ * ATT_TQ, HEAD_W), lambda b, h, qi: (b, qi, h)),
        out_shape=jax.ShapeDtypeStruct((B, S, D_MODEL), BF16),
        scratch_shapes=[pltpu.VMEM((ATT_NQ, 2, ATT_TK, ATT_TQ), F32), pltpu.VMEM((ATT_NQ, 2, ATT_VROWS, ATT_TQ), F32)],
        compiler_params=_params(("parallel", "parallel", "parallel")),
        name="diff_attn",
    )(qt, kr, vt, lam_a, lam_b, norm_g)


def _merge_kernel(of_ref, ob_ref, z_ref, oB_ref, ga_ref, gb_ref, bga_ref, bgb_ref, x_ref, g1_ref, gn_ref,
                  wa_ref, wb_ref, wo_ref, o_ref, ya_ref):
    oa = of_ref[...].astype(F32) + ob_ref[...].astype(F32)
    z = z_ref[...].astype(F32)
    gate = z * _sigmoid(z)
    for h in range(N_HEADS):
        cols = slice(h * HEAD_W, (h + 1) * HEAD_W)
        oh = oa[:, cols]
        ms = jnp.mean(oh * oh, axis=-1, keepdims=True)
        ya_ref[:, cols] = (oh * lax.rsqrt(ms + EPS) * gn_ref[...] * gate[:, cols]).astype(BF16)
    y_a = _dot(ya_ref[...], wa_ref[...])
    y_b = _dot(oB_ref[...], wb_ref[...])
    gate_a = _sigmoid(ga_ref[...].astype(F32) + bga_ref[...])
    gate_b = _sigmoid(gb_ref[...].astype(F32) + bgb_ref[...])
    mix = _dot((gate_a * y_a + gate_b * y_b).astype(BF16), wo_ref[...])
    o_ref[...] = x_ref[...] + g1_ref[0] * mix


def _merge(oAf, oAb, proj, oB, b_gate, x2, gate1, gn, wa, wb, wo, S):
    N = x2.shape[0]
    TM = min(512, S)
    tpb = S // TM
    row = lambda i: (i, 0)
    full = pl.BlockSpec((D_MODEL, D_MODEL), lambda i: (0, 0))
    return pl.pallas_call(
        _merge_kernel,
        grid=(N // TM,),
        in_specs=[pl.BlockSpec((TM, D_MODEL), row),
                  pl.BlockSpec((TM, D_MODEL), row),
                  pl.BlockSpec((TM, D_MODEL), lambda i: (i, COL_ZA)),
                  pl.BlockSpec((TM, D_MODEL), row),
                  pl.BlockSpec((TM, D_MODEL), lambda i: (i, COL_GA)),
                  pl.BlockSpec((TM, D_MODEL), lambda i: (i, COL_GB)),
                  pl.BlockSpec((1, D_MODEL), lambda i: (0, 0)),
                  pl.BlockSpec((1, D_MODEL), lambda i: (0, 1)),
                  pl.BlockSpec((TM, D_MODEL), row),
                  pl.BlockSpec((1, 1, D_MODEL), lambda i: (i // tpb, 0, 0)),
                  pl.BlockSpec((1, HEAD_W), lambda i: (0, 0)),
                  full, full, full],
        out_specs=pl.BlockSpec((TM, D_MODEL), row),
        out_shape=jax.ShapeDtypeStruct((N, D_MODEL), F32),
        scratch_shapes=[pltpu.VMEM((TM, D_MODEL), BF16)],
        compiler_params=_params(("parallel",)),
        name="merge",
    )(oAf, oAb, proj, oB, proj, proj, b_gate, b_gate, x2, gate1, gn, wa, wb, wo)


def _router_kernel(x_ref, sh_ref, sc_ref, g_ref, rw0_ref, rw1_ref, rb_ref, tri_ref,
                   h_ref, idx_ref, gate_ref, rank_ref, cnt_ref, base_ref):
    i = pl.program_id(0)

    @pl.when(i == 0)
    def _():
        base_ref[...] = jnp.zeros_like(base_ref)

    x = x_ref[...]
    ms = jnp.mean(x * x, axis=-1, keepdims=True)
    h = x * lax.rsqrt(ms + EPS) * g_ref[...] * (1.0 + sc_ref[0]) + sh_ref[0]
    h_ref[...] = h
    h0 = h.astype(BF16)
    h1 = (h - h0.astype(F32)).astype(BF16)
    logits = _dot(h0, rw0_ref[...]) + (_dot(h0, rw1_ref[...]) + _dot(h1, rw0_ref[...])) + rb_ref[...]
    lane = lax.broadcasted_iota(jnp.int32, logits.shape, 1)
    lane_f = lane.astype(F32)
    cur = jnp.where(lane < N_EXPERTS, logits, NEG_INF)
    vals, sel = [], []
    for _ in range(TOP_K):
        m = jnp.max(cur, axis=1, keepdims=True)
        ix = jnp.min(jnp.where(cur == m, lane_f, float(LANES)), axis=1, keepdims=True)
        hit = lane_f == ix
        vals.append(m)
        sel.append(hit)
        cur = jnp.where(hit, NEG_INF, cur)
    exps = [jnp.exp(v - vals[0]) for v in vals]
    den = exps[0] + exps[1] + exps[2] + exps[3]
    onehot = jnp.zeros(logits.shape, F32)
    for hit in sel:
        onehot = onehot + jnp.where(hit, 1.0, 0.0)
    before = _dot(tri_ref[...], onehot.astype(BF16)) + base_ref[...]
    idx_out = jnp.zeros(logits.shape, F32)
    gate_out = jnp.zeros(logits.shape, F32)
    rank_out = jnp.zeros(logits.shape, F32)
    for kk in range(TOP_K):
        slot = lane == kk
        e_id = jnp.sum(jnp.where(sel[kk], lane_f, 0.0), axis=1, keepdims=True)
        rk = jnp.sum(jnp.where(sel[kk], before, 0.0), axis=1, keepdims=True)
        idx_out = jnp.where(slot, e_id, idx_out)
        gate_out = jnp.where(slot, exps[kk] / den, gate_out)
        rank_out = jnp.where(slot, rk, rank_out)
    idx_ref[...] = idx_out.astype(jnp.int32)
    gate_ref[...] = gate_out
    rank_ref[...] = rank_out.astype(jnp.int32)
    base_ref[...] = base_ref[...] + jnp.sum(onehot, axis=0, keepdims=True)
    cnt_ref[...] = base_ref[...]


def _router(x1, shift, scale, g, rw0, rw1, rb, S):
    N = x1.shape[0]
    TM = min(512, S)
    tpb = S // TM
    r = jnp.arange(TM)
    tri = (r[None, :] < r[:, None]).astype(BF16)
    row = lambda i: (i, 0)
    const = lambda i: (0, 0)
    lanes = pl.BlockSpec((TM, LANES), row)
    return pl.pallas_call(
        _router_kernel,
        grid=(N // TM,),
        in_specs=[pl.BlockSpec((TM, D_MODEL), row),
                  pl.BlockSpec((1, 1, D_MODEL), lambda i: (i // tpb, 0, 0)),
                  pl.BlockSpec((1, 1, D_MODEL), lambda i: (i // tpb, 0, 0)),
                  pl.BlockSpec((1, D_MODEL), const),
                  pl.BlockSpec((D_MODEL, LANES), const),
                  pl.BlockSpec((D_MODEL, LANES), const),
                  pl.BlockSpec((1, LANES), const),
                  pl.BlockSpec((TM, TM), const)],
        out_specs=[pl.BlockSpec((TM, D_MODEL), row), lanes, lanes, lanes, pl.BlockSpec((1, LANES), const)],
        out_shape=[jax.ShapeDtypeStruct((N, D_MODEL), F32),
                   jax.ShapeDtypeStruct((N, LANES), jnp.int32),
                   jax.ShapeDtypeStruct((N, LANES), F32),
                   jax.ShapeDtypeStruct((N, LANES), jnp.int32),
                   jax.ShapeDtypeStruct((1, LANES), F32)],
        scratch_shapes=[pltpu.VMEM((1, LANES), F32)],
        compiler_params=_params(("arbitrary",)),
        name="router",
    )(x1, shift, scale, g, rw0, rw1, rb, tri)


INDEX_SLICE = ROW_MOVE_TILE * TOP_K


def _row_copy_out(h_ref, xs_hbm, sem, r, dst):
    return pltpu.make_async_copy(h_ref.at[pl.ds(r, 1)], xs_hbm.at[pl.ds(dst, 1)], sem)


def _zero_tile_copy(zero_ref, xs_hbm, sem, start):
    return pltpu.make_async_copy(zero_ref, xs_hbm.at[pl.ds(pl.multiple_of(start, MOE_TILE), MOE_TILE)], sem)


def _dispatch_kernel(pad_end_ref, padded_ref, dest_hbm, h_ref, xs_hbm, idx_smem, zero_ref, sem_idx, sem_rows, sem_zero):
    i = pl.program_id(0)
    fetch = pltpu.make_async_copy(dest_hbm.at[pl.ds(i * INDEX_SLICE, INDEX_SLICE)], idx_smem, sem_idx)
    fetch.start()

    @pl.when(i == 0)
    def _():
        zero_ref[...] = jnp.zeros_like(zero_ref)
        for e in range(N_EXPERTS):
            @pl.when(padded_ref[e] > 0)
            def _():
                _zero_tile_copy(zero_ref, xs_hbm, sem_zero, pad_end_ref[e] - MOE_TILE).start()
        for e in range(N_EXPERTS):
            @pl.when(padded_ref[e] > 0)
            def _():
                _zero_tile_copy(zero_ref, xs_hbm, sem_zero, 0).wait()

    fetch.wait()

    def start(r, carry):
        for kk in range(TOP_K):
            _row_copy_out(h_ref, xs_hbm, sem_rows, r, idx_smem[r * TOP_K + kk]).start()
        return carry

    lax.fori_loop(0, ROW_MOVE_TILE, start, 0)

    def drain(r, carry):
        for kk in range(TOP_K):
            _row_copy_out(h_ref, xs_hbm, sem_rows, 0, 0).wait()
        return carry

    lax.fori_loop(0, ROW_MOVE_TILE, drain, 0)


def _dispatch(pad_ends, padded, dest_flat, h2, n_rows):
    N = h2.shape[0]
    return pl.pallas_call(
        _dispatch_kernel,
        grid_spec=pltpu.PrefetchScalarGridSpec(
            num_scalar_prefetch=2,
            grid=(N // ROW_MOVE_TILE,),
            in_specs=[pl.BlockSpec(memory_space=pl.ANY),
                      pl.BlockSpec((ROW_MOVE_TILE, D_MODEL), lambda i, pe, pd: (i, 0))],
            out_specs=pl.BlockSpec(memory_space=pl.ANY),
            scratch_shapes=[pltpu.SMEM((INDEX_SLICE,), jnp.int32), pltpu.VMEM((MOE_TILE, D_MODEL), F32),
                            pltpu.SemaphoreType.DMA, pltpu.SemaphoreType.DMA, pltpu.SemaphoreType.DMA]),
        out_shape=jax.ShapeDtypeStruct((n_rows, D_MODEL), F32),
        compiler_params=_params(("arbitrary",)),
        name="moe_dispatch",
    )(pad_ends, padded, dest_flat, h2)


def _expert_kernel(te_ref, nu_ref, xs_ref, wg_ref, bg_ref, wl_ref, bl_ref, wd_ref, bd_ref, ys_ref):
    del te_ref

    @pl.when(pl.program_id(0) < nu_ref[0])
    def _():
        xb = xs_ref[...].astype(BF16)
        glu = jnp.minimum(_dot(xb, wg_ref[0]) + bg_ref[0], SWIGLU_LIMIT)
        lin = jnp.clip(_dot(xb, wl_ref[0]) + bl_ref[0], -SWIGLU_LIMIT, SWIGLU_LIMIT)
        act = glu * _sigmoid(SWIGLU_ALPHA * glu) * (lin + 1.0)
        ys_ref[...] = _dot(act.astype(BF16), wd_ref[0]) + bd_ref[0]


def _experts(tile_expert, n_used, xs, wg, bg, wl, bl, wd, bd):
    n_rows = xs.shape[0]
    n_tiles = n_rows // MOE_TILE
    wspec = pl.BlockSpec((1, D_MODEL, D_MODEL), lambda i, te, nu: (te[i], 0, 0))
    bspec = pl.BlockSpec((1, 1, D_MODEL), lambda i, te, nu: (te[i], 0, 0))
    rows = pl.BlockSpec((MOE_TILE, D_MODEL), lambda i, te, nu: (jnp.minimum(i, nu[0] - 1), 0))
    return pl.pallas_call(
        _expert_kernel,
        grid_spec=pltpu.PrefetchScalarGridSpec(
            num_scalar_prefetch=2,
            grid=(n_tiles,),
            in_specs=[rows, wspec, bspec, wspec, bspec, wspec, bspec],
            out_specs=rows),
        out_shape=jax.ShapeDtypeStruct((n_rows, D_MODEL), F32),
        compiler_params=_params(("arbitrary",)),
        name="moe_experts",
    )(tile_expert, n_used, xs, wg, bg, wl, bl, wd, bd)


def _row_copy_in(ys_hbm, buf_ref, sem, src, kk, r):
    return pltpu.make_async_copy(ys_hbm.at[pl.ds(src, 1)], buf_ref.at[kk, pl.ds(r, 1)], sem)


def _combine_kernel(dest_hbm, ys_hbm, gate_ref, x_ref, g2_ref, fg_ref, o_ref, idx_smem, buf_ref, sem_idx, sem_rows):
    i = pl.program_id(0)
    fetch = pltpu.make_async_copy(dest_hbm.at[pl.ds(i * INDEX_SLICE, INDEX_SLICE)], idx_smem, sem_idx)
    fetch.start()
    fetch.wait()

    def start(r, carry):
        for kk in range(TOP_K):
            _row_copy_in(ys_hbm, buf_ref, sem_rows, idx_smem[r * TOP_K + kk], kk, r).start()
        return carry

    lax.fori_loop(0, ROW_MOVE_TILE, start, 0)

    def drain(r, carry):
        for kk in range(TOP_K):
            _row_copy_in(ys_hbm, buf_ref, sem_rows, 0, 0, 0).wait()
        return carry

    lax.fori_loop(0, ROW_MOVE_TILE, drain, 0)

    gates = gate_ref[...]
    moe = gates[:, 0:1] * buf_ref[0]
    for kk in range(1, TOP_K):
        moe = moe + gates[:, kk:kk + 1] * buf_ref[kk]
    x = x_ref[...] + g2_ref[0] * moe
    ms = jnp.mean(x * x, axis=-1, keepdims=True)
    o_ref[...] = x * lax.rsqrt(ms + EPS) * fg_ref[...]


def _combine(dest_flat, ys, gates, x1, gate2, final_g, S):
    N = x1.shape[0]
    TM = ROW_MOVE_TILE
    tpb = S // TM
    row = lambda i: (i, 0)
    return pl.pallas_call(
        _combine_kernel,
        grid=(N // TM,),
        in_specs=[pl.BlockSpec(memory_space=pl.ANY),
                  pl.BlockSpec(memory_space=pl.ANY),
                  pl.BlockSpec((TM, LANES), row),
                  pl.BlockSpec((TM, D_MODEL), row),
                  pl.BlockSpec((1, 1, D_MODEL), lambda i: (i // tpb, 0, 0)),
                  pl.BlockSpec((1, D_MODEL), lambda i: (0, 0))],
        out_specs=pl.BlockSpec((TM, D_MODEL), row),
        out_shape=jax.ShapeDtypeStruct((N, D_MODEL), F32),
        scratch_shapes=[pltpu.SMEM((INDEX_SLICE,), jnp.int32), pltpu.VMEM((TOP_K, TM, D_MODEL), F32),
                        pltpu.SemaphoreType.DMA, pltpu.SemaphoreType.DMA],
        compiler_params=_params(("arbitrary",)),
        name="moe_combine",
    )(dest_flat, ys, gates, x1, gate2, final_g)


def _pad_lanes(a, offset=0):
    return jnp.pad(a, ((0, 0), (offset, LANES - offset - a.shape[1])))


def kernel(x, c, ada_w, ada_b, norm1_g, norm2_g, w_in, b_gate, conv_w, a_log, dt_bias, gdn_norm_g, w_branch_a,
           diff_lambda, diff_norm_g, w_branch_b, w_out, router_w, router_b, w_glu, b_glu, w_lin, b_lin, w_down,
           b_down, final_g):
    B, S, D = x.shape
    N = B * S
    assert D == D_MODEL and S % GROUP == 0 and ada_w.shape[0] == 1
    x2 = x.reshape(N, D)

    mod = _adaln(c, ada_w[0], ada_b[0])
    shift1, scale1, gate1, shift2, scale2, gate2 = [m.reshape(B, 1, D) for m in jnp.split(mod, 6, axis=-1)]

    wi = w_in[0]
    n_a = 4 * D
    n_small = 4 * N_HEADS
    w_big = jnp.concatenate([wi[:, :n_a], wi[:, n_a + n_small:]], axis=1).astype(BF16)
    w_small = _pad_lanes(wi[:, n_a:n_a + n_small]).astype(BF16)
    proj, small = _inproj(x2, shift1, scale1, norm1_g, w_big, w_small, S)
    proj3 = proj.reshape(B, S, N_COL_BLOCKS * D)

    qkv = _gdn_conv(proj3, conv_w[0])
    alog_row = _pad_lanes(a_log[0].reshape(1, -1), GATE_LANE0)
    dt_row = _pad_lanes(dt_bias[0].reshape(1, -1), GATE_LANE0)
    beta, G, Gl, cdl = _gdn_gates(small, alog_row, dt_row)
    r3 = lambda a: a.reshape(B, S, LANES)
    nC = S // GDN_CHUNK
    GT = jnp.transpose(r3(G)[:, :, GATE_LANE0:GATE_LANE0 + 2 * N_HEADS], (0, 2, 1)).reshape(B, 2 * N_HEADS, 1, S)
    cd = r3(cdl).reshape(B, nC, GDN_CHUNK, LANES)[:, :, 0, GATE_LANE0:GATE_LANE0 + 2 * N_HEADS]
    cd = jnp.transpose(cd.reshape(B, nC, 2, N_HEADS), (0, 2, 3, 1)).reshape(-1)
    u, w, qd, att, kdt = _gdn_prep(qkv, r3(beta), r3(G), r3(Gl), GT)
    oAf, oAb = _gdn_scan(cd, u, w, qd, att, kdt)

    half = DIFF_DH // 2
    inv_freq = ROPE_THETA ** (-jnp.arange(half, dtype=F32) / half)
    ang = jnp.arange(S, dtype=F32)[:, None] * inv_freq[None, :]
    cos_t = jnp.tile(jnp.cos(ang), (1, 4))
    sin_h = jnp.sin(ang)
    sin_t = jnp.tile(jnp.concatenate([-sin_h, sin_h], axis=1), (1, 2))
    qt, kr, vt = _rope(proj3, cos_t, sin_t)
    lam_a = jnp.pad(_pad_lanes(diff_lambda[0][0::2]), ((0, 6), (0, 0)))
    lam_b = jnp.pad(_pad_lanes(diff_lambda[0][1::2]), ((0, 6), (0, 0)))
    oB = _diff_attn(qt, kr, vt, lam_a, lam_b, diff_norm_g)

    x1 = _merge(oAf.reshape(N, D), oAb.reshape(N, D), proj, oB.reshape(N, D), b_gate, x2, gate1, gdn_norm_g,
                w_branch_a[0].astype(BF16), w_branch_b[0].astype(BF16), w_out[0].astype(BF16), S)

    rw = _pad_lanes(router_w[0])
    rw0 = rw.astype(BF16)
    rw1 = (rw - rw0.astype(F32)).astype(BF16)
    h2, idx, gates, rank, counts = _router(x1, shift2, scale2, norm2_g, rw0, rw1, _pad_lanes(router_b), S)
    cnt = counts[0, :N_EXPERTS].astype(jnp.int32)
    padded = (cnt + MOE_TILE - 1) // MOE_TILE * MOE_TILE
    pad_ends = jnp.cumsum(padded)
    pad_starts = pad_ends - padded
    dest = (pad_starts[idx[:, :TOP_K]] + rank[:, :TOP_K]).reshape(-1)
    n_tiles = -(-(N * TOP_K) // MOE_TILE) + N_EXPERTS
    tile_start = jnp.arange(n_tiles, dtype=jnp.int32) * MOE_TILE
    tile_expert = jnp.sum((tile_start[:, None] >= pad_ends[None, :]).astype(jnp.int32), axis=1)
    tile_expert = jnp.minimum(tile_expert, N_EXPERTS - 1)
    n_used = (pad_ends[N_EXPERTS - 1:] // MOE_TILE).astype(jnp.int32)
    xs = _dispatch(pad_ends.astype(jnp.int32), padded, dest, h2, n_tiles * MOE_TILE)
    ys = _experts(tile_expert, n_used, xs, w_glu[0].astype(BF16), b_glu[0][:, None, :], w_lin[0].astype(BF16),
                  b_lin[0][:, None, :], w_down[0].astype(BF16), b_down[0][:, None, :])
    out = _combine(dest, ys, gates, x1, gate2, final_g.reshape(1, D), S)
    return out.reshape(B, S, D)
```

```python
import functools
import math

import jax
import jax.numpy as jnp
from jax import lax
from jax.experimental import pallas as pl
from jax.experimental.pallas import tpu as pltpu

F32 = jnp.float32
BF16 = jnp.bfloat16

D_MODEL = 1024
EPS = 1e-6
N_HEADS = 8
HEAD_W = 128
GDN_CHUNK = 64
CONV_WIDTH = 5
DIFF_DH = 64
ROPE_THETA = 10000.0
LAM_INIT = 0.8 - 0.6 * math.exp(-0.3 * 0)
N_EXPERTS = 32
TOP_K = 4
SWIGLU_ALPHA = 1.702
SWIGLU_LIMIT = 7.0

LANES = 128
GROUP = 256
CHUNKS_PER_GROUP = GROUP // GDN_CHUNK
CHUNK_SHIFT = GDN_CHUNK.bit_length() - 1
MOE_TILE = 512
ROW_MOVE_TILE = 1024
NEG_INF = float("-inf")

COL_QA, COL_KA, COL_VA, COL_ZA, COL_QB, COL_KB, COL_VB, COL_GA, COL_GB = range(9)
N_COL_BLOCKS = 9


def _params(sem, vmem_mb=48):
    return pltpu.CompilerParams(dimension_semantics=sem, vmem_limit_bytes=vmem_mb * 1024 * 1024)


def _dot(a, b):
    return jnp.dot(a, b, preferred_element_type=F32)


def _dot_nt(a, b):
    return lax.dot_general(a, b, (((1,), (1,)), ((), ())), preferred_element_type=F32)


def _sigmoid(x):
    return 1.0 / (1.0 + jnp.exp(-x))


def _split3(x):
    a = x.astype(BF16)
    r = x - a.astype(F32)
    b = r.astype(BF16)
    c = (r - b.astype(F32)).astype(BF16)
    return a, b, c


def _adaln_kernel(c_ref, w_ref, b_ref, o_ref):
    c = c_ref[...]
    cond = c * _sigmoid(c)
    c0, c1, c2 = _split3(cond)
    w0, w1, w2 = _split3(w_ref[...])
    acc = _dot(c0, w0) + (_dot(c0, w1) + _dot(c1, w0)) + (_dot(c0, w2) + _dot(c1, w1) + _dot(c2, w0))
    o_ref[...] = acc + b_ref[...]


def _adaln(c, ada_w, ada_b):
    B = c.shape[0]
    n = ada_w.shape[1] // D_MODEL
    return pl.pallas_call(
        _adaln_kernel,
        grid=(n,),
        in_specs=[pl.BlockSpec((B, D_MODEL), lambda j: (0, 0)),
                  pl.BlockSpec((D_MODEL, D_MODEL), lambda j: (0, j)),
                  pl.BlockSpec((1, D_MODEL), lambda j: (0, j))],
        out_specs=pl.BlockSpec((B, D_MODEL), lambda j: (0, j)),
        out_shape=jax.ShapeDtypeStruct((B, n * D_MODEL), F32),
        compiler_params=_params(("parallel",)),
        name="adaln",
    )(c, ada_w, ada_b.reshape(1, -1))


def _inproj_kernel(x_ref, sh_ref, sc_ref, g_ref, w_ref, ws_ref, o_ref, os_ref, h_ref):
    @pl.when(pl.program_id(1) == 0)
    def _():
        x = x_ref[...]
        ms = jnp.mean(x * x, axis=-1, keepdims=True)
        y = x * lax.rsqrt(ms + EPS) * g_ref[...]
        h = (y * (1.0 + sc_ref[0]) + sh_ref[0]).astype(BF16)
        h_ref[...] = h
        os_ref[...] = _dot(h, ws_ref[...])

    o_ref[...] = _dot(h_ref[...], w_ref[...]).astype(o_ref.dtype)


def _inproj(x2, shift, scale, g, w_big, w_small, S):
    N = x2.shape[0]
    TM = min(1024, S)
    tpb = S // TM
    return pl.pallas_call(
        _inproj_kernel,
        grid=(N // TM, N_COL_BLOCKS),
        in_specs=[pl.BlockSpec((TM, D_MODEL), lambda i, j: (i, 0)),
                  pl.BlockSpec((1, 1, D_MODEL), lambda i, j: (i // tpb, 0, 0)),
                  pl.BlockSpec((1, 1, D_MODEL), lambda i, j: (i // tpb, 0, 0)),
                  pl.BlockSpec((1, D_MODEL), lambda i, j: (0, 0)),
                  pl.BlockSpec((D_MODEL, D_MODEL), lambda i, j: (0, j)),
                  pl.BlockSpec((D_MODEL, LANES), lambda i, j: (0, 0))],
        out_specs=[pl.BlockSpec((TM, D_MODEL), lambda i, j: (i, j)),
                   pl.BlockSpec((TM, LANES), lambda i, j: (i, 0))],
        out_shape=[jax.ShapeDtypeStruct((N, N_COL_BLOCKS * D_MODEL), BF16),
                   jax.ShapeDtypeStruct((N, LANES), F32)],
        scratch_shapes=[pltpu.VMEM((TM, D_MODEL), BF16)],
        compiler_params=_params(("parallel", "arbitrary")),
        name="inproj",
    )(x2, shift, scale, g, w_big, w_small)


HALO = 16


def _conv_kernel(cur_ref, prev_ref, next_ref, w_ref, o_ref, ext_ref, *, TR):
    i = pl.program_id(1)
    g = pl.program_id(2)
    last = pl.num_programs(1) - 1
    ext_ref[8:8 + TR, :] = cur_ref[0].astype(F32)
    pv = prev_ref[0].astype(F32)[HALO - 8:HALO]
    nx = next_ref[0].astype(F32)[0:8]
    ext_ref[0:8, :] = jnp.where(i > 0, pv, 0.0)
    ext_ref[TR + 8:TR + 16, :] = jnp.where(i < last, nx, 0.0)
    pad = (CONV_WIDTH - 1) // 2
    acc = ext_ref[8 - pad:8 - pad + TR, :] * w_ref[0:1, :]
    for j in range(1, CONV_WIDTH):
        acc = acc + ext_ref[8 - pad + j:8 - pad + j + TR, :] * w_ref[j:j + 1, :]
    y = acc * _sigmoid(acc)
    ones = jnp.ones((HEAD_W, HEAD_W), BF16)
    qscale = jnp.where(g == 0, HEAD_W ** -0.5, 1.0)
    for h in range(N_HEADS):
        yh = y[:, h * HEAD_W:(h + 1) * HEAD_W]
        ss = _dot((yh * yh).astype(BF16), ones)
        normed = yh * (lax.rsqrt(ss + EPS) * qscale)
        o_ref[0, :, h * HEAD_W:(h + 1) * HEAD_W] = jnp.where(g < 2, normed, yh).astype(o_ref.dtype)


def _gdn_conv(proj3, conv_w):
    B, S, _ = proj3.shape
    TR = min(512, S)
    nT = S // TR
    rb = TR // HALO
    nH = S // HALO
    return pl.pallas_call(
        functools.partial(_conv_kernel, TR=TR),
        grid=(B, nT, 3),
        in_specs=[pl.BlockSpec((1, TR, D_MODEL), lambda b, i, g: (b, i, g)),
                  pl.BlockSpec((1, HALO, D_MODEL), lambda b, i, g: (b, jnp.maximum(i * rb - 1, 0), g)),
                  pl.BlockSpec((1, HALO, D_MODEL), lambda b, i, g: (b, jnp.minimum((i + 1) * rb, nH - 1), g)),
                  pl.BlockSpec((CONV_WIDTH, D_MODEL), lambda b, i, g: (0, g))],
        out_specs=pl.BlockSpec((1, TR, D_MODEL), lambda b, i, g: (b, i, g)),
        out_shape=jax.ShapeDtypeStruct((B, S, 3 * D_MODEL), BF16),
        scratch_shapes=[pltpu.VMEM((TR + 16, D_MODEL), F32)],
        compiler_params=_params(("parallel", "parallel", "parallel")),
        name="gdn_conv",
    )(proj3, proj3, proj3, conv_w)


GATE_LANE0 = 16


def _gates_kernel(x_ref, alog_ref, dt_ref, beta_ref, g_ref, gl_ref, cd_ref):
    x = x_ref[...]
    R = x.shape[0]
    lane = lax.broadcasted_iota(jnp.int32, x.shape, 1)
    beta_ref[...] = _sigmoid(x)
    z = x + dt_ref[...]
    softplus = jnp.maximum(z, 0.0) + jnp.log(1.0 + jnp.exp(-jnp.abs(z)))
    gd = -jnp.exp(alog_ref[...]) * softplus
    gd = jnp.where((lane >= GATE_LANE0) & (lane < GATE_LANE0 + 2 * N_HEADS), gd, 0.0)
    r = lax.broadcasted_iota(jnp.int32, (R, R), 0)
    c = lax.broadcasted_iota(jnp.int32, (R, R), 1)
    same = (r >> CHUNK_SHIFT) == (c >> CHUNK_SHIFT)
    lower = jnp.where(same & (c <= r), 1.0, 0.0).astype(BF16)
    upper = jnp.where(same & (c >= r), 1.0, 0.0).astype(BF16)
    block = jnp.where(same, 1.0, 0.0).astype(BF16)
    p0, p1, p2 = _split3(gd)
    g_fwd = _dot(lower, p0) + _dot(lower, p1) + _dot(lower, p2)
    g_bwd = _dot(upper, p0) + _dot(upper, p1) + _dot(upper, p2)
    tot = _dot(block, p0) + _dot(block, p1) + _dot(block, p2)
    G = jnp.where(lane < GATE_LANE0 + N_HEADS, g_fwd, g_bwd)
    g_ref[...] = G
    gl_ref[...] = tot - G
    cd_ref[...] = jnp.exp(tot)


def _gdn_gates(small, alog_row, dt_row):
    N = small.shape[0]
    spec = pl.BlockSpec((GROUP, LANES), lambda i: (i, 0))
    row = pl.BlockSpec((1, LANES), lambda i: (0, 0))
    return pl.pallas_call(
        _gates_kernel,
        grid=(N // GROUP,),
        in_specs=[spec, row, row],
        out_specs=[spec] * 4,
        out_shape=[jax.ShapeDtypeStruct((N, LANES), F32)] * 4,
        compiler_params=_params(("parallel",)),
        name="gdn_gates",
    )(small, alog_row, dt_row)


PREP_HEADS = 2
(MASK_STRICT_LO, MASK_STRICT_UP, MASK_INCL_LO, MASK_INCL_UP, MASK_EYE, MASK_BLK4, MASK_OFF0) = range(7)
N_MASKS = MASK_OFF0 + (CHUNK_SHIFT - 2)


def _prep_masks():
    r = jnp.arange(GROUP)[:, None]
    c = jnp.arange(GROUP)[None, :]
    same = (r >> CHUNK_SHIFT) == (c >> CHUNK_SHIFT)
    masks = [same & (c < r), same & (c > r), same & (c <= r), same & (c >= r), r == c, (r >> 2) == (c >> 2)]
    for shift in range(2, CHUNK_SHIFT):
        masks.append(((r >> shift) != (c >> shift)) & ((r >> (shift + 1)) == (c >> (shift + 1))))
    return jnp.stack(masks).astype(F32)


def _col(x, l, lane):
    return jnp.broadcast_to(jnp.sum(jnp.where(lane == l, x, 0.0), axis=1, keepdims=True), x.shape)


def _prep_kernel(q_ref, k_ref, v_ref, beta_ref, g_ref, gl_ref, gtf_ref, gtb_ref, mask_ref, bmask_ref,
                 u_ref, w_ref, qd_ref, at_ref, kdt_ref):
    hp = pl.program_id(1)
    lane = lax.broadcasted_iota(jnp.int32, (GROUP, LANES), 1)
    wide = lambda a: jnp.concatenate([a, a], axis=1)
    chains = [(hh, d) for hh in range(PREP_HEADS) for d in range(2)]
    p, rhs = {}, {}
    for hh in range(PREP_HEADS):
        cols = slice(hh * HEAD_W, (hh + 1) * HEAD_W)
        q = q_ref[0, :, cols]
        k = k_ref[0, :, cols]
        qf = q.astype(F32)
        kf = k.astype(F32)
        vf = v_ref[0, :, cols].astype(F32)
        kk = _dot_nt(k, k)
        qk = _dot_nt(q, k)
        for d in range(2):
            lb = d * N_HEADS + hp * PREP_HEADS + hh
            beta_c = _col(beta_ref[0], lb, lane)
            g_c = _col(g_ref[0], GATE_LANE0 + lb, lane)
            eg_c = jnp.exp(g_c)
            egl_c = jnp.exp(_col(gl_ref[0], GATE_LANE0 + lb, lane))
            g_r = (gtf_ref if d == 0 else gtb_ref)[0, hh]
            dec = jnp.exp(jnp.minimum(wide(g_c) - g_r, 0.0))
            p[hh, d] = (-(kk * wide(beta_c)) * dec * mask_ref[MASK_STRICT_LO + d]).astype(BF16)
            att = qk * dec * mask_ref[MASK_INCL_LO + d]
            rhs[hh, d] = jnp.concatenate([vf * beta_c, kf * (beta_c * eg_c)], axis=1).astype(BF16)
            qd_ref[0, d, :, cols] = (qf * eg_c).astype(qd_ref.dtype)
            kdt = (kf * egl_c).T
            for ci in range(CHUNKS_PER_GROUP):
                sl = slice(ci * GDN_CHUNK, (ci + 1) * GDN_CHUNK)
                at_ref[0, d, hh, ci] = att[sl, sl].astype(at_ref.dtype)
                kdt_ref[0, d, hh, ci] = kdt[:, sl].astype(kdt_ref.dtype)
    p4 = {ch: p[ch] * bmask_ref[0] for ch in chains}
    sq = {ch: _dot(p4[ch], p4[ch]).astype(BF16) for ch in chains}
    t = {ch: mask_ref[MASK_EYE] + p4[ch].astype(F32) for ch in chains}
    t = {ch: t[ch] + _dot(t[ch].astype(BF16), sq[ch]) for ch in chains}
    for lvl in range(CHUNK_SHIFT - 2):
        tb = {ch: t[ch].astype(BF16) for ch in chains}
        x = {ch: _dot(tb[ch], p[ch] * bmask_ref[1 + lvl]).astype(BF16) for ch in chains}
        t = {ch: t[ch] + _dot(x[ch], tb[ch]) for ch in chains}
    uw = {ch: _dot(t[ch].astype(BF16), rhs[ch]) for ch in chains}
    for hh, d in chains:
        cols = slice(hh * HEAD_W, (hh + 1) * HEAD_W)
        u_ref[0, d, :, cols] = uw[hh, d][:, :HEAD_W].astype(u_ref.dtype)
        w_ref[0, d, :, cols] = uw[hh, d][:, HEAD_W:].astype(w_ref.dtype)


def _gdn_prep(qkv, beta, G, Gl, GT):
    B, S, _ = qkv.shape
    nG = S // GROUP
    nC = S // GDN_CHUNK
    PW = PREP_HEADS * HEAD_W
    nP = N_HEADS // PREP_HEADS
    sm = pl.BlockSpec((1, GROUP, LANES), lambda b, h, g: (b, g, 0))
    big = pl.BlockSpec((1, 2, GROUP, PW), lambda b, h, g: (b, 0, g, h))
    masks = _prep_masks()
    return pl.pallas_call(
        _prep_kernel,
        grid=(B, nP, nG),
        in_specs=[pl.BlockSpec((1, GROUP, PW), lambda b, h, g: (b, g, h)),
                  pl.BlockSpec((1, GROUP, PW), lambda b, h, g: (b, g, nP + h)),
                  pl.BlockSpec((1, GROUP, PW), lambda b, h, g: (b, g, 2 * nP + h)),
                  sm, sm, sm,
                  pl.BlockSpec((1, PREP_HEADS, 1, GROUP), lambda b, h, g: (b, h, 0, g)),
                  pl.BlockSpec((1, PREP_HEADS, 1, GROUP), lambda b, h, g: (b, nP + h, 0, g)),
                  pl.BlockSpec((MASK_BLK4, GROUP, GROUP), lambda b, h, g: (0, 0, 0)),
                  pl.BlockSpec((N_MASKS - MASK_BLK4, GROUP, GROUP), lambda b, h, g: (0, 0, 0))],
        out_specs=[big, big, big,
                   pl.BlockSpec((1, 2, PREP_HEADS, CHUNKS_PER_GROUP, GDN_CHUNK, GDN_CHUNK),
                                lambda b, h, g: (b, 0, h, g, 0, 0)),
                   pl.BlockSpec((1, 2, PREP_HEADS, CHUNKS_PER_GROUP, HEAD_W, GDN_CHUNK),
                                lambda b, h, g: (b, 0, h, g, 0, 0))],
        out_shape=[jax.ShapeDtypeStruct((B, 2, S, D_MODEL), BF16)] * 3
                  + [jax.ShapeDtypeStruct((B, 2, N_HEADS, nC, GDN_CHUNK, GDN_CHUNK), BF16),
                     jax.ShapeDtypeStruct((B, 2, N_HEADS, nC, HEAD_W, GDN_CHUNK), BF16)],
        compiler_params=_params(("parallel", "parallel", "parallel")),
        name="gdn_prep",
    )(qkv, qkv, qkv, beta, G, Gl, GT, GT, masks[:MASK_BLK4], masks[MASK_BLK4:].astype(BF16))


def _scan_kernel(cd_ref, uf_ref, wf_ref, qdf_ref, atf_ref, kdtf_ref, ub_ref, wb_ref, qdb_ref, atb_ref, kdtb_ref,
                 of_ref, ob_ref, state_ref, *, nc, nC):
    b = pl.program_id(0)
    t = pl.program_id(1)
    nT = pl.num_programs(1)

    @pl.when(t == 0)
    def _():
        state_ref[...] = jnp.zeros_like(state_ref)

    dirs = ((uf_ref, wf_ref, qdf_ref, atf_ref, kdtf_ref, of_ref), (ub_ref, wb_ref, qdb_ref, atb_ref, kdtb_ref, ob_ref))

    def chunk(ci, carry):
        work = []
        for d, refs in enumerate(dirs):
            c = ci if d == 0 else nc - 1 - ci
            tt = t if d == 0 else nT - 1 - t
            row = pl.multiple_of(c * GDN_CHUNK, GDN_CHUNK)
            for h in range(N_HEADS):
                work.append((d, h, c, row, ((b * 2 + d) * N_HEADS + h) * nC + tt * nc + c, refs))
        s_old = [state_ref[d, h] for d, h, *_ in work]
        sb = [s.astype(BF16) for s in s_old]
        tile = lambda ref, row, h: ref[0, 0, pl.ds(row, GDN_CHUNK), h * HEAD_W:(h + 1) * HEAD_W]
        ws = [_dot(tile(refs[1], row, h), sb[i]) for i, (d, h, c, row, gi, refs) in enumerate(work)]
        qs = [_dot(tile(refs[2], row, h), sb[i]) for i, (d, h, c, row, gi, refs) in enumerate(work)]
        vb = [(tile(refs[0], row, h).astype(F32) - ws[i]).astype(BF16)
              for i, (d, h, c, row, gi, refs) in enumerate(work)]
        o = [qs[i] + _dot(refs[3][0, 0, h, c], vb[i]) for i, (d, h, c, row, gi, refs) in enumerate(work)]
        upd = [_dot(refs[4][0, 0, h, c], vb[i]) for i, (d, h, c, row, gi, refs) in enumerate(work)]
        for i, (d, h, c, row, gi, refs) in enumerate(work):
            state_ref[d, h] = s_old[i] * cd_ref[gi] + upd[i]
            refs[5][0, pl.ds(row, GDN_CHUNK), h * HEAD_W:(h + 1) * HEAD_W] = o[i].astype(of_ref.dtype)
        return carry

    lax.fori_loop(0, nc, chunk, 0)


def _gdn_scan(cd, u, w, qd, att, kdt):
    B, _, S, _ = u.shape
    TC = min(512, S)
    nT = S // TC
    nc = TC // GDN_CHUNK
    nC = S // GDN_CHUNK
    fwd = lambda b, t: t
    bwd = lambda b, t: nT - 1 - t

    def specs(d, tm):
        big = pl.BlockSpec((1, 1, TC, D_MODEL), lambda b, t: (b, d, tm(b, t), 0))
        return [big, big, big,
                pl.BlockSpec((1, 1, N_HEADS, nc, GDN_CHUNK, GDN_CHUNK), lambda b, t: (b, d, 0, tm(b, t), 0, 0)),
                pl.BlockSpec((1, 1, N_HEADS, nc, HEAD_W, GDN_CHUNK), lambda b, t: (b, d, 0, tm(b, t), 0, 0))]

    return pl.pallas_call(
        functools.partial(_scan_kernel, nc=nc, nC=nC),
        grid=(B, nT),
        in_specs=[pl.BlockSpec(memory_space=pltpu.SMEM)] + specs(0, fwd) + specs(1, bwd),
        out_specs=[pl.BlockSpec((1, TC, D_MODEL), lambda b, t: (b, t, 0)),
                   pl.BlockSpec((1, TC, D_MODEL), lambda b, t: (b, nT - 1 - t, 0))],
        out_shape=[jax.ShapeDtypeStruct((B, S, D_MODEL), BF16)] * 2,
        scratch_shapes=[pltpu.VMEM((2, N_HEADS, HEAD_W, HEAD_W), F32)],
        compiler_params=_params(("parallel", "arbitrary")),
        name="gdn_scan",
    )(cd, u, w, qd, att, kdt, u, w, qd, att, kdt)


ATT_TQ = 256
ATT_TK = 256
ATT_NQ = 4
ATT_VROWS = HEAD_W + 16
LOG2E = 1.4426950408889634


def _rope_kernel(q_ref, k_ref, v_ref, cos_ref, sin_ref, qt_ref, kr_ref, vt_ref, *, TR):
    cs = cos_ref[...]
    sn = sin_ref[...]
    lane = lax.broadcasted_iota(jnp.int32, cs.shape, 1)
    first_half = (lane & (DIFF_DH - 1)) < (DIFF_DH // 2)
    qscale = DIFF_DH ** -0.5 * LOG2E

    def rot(x):
        partner = jnp.where(first_half, pltpu.roll(x, HEAD_W - DIFF_DH // 2, 1), pltpu.roll(x, DIFF_DH // 2, 1))
        return x * cs + partner * sn

    for h in range(N_HEADS):
        cols = slice(h * HEAD_W, (h + 1) * HEAD_W)
        qr = rot(q_ref[0, :, cols].astype(F32)) * qscale
        kr_ref[0, :, cols] = rot(k_ref[0, :, cols].astype(F32)).astype(kr_ref.dtype)
        vf = v_ref[0, :, cols].astype(F32)
        for ci in range(TR // ATT_TK):
            rows = slice(ci * ATT_TK, (ci + 1) * ATT_TK)
            vt_ref[0, h, ci, 0:HEAD_W, :] = vf[rows].T.astype(vt_ref.dtype)
            vt_ref[0, h, ci, HEAD_W:ATT_VROWS, :] = jnp.ones((ATT_VROWS - HEAD_W, ATT_TK), vt_ref.dtype)
        for ci in range(TR // ATT_TQ):
            rows = slice(ci * ATT_TQ, (ci + 1) * ATT_TQ)
            qt_ref[0, h, ci] = qr[rows].T.astype(qt_ref.dtype)


def _rope(proj3, cos_t, sin_t):
    B, S, _ = proj3.shape
    TR = min(512, S)
    tab = pl.BlockSpec((TR, HEAD_W), lambda b, i: (i, 0))
    col = lambda cb: pl.BlockSpec((1, TR, D_MODEL), lambda b, i: (b, i, cb))
    return pl.pallas_call(
        functools.partial(_rope_kernel, TR=TR),
        grid=(B, S // TR),
        in_specs=[col(COL_QB), col(COL_KB), col(COL_VB), tab, tab],
        out_specs=[pl.BlockSpec((1, N_HEADS, TR // ATT_TQ, HEAD_W, ATT_TQ), lambda b, i: (b, 0, i, 0, 0)),
                   pl.BlockSpec((1, TR, D_MODEL), lambda b, i: (b, i, 0)),
                   pl.BlockSpec((1, N_HEADS, TR // ATT_TK, ATT_VROWS, ATT_TK), lambda b, i: (b, 0, i, 0, 0))],
        out_shape=[jax.ShapeDtypeStruct((B, N_HEADS, S // ATT_TQ, HEAD_W, ATT_TQ), BF16),
                   jax.ShapeDtypeStruct((B, S, D_MODEL), BF16),
                   jax.ShapeDtypeStruct((B, N_HEADS, S // ATT_TK, ATT_VROWS, ATT_TK), BF16)],
        compiler_params=_params(("parallel", "parallel")),
        name="rope",
    )(proj3, proj3, proj3, cos_t, sin_t)


def _attn_kernel(qt_ref, k_ref, vt_ref, la_ref, lb_ref, g_ref, o_ref, s_ref, acc_ref, *, n_chunks):
    row = lax.broadcasted_iota(jnp.int32, (HEAD_W, ATT_TQ), 0)
    qw = []
    for qb in range(ATT_NQ):
        qt = qt_ref[0, 0, qb]
        zero = jnp.zeros_like(qt)
        qw.append((jnp.where(row < DIFF_DH, qt, zero), jnp.where(row >= DIFF_DH, qt, zero)))
    chains = [(qb, comp) for qb in range(ATT_NQ) for comp in range(2)]

    def scores(j):
        kc = k_ref[0, pl.ds(pl.multiple_of(j * ATT_TK, ATT_TK), ATT_TK), :]
        return [_dot(kc, qw[qb][comp]) for qb, comp in chains]

    acc_ref[...] = jnp.zeros_like(acc_ref)
    for (qb, comp), s0 in zip(chains, scores(0)):
        s_ref[qb, comp] = s0

    def chunk(j, carry):
        s_next = scores(jnp.minimum(j + 1, n_chunks - 1))
        vt = vt_ref[0, 0, j]
        out = []
        for ci, (qb, comp) in enumerate(chains):
            m_prev = carry[ci]
            s = s_ref[qb, comp]
            m_new = jnp.maximum(m_prev, jnp.max(s, axis=0, keepdims=True))
            alpha = jnp.exp2(m_prev - m_new)
            p = jnp.exp2(s - m_new)
            out.append(m_new)
            acc_ref[qb, comp] = alpha * acc_ref[qb, comp] + _dot(vt, p.astype(BF16))
        for (qb, comp), sn in zip(chains, s_next):
            s_ref[qb, comp] = sn
        return tuple(out)

    neg = jnp.full((1, ATT_TQ), NEG_INF, F32)
    lax.fori_loop(0, n_chunks, chunk, (neg,) * len(chains), unroll=4)

    sums = jnp.sum(la_ref[...] * lb_ref[...], axis=1, keepdims=True)
    lrow = lax.broadcasted_iota(jnp.int32, sums.shape, 0)
    sign = jnp.where(lrow == 0, 1.0, jnp.where(lrow == 1, -1.0, 0.0))
    lam = jnp.sum(sign * jnp.exp(sums), axis=0, keepdims=True) + LAM_INIT
    for qb in range(ATT_NQ):
        l0 = acc_ref[qb, 0, HEAD_W:HEAD_W + 1, :]
        l1 = acc_ref[qb, 1, HEAD_W:HEAD_W + 1, :]
        ot = acc_ref[qb, 0, 0:HEAD_W, :] / l0 - lam * (acc_ref[qb, 1, 0:HEAD_W, :] / l1)
        ms = jnp.mean(ot * ot, axis=0, keepdims=True)
        y = (ot * lax.rsqrt(ms + EPS)).T * g_ref[...] * (1.0 - LAM_INIT)
        o_ref[0, qb * ATT_TQ:(qb + 1) * ATT_TQ, :] = y.astype(o_ref.dtype)


def _diff_attn(qt, kr, vt, lam_a, lam_b, norm_g):
    B, S, _ = kr.shape
    lam_spec = pl.BlockSpec((8, LANES), lambda b, h, qi: (0, 0))
    return pl.pallas_call(
        functools.partial(_attn_kernel, n_chunks=S // ATT_TK),
        grid=(B, N_HEADS, S // (ATT_NQ * ATT_TQ)),
        in_specs=[pl.BlockSpec((1, 1, ATT_NQ, HEAD_W, ATT_TQ), lambda b, h, qi: (b, h, qi, 0, 0)),
                  pl.BlockSpec((1, S, HEAD_W), lambda b, h, qi: (b, 0, h)),
                  pl.BlockSpec((1, 1, S // ATT_TK, ATT_VROWS, ATT_TK), lambda b, h, qi: (b, h, 0, 0, 0)),
                  lam_spec, lam_spec,
                  pl.BlockSpec((1, HEAD_W), lambda b, h, qi: (0, 0))],
        out_specs=pl.BlockSpec((1, ATT_NQ * ATT_TQ, HEAD_W), lambda b, h, qi: (b, qi, h)),
        out_shape=jax.ShapeDtypeStruct((B, S, D_MODEL), BF16),
        scratch_shapes=[pltpu.VMEM((ATT_NQ, 2, ATT_TK, ATT_TQ), F32), pltpu.VMEM((ATT_NQ, 2, ATT_VROWS, ATT_TQ), F32)],
        compiler_params=_params(("parallel", "parallel", "parallel")),
        name="diff_attn",
    )(qt, kr, vt, lam_a, lam_b, norm_g)


def _merge_kernel(of_ref, ob_ref, z_ref, oB_ref, ga_ref, gb_ref, bga_ref, bgb_ref, x_ref, g1_ref, gn_ref,
                  wa_ref, wb_ref, wo_ref, o_ref, ya_ref):
    oa = of_ref[...].astype(F32) + ob_ref[...].astype(F32)
    z = z_ref[...].astype(F32)
    gate = z * _sigmoid(z)
    for h in range(N_HEADS):
        cols = slice(h * HEAD_W, (h + 1) * HEAD_W)
        oh = oa[:, cols]
        ms = jnp.mean(oh * oh, axis=-1, keepdims=True)
        ya_ref[:, cols] = (oh * lax.rsqrt(ms + EPS) * gn_ref[...] * gate[:, cols]).astype(BF16)
    y_a = _dot(ya_ref[...], wa_ref[...])
    y_b = _dot(oB_ref[...], wb_ref[...])
    gate_a = _sigmoid(ga_ref[...].astype(F32) + bga_ref[...])
    gate_b = _sigmoid(gb_ref[...].astype(F32) + bgb_ref[...])
    mix = _dot((gate_a * y_a + gate_b * y_b).astype(BF16), wo_ref[...])
    o_ref[...] = x_ref[...] + g1_ref[0] * mix


def _merge(oAf, oAb, proj, oB, b_gate, x2, gate1, gn, wa, wb, wo, S):
    N = x2.shape[0]
    TM = min(512, S)
    tpb = S // TM
    row = lambda i: (i, 0)
    full = pl.BlockSpec((D_MODEL, D_MODEL), lambda i: (0, 0))
    return pl.pallas_call(
        _merge_kernel,
        grid=(N // TM,),
        in_specs=[pl.BlockSpec((TM, D_MODEL), row),
                  pl.BlockSpec((TM, D_MODEL), row),
                  pl.BlockSpec((TM, D_MODEL), lambda i: (i, COL_ZA)),
                  pl.BlockSpec((TM, D_MODEL), row),
                  pl.BlockSpec((TM, D_MODEL), lambda i: (i, COL_GA)),
                  pl.BlockSpec((TM, D_MODEL), lambda i: (i, COL_GB)),
                  pl.BlockSpec((1, D_MODEL), lambda i: (0, 0)),
                  pl.BlockSpec((1, D_MODEL), lambda i: (0, 1)),
                  pl.BlockSpec((TM, D_MODEL), row),
                  pl.BlockSpec((1, 1, D_MODEL), lambda i: (i // tpb, 0, 0)),
                  pl.BlockSpec((1, HEAD_W), lambda i: (0, 0)),
                  full, full, full],
        out_specs=pl.BlockSpec((TM, D_MODEL), row),
        out_shape=jax.ShapeDtypeStruct((N, D_MODEL), F32),
        scratch_shapes=[pltpu.VMEM((TM, D_MODEL), BF16)],
        compiler_params=_params(("parallel",)),
        name="merge",
    )(oAf, oAb, proj, oB, proj, proj, b_gate, b_gate, x2, gate1, gn, wa, wb, wo)


def _router_kernel(x_ref, sh_ref, sc_ref, g_ref, rw0_ref, rw1_ref, rb_ref, tri_ref,
                   h_ref, idx_ref, gate_ref, rank_ref, cnt_ref, base_ref):
    i = pl.program_id(0)

    @pl.when(i == 0)
    def _():
        base_ref[...] = jnp.zeros_like(base_ref)

    x = x_ref[...]
    ms = jnp.mean(x * x, axis=-1, keepdims=True)
    h = x * lax.rsqrt(ms + EPS) * g_ref[...] * (1.0 + sc_ref[0]) + sh_ref[0]
    h_ref[...] = h
    h0 = h.astype(BF16)
    h1 = (h - h0.astype(F32)).astype(BF16)
    logits = _dot(h0, rw0_ref[...]) + (_dot(h0, rw1_ref[...]) + _dot(h1, rw0_ref[...])) + rb_ref[...]
    lane = lax.broadcasted_iota(jnp.int32, logits.shape, 1)
    lane_f = lane.astype(F32)
    cur = jnp.where(lane < N_EXPERTS, logits, NEG_INF)
    vals, sel = [], []
    for _ in range(TOP_K):
        m = jnp.max(cur, axis=1, keepdims=True)
        ix = jnp.min(jnp.where(cur == m, lane_f, float(LANES)), axis=1, keepdims=True)
        hit = lane_f == ix
        vals.append(m)
        sel.append(hit)
        cur = jnp.where(hit, NEG_INF, cur)
    exps = [jnp.exp(v - vals[0]) for v in vals]
    den = exps[0] + exps[1] + exps[2] + exps[3]
    onehot = jnp.zeros(logits.shape, F32)
    for hit in sel:
        onehot = onehot + jnp.where(hit, 1.0, 0.0)
    before = _dot(tri_ref[...], onehot.astype(BF16)) + base_ref[...]
    idx_out = jnp.zeros(logits.shape, F32)
    gate_out = jnp.zeros(logits.shape, F32)
    rank_out = jnp.zeros(logits.shape, F32)
    for kk in range(TOP_K):
        slot = lane == kk
        e_id = jnp.sum(jnp.where(sel[kk], lane_f, 0.0), axis=1, keepdims=True)
        rk = jnp.sum(jnp.where(sel[kk], before, 0.0), axis=1, keepdims=True)
        idx_out = jnp.where(slot, e_id, idx_out)
        gate_out = jnp.where(slot, exps[kk] / den, gate_out)
        rank_out = jnp.where(slot, rk, rank_out)
    idx_ref[...] = idx_out.astype(jnp.int32)
    gate_ref[...] = gate_out
    rank_ref[...] = rank_out.astype(jnp.int32)
    base_ref[...] = base_ref[...] + jnp.sum(onehot, axis=0, keepdims=True)
    cnt_ref[...] = base_ref[...]


def _router(x1, shift, scale, g, rw0, rw1, rb, S):
    N = x1.shape[0]
    TM = min(512, S)
    tpb = S // TM
    r = jnp.arange(TM)
    tri = (r[None, :] < r[:, None]).astype(BF16)
    row = lambda i: (i, 0)
    const = lambda i: (0, 0)
    lanes = pl.BlockSpec((TM, LANES), row)
    return pl.pallas_call(
        _router_kernel,
        grid=(N // TM,),
        in_specs=[pl.BlockSpec((TM, D_MODEL), row),
                  pl.BlockSpec((1, 1, D_MODEL), lambda i: (i // tpb, 0, 0)),
                  pl.BlockSpec((1, 1, D_MODEL), lambda i: (i // tpb, 0, 0)),
                  pl.BlockSpec((1, D_MODEL), const),
                  pl.BlockSpec((D_MODEL, LANES), const),
                  pl.BlockSpec((D_MODEL, LANES), const),
                  pl.BlockSpec((1, LANES), const),
                  pl.BlockSpec((TM, TM), const)],
        out_specs=[pl.BlockSpec((TM, D_MODEL), row), lanes, lanes, lanes, pl.BlockSpec((1, LANES), const)],
        out_shape=[jax.ShapeDtypeStruct((N, D_MODEL), F32),
                   jax.ShapeDtypeStruct((N, LANES), jnp.int32),
                   jax.ShapeDtypeStruct((N, LANES), F32),
                   jax.ShapeDtypeStruct((N, LANES), jnp.int32),
                   jax.ShapeDtypeStruct((1, LANES), F32)],
        scratch_shapes=[pltpu.VMEM((1, LANES), F32)],
        compiler_params=_params(("arbitrary",)),
        name="router",
    )(x1, shift, scale, g, rw0, rw1, rb, tri)


INDEX_SLICE = ROW_MOVE_TILE * TOP_K


def _row_copy_out(h_ref, xs_hbm, sem, r, dst):
    return pltpu.make_async_copy(h_ref.at[pl.ds(r, 1)], xs_hbm.at[pl.ds(dst, 1)], sem)


def _zero_tile_copy(zero_ref, xs_hbm, sem, start):
    return pltpu.make_async_copy(zero_ref, xs_hbm.at[pl.ds(pl.multiple_of(start, MOE_TILE), MOE_TILE)], sem)


def _dispatch_kernel(pad_end_ref, padded_ref, dest_hbm, h_ref, xs_hbm, idx_smem, zero_ref, sem_idx, sem_rows, sem_zero):
    i = pl.program_id(0)
    fetch = pltpu.make_async_copy(dest_hbm.at[pl.ds(i * INDEX_SLICE, INDEX_SLICE)], idx_smem, sem_idx)
    fetch.start()

    @pl.when(i == 0)
    def _():
        zero_ref[...] = jnp.zeros_like(zero_ref)
        for e in range(N_EXPERTS):
            @pl.when(padded_ref[e] > 0)
            def _():
                _zero_tile_copy(zero_ref, xs_hbm, sem_zero, pad_end_ref[e] - MOE_TILE).start()
        for e in range(N_EXPERTS):
            @pl.when(padded_ref[e] > 0)
            def _():
                _zero_tile_copy(zero_ref, xs_hbm, sem_zero, 0).wait()

    fetch.wait()

    def start(r, carry):
        for kk in range(TOP_K):
            _row_copy_out(h_ref, xs_hbm, sem_rows, r, idx_smem[r * TOP_K + kk]).start()
        return carry

    lax.fori_loop(0, ROW_MOVE_TILE, start, 0, unroll=4)
    for kk in range(TOP_K):
        pltpu.make_async_copy(h_ref, xs_hbm.at[pl.ds(0, ROW_MOVE_TILE)], sem_rows).wait()


def _dispatch(pad_ends, padded, dest_flat, h2, n_rows):
    N = h2.shape[0]
    return pl.pallas_call(
        _dispatch_kernel,
        grid_spec=pltpu.PrefetchScalarGridSpec(
            num_scalar_prefetch=2,
            grid=(N // ROW_MOVE_TILE,),
            in_specs=[pl.BlockSpec(memory_space=pl.ANY),
                      pl.BlockSpec((ROW_MOVE_TILE, D_MODEL), lambda i, pe, pd: (i, 0))],
            out_specs=pl.BlockSpec(memory_space=pl.ANY),
            scratch_shapes=[pltpu.SMEM((INDEX_SLICE,), jnp.int32), pltpu.VMEM((MOE_TILE, D_MODEL), F32),
                            pltpu.SemaphoreType.DMA, pltpu.SemaphoreType.DMA, pltpu.SemaphoreType.DMA]),
        out_shape=jax.ShapeDtypeStruct((n_rows, D_MODEL), F32),
        compiler_params=_params(("arbitrary",)),
        name="moe_dispatch",
    )(pad_ends, padded, dest_flat, h2)


def _expert_kernel(te_ref, nu_ref, xs_ref, wg_ref, bg_ref, wl_ref, bl_ref, wd_ref, bd_ref, ys_ref):
    del te_ref

    @pl.when(pl.program_id(0) < nu_ref[0])
    def _():
        xb = xs_ref[...].astype(BF16)
        glu = jnp.minimum(_dot(xb, wg_ref[0]) + bg_ref[0], SWIGLU_LIMIT)
        lin = jnp.clip(_dot(xb, wl_ref[0]) + bl_ref[0], -SWIGLU_LIMIT, SWIGLU_LIMIT)
        act = glu * _sigmoid(SWIGLU_ALPHA * glu) * (lin + 1.0)
        ys_ref[...] = _dot(act.astype(BF16), wd_ref[0]) + bd_ref[0]


def _experts(tile_expert, n_used, xs, wg, bg, wl, bl, wd, bd):
    n_rows = xs.shape[0]
    n_tiles = n_rows // MOE_TILE
    wspec = pl.BlockSpec((1, D_MODEL, D_MODEL), lambda i, te, nu: (te[i], 0, 0))
    bspec = pl.BlockSpec((1, 1, D_MODEL), lambda i, te, nu: (te[i], 0, 0))
    rows = pl.BlockSpec((MOE_TILE, D_MODEL), lambda i, te, nu: (jnp.minimum(i, nu[0] - 1), 0))
    return pl.pallas_call(
        _expert_kernel,
        grid_spec=pltpu.PrefetchScalarGridSpec(
            num_scalar_prefetch=2,
            grid=(n_tiles,),
            in_specs=[rows, wspec, bspec, wspec, bspec, wspec, bspec],
            out_specs=rows),
        out_shape=jax.ShapeDtypeStruct((n_rows, D_MODEL), F32),
        compiler_params=_params(("arbitrary",)),
        name="moe_experts",
    )(tile_expert, n_used, xs, wg, bg, wl, bl, wd, bd)


def _row_copy_in(ys_hbm, buf_ref, sem, src, kk, r):
    return pltpu.make_async_copy(ys_hbm.at[pl.ds(src, 1)], buf_ref.at[kk, pl.ds(r, 1)], sem)


def _combine_kernel(dest_hbm, ys_hbm, gate_ref, x_ref, g2_ref, fg_ref, o_ref, idx_smem, buf_ref, sem_idx, sem_rows):
    i = pl.program_id(0)
    fetch = pltpu.make_async_copy(dest_hbm.at[pl.ds(i * INDEX_SLICE, INDEX_SLICE)], idx_smem, sem_idx)
    fetch.start()
    fetch.wait()

    def start(r, carry):
        for kk in range(TOP_K):
            _row_copy_in(ys_hbm, buf_ref, sem_rows, idx_smem[r * TOP_K + kk], kk, r).start()
        return carry

    lax.fori_loop(0, ROW_MOVE_TILE, start, 0, unroll=4)
    for kk in range(TOP_K):
        pltpu.make_async_copy(ys_hbm.at[pl.ds(0, ROW_MOVE_TILE)], buf_ref.at[kk], sem_rows).wait()

    gates = gate_ref[...]
    moe = gates[:, 0:1] * buf_ref[0]
    for kk in range(1, TOP_K):
        moe = moe + gates[:, kk:kk + 1] * buf_ref[kk]
    x = x_ref[...] + g2_ref[0] * moe
    ms = jnp.mean(x * x, axis=-1, keepdims=True)
    o_ref[...] = x * lax.rsqrt(ms + EPS) * fg_ref[...]


def _combine(dest_flat, ys, gates, x1, gate2, final_g, S):
    N = x1.shape[0]
    TM = ROW_MOVE_TILE
    tpb = S // TM
    row = lambda i: (i, 0)
    return pl.pallas_call(
        _combine_kernel,
        grid=(N // TM,),
        in_specs=[pl.BlockSpec(memory_space=pl.ANY),
                  pl.BlockSpec(memory_space=pl.ANY),
                  pl.BlockSpec((TM, LANES), row),
                  pl.BlockSpec((TM, D_MODEL), row),
                  pl.BlockSpec((1, 1, D_MODEL), lambda i: (i // tpb, 0, 0)),
                  pl.BlockSpec((1, D_MODEL), lambda i: (0, 0))],
        out_specs=pl.BlockSpec((TM, D_MODEL), row),
        out_shape=jax.ShapeDtypeStruct((N, D_MODEL), F32),
        scratch_shapes=[pltpu.SMEM((INDEX_SLICE,), jnp.int32), pltpu.VMEM((TOP_K, TM, D_MODEL), F32),
                        pltpu.SemaphoreType.DMA, pltpu.SemaphoreType.DMA],
        compiler_params=_params(("arbitrary",)),
        name="moe_combine",
    )(dest_flat, ys, gates, x1, gate2, final_g)


def _pad_lanes(a, offset=0):
    return jnp.pad(a, ((0, 0), (offset, LANES - offset - a.shape[1])))


def kernel(x, c, ada_w, ada_b, norm1_g, norm2_g, w_in, b_gate, conv_w, a_log, dt_bias, gdn_norm_g, w_branch_a,
           diff_lambda, diff_norm_g, w_branch_b, w_out, router_w, router_b, w_glu, b_glu, w_lin, b_lin, w_down,
           b_down, final_g):
    B, S, D = x.shape
    N = B * S
    assert D == D_MODEL and S % GROUP == 0 and ada_w.shape[0] == 1
    x2 = x.reshape(N, D)

    mod = _adaln(c, ada_w[0], ada_b[0])
    shift1, scale1, gate1, shift2, scale2, gate2 = [m.reshape(B, 1, D) for m in jnp.split(mod, 6, axis=-1)]

    wi = w_in[0]
    n_a = 4 * D
    n_small = 4 * N_HEADS
    w_big = jnp.concatenate([wi[:, :n_a], wi[:, n_a + n_small:]], axis=1).astype(BF16)
    w_small = _pad_lanes(wi[:, n_a:n_a + n_small]).astype(BF16)
    proj, small = _inproj(x2, shift1, scale1, norm1_g, w_big, w_small, S)
    proj3 = proj.reshape(B, S, N_COL_BLOCKS * D)

    qkv = _gdn_conv(proj3, conv_w[0])
    alog_row = _pad_lanes(a_log[0].reshape(1, -1), GATE_LANE0)
    dt_row = _pad_lanes(dt_bias[0].reshape(1, -1), GATE_LANE0)
    beta, G, Gl, cdl = _gdn_gates(small, alog_row, dt_row)
    r3 = lambda a: a.reshape(B, S, LANES)
    nC = S // GDN_CHUNK
    GT = jnp.transpose(r3(G)[:, :, GATE_LANE0:GATE_LANE0 + 2 * N_HEADS], (0, 2, 1)).reshape(B, 2 * N_HEADS, 1, S)
    cd = r3(cdl).reshape(B, nC, GDN_CHUNK, LANES)[:, :, 0, GATE_LANE0:GATE_LANE0 + 2 * N_HEADS]
    cd = jnp.transpose(cd.reshape(B, nC, 2, N_HEADS), (0, 2, 3, 1)).reshape(-1)
    u, w, qd, att, kdt = _gdn_prep(qkv, r3(beta), r3(G), r3(Gl), GT)
    oAf, oAb = _gdn_scan(cd, u, w, qd, att, kdt)

    half = DIFF_DH // 2
    inv_freq = ROPE_THETA ** (-jnp.arange(half, dtype=F32) / half)
    ang = jnp.arange(S, dtype=F32)[:, None] * inv_freq[None, :]
    cos_t = jnp.tile(jnp.cos(ang), (1, 4))
    sin_h = jnp.sin(ang)
    sin_t = jnp.tile(jnp.concatenate([-sin_h, sin_h], axis=1), (1, 2))
    qt, kr, vt = _rope(proj3, cos_t, sin_t)
    lam_a = jnp.pad(_pad_lanes(diff_lambda[0][0::2]), ((0, 6), (0, 0)))
    lam_b = jnp.pad(_pad_lanes(diff_lambda[0][1::2]), ((0, 6), (0, 0)))
    oB = _diff_attn(qt, kr, vt, lam_a, lam_b, diff_norm_g)

    x1 = _merge(oAf.reshape(N, D), oAb.reshape(N, D), proj, oB.reshape(N, D), b_gate, x2, gate1, gdn_norm_g,
                w_branch_a[0].astype(BF16), w_branch_b[0].astype(BF16), w_out[0].astype(BF16), S)

    rw = _pad_lanes(router_w[0])
    rw0 = rw.astype(BF16)
    rw1 = (rw - rw0.astype(F32)).astype(BF16)
    h2, idx, gates, rank, counts = _router(x1, shift2, scale2, norm2_g, rw0, rw1, _pad_lanes(router_b), S)
    cnt = counts[0, :N_EXPERTS].astype(jnp.int32)
    padded = (cnt + MOE_TILE - 1) // MOE_TILE * MOE_TILE
    pad_ends = jnp.cumsum(padded)
    pad_starts = pad_ends - padded
    dest = (pad_starts[idx[:, :TOP_K]] + rank[:, :TOP_K]).reshape(-1)
    n_tiles = -(-(N * TOP_K) // MOE_TILE) + N_EXPERTS
    tile_start = jnp.arange(n_tiles, dtype=jnp.int32) * MOE_TILE
    tile_expert = jnp.sum((tile_start[:, None] >= pad_ends[None, :]).astype(jnp.int32), axis=1)
    tile_expert = jnp.minimum(tile_expert, N_EXPERTS - 1)
    n_used = (pad_ends[N_EXPERTS - 1:] // MOE_TILE).astype(jnp.int32)
    xs = _dispatch(pad_ends.astype(jnp.int32), padded, dest, h2, n_tiles * MOE_TILE)
    ys = _experts(tile_expert, n_used, xs, w_glu[0].astype(BF16), b_glu[0][:, None, :], w_lin[0].astype(BF16),
                  b_lin[0][:, None, :], w_down[0].astype(BF16), b_down[0][:, None, :])
    out = _combine(dest, ys, gates, x1, gate2, final_g.reshape(1, D), S)
    return out.reshape(B, S, D)
```

```python
import functools
import math

import jax
import jax.numpy as jnp
from jax import lax
from jax.experimental import pallas as pl
from jax.experimental.pallas import tpu as pltpu

F32 = jnp.float32
BF16 = jnp.bfloat16

D_MODEL = 1024
EPS = 1e-6
N_HEADS = 8
HEAD_W = 128
GDN_CHUNK = 64
CONV_WIDTH = 5
DIFF_DH = 64
ROPE_THETA = 10000.0
LAM_INIT = 0.8 - 0.6 * math.exp(-0.3 * 0)
N_EXPERTS = 32
TOP_K = 4
SWIGLU_ALPHA = 1.702
SWIGLU_LIMIT = 7.0

LANES = 128
GROUP = 256
CHUNKS_PER_GROUP = GROUP // GDN_CHUNK
CHUNK_SHIFT = GDN_CHUNK.bit_length() - 1
MOE_TILE = 512
ROW_MOVE_TILE = 1024
NEG_INF = float("-inf")

COL_QA, COL_KA, COL_VA, COL_ZA, COL_QB, COL_KB, COL_VB, COL_GA, COL_GB = range(9)
N_COL_BLOCKS = 9


def _params(sem, vmem_mb=48):
    return pltpu.CompilerParams(dimension_semantics=sem, vmem_limit_bytes=vmem_mb * 1024 * 1024)


def _dot(a, b):
    return jnp.dot(a, b, preferred_element_type=F32)


def _dot_nt(a, b):
    return lax.dot_general(a, b, (((1,), (1,)), ((), ())), preferred_element_type=F32)


def _sigmoid(x):
    return 1.0 / (1.0 + jnp.exp(-x))


def _split3(x):
    a = x.astype(BF16)
    r = x - a.astype(F32)
    b = r.astype(BF16)
    c = (r - b.astype(F32)).astype(BF16)
    return a, b, c


def _adaln_kernel(c_ref, w_ref, b_ref, o_ref):
    c = c_ref[...]
    cond = c * _sigmoid(c)
    c0, c1, c2 = _split3(cond)
    w0, w1, w2 = _split3(w_ref[...])
    acc = _dot(c0, w0) + (_dot(c0, w1) + _dot(c1, w0)) + (_dot(c0, w2) + _dot(c1, w1) + _dot(c2, w0))
    o_ref[...] = acc + b_ref[...]


def _adaln(c, ada_w, ada_b):
    B = c.shape[0]
    n = ada_w.shape[1] // D_MODEL
    return pl.pallas_call(
        _adaln_kernel,
        grid=(n,),
        in_specs=[pl.BlockSpec((B, D_MODEL), lambda j: (0, 0)),
                  pl.BlockSpec((D_MODEL, D_MODEL), lambda j: (0, j)),
                  pl.BlockSpec((1, D_MODEL), lambda j: (0, j))],
        out_specs=pl.BlockSpec((B, D_MODEL), lambda j: (0, j)),
        out_shape=jax.ShapeDtypeStruct((B, n * D_MODEL), F32),
        compiler_params=_params(("parallel",)),
        name="adaln",
    )(c, ada_w, ada_b.reshape(1, -1))


def _inproj_kernel(x_ref, sh_ref, sc_ref, g_ref, w_ref, ws_ref, alog_ref, dt_ref,
                   o_ref, beta_ref, gcum_ref, glast_ref, cd_ref, h_ref):
    @pl.when(pl.program_id(1) == 0)
    def _():
        x = x_ref[...]
        ms = jnp.mean(x * x, axis=-1, keepdims=True)
        y = x * lax.rsqrt(ms + EPS) * g_ref[...]
        h = (y * (1.0 + sc_ref[0]) + sh_ref[0]).astype(BF16)
        h_ref[...] = h
        _gdn_gates(_dot(h, ws_ref[...]), alog_ref[...], dt_ref[...], beta_ref, gcum_ref, glast_ref, cd_ref)

    o_ref[...] = _dot(h_ref[...], w_ref[...]).astype(o_ref.dtype)


def _inproj(x2, shift, scale, g, w_big, w_small, alog_row, dt_row, S):
    N = x2.shape[0]
    TM = min(1024, S)
    tpb = S // TM
    const = lambda i, j: (0, 0)
    lanes = pl.BlockSpec((TM, LANES), lambda i, j: (i, 0))
    return pl.pallas_call(
        _inproj_kernel,
        grid=(N // TM, N_COL_BLOCKS),
        in_specs=[pl.BlockSpec((TM, D_MODEL), lambda i, j: (i, 0)),
                  pl.BlockSpec((1, 1, D_MODEL), lambda i, j: (i // tpb, 0, 0)),
                  pl.BlockSpec((1, 1, D_MODEL), lambda i, j: (i // tpb, 0, 0)),
                  pl.BlockSpec((1, D_MODEL), const),
                  pl.BlockSpec((D_MODEL, D_MODEL), lambda i, j: (0, j)),
                  pl.BlockSpec((D_MODEL, LANES), const),
                  pl.BlockSpec((1, LANES), const),
                  pl.BlockSpec((1, LANES), const)],
        out_specs=[pl.BlockSpec((TM, D_MODEL), lambda i, j: (i, j)), lanes, lanes, lanes, lanes],
        out_shape=[jax.ShapeDtypeStruct((N, N_COL_BLOCKS * D_MODEL), BF16)]
                  + [jax.ShapeDtypeStruct((N, LANES), F32)] * 4,
        scratch_shapes=[pltpu.VMEM((TM, D_MODEL), BF16)],
        compiler_params=_params(("parallel", "arbitrary")),
        name="inproj",
    )(x2, shift, scale, g, w_big, w_small, alog_row, dt_row)


HALO = 16


def _conv_kernel(cur_ref, prev_ref, next_ref, w_ref, o_ref, ext_ref, *, TR):
    i = pl.program_id(1)
    g = pl.program_id(2)
    last = pl.num_programs(1) - 1
    ext_ref[8:8 + TR, :] = cur_ref[0].astype(F32)
    pv = prev_ref[0].astype(F32)[HALO - 8:HALO]
    nx = next_ref[0].astype(F32)[0:8]
    ext_ref[0:8, :] = jnp.where(i > 0, pv, 0.0)
    ext_ref[TR + 8:TR + 16, :] = jnp.where(i < last, nx, 0.0)
    pad = (CONV_WIDTH - 1) // 2
    acc = ext_ref[8 - pad:8 - pad + TR, :] * w_ref[0:1, :]
    for j in range(1, CONV_WIDTH):
        acc = acc + ext_ref[8 - pad + j:8 - pad + j + TR, :] * w_ref[j:j + 1, :]
    y = acc * _sigmoid(acc)

    @pl.when(g < 2)
    def _():
        ones = jnp.ones((HEAD_W, HEAD_W), BF16)
        qscale = jnp.where(g == 0, HEAD_W ** -0.5, 1.0)
        for h in range(N_HEADS):
            yh = y[:, h * HEAD_W:(h + 1) * HEAD_W]
            ss = _dot((yh * yh).astype(BF16), ones)
            o_ref[0, :, h * HEAD_W:(h + 1) * HEAD_W] = (yh * (lax.rsqrt(ss + EPS) * qscale)).astype(o_ref.dtype)

    @pl.when(g == 2)
    def _():
        o_ref[0] = y.astype(o_ref.dtype)


def _gdn_conv(proj3, conv_w):
    B, S, _ = proj3.shape
    TR = min(512, S)
    nT = S // TR
    rb = TR // HALO
    nH = S // HALO
    return pl.pallas_call(
        functools.partial(_conv_kernel, TR=TR),
        grid=(B, nT, 3),
        in_specs=[pl.BlockSpec((1, TR, D_MODEL), lambda b, i, g: (b, i, g)),
                  pl.BlockSpec((1, HALO, D_MODEL), lambda b, i, g: (b, jnp.maximum(i * rb - 1, 0), g)),
                  pl.BlockSpec((1, HALO, D_MODEL), lambda b, i, g: (b, jnp.minimum((i + 1) * rb, nH - 1), g)),
                  pl.BlockSpec((CONV_WIDTH, D_MODEL), lambda b, i, g: (0, g))],
        out_specs=pl.BlockSpec((1, TR, D_MODEL), lambda b, i, g: (b, i, g)),
        out_shape=jax.ShapeDtypeStruct((B, S, 3 * D_MODEL), BF16),
        scratch_shapes=[pltpu.VMEM((TR + 16, D_MODEL), F32)],
        compiler_params=_params(("parallel", "parallel", "parallel")),
        name="gdn_conv",
    )(proj3, proj3, proj3, conv_w)


GATE_LANE0 = 16


def _gdn_gates(x, alog, dt, beta_ref, g_ref, gl_ref, cd_ref):
    lane = lax.broadcasted_iota(jnp.int32, (GROUP, LANES), 1)
    r = lax.broadcasted_iota(jnp.int32, (GROUP, GROUP), 0)
    c = lax.broadcasted_iota(jnp.int32, (GROUP, GROUP), 1)
    same = (r >> CHUNK_SHIFT) == (c >> CHUNK_SHIFT)
    lower = jnp.where(same & (c <= r), 1.0, 0.0).astype(BF16)
    upper = jnp.where(same & (c >= r), 1.0, 0.0).astype(BF16)
    block = jnp.where(same, 1.0, 0.0).astype(BF16)
    beta_ref[...] = _sigmoid(x)
    for gi in range(x.shape[0] // GROUP):
        rows = slice(gi * GROUP, (gi + 1) * GROUP)
        z = x[rows] + dt
        softplus = jnp.maximum(z, 0.0) + jnp.log(1.0 + jnp.exp(-jnp.abs(z)))
        gd = -jnp.exp(alog) * softplus
        gd = jnp.where((lane >= GATE_LANE0) & (lane < GATE_LANE0 + 2 * N_HEADS), gd, 0.0)
        p0, p1, p2 = _split3(gd)
        g_fwd = _dot(lower, p0) + _dot(lower, p1) + _dot(lower, p2)
        g_bwd = _dot(upper, p0) + _dot(upper, p1) + _dot(upper, p2)
        tot = _dot(block, p0) + _dot(block, p1) + _dot(block, p2)
        G = jnp.where(lane < GATE_LANE0 + N_HEADS, g_fwd, g_bwd)
        g_ref[rows, :] = G
        gl_ref[rows, :] = tot - G
        cd_ref[rows, :] = jnp.exp(tot)


PREP_HEADS = 2
(MASK_STRICT_LO, MASK_STRICT_UP, MASK_INCL_LO, MASK_INCL_UP, MASK_EYE, MASK_BLK4, MASK_OFF0) = range(7)
N_MASKS = MASK_OFF0 + (CHUNK_SHIFT - 2)


def _prep_masks():
    r = jnp.arange(GROUP)[:, None]
    c = jnp.arange(GROUP)[None, :]
    same = (r >> CHUNK_SHIFT) == (c >> CHUNK_SHIFT)
    masks = [same & (c < r), same & (c > r), same & (c <= r), same & (c >= r), r == c, (r >> 2) == (c >> 2)]
    for shift in range(2, CHUNK_SHIFT):
        masks.append(((r >> shift) != (c >> shift)) & ((r >> (shift + 1)) == (c >> (shift + 1))))
    return jnp.stack(masks).astype(F32)


def _col(x, l, lane):
    return jnp.broadcast_to(jnp.sum(jnp.where(lane == l, x, 0.0), axis=1, keepdims=True), x.shape)


def _prep_kernel(q_ref, k_ref, v_ref, beta_ref, g_ref, gl_ref, gtf_ref, gtb_ref, mask_ref, bmask_ref,
                 u_ref, w_ref, qd_ref, at_ref, kdt_ref):
    hp = pl.program_id(1)
    lane = lax.broadcasted_iota(jnp.int32, (GROUP, LANES), 1)
    wide = lambda a: jnp.concatenate([a, a], axis=1)
    chains = [(hh, d) for hh in range(PREP_HEADS) for d in range(2)]
    p, rhs = {}, {}
    for hh in range(PREP_HEADS):
        cols = slice(hh * HEAD_W, (hh + 1) * HEAD_W)
        q = q_ref[0, :, cols]
        k = k_ref[0, :, cols]
        qf = q.astype(F32)
        kf = k.astype(F32)
        vf = v_ref[0, :, cols].astype(F32)
        kk = _dot_nt(k, k)
        qk = _dot_nt(q, k)
        for d in range(2):
            lb = d * N_HEADS + hp * PREP_HEADS + hh
            beta_c = _col(beta_ref[0], lb, lane)
            g_c = _col(g_ref[0], GATE_LANE0 + lb, lane)
            eg_c = jnp.exp(g_c)
            egl_c = jnp.exp(_col(gl_ref[0], GATE_LANE0 + lb, lane))
            g_r = (gtf_ref if d == 0 else gtb_ref)[0, hh]
            dec = jnp.exp(jnp.minimum(wide(g_c) - g_r, 0.0))
            p[hh, d] = (-(kk * wide(beta_c)) * dec * mask_ref[MASK_STRICT_LO + d]).astype(BF16)
            att = qk * dec * mask_ref[MASK_INCL_LO + d]
            rhs[hh, d] = jnp.concatenate([vf * beta_c, kf * (beta_c * eg_c)], axis=1).astype(BF16)
            qd_ref[0, d, :, cols] = (qf * eg_c).astype(qd_ref.dtype)
            kdt = (kf * egl_c).T
            for ci in range(CHUNKS_PER_GROUP):
                sl = slice(ci * GDN_CHUNK, (ci + 1) * GDN_CHUNK)
                at_ref[0, d, hh, ci] = att[sl, sl].astype(at_ref.dtype)
                kdt_ref[0, d, hh, ci] = kdt[:, sl].astype(kdt_ref.dtype)
    p4 = {ch: p[ch] * bmask_ref[0] for ch in chains}
    sq = {ch: _dot(p4[ch], p4[ch]).astype(BF16) for ch in chains}
    t = {ch: mask_ref[MASK_EYE] + p4[ch].astype(F32) for ch in chains}
    t = {ch: t[ch] + _dot(t[ch].astype(BF16), sq[ch]) for ch in chains}
    for lvl in range(CHUNK_SHIFT - 2):
        tb = {ch: t[ch].astype(BF16) for ch in chains}
        x = {ch: _dot(tb[ch], p[ch] * bmask_ref[1 + lvl]).astype(BF16) for ch in chains}
        t = {ch: t[ch] + _dot(x[ch], tb[ch]) for ch in chains}
    uw = {ch: _dot(t[ch].astype(BF16), rhs[ch]) for ch in chains}
    for hh, d in chains:
        cols = slice(hh * HEAD_W, (hh + 1) * HEAD_W)
        u_ref[0, d, :, cols] = uw[hh, d][:, :HEAD_W].astype(u_ref.dtype)
        w_ref[0, d, :, cols] = uw[hh, d][:, HEAD_W:].astype(w_ref.dtype)


def _gdn_prep(qkv, beta, G, Gl, GT):
    B, S, _ = qkv.shape
    nG = S // GROUP
    nC = S // GDN_CHUNK
    PW = PREP_HEADS * HEAD_W
    nP = N_HEADS // PREP_HEADS
    sm = pl.BlockSpec((1, GROUP, LANES), lambda b, h, g: (b, g, 0))
    big = pl.BlockSpec((1, 2, GROUP, PW), lambda b, h, g: (b, 0, g, h))
    masks = _prep_masks()
    return pl.pallas_call(
        _prep_kernel,
        grid=(B, nP, nG),
        in_specs=[pl.BlockSpec((1, GROUP, PW), lambda b, h, g: (b, g, h)),
                  pl.BlockSpec((1, GROUP, PW), lambda b, h, g: (b, g, nP + h)),
                  pl.BlockSpec((1, GROUP, PW), lambda b, h, g: (b, g, 2 * nP + h)),
                  sm, sm, sm,
                  pl.BlockSpec((1, PREP_HEADS, 1, GROUP), lambda b, h, g: (b, h, 0, g)),
                  pl.BlockSpec((1, PREP_HEADS, 1, GROUP), lambda b, h, g: (b, nP + h, 0, g)),
                  pl.BlockSpec((MASK_BLK4, GROUP, GROUP), lambda b, h, g: (0, 0, 0)),
                  pl.BlockSpec((N_MASKS - MASK_BLK4, GROUP, GROUP), lambda b, h, g: (0, 0, 0))],
        out_specs=[big, big, big,
                   pl.BlockSpec((1, 2, PREP_HEADS, CHUNKS_PER_GROUP, GDN_CHUNK, GDN_CHUNK),
                                lambda b, h, g: (b, 0, h, g, 0, 0)),
                   pl.BlockSpec((1, 2, PREP_HEADS, CHUNKS_PER_GROUP, HEAD_W, GDN_CHUNK),
                                lambda b, h, g: (b, 0, h, g, 0, 0))],
        out_shape=[jax.ShapeDtypeStruct((B, 2, S, D_MODEL), BF16)] * 3
                  + [jax.ShapeDtypeStruct((B, 2, N_HEADS, nC, GDN_CHUNK, GDN_CHUNK), BF16),
                     jax.ShapeDtypeStruct((B, 2, N_HEADS, nC, HEAD_W, GDN_CHUNK), BF16)],
        compiler_params=_params(("parallel", "parallel", "parallel")),
        name="gdn_prep",
    )(qkv, qkv, qkv, beta, G, Gl, GT, GT, masks[:MASK_BLK4], masks[MASK_BLK4:].astype(BF16))


def _scan_kernel(cd_ref, uf_ref, wf_ref, qdf_ref, atf_ref, kdtf_ref, ub_ref, wb_ref, qdb_ref, atb_ref, kdtb_ref,
                 of_ref, ob_ref, state_ref, *, nc, nC):
    b = pl.program_id(0)
    t = pl.program_id(1)
    nT = pl.num_programs(1)

    @pl.when(t == 0)
    def _():
        state_ref[...] = jnp.zeros_like(state_ref)

    dirs = ((uf_ref, wf_ref, qdf_ref, atf_ref, kdtf_ref, of_ref), (ub_ref, wb_ref, qdb_ref, atb_ref, kdtb_ref, ob_ref))

    def chunk(ci, carry):
        work = []
        for d, refs in enumerate(dirs):
            c = ci if d == 0 else nc - 1 - ci
            tt = t if d == 0 else nT - 1 - t
            row = pl.multiple_of(c * GDN_CHUNK, GDN_CHUNK)
            for h in range(N_HEADS):
                work.append((d, h, c, row, ((b * 2 + d) * N_HEADS + h) * nC + tt * nc + c, refs))
        s_old = [state_ref[d, h] for d, h, *_ in work]
        sb = [s.astype(BF16) for s in s_old]
        tile = lambda ref, row, h: ref[0, 0, pl.ds(row, GDN_CHUNK), h * HEAD_W:(h + 1) * HEAD_W]
        ws = [_dot(tile(refs[1], row, h), sb[i]) for i, (d, h, c, row, gi, refs) in enumerate(work)]
        qs = [_dot(tile(refs[2], row, h), sb[i]) for i, (d, h, c, row, gi, refs) in enumerate(work)]
        vb = [(tile(refs[0], row, h).astype(F32) - ws[i]).astype(BF16)
              for i, (d, h, c, row, gi, refs) in enumerate(work)]
        o = [qs[i] + _dot(refs[3][0, 0, h, c], vb[i]) for i, (d, h, c, row, gi, refs) in enumerate(work)]
        upd = [_dot(refs[4][0, 0, h, c], vb[i]) for i, (d, h, c, row, gi, refs) in enumerate(work)]
        for i, (d, h, c, row, gi, refs) in enumerate(work):
            state_ref[d, h] = s_old[i] * cd_ref[gi] + upd[i]
            refs[5][0, pl.ds(row, GDN_CHUNK), h * HEAD_W:(h + 1) * HEAD_W] = o[i].astype(of_ref.dtype)
        return carry

    lax.fori_loop(0, nc, chunk, 0)


def _gdn_scan(cd, u, w, qd, att, kdt):
    B, _, S, _ = u.shape
    TC = min(512, S)
    nT = S // TC
    nc = TC // GDN_CHUNK
    nC = S // GDN_CHUNK
    fwd = lambda b, t: t
    bwd = lambda b, t: nT - 1 - t

    def specs(d, tm):
        big = pl.BlockSpec((1, 1, TC, D_MODEL), lambda b, t: (b, d, tm(b, t), 0))
        return [big, big, big,
                pl.BlockSpec((1, 1, N_HEADS, nc, GDN_CHUNK, GDN_CHUNK), lambda b, t: (b, d, 0, tm(b, t), 0, 0)),
                pl.BlockSpec((1, 1, N_HEADS, nc, HEAD_W, GDN_CHUNK), lambda b, t: (b, d, 0, tm(b, t), 0, 0))]

    return pl.pallas_call(
        functools.partial(_scan_kernel, nc=nc, nC=nC),
        grid=(B, nT),
        in_specs=[pl.BlockSpec(memory_space=pltpu.SMEM)] + specs(0, fwd) + specs(1, bwd),
        out_specs=[pl.BlockSpec((1, TC, D_MODEL), lambda b, t: (b, t, 0)),
                   pl.BlockSpec((1, TC, D_MODEL), lambda b, t: (b, nT - 1 - t, 0))],
        out_shape=[jax.ShapeDtypeStruct((B, S, D_MODEL), BF16)] * 2,
        scratch_shapes=[pltpu.VMEM((2, N_HEADS, HEAD_W, HEAD_W), F32)],
        compiler_params=_params(("parallel", "arbitrary")),
        name="gdn_scan",
    )(cd, u, w, qd, att, kdt, u, w, qd, att, kdt)


ATT_TQ = 256
ATT_TK = 256
ATT_NQ = 4
ATT_VROWS = HEAD_W + 16
LOG2E = 1.4426950408889634


def _rope_kernel(q_ref, k_ref, v_ref, cos_ref, sin_ref, qt_ref, kr_ref, vt_ref, *, TR):
    cs = cos_ref[...]
    sn = sin_ref[...]
    lane = lax.broadcasted_iota(jnp.int32, cs.shape, 1)
    first_half = (lane & (DIFF_DH - 1)) < (DIFF_DH // 2)
    qscale = DIFF_DH ** -0.5 * LOG2E

    def rot(x):
        partner = jnp.where(first_half, pltpu.roll(x, HEAD_W - DIFF_DH // 2, 1), pltpu.roll(x, DIFF_DH // 2, 1))
        return x * cs + partner * sn

    for h in range(N_HEADS):
        cols = slice(h * HEAD_W, (h + 1) * HEAD_W)
        qr = rot(q_ref[0, :, cols].astype(F32)) * qscale
        kr_ref[0, :, cols] = rot(k_ref[0, :, cols].astype(F32)).astype(kr_ref.dtype)
        vf = v_ref[0, :, cols].astype(F32)
        for ci in range(TR // ATT_TK):
            rows = slice(ci * ATT_TK, (ci + 1) * ATT_TK)
            vt_ref[0, h, ci, 0:HEAD_W, :] = vf[rows].T.astype(vt_ref.dtype)
            vt_ref[0, h, ci, HEAD_W:ATT_VROWS, :] = jnp.ones((ATT_VROWS - HEAD_W, ATT_TK), vt_ref.dtype)
        for ci in range(TR // ATT_TQ):
            rows = slice(ci * ATT_TQ, (ci + 1) * ATT_TQ)
            qt_ref[0, h, ci] = qr[rows].T.astype(qt_ref.dtype)


def _rope(proj3, cos_t, sin_t):
    B, S, _ = proj3.shape
    TR = min(512, S)
    tab = pl.BlockSpec((TR, HEAD_W), lambda b, i: (i, 0))
    col = lambda cb: pl.BlockSpec((1, TR, D_MODEL), lambda b, i: (b, i, cb))
    return pl.pallas_call(
        functools.partial(_rope_kernel, TR=TR),
        grid=(B, S // TR),
        in_specs=[col(COL_QB), col(COL_KB), col(COL_VB), tab, tab],
        out_specs=[pl.BlockSpec((1, N_HEADS, TR // ATT_TQ, HEAD_W, ATT_TQ), lambda b, i: (b, 0, i, 0, 0)),
                   pl.BlockSpec((1, TR, D_MODEL), lambda b, i: (b, i, 0)),
                   pl.BlockSpec((1, N_HEADS, TR // ATT_TK, ATT_VROWS, ATT_TK), lambda b, i: (b, 0, i, 0, 0))],
        out_shape=[jax.ShapeDtypeStruct((B, N_HEADS, S // ATT_TQ, HEAD_W, ATT_TQ), BF16),
                   jax.ShapeDtypeStruct((B, S, D_MODEL), BF16),
                   jax.ShapeDtypeStruct((B, N_HEADS, S // ATT_TK, ATT_VROWS, ATT_TK), BF16)],
        compiler_params=_params(("parallel", "parallel")),
        name="rope",
    )(proj3, proj3, proj3, cos_t, sin_t)


def _attn_kernel(qt_ref, k_ref, vt_ref, la_ref, lb_ref, g_ref, o_ref, s_ref, acc_ref, *, n_chunks):
    row = lax.broadcasted_iota(jnp.int32, (HEAD_W, ATT_TQ), 0)
    qw = []
    for qb in range(ATT_NQ):
        qt = qt_ref[0, 0, qb]
        zero = jnp.zeros_like(qt)
        qw.append((jnp.where(row < DIFF_DH, qt, zero), jnp.where(row >= DIFF_DH, qt, zero)))
    chains = [(qb, comp) for qb in range(ATT_NQ) for comp in range(2)]

    def scores(j):
        kc = k_ref[0, pl.ds(pl.multiple_of(j * ATT_TK, ATT_TK), ATT_TK), :]
        return [_dot(kc, qw[qb][comp]) for qb, comp in chains]

    acc_ref[...] = jnp.zeros_like(acc_ref)
    for (qb, comp), s0 in zip(chains, scores(0)):
        s_ref[qb, comp] = s0

    def chunk(j, carry):
        s_next = scores(jnp.minimum(j + 1, n_chunks - 1))
        vt = vt_ref[0, 0, j]
        out = []
        for ci, (qb, comp) in enumerate(chains):
            m_prev = carry[ci]
            s = s_ref[qb, comp]
            m_new = jnp.maximum(m_prev, jnp.max(s, axis=0, keepdims=True))
            alpha = jnp.exp2(m_prev - m_new)
            p = jnp.exp2(s - m_new)
            out.append(m_new)
            acc_ref[qb, comp] = alpha * acc_ref[qb, comp] + _dot(vt, p.astype(BF16))
        for (qb, comp), sn in zip(chains, s_next):
            s_ref[qb, comp] = sn
        return tuple(out)

    neg = jnp.full((1, ATT_TQ), NEG_INF, F32)
    lax.fori_loop(0, n_chunks, chunk, (neg,) * len(chains), unroll=8)

    sums = jnp.sum(la_ref[...] * lb_ref[...], axis=1, keepdims=True)
    lrow = lax.broadcasted_iota(jnp.int32, sums.shape, 0)
    sign = jnp.where(lrow == 0, 1.0, jnp.where(lrow == 1, -1.0, 0.0))
    lam = jnp.sum(sign * jnp.exp(sums), axis=0, keepdims=True) + LAM_INIT
    for qb in range(ATT_NQ):
        l0 = acc_ref[qb, 0, HEAD_W:HEAD_W + 1, :]
        l1 = acc_ref[qb, 1, HEAD_W:HEAD_W + 1, :]
        ot = acc_ref[qb, 0, 0:HEAD_W, :] / l0 - lam * (acc_ref[qb, 1, 0:HEAD_W, :] / l1)
        ms = jnp.mean(ot * ot, axis=0, keepdims=True)
        y = (ot * lax.rsqrt(ms + EPS)).T * g_ref[...] * (1.0 - LAM_INIT)
        o_ref[0, qb * ATT_TQ:(qb + 1) * ATT_TQ, :] = y.astype(o_ref.dtype)


def _diff_attn(qt, kr, vt, lam_a, lam_b, norm_g):
    B, S, _ = kr.shape
    lam_spec = pl.BlockSpec((8, LANES), lambda b, h, qi: (0, 0))
    return pl.pallas_call(
        functools.partial(_attn_kernel, n_chunks=S // ATT_TK),
        grid=(B, N_HEADS, S // (ATT_NQ * ATT_TQ)),
        in_specs=[pl.BlockSpec((1, 1, ATT_NQ, HEAD_W, ATT_TQ), lambda b, h, qi: (b, h, qi, 0, 0)),
                  pl.BlockSpec((1, S, HEAD_W), lambda b, h, qi: (b, 0, h)),
                  pl.BlockSpec((1, 1, S // ATT_TK, ATT_VROWS, ATT_TK), lambda b, h, qi: (b, h, 0, 0, 0)),
                  lam_spec, lam_spec,
                  pl.BlockSpec((1, HEAD_W), lambda b, h, qi: (0, 0))],
        out_specs=pl.BlockSpec((1, ATT_NQ * ATT_TQ, HEAD_W), lambda b, h, qi: (b, qi, h)),
        out_shape=jax.ShapeDtypeStruct((B, S, D_MODEL), BF16),
        scratch_shapes=[pltpu.VMEM((ATT_NQ, 2, ATT_TK, ATT_TQ), F32), pltpu.VMEM((ATT_NQ, 2, ATT_VROWS, ATT_TQ), F32)],
        compiler_params=_params(("parallel", "parallel", "parallel")),
        name="diff_attn",
    )(qt, kr, vt, lam_a, lam_b, norm_g)


def _merge_kernel(of_ref, ob_ref, z_ref, oB_ref, ga_ref, gb_ref, bga_ref, bgb_ref, x_ref, g1_ref, gn_ref,
                  wa_ref, wb_ref, wo_ref, o_ref, ya_ref):
    oa = of_ref[...].astype(F32) + ob_ref[...].astype(F32)
    z = z_ref[...].astype(F32)
    gate = z * _sigmoid(z)
    for h in range(N_HEADS):
        cols = slice(h * HEAD_W, (h + 1) * HEAD_W)
        oh = oa[:, cols]
        ms = jnp.mean(oh * oh, axis=-1, keepdims=True)
        ya_ref[:, cols] = (oh * lax.rsqrt(ms + EPS) * gn_ref[...] * gate[:, cols]).astype(BF16)
    y_a = _dot(ya_ref[...], wa_ref[...])
    y_b = _dot(oB_ref[...], wb_ref[...])
    gate_a = _sigmoid(ga_ref[...].astype(F32) + bga_ref[...])
    gate_b = _sigmoid(gb_ref[...].astype(F32) + bgb_ref[...])
    mix = _dot((gate_a * y_a + gate_b * y_b).astype(BF16), wo_ref[...])
    o_ref[...] = x_ref[...] + g1_ref[0] * mix


def _merge(oAf, oAb, proj, oB, b_gate, x2, gate1, gn, wa, wb, wo, S):
    N = x2.shape[0]
    TM = min(512, S)
    tpb = S // TM
    row = lambda i: (i, 0)
    full = pl.BlockSpec((D_MODEL, D_MODEL), lambda i: (0, 0))
    return pl.pallas_call(
        _merge_kernel,
        grid=(N // TM,),
        in_specs=[pl.BlockSpec((TM, D_MODEL), row),
                  pl.BlockSpec((TM, D_MODEL), row),
                  pl.BlockSpec((TM, D_MODEL), lambda i: (i, COL_ZA)),
                  pl.BlockSpec((TM, D_MODEL), row),
                  pl.BlockSpec((TM, D_MODEL), lambda i: (i, COL_GA)),
                  pl.BlockSpec((TM, D_MODEL), lambda i: (i, COL_GB)),
                  pl.BlockSpec((1, D_MODEL), lambda i: (0, 0)),
                  pl.BlockSpec((1, D_MODEL), lambda i: (0, 1)),
                  pl.BlockSpec((TM, D_MODEL), row),
                  pl.BlockSpec((1, 1, D_MODEL), lambda i: (i // tpb, 0, 0)),
                  pl.BlockSpec((1, HEAD_W), lambda i: (0, 0)),
                  full, full, full],
        out_specs=pl.BlockSpec((TM, D_MODEL), row),
        out_shape=jax.ShapeDtypeStruct((N, D_MODEL), F32),
        scratch_shapes=[pltpu.VMEM((TM, D_MODEL), BF16)],
        compiler_params=_params(("parallel",)),
        name="merge",
    )(oAf, oAb, proj, oB, proj, proj, b_gate, b_gate, x2, gate1, gn, wa, wb, wo)


def _router_kernel(x_ref, sh_ref, sc_ref, g_ref, rw0_ref, rw1_ref, rb_ref, tri_ref,
                   h_ref, idx_ref, gate_ref, rank_ref, cnt_ref, base_ref):
    i = pl.program_id(0)

    @pl.when(i == 0)
    def _():
        base_ref[...] = jnp.zeros_like(base_ref)

    x = x_ref[...]
    ms = jnp.mean(x * x, axis=-1, keepdims=True)
    h = x * lax.rsqrt(ms + EPS) * g_ref[...] * (1.0 + sc_ref[0]) + sh_ref[0]
    h_ref[...] = h
    h0 = h.astype(BF16)
    h1 = (h - h0.astype(F32)).astype(BF16)
    logits = _dot(h0, rw0_ref[...]) + (_dot(h0, rw1_ref[...]) + _dot(h1, rw0_ref[...])) + rb_ref[...]
    lane = lax.broadcasted_iota(jnp.int32, logits.shape, 1)
    lane_f = lane.astype(F32)
    cur = jnp.where(lane < N_EXPERTS, logits, NEG_INF)
    vals, sel = [], []
    for _ in range(TOP_K):
        m = jnp.max(cur, axis=1, keepdims=True)
        ix = jnp.min(jnp.where(cur == m, lane_f, float(LANES)), axis=1, keepdims=True)
        hit = lane_f == ix
        vals.append(m)
        sel.append(hit)
        cur = jnp.where(hit, NEG_INF, cur)
    exps = [jnp.exp(v - vals[0]) for v in vals]
    den = exps[0] + exps[1] + exps[2] + exps[3]
    onehot = jnp.zeros(logits.shape, F32)
    for hit in sel:
        onehot = onehot + jnp.where(hit, 1.0, 0.0)
    before = _dot(tri_ref[...], onehot.astype(BF16)) + base_ref[...]
    idx_out = jnp.zeros(logits.shape, F32)
    gate_out = jnp.zeros(logits.shape, F32)
    rank_out = jnp.zeros(logits.shape, F32)
    for kk in range(TOP_K):
        slot = lane == kk
        e_id = jnp.sum(jnp.where(sel[kk], lane_f, 0.0), axis=1, keepdims=True)
        rk = jnp.sum(jnp.where(sel[kk], before, 0.0), axis=1, keepdims=True)
        idx_out = jnp.where(slot, e_id, idx_out)
        gate_out = jnp.where(slot, exps[kk] / den, gate_out)
        rank_out = jnp.where(slot, rk, rank_out)
    idx_ref[...] = idx_out.astype(jnp.int32)
    gate_ref[...] = gate_out
    rank_ref[...] = rank_out.astype(jnp.int32)
    base_ref[...] = base_ref[...] + jnp.sum(onehot, axis=0, keepdims=True)
    cnt_ref[...] = base_ref[...]


def _router(x1, shift, scale, g, rw0, rw1, rb, S):
    N = x1.shape[0]
    TM = min(512, S)
    tpb = S // TM
    r = jnp.arange(TM)
    tri = (r[None, :] < r[:, None]).astype(BF16)
    row = lambda i: (i, 0)
    const = lambda i: (0, 0)
    lanes = pl.BlockSpec((TM, LANES), row)
    return pl.pallas_call(
        _router_kernel,
        grid=(N // TM,),
        in_specs=[pl.BlockSpec((TM, D_MODEL), row),
                  pl.BlockSpec((1, 1, D_MODEL), lambda i: (i // tpb, 0, 0)),
                  pl.BlockSpec((1, 1, D_MODEL), lambda i: (i // tpb, 0, 0)),
                  pl.BlockSpec((1, D_MODEL), const),
                  pl.BlockSpec((D_MODEL, LANES), const),
                  pl.BlockSpec((D_MODEL, LANES), const),
                  pl.BlockSpec((1, LANES), const),
                  pl.BlockSpec((TM, TM), const)],
        out_specs=[pl.BlockSpec((TM, D_MODEL), row), lanes, lanes, lanes, pl.BlockSpec((1, LANES), const)],
        out_shape=[jax.ShapeDtypeStruct((N, D_MODEL), F32),
                   jax.ShapeDtypeStruct((N, LANES), jnp.int32),
                   jax.ShapeDtypeStruct((N, LANES), F32),
                   jax.ShapeDtypeStruct((N, LANES), jnp.int32),
                   jax.ShapeDtypeStruct((1, LANES), F32)],
        scratch_shapes=[pltpu.VMEM((1, LANES), F32)],
        compiler_params=_params(("arbitrary",)),
        name="router",
    )(x1, shift, scale, g, rw0, rw1, rb, tri)


INDEX_SLICE = ROW_MOVE_TILE * TOP_K


def _row_copy_out(h_ref, xs_hbm, sem, r, dst):
    return pltpu.make_async_copy(h_ref.at[pl.ds(r, 1)], xs_hbm.at[pl.ds(dst, 1)], sem)


def _zero_tile_copy(zero_ref, xs_hbm, sem, start):
    return pltpu.make_async_copy(zero_ref, xs_hbm.at[pl.ds(pl.multiple_of(start, MOE_TILE), MOE_TILE)], sem)


def _dispatch_kernel(pad_end_ref, padded_ref, dest_hbm, h_ref, xs_hbm, idx_smem, zero_ref, sem_idx, sem_rows, sem_zero):
    i = pl.program_id(0)
    fetch = pltpu.make_async_copy(dest_hbm.at[pl.ds(i * INDEX_SLICE, INDEX_SLICE)], idx_smem, sem_idx)
    fetch.start()

    @pl.when(i == 0)
    def _():
        zero_ref[...] = jnp.zeros_like(zero_ref)
        for e in range(N_EXPERTS):
            @pl.when(padded_ref[e] > 0)
            def _():
                _zero_tile_copy(zero_ref, xs_hbm, sem_zero, pad_end_ref[e] - MOE_TILE).start()
        for e in range(N_EXPERTS):
            @pl.when(padded_ref[e] > 0)
            def _():
                _zero_tile_copy(zero_ref, xs_hbm, sem_zero, 0).wait()

    fetch.wait()

    def start(r, carry):
        for kk in range(TOP_K):
            _row_copy_out(h_ref, xs_hbm, sem_rows, r, idx_smem[r * TOP_K + kk]).start()
        return carry

    lax.fori_loop(0, ROW_MOVE_TILE, start, 0, unroll=4)
    for kk in range(TOP_K):
        pltpu.make_async_copy(h_ref, xs_hbm.at[pl.ds(0, ROW_MOVE_TILE)], sem_rows).wait()


def _dispatch(pad_ends, padded, dest_flat, h2, n_rows):
    N = h2.shape[0]
    return pl.pallas_call(
        _dispatch_kernel,
        grid_spec=pltpu.PrefetchScalarGridSpec(
            num_scalar_prefetch=2,
            grid=(N // ROW_MOVE_TILE,),
            in_specs=[pl.BlockSpec(memory_space=pl.ANY),
                      pl.BlockSpec((ROW_MOVE_TILE, D_MODEL), lambda i, pe, pd: (i, 0))],
            out_specs=pl.BlockSpec(memory_space=pl.ANY),
            scratch_shapes=[pltpu.SMEM((INDEX_SLICE,), jnp.int32), pltpu.VMEM((MOE_TILE, D_MODEL), F32),
                            pltpu.SemaphoreType.DMA, pltpu.SemaphoreType.DMA, pltpu.SemaphoreType.DMA]),
        out_shape=jax.ShapeDtypeStruct((n_rows, D_MODEL), F32),
        compiler_params=_params(("arbitrary",)),
        name="moe_dispatch",
    )(pad_ends, padded, dest_flat, h2)


def _expert_kernel(te_ref, nu_ref, xs_ref, wg_ref, bg_ref, wl_ref, bl_ref, wd_ref, bd_ref, ys_ref):
    del te_ref

    @pl.when(pl.program_id(0) < nu_ref[0])
    def _():
        xb = xs_ref[...].astype(BF16)
        glu = jnp.minimum(_dot(xb, wg_ref[0]) + bg_ref[0], SWIGLU_LIMIT)
        lin = jnp.clip(_dot(xb, wl_ref[0]) + bl_ref[0], -SWIGLU_LIMIT, SWIGLU_LIMIT)
        act = glu * _sigmoid(SWIGLU_ALPHA * glu) * (lin + 1.0)
        ys_ref[...] = _dot(act.astype(BF16), wd_ref[0]) + bd_ref[0]


def _experts(tile_expert, n_used, xs, wg, bg, wl, bl, wd, bd):
    n_rows = xs.shape[0]
    n_tiles = n_rows // MOE_TILE
    wspec = pl.BlockSpec((1, D_MODEL, D_MODEL), lambda i, te, nu: (te[i], 0, 0))
    bspec = pl.BlockSpec((1, 1, D_MODEL), lambda i, te, nu: (te[i], 0, 0))
    rows = pl.BlockSpec((MOE_TILE, D_MODEL), lambda i, te, nu: (jnp.minimum(i, nu[0] - 1), 0))
    return pl.pallas_call(
        _expert_kernel,
        grid_spec=pltpu.PrefetchScalarGridSpec(
            num_scalar_prefetch=2,
            grid=(n_tiles,),
            in_specs=[rows, wspec, bspec, wspec, bspec, wspec, bspec],
            out_specs=rows),
        out_shape=jax.ShapeDtypeStruct((n_rows, D_MODEL), F32),
        compiler_params=_params(("arbitrary",)),
        name="moe_experts",
    )(tile_expert, n_used, xs, wg, bg, wl, bl, wd, bd)


def _row_copy_in(ys_hbm, buf_ref, sem, src, kk, r):
    return pltpu.make_async_copy(ys_hbm.at[pl.ds(src, 1)], buf_ref.at[kk, pl.ds(r, 1)], sem)


def _combine_kernel(dest_hbm, ys_hbm, gate_ref, x_ref, g2_ref, fg_ref, o_ref, idx_smem, buf_ref, sem_idx, sem_rows):
    i = pl.program_id(0)
    fetch = pltpu.make_async_copy(dest_hbm.at[pl.ds(i * INDEX_SLICE, INDEX_SLICE)], idx_smem, sem_idx)
    fetch.start()
    fetch.wait()

    def start(r, carry):
        for kk in range(TOP_K):
            _row_copy_in(ys_hbm, buf_ref, sem_rows, idx_smem[r * TOP_K + kk], kk, r).start()
        return carry

    lax.fori_loop(0, ROW_MOVE_TILE, start, 0, unroll=4)
    for kk in range(TOP_K):
        pltpu.make_async_copy(ys_hbm.at[pl.ds(0, ROW_MOVE_TILE)], buf_ref.at[kk], sem_rows).wait()

    gates = gate_ref[...]
    moe = gates[:, 0:1] * buf_ref[0]
    for kk in range(1, TOP_K):
        moe = moe + gates[:, kk:kk + 1] * buf_ref[kk]
    x = x_ref[...] + g2_ref[0] * moe
    ms = jnp.mean(x * x, axis=-1, keepdims=True)
    o_ref[...] = x * lax.rsqrt(ms + EPS) * fg_ref[...]


def _combine(dest_flat, ys, gates, x1, gate2, final_g, S):
    N = x1.shape[0]
    TM = ROW_MOVE_TILE
    tpb = S // TM
    row = lambda i: (i, 0)
    return pl.pallas_call(
        _combine_kernel,
        grid=(N // TM,),
        in_specs=[pl.BlockSpec(memory_space=pl.ANY),
                  pl.BlockSpec(memory_space=pl.ANY),
                  pl.BlockSpec((TM, LANES), row),
                  pl.BlockSpec((TM, D_MODEL), row),
                  pl.BlockSpec((1, 1, D_MODEL), lambda i: (i // tpb, 0, 0)),
                  pl.BlockSpec((1, D_MODEL), lambda i: (0, 0))],
        out_specs=pl.BlockSpec((TM, D_MODEL), row),
        out_shape=jax.ShapeDtypeStruct((N, D_MODEL), F32),
        scratch_shapes=[pltpu.SMEM((INDEX_SLICE,), jnp.int32), pltpu.VMEM((TOP_K, TM, D_MODEL), F32),
                        pltpu.SemaphoreType.DMA, pltpu.SemaphoreType.DMA],
        compiler_params=_params(("arbitrary",)),
        name="moe_combine",
    )(dest_flat, ys, gates, x1, gate2, final_g)


def _pad_lanes(a, offset=0):
    return jnp.pad(a, ((0, 0), (offset, LANES - offset - a.shape[1])))


def kernel(x, c, ada_w, ada_b, norm1_g, norm2_g, w_in, b_gate, conv_w, a_log, dt_bias, gdn_norm_g, w_branch_a,
           diff_lambda, diff_norm_g, w_branch_b, w_out, router_w, router_b, w_glu, b_glu, w_lin, b_lin, w_down,
           b_down, final_g):
    B, S, D = x.shape
    N = B * S
    assert D == D_MODEL and S % GROUP == 0 and ada_w.shape[0] == 1
    x2 = x.reshape(N, D)

    mod = _adaln(c, ada_w[0], ada_b[0])
    shift1, scale1, gate1, shift2, scale2, gate2 = [m.reshape(B, 1, D) for m in jnp.split(mod, 6, axis=-1)]

    wi = w_in[0]
    n_a = 4 * D
    n_small = 4 * N_HEADS
    w_big = jnp.concatenate([wi[:, :n_a], wi[:, n_a + n_small:]], axis=1).astype(BF16)
    w_small = _pad_lanes(wi[:, n_a:n_a + n_small]).astype(BF16)
    alog_row = _pad_lanes(a_log[0].reshape(1, -1), GATE_LANE0)
    dt_row = _pad_lanes(dt_bias[0].reshape(1, -1), GATE_LANE0)
    proj, beta, G, Gl, cdl = _inproj(x2, shift1, scale1, norm1_g, w_big, w_small, alog_row, dt_row, S)
    proj3 = proj.reshape(B, S, N_COL_BLOCKS * D)

    qkv = _gdn_conv(proj3, conv_w[0])
    r3 = lambda a: a.reshape(B, S, LANES)
    nC = S // GDN_CHUNK
    GT = jnp.transpose(r3(G)[:, :, GATE_LANE0:GATE_LANE0 + 2 * N_HEADS], (0, 2, 1)).reshape(B, 2 * N_HEADS, 1, S)
    cd = r3(cdl).reshape(B, nC, GDN_CHUNK, LANES)[:, :, 0, GATE_LANE0:GATE_LANE0 + 2 * N_HEADS]
    cd = jnp.transpose(cd.reshape(B, nC, 2, N_HEADS), (0, 2, 3, 1)).reshape(-1)
    u, w, qd, att, kdt = _gdn_prep(qkv, r3(beta), r3(G), r3(Gl), GT)
    oAf, oAb = _gdn_scan(cd, u, w, qd, att, kdt)

    half = DIFF_DH // 2
    inv_freq = ROPE_THETA ** (-jnp.arange(half, dtype=F32) / half)
    ang = jnp.arange(S, dtype=F32)[:, None] * inv_freq[None, :]
    cos_t = jnp.tile(jnp.cos(ang), (1, 4))
    sin_h = jnp.sin(ang)
    sin_t = jnp.tile(jnp.concatenate([-sin_h, sin_h], axis=1), (1, 2))
    qt, kr, vt = _rope(proj3, cos_t, sin_t)
    lam_a = jnp.pad(_pad_lanes(diff_lambda[0][0::2]), ((0, 6), (0, 0)))
    lam_b = jnp.pad(_pad_lanes(diff_lambda[0][1::2]), ((0, 6), (0, 0)))
    oB = _diff_attn(qt, kr, vt, lam_a, lam_b, diff_norm_g)

    x1 = _merge(oAf.reshape(N, D), oAb.reshape(N, D), proj, oB.reshape(N, D), b_gate, x2, gate1, gdn_norm_g,
                w_branch_a[0].astype(BF16), w_branch_b[0].astype(BF16), w_out[0].astype(BF16), S)

    rw = _pad_lanes(router_w[0])
    rw0 = rw.astype(BF16)
    rw1 = (rw - rw0.astype(F32)).astype(BF16)
    h2, idx, gates, rank, counts = _router(x1, shift2, scale2, norm2_g, rw0, rw1, _pad_lanes(router_b), S)
    cnt = counts[0, :N_EXPERTS].astype(jnp.int32)
    padded = (cnt + MOE_TILE - 1) // MOE_TILE * MOE_TILE
    pad_ends = jnp.cumsum(padded)
    pad_starts = pad_ends - padded
    dest = (pad_starts[idx[:, :TOP_K]] + rank[:, :TOP_K]).reshape(-1)
    n_tiles = -(-(N * TOP_K) // MOE_TILE) + N_EXPERTS
    tile_start = jnp.arange(n_tiles, dtype=jnp.int32) * MOE_TILE
    tile_expert = jnp.sum((tile_start[:, None] >= pad_ends[None, :]).astype(jnp.int32), axis=1)
    tile_expert = jnp.minimum(tile_expert, N_EXPERTS - 1)
    n_used = (pad_ends[N_EXPERTS - 1:] // MOE_TILE).astype(jnp.int32)
    xs = _dispatch(pad_ends.astype(jnp.int32), padded, dest, h2, n_tiles * MOE_TILE)
    ys = _experts(tile_expert, n_used, xs, w_glu[0].astype(BF16), b_glu[0][:, None, :], w_lin[0].astype(BF16),
                  b_lin[0][:, None, :], w_down[0].astype(BF16), b_down[0][:, None, :])
    out = _combine(dest, ys, gates, x1, gate2, final_g.reshape(1, D), S)
    return out.reshape(B, S, D)
```

```python
import functools
import math

import jax
import jax.numpy as jnp
from jax import lax
from jax.experimental import pallas as pl
from jax.experimental.pallas import tpu as pltpu

F32 = jnp.float32
BF16 = jnp.bfloat16

D_MODEL = 1024
EPS = 1e-6
N_HEADS = 8
HEAD_W = 128
GDN_CHUNK = 64
CONV_WIDTH = 5
DIFF_DH = 64
ROPE_THETA = 10000.0
LAM_INIT = 0.8 - 0.6 * math.exp(-0.3 * 0)
N_EXPERTS = 32
TOP_K = 4
SWIGLU_ALPHA = 1.702
SWIGLU_LIMIT = 7.0

LANES = 128
GROUP = 256
CHUNKS_PER_GROUP = GROUP // GDN_CHUNK
CHUNK_SHIFT = GDN_CHUNK.bit_length() - 1
MOE_TILE = 512
ROW_MOVE_TILE = 1024
NEG_INF = float("-inf")

COL_QA, COL_KA, COL_VA, COL_ZA, COL_QB, COL_KB, COL_VB, COL_GA, COL_GB = range(9)
N_COL_BLOCKS = 9


def _params(sem, vmem_mb=48):
    return pltpu.CompilerParams(dimension_semantics=sem, vmem_limit_bytes=vmem_mb * 1024 * 1024)


def _dot(a, b):
    return jnp.dot(a, b, preferred_element_type=F32)


def _dot_nt(a, b):
    return lax.dot_general(a, b, (((1,), (1,)), ((), ())), preferred_element_type=F32)


def _sigmoid(x):
    return 1.0 / (1.0 + jnp.exp(-x))


def _split3(x):
    a = x.astype(BF16)
    r = x - a.astype(F32)
    b = r.astype(BF16)
    c = (r - b.astype(F32)).astype(BF16)
    return a, b, c


def _adaln_kernel(c_ref, w_ref, b_ref, o_ref):
    c = c_ref[...]
    cond = c * _sigmoid(c)
    c0, c1, c2 = _split3(cond)
    w0, w1, w2 = _split3(w_ref[...])
    acc = _dot(c0, w0) + (_dot(c0, w1) + _dot(c1, w0)) + (_dot(c0, w2) + _dot(c1, w1) + _dot(c2, w0))
    o_ref[...] = acc + b_ref[...]


def _adaln(c, ada_w, ada_b):
    B = c.shape[0]
    n = ada_w.shape[1] // D_MODEL
    return pl.pallas_call(
        _adaln_kernel,
        grid=(n,),
        in_specs=[pl.BlockSpec((B, D_MODEL), lambda j: (0, 0)),
                  pl.BlockSpec((D_MODEL, D_MODEL), lambda j: (0, j)),
                  pl.BlockSpec((1, D_MODEL), lambda j: (0, j))],
        out_specs=pl.BlockSpec((B, D_MODEL), lambda j: (0, j)),
        out_shape=jax.ShapeDtypeStruct((B, n * D_MODEL), F32),
        compiler_params=_params(("parallel",)),
        name="adaln",
    )(c, ada_w, ada_b.reshape(1, -1))


def _inproj_kernel(x_ref, sh_ref, sc_ref, g_ref, w_ref, ws_ref, alog_ref, dt_ref,
                   o_ref, beta_ref, gcum_ref, glast_ref, cd_ref, h_ref):
    @pl.when(pl.program_id(1) == 0)
    def _():
        x = x_ref[...]
        ms = jnp.mean(x * x, axis=-1, keepdims=True)
        y = x * lax.rsqrt(ms + EPS) * g_ref[...]
        h = (y * (1.0 + sc_ref[0]) + sh_ref[0]).astype(BF16)
        h_ref[...] = h
        _gdn_gates(_dot(h, ws_ref[...]), alog_ref[...], dt_ref[...], beta_ref, gcum_ref, glast_ref, cd_ref)

    o_ref[...] = _dot(h_ref[...], w_ref[...]).astype(o_ref.dtype)


def _inproj(x2, shift, scale, g, w_big, w_small, alog_row, dt_row, S):
    N = x2.shape[0]
    TM = min(1024, S)
    tpb = S // TM
    const = lambda i, j: (0, 0)
    lanes = pl.BlockSpec((TM, LANES), lambda i, j: (i, 0))
    return pl.pallas_call(
        _inproj_kernel,
        grid=(N // TM, N_COL_BLOCKS),
        in_specs=[pl.BlockSpec((TM, D_MODEL), lambda i, j: (i, 0)),
                  pl.BlockSpec((1, 1, D_MODEL), lambda i, j: (i // tpb, 0, 0)),
                  pl.BlockSpec((1, 1, D_MODEL), lambda i, j: (i // tpb, 0, 0)),
                  pl.BlockSpec((1, D_MODEL), const),
                  pl.BlockSpec((D_MODEL, D_MODEL), lambda i, j: (0, j)),
                  pl.BlockSpec((D_MODEL, LANES), const),
                  pl.BlockSpec((1, LANES), const),
                  pl.BlockSpec((1, LANES), const)],
        out_specs=[pl.BlockSpec((TM, D_MODEL), lambda i, j: (i, j)), lanes, lanes, lanes, lanes],
        out_shape=[jax.ShapeDtypeStruct((N, N_COL_BLOCKS * D_MODEL), BF16)]
                  + [jax.ShapeDtypeStruct((N, LANES), F32)] * 4,
        scratch_shapes=[pltpu.VMEM((TM, D_MODEL), BF16)],
        compiler_params=_params(("parallel", "arbitrary")),
        name="inproj",
    )(x2, shift, scale, g, w_big, w_small, alog_row, dt_row)


HALO = 16


CONV_BLOCK = 256
CONV_PAD = (CONV_WIDTH - 1) // 2
CONV_TAPS = tuple(j for j in range(CONV_WIDTH) if j != CONV_PAD)


def _conv_shifts():
    r = jnp.arange(CONV_BLOCK)[:, None]
    c = jnp.arange(CONV_BLOCK)[None, :]
    return jnp.stack([c == r + (j - CONV_PAD) for j in CONV_TAPS]).astype(BF16)


def _conv_kernel(cur_ref, prev_ref, next_ref, w_ref, shift_ref, o_ref, ext_ref, *, TR):
    i = pl.program_id(1)
    g = pl.program_id(2)
    last = pl.num_programs(1) - 1
    ext_ref[8:8 + TR, :] = cur_ref[0].astype(F32)
    pv = prev_ref[0].astype(F32)[HALO - 8:HALO]
    nx = next_ref[0].astype(F32)[0:8]
    ext_ref[0:8, :] = jnp.where(i > 0, pv, 0.0)
    ext_ref[TR + 8:TR + 16, :] = jnp.where(i < last, nx, 0.0)
    ones = jnp.ones((HEAD_W, HEAD_W), BF16)
    qscale = jnp.where(g == 0, HEAD_W ** -0.5, 1.0)

    def edge_rows(row0):
        e = ext_ref[8 + row0 - CONV_PAD:16 + row0 - CONV_PAD, :] * w_ref[0:1, :]
        for j in range(1, CONV_WIDTH):
            e = e + ext_ref[8 + row0 - CONV_PAD + j:16 + row0 - CONV_PAD + j, :] * w_ref[j:j + 1, :]
        return e

    for blk in range(TR // CONV_BLOCK):
        r0 = blk * CONV_BLOCK
        rows = slice(r0, r0 + CONV_BLOCK)
        ub = cur_ref[0, rows, :]
        acc = ext_ref[8 + r0:8 + r0 + CONV_BLOCK, :] * w_ref[CONV_PAD:CONV_PAD + 1, :]
        for si, j in enumerate(CONV_TAPS):
            acc = acc + _dot(shift_ref[si], ub) * w_ref[j:j + 1, :]
        acc = jnp.concatenate([edge_rows(r0), acc[8:CONV_BLOCK - 8], edge_rows(r0 + CONV_BLOCK - 8)], axis=0)
        y = acc * _sigmoid(acc)

        for h in range(N_HEADS):
            cols = slice(h * HEAD_W, (h + 1) * HEAD_W)
            yh = y[:, cols]
            ss = _dot((yh * yh).astype(BF16), ones)
            scale = jnp.where(g < 2, lax.rsqrt(ss + EPS) * qscale, 1.0)
            o_ref[0, rows, cols] = (yh * scale).astype(o_ref.dtype)


def _gdn_conv(proj3, conv_w):
    B, S, _ = proj3.shape
    TR = min(512, S)
    nT = S // TR
    rb = TR // HALO
    nH = S // HALO
    return pl.pallas_call(
        functools.partial(_conv_kernel, TR=TR),
        grid=(B, nT, 3),
        in_specs=[pl.BlockSpec((1, TR, D_MODEL), lambda b, i, g: (b, i, g)),
                  pl.BlockSpec((1, HALO, D_MODEL), lambda b, i, g: (b, jnp.maximum(i * rb - 1, 0), g)),
                  pl.BlockSpec((1, HALO, D_MODEL), lambda b, i, g: (b, jnp.minimum((i + 1) * rb, nH - 1), g)),
                  pl.BlockSpec((CONV_WIDTH, D_MODEL), lambda b, i, g: (0, g)),
                  pl.BlockSpec((len(CONV_TAPS), CONV_BLOCK, CONV_BLOCK), lambda b, i, g: (0, 0, 0))],
        out_specs=pl.BlockSpec((1, TR, D_MODEL), lambda b, i, g: (b, i, g)),
        out_shape=jax.ShapeDtypeStruct((B, S, 3 * D_MODEL), BF16),
        scratch_shapes=[pltpu.VMEM((TR + 16, D_MODEL), F32)],
        compiler_params=_params(("parallel", "parallel", "parallel")),
        name="gdn_conv",
    )(proj3, proj3, proj3, conv_w, _conv_shifts())


GATE_LANE0 = 16


def _gdn_gates(x, alog, dt, beta_ref, g_ref, gl_ref, cd_ref):
    lane = lax.broadcasted_iota(jnp.int32, (GROUP, LANES), 1)
    r = lax.broadcasted_iota(jnp.int32, (GROUP, GROUP), 0)
    c = lax.broadcasted_iota(jnp.int32, (GROUP, GROUP), 1)
    same = (r >> CHUNK_SHIFT) == (c >> CHUNK_SHIFT)
    lower = jnp.where(same & (c <= r), 1.0, 0.0).astype(BF16)
    upper = jnp.where(same & (c >= r), 1.0, 0.0).astype(BF16)
    block = jnp.where(same, 1.0, 0.0).astype(BF16)
    beta_ref[...] = _sigmoid(x)
    for gi in range(x.shape[0] // GROUP):
        rows = slice(gi * GROUP, (gi + 1) * GROUP)
        z = x[rows] + dt
        softplus = jnp.maximum(z, 0.0) + jnp.log(1.0 + jnp.exp(-jnp.abs(z)))
        gd = -jnp.exp(alog) * softplus
        gd = jnp.where((lane >= GATE_LANE0) & (lane < GATE_LANE0 + 2 * N_HEADS), gd, 0.0)
        p0, p1, p2 = _split3(gd)
        g_fwd = _dot(lower, p0) + _dot(lower, p1) + _dot(lower, p2)
        g_bwd = _dot(upper, p0) + _dot(upper, p1) + _dot(upper, p2)
        tot = _dot(block, p0) + _dot(block, p1) + _dot(block, p2)
        G = jnp.where(lane < GATE_LANE0 + N_HEADS, g_fwd, g_bwd)
        g_ref[rows, :] = G
        gl_ref[rows, :] = tot - G
        cd_ref[rows, :] = jnp.exp(tot)


PREP_HEADS = 2
(MASK_STRICT_LO, MASK_STRICT_UP, MASK_INCL_LO, MASK_INCL_UP, MASK_EYE, MASK_BLK4, MASK_OFF0) = range(7)
N_MASKS = MASK_OFF0 + (CHUNK_SHIFT - 2)


def _prep_masks():
    r = jnp.arange(GROUP)[:, None]
    c = jnp.arange(GROUP)[None, :]
    same = (r >> CHUNK_SHIFT) == (c >> CHUNK_SHIFT)
    masks = [same & (c < r), same & (c > r), same & (c <= r), same & (c >= r), r == c, (r >> 2) == (c >> 2)]
    for shift in range(2, CHUNK_SHIFT):
        masks.append(((r >> shift) != (c >> shift)) & ((r >> (shift + 1)) == (c >> (shift + 1))))
    return jnp.stack(masks).astype(F32)


def _col(x, l, lane):
    return jnp.broadcast_to(jnp.sum(jnp.where(lane == l, x, 0.0), axis=1, keepdims=True), x.shape)


def _prep_kernel(q_ref, k_ref, v_ref, beta_ref, g_ref, gl_ref, gtf_ref, gtb_ref, mask_ref, bmask_ref,
                 u_ref, w_ref, qd_ref, at_ref, kdt_ref):
    hp = pl.program_id(1)
    lane = lax.broadcasted_iota(jnp.int32, (GROUP, LANES), 1)
    wide = lambda a: jnp.concatenate([a, a], axis=1)
    chains = [(hh, d) for hh in range(PREP_HEADS) for d in range(2)]
    p, rhs = {}, {}
    for hh in range(PREP_HEADS):
        cols = slice(hh * HEAD_W, (hh + 1) * HEAD_W)
        q = q_ref[0, :, cols]
        k = k_ref[0, :, cols]
        qf = q.astype(F32)
        kf = k.astype(F32)
        vf = v_ref[0, :, cols].astype(F32)
        kk = _dot_nt(k, k)
        qk = _dot_nt(q, k)
        for d in range(2):
            lb = d * N_HEADS + hp * PREP_HEADS + hh
            beta_c = _col(beta_ref[0], lb, lane)
            g_c = _col(g_ref[0], GATE_LANE0 + lb, lane)
            eg_c = jnp.exp(g_c)
            egl_c = jnp.exp(_col(gl_ref[0], GATE_LANE0 + lb, lane))
            g_r = (gtf_ref if d == 0 else gtb_ref)[0, hh]
            dec = jnp.exp(jnp.minimum(wide(g_c) - g_r, 0.0))
            p[hh, d] = (-(kk * wide(beta_c)) * dec * mask_ref[MASK_STRICT_LO + d]).astype(BF16)
            att = qk * dec * mask_ref[MASK_INCL_LO + d]
            rhs[hh, d] = jnp.concatenate([vf * beta_c, kf * (beta_c * eg_c)], axis=1).astype(BF16)
            qd_ref[0, d, :, cols] = (qf * eg_c).astype(qd_ref.dtype)
            kdt = (kf * egl_c).T
            for ci in range(CHUNKS_PER_GROUP):
                sl = slice(ci * GDN_CHUNK, (ci + 1) * GDN_CHUNK)
                at_ref[0, d, hh, ci] = att[sl, sl].astype(at_ref.dtype)
                kdt_ref[0, d, hh, ci] = kdt[:, sl].astype(kdt_ref.dtype)
    p4 = {ch: p[ch] * bmask_ref[0] for ch in chains}
    sq = {ch: _dot(p4[ch], p4[ch]).astype(BF16) for ch in chains}
    t = {ch: mask_ref[MASK_EYE] + p4[ch].astype(F32) for ch in chains}
    t = {ch: t[ch] + _dot(t[ch].astype(BF16), sq[ch]) for ch in chains}
    for lvl in range(CHUNK_SHIFT - 2):
        tb = {ch: t[ch].astype(BF16) for ch in chains}
        x = {ch: _dot(tb[ch], p[ch] * bmask_ref[1 + lvl]).astype(BF16) for ch in chains}
        t = {ch: t[ch] + _dot(x[ch], tb[ch]) for ch in chains}
    uw = {ch: _dot(t[ch].astype(BF16), rhs[ch]) for ch in chains}
    for hh, d in chains:
        cols = slice(hh * HEAD_W, (hh + 1) * HEAD_W)
        u_ref[0, d, :, cols] = uw[hh, d][:, :HEAD_W].astype(u_ref.dtype)
        w_ref[0, d, :, cols] = uw[hh, d][:, HEAD_W:].astype(w_ref.dtype)


def _gdn_prep(qkv, beta, G, Gl, GT):
    B, S, _ = qkv.shape
    nG = S // GROUP
    nC = S // GDN_CHUNK
    PW = PREP_HEADS * HEAD_W
    nP = N_HEADS // PREP_HEADS
    sm = pl.BlockSpec((1, GROUP, LANES), lambda b, h, g: (b, g, 0))
    big = pl.BlockSpec((1, 2, GROUP, PW), lambda b, h, g: (b, 0, g, h))
    masks = _prep_masks()
    return pl.pallas_call(
        _prep_kernel,
        grid=(B, nP, nG),
        in_specs=[pl.BlockSpec((1, GROUP, PW), lambda b, h, g: (b, g, h)),
                  pl.BlockSpec((1, GROUP, PW), lambda b, h, g: (b, g, nP + h)),
                  pl.BlockSpec((1, GROUP, PW), lambda b, h, g: (b, g, 2 * nP + h)),
                  sm, sm, sm,
                  pl.BlockSpec((1, PREP_HEADS, 1, GROUP), lambda b, h, g: (b, h, 0, g)),
                  pl.BlockSpec((1, PREP_HEADS, 1, GROUP), lambda b, h, g: (b, nP + h, 0, g)),
                  pl.BlockSpec((MASK_BLK4, GROUP, GROUP), lambda b, h, g: (0, 0, 0)),
                  pl.BlockSpec((N_MASKS - MASK_BLK4, GROUP, GROUP), lambda b, h, g: (0, 0, 0))],
        out_specs=[big, big, big,
                   pl.BlockSpec((1, 2, PREP_HEADS, CHUNKS_PER_GROUP, GDN_CHUNK, GDN_CHUNK),
                                lambda b, h, g: (b, 0, h, g, 0, 0)),
                   pl.BlockSpec((1, 2, PREP_HEADS, CHUNKS_PER_GROUP, HEAD_W, GDN_CHUNK),
                                lambda b, h, g: (b, 0, h, g, 0, 0))],
        out_shape=[jax.ShapeDtypeStruct((B, 2, S, D_MODEL), BF16)] * 3
                  + [jax.ShapeDtypeStruct((B, 2, N_HEADS, nC, GDN_CHUNK, GDN_CHUNK), BF16),
                     jax.ShapeDtypeStruct((B, 2, N_HEADS, nC, HEAD_W, GDN_CHUNK), BF16)],
        compiler_params=_params(("parallel", "parallel", "parallel")),
        name="gdn_prep",
    )(qkv, qkv, qkv, beta, G, Gl, GT, GT, masks[:MASK_BLK4], masks[MASK_BLK4:].astype(BF16))


def _scan_kernel(cd_ref, uf_ref, wf_ref, qdf_ref, atf_ref, kdtf_ref, ub_ref, wb_ref, qdb_ref, atb_ref, kdtb_ref,
                 of_ref, ob_ref, state_ref, *, nc, nC):
    b = pl.program_id(0)
    t = pl.program_id(1)
    nT = pl.num_programs(1)

    @pl.when(t == 0)
    def _():
        state_ref[...] = jnp.zeros_like(state_ref)

    dirs = ((uf_ref, wf_ref, qdf_ref, atf_ref, kdtf_ref, of_ref), (ub_ref, wb_ref, qdb_ref, atb_ref, kdtb_ref, ob_ref))

    def chunk(ci, carry):
        work = []
        for d, refs in enumerate(dirs):
            c = ci if d == 0 else nc - 1 - ci
            tt = t if d == 0 else nT - 1 - t
            row = pl.multiple_of(c * GDN_CHUNK, GDN_CHUNK)
            for h in range(N_HEADS):
                work.append((d, h, c, row, ((b * 2 + d) * N_HEADS + h) * nC + tt * nc + c, refs))
        s_old = [state_ref[d, h] for d, h, *_ in work]
        sb = [s.astype(BF16) for s in s_old]
        tile = lambda ref, row, h: ref[0, 0, pl.ds(row, GDN_CHUNK), h * HEAD_W:(h + 1) * HEAD_W]
        ws = [_dot(tile(refs[1], row, h), sb[i]) for i, (d, h, c, row, gi, refs) in enumerate(work)]
        qs = [_dot(tile(refs[2], row, h), sb[i]) for i, (d, h, c, row, gi, refs) in enumerate(work)]
        vb = [(tile(refs[0], row, h).astype(F32) - ws[i]).astype(BF16)
              for i, (d, h, c, row, gi, refs) in enumerate(work)]
        o = [qs[i] + _dot(refs[3][0, 0, h, c], vb[i]) for i, (d, h, c, row, gi, refs) in enumerate(work)]
        upd = [_dot(refs[4][0, 0, h, c], vb[i]) for i, (d, h, c, row, gi, refs) in enumerate(work)]
        for i, (d, h, c, row, gi, refs) in enumerate(work):
            state_ref[d, h] = s_old[i] * cd_ref[gi] + upd[i]
            refs[5][0, pl.ds(row, GDN_CHUNK), h * HEAD_W:(h + 1) * HEAD_W] = o[i].astype(of_ref.dtype)
        return carry

    lax.fori_loop(0, nc, chunk, 0)


def _gdn_scan(cd, u, w, qd, att, kdt):
    B, _, S, _ = u.shape
    TC = min(512, S)
    nT = S // TC
    nc = TC // GDN_CHUNK
    nC = S // GDN_CHUNK
    fwd = lambda b, t: t
    bwd = lambda b, t: nT - 1 - t

    def specs(d, tm):
        big = pl.BlockSpec((1, 1, TC, D_MODEL), lambda b, t: (b, d, tm(b, t), 0))
        return [big, big, big,
                pl.BlockSpec((1, 1, N_HEADS, nc, GDN_CHUNK, GDN_CHUNK), lambda b, t: (b, d, 0, tm(b, t), 0, 0)),
                pl.BlockSpec((1, 1, N_HEADS, nc, HEAD_W, GDN_CHUNK), lambda b, t: (b, d, 0, tm(b, t), 0, 0))]

    return pl.pallas_call(
        functools.partial(_scan_kernel, nc=nc, nC=nC),
        grid=(B, nT),
        in_specs=[pl.BlockSpec(memory_space=pltpu.SMEM)] + specs(0, fwd) + specs(1, bwd),
        out_specs=[pl.BlockSpec((1, TC, D_MODEL), lambda b, t: (b, t, 0)),
                   pl.BlockSpec((1, TC, D_MODEL), lambda b, t: (b, nT - 1 - t, 0))],
        out_shape=[jax.ShapeDtypeStruct((B, S, D_MODEL), BF16)] * 2,
        scratch_shapes=[pltpu.VMEM((2, N_HEADS, HEAD_W, HEAD_W), F32)],
        compiler_params=_params(("parallel", "arbitrary")),
        name="gdn_scan",
    )(cd, u, w, qd, att, kdt, u, w, qd, att, kdt)


ATT_TQ = 256
ATT_TK = 256
ATT_NQ = 4
ATT_VROWS = HEAD_W + 16
LOG2E = 1.4426950408889634


def _rope_kernel(q_ref, k_ref, v_ref, cos_ref, sin_ref, qt_ref, kr_ref, vt_ref, *, TR):
    cs = cos_ref[...]
    sn = sin_ref[...]
    lane = lax.broadcasted_iota(jnp.int32, cs.shape, 1)
    first_half = (lane & (DIFF_DH - 1)) < (DIFF_DH // 2)
    qscale = DIFF_DH ** -0.5 * LOG2E

    def rot(x):
        partner = jnp.where(first_half, pltpu.roll(x, HEAD_W - DIFF_DH // 2, 1), pltpu.roll(x, DIFF_DH // 2, 1))
        return x * cs + partner * sn

    for h in range(N_HEADS):
        cols = slice(h * HEAD_W, (h + 1) * HEAD_W)
        qr = rot(q_ref[0, :, cols].astype(F32)) * qscale
        kr_ref[0, :, cols] = rot(k_ref[0, :, cols].astype(F32)).astype(kr_ref.dtype)
        vf = v_ref[0, :, cols].astype(F32)
        for ci in range(TR // ATT_TK):
            rows = slice(ci * ATT_TK, (ci + 1) * ATT_TK)
            vt_ref[0, h, ci, 0:HEAD_W, :] = vf[rows].T.astype(vt_ref.dtype)
            vt_ref[0, h, ci, HEAD_W:ATT_VROWS, :] = jnp.ones((ATT_VROWS - HEAD_W, ATT_TK), vt_ref.dtype)
        for ci in range(TR // ATT_TQ):
            rows = slice(ci * ATT_TQ, (ci + 1) * ATT_TQ)
            qt_ref[0, h, ci] = qr[rows].T.astype(qt_ref.dtype)


def _rope(proj3, cos_t, sin_t):
    B, S, _ = proj3.shape
    TR = min(512, S)
    tab = pl.BlockSpec((TR, HEAD_W), lambda b, i: (i, 0))
    col = lambda cb: pl.BlockSpec((1, TR, D_MODEL), lambda b, i: (b, i, cb))
    return pl.pallas_call(
        functools.partial(_rope_kernel, TR=TR),
        grid=(B, S // TR),
        in_specs=[col(COL_QB), col(COL_KB), col(COL_VB), tab, tab],
        out_specs=[pl.BlockSpec((1, N_HEADS, TR // ATT_TQ, HEAD_W, ATT_TQ), lambda b, i: (b, 0, i, 0, 0)),
                   pl.BlockSpec((1, TR, D_MODEL), lambda b, i: (b, i, 0)),
                   pl.BlockSpec((1, N_HEADS, TR // ATT_TK, ATT_VROWS, ATT_TK), lambda b, i: (b, 0, i, 0, 0))],
        out_shape=[jax.ShapeDtypeStruct((B, N_HEADS, S // ATT_TQ, HEAD_W, ATT_TQ), BF16),
                   jax.ShapeDtypeStruct((B, S, D_MODEL), BF16),
                   jax.ShapeDtypeStruct((B, N_HEADS, S // ATT_TK, ATT_VROWS, ATT_TK), BF16)],
        compiler_params=_params(("parallel", "parallel")),
        name="rope",
    )(proj3, proj3, proj3, cos_t, sin_t)


def _attn_kernel(qt_ref, k_ref, vt_ref, la_ref, lb_ref, g_ref, o_ref, s_ref, acc_ref, *, n_chunks):
    row = lax.broadcasted_iota(jnp.int32, (HEAD_W, ATT_TQ), 0)
    qw = []
    for qb in range(ATT_NQ):
        qt = qt_ref[0, 0, qb]
        zero = jnp.zeros_like(qt)
        qw.append((jnp.where(row < DIFF_DH, qt, zero), jnp.where(row >= DIFF_DH, qt, zero)))
    chains = [(qb, comp) for qb in range(ATT_NQ) for comp in range(2)]

    def scores(j):
        kc = k_ref[0, pl.ds(pl.multiple_of(j * ATT_TK, ATT_TK), ATT_TK), :]
        return [_dot(kc, qw[qb][comp]) for qb, comp in chains]

    acc_ref[...] = jnp.zeros_like(acc_ref)
    for (qb, comp), s0 in zip(chains, scores(0)):
        s_ref[qb, comp] = s0

    def chunk(j, carry):
        s_next = scores(jnp.minimum(j + 1, n_chunks - 1))
        vt = vt_ref[0, 0, j]
        out = []
        for ci, (qb, comp) in enumerate(chains):
            m_prev = carry[ci]
            s = s_ref[qb, comp]
            m_new = jnp.maximum(m_prev, jnp.max(s, axis=0, keepdims=True))
            alpha = jnp.exp2(m_prev - m_new)
            p = jnp.exp2(s - m_new)
            out.append(m_new)
            acc_ref[qb, comp] = alpha * acc_ref[qb, comp] + _dot(vt, p.astype(BF16))
        for (qb, comp), sn in zip(chains, s_next):
            s_ref[qb, comp] = sn
        return tuple(out)

    neg = jnp.full((1, ATT_TQ), NEG_INF, F32)
    lax.fori_loop(0, n_chunks, chunk, (neg,) * len(chains), unroll=8)

    sums = jnp.sum(la_ref[...] * lb_ref[...], axis=1, keepdims=True)
    lrow = lax.broadcasted_iota(jnp.int32, sums.shape, 0)
    sign = jnp.where(lrow == 0, 1.0, jnp.where(lrow == 1, -1.0, 0.0))
    lam = jnp.sum(sign * jnp.exp(sums), axis=0, keepdims=True) + LAM_INIT
    for qb in range(ATT_NQ):
        l0 = acc_ref[qb, 0, HEAD_W:HEAD_W + 1, :]
        l1 = acc_ref[qb, 1, HEAD_W:HEAD_W + 1, :]
        ot = acc_ref[qb, 0, 0:HEAD_W, :] / l0 - lam * (acc_ref[qb, 1, 0:HEAD_W, :] / l1)
        ms = jnp.mean(ot * ot, axis=0, keepdims=True)
        y = (ot * lax.rsqrt(ms + EPS)).T * g_ref[...] * (1.0 - LAM_INIT)
        o_ref[0, qb * ATT_TQ:(qb + 1) * ATT_TQ, :] = y.astype(o_ref.dtype)


def _diff_attn(qt, kr, vt, lam_a, lam_b, norm_g):
    B, S, _ = kr.shape
    lam_spec = pl.BlockSpec((8, LANES), lambda b, h, qi: (0, 0))
    return pl.pallas_call(
        functools.partial(_attn_kernel, n_chunks=S // ATT_TK),
        grid=(B, N_HEADS, S // (ATT_NQ * ATT_TQ)),
        in_specs=[pl.BlockSpec((1, 1, ATT_NQ, HEAD_W, ATT_TQ), lambda b, h, qi: (b, h, qi, 0, 0)),
                  pl.BlockSpec((1, S, HEAD_W), lambda b, h, qi: (b, 0, h)),
                  pl.BlockSpec((1, 1, S // ATT_TK, ATT_VROWS, ATT_TK), lambda b, h, qi: (b, h, 0, 0, 0)),
                  lam_spec, lam_spec,
                  pl.BlockSpec((1, HEAD_W), lambda b, h, qi: (0, 0))],
        out_specs=pl.BlockSpec((1, ATT_NQ * ATT_TQ, HEAD_W), lambda b, h, qi: (b, qi, h)),
        out_shape=jax.ShapeDtypeStruct((B, S, D_MODEL), BF16),
        scratch_shapes=[pltpu.VMEM((ATT_NQ, 2, ATT_TK, ATT_TQ), F32), pltpu.VMEM((ATT_NQ, 2, ATT_VROWS, ATT_TQ), F32)],
        compiler_params=_params(("parallel", "parallel", "parallel")),
        name="diff_attn",
    )(qt, kr, vt, lam_a, lam_b, norm_g)


def _merge_kernel(of_ref, ob_ref, z_ref, oB_ref, ga_ref, gb_ref, bga_ref, bgb_ref, x_ref, g1_ref, gn_ref,
                  wa_ref, wb_ref, wo_ref, o_ref, ya_ref):
    oa = of_ref[...].astype(F32) + ob_ref[...].astype(F32)
    z = z_ref[...].astype(F32)
    gate = z * _sigmoid(z)
    for h in range(N_HEADS):
        cols = slice(h * HEAD_W, (h + 1) * HEAD_W)
        oh = oa[:, cols]
        ms = jnp.mean(oh * oh, axis=-1, keepdims=True)
        ya_ref[:, cols] = (oh * lax.rsqrt(ms + EPS) * gn_ref[...] * gate[:, cols]).astype(BF16)
    y_a = _dot(ya_ref[...], wa_ref[...])
    y_b = _dot(oB_ref[...], wb_ref[...])
    gate_a = _sigmoid(ga_ref[...].astype(F32) + bga_ref[...])
    gate_b = _sigmoid(gb_ref[...].astype(F32) + bgb_ref[...])
    mix = _dot((gate_a * y_a + gate_b * y_b).astype(BF16), wo_ref[...])
    o_ref[...] = x_ref[...] + g1_ref[0] * mix


def _merge(oAf, oAb, proj, oB, b_gate, x2, gate1, gn, wa, wb, wo, S):
    N = x2.shape[0]
    TM = min(512, S)
    tpb = S // TM
    row = lambda i: (i, 0)
    full = pl.BlockSpec((D_MODEL, D_MODEL), lambda i: (0, 0))
    return pl.pallas_call(
        _merge_kernel,
        grid=(N // TM,),
        in_specs=[pl.BlockSpec((TM, D_MODEL), row),
                  pl.BlockSpec((TM, D_MODEL), row),
                  pl.BlockSpec((TM, D_MODEL), lambda i: (i, COL_ZA)),
                  pl.BlockSpec((TM, D_MODEL), row),
                  pl.BlockSpec((TM, D_MODEL), lambda i: (i, COL_GA)),
                  pl.BlockSpec((TM, D_MODEL), lambda i: (i, COL_GB)),
                  pl.BlockSpec((1, D_MODEL), lambda i: (0, 0)),
                  pl.BlockSpec((1, D_MODEL), lambda i: (0, 1)),
                  pl.BlockSpec((TM, D_MODEL), row),
                  pl.BlockSpec((1, 1, D_MODEL), lambda i: (i // tpb, 0, 0)),
                  pl.BlockSpec((1, HEAD_W), lambda i: (0, 0)),
                  full, full, full],
        out_specs=pl.BlockSpec((TM, D_MODEL), row),
        out_shape=jax.ShapeDtypeStruct((N, D_MODEL), F32),
        scratch_shapes=[pltpu.VMEM((TM, D_MODEL), BF16)],
        compiler_params=_params(("parallel",)),
        name="merge",
    )(oAf, oAb, proj, oB, proj, proj, b_gate, b_gate, x2, gate1, gn, wa, wb, wo)


def _router_kernel(x_ref, sh_ref, sc_ref, g_ref, rw0_ref, rw1_ref, rb_ref, tri_ref,
                   h_ref, idx_ref, gate_ref, rank_ref, cnt_ref, base_ref):
    i = pl.program_id(0)

    @pl.when(i == 0)
    def _():
        base_ref[...] = jnp.zeros_like(base_ref)

    x = x_ref[...]
    ms = jnp.mean(x * x, axis=-1, keepdims=True)
    h = x * lax.rsqrt(ms + EPS) * g_ref[...] * (1.0 + sc_ref[0]) + sh_ref[0]
    h_ref[...] = h
    h0 = h.astype(BF16)
    h1 = (h - h0.astype(F32)).astype(BF16)
    logits = _dot(h0, rw0_ref[...]) + (_dot(h0, rw1_ref[...]) + _dot(h1, rw0_ref[...])) + rb_ref[...]
    lane = lax.broadcasted_iota(jnp.int32, logits.shape, 1)
    lane_f = lane.astype(F32)
    cur = jnp.where(lane < N_EXPERTS, logits, NEG_INF)
    vals, sel = [], []
    for _ in range(TOP_K):
        m = jnp.max(cur, axis=1, keepdims=True)
        ix = jnp.min(jnp.where(cur == m, lane_f, float(LANES)), axis=1, keepdims=True)
        hit = lane_f == ix
        vals.append(m)
        sel.append(hit)
        cur = jnp.where(hit, NEG_INF, cur)
    exps = [jnp.exp(v - vals[0]) for v in vals]
    den = exps[0] + exps[1] + exps[2] + exps[3]
    onehot = jnp.zeros(logits.shape, F32)
    for hit in sel:
        onehot = onehot + jnp.where(hit, 1.0, 0.0)
    before = _dot(tri_ref[...], onehot.astype(BF16)) + base_ref[...]
    idx_out = jnp.zeros(logits.shape, F32)
    gate_out = jnp.zeros(logits.shape, F32)
    rank_out = jnp.zeros(logits.shape, F32)
    for kk in range(TOP_K):
        slot = lane == kk
        e_id = jnp.sum(jnp.where(sel[kk], lane_f, 0.0), axis=1, keepdims=True)
        rk = jnp.sum(jnp.where(sel[kk], before, 0.0), axis=1, keepdims=True)
        idx_out = jnp.where(slot, e_id, idx_out)
        gate_out = jnp.where(slot, exps[kk] / den, gate_out)
        rank_out = jnp.where(slot, rk, rank_out)
    idx_ref[...] = idx_out.astype(jnp.int32)
    gate_ref[...] = gate_out
    rank_ref[...] = rank_out.astype(jnp.int32)
    base_ref[...] = base_ref[...] + jnp.sum(onehot, axis=0, keepdims=True)
    cnt_ref[...] = base_ref[...]


def _router(x1, shift, scale, g, rw0, rw1, rb, S):
    N = x1.shape[0]
    TM = min(512, S)
    tpb = S // TM
    r = jnp.arange(TM)
    tri = (r[None, :] < r[:, None]).astype(BF16)
    row = lambda i: (i, 0)
    const = lambda i: (0, 0)
    lanes = pl.BlockSpec((TM, LANES), row)
    return pl.pallas_call(
        _router_kernel,
        grid=(N // TM,),
        in_specs=[pl.BlockSpec((TM, D_MODEL), row),
                  pl.BlockSpec((1, 1, D_MODEL), lambda i: (i // tpb, 0, 0)),
                  pl.BlockSpec((1, 1, D_MODEL), lambda i: (i // tpb, 0, 0)),
                  pl.BlockSpec((1, D_MODEL), const),
                  pl.BlockSpec((D_MODEL, LANES), const),
                  pl.BlockSpec((D_MODEL, LANES), const),
                  pl.BlockSpec((1, LANES), const),
                  pl.BlockSpec((TM, TM), const)],
        out_specs=[pl.BlockSpec((TM, D_MODEL), row), lanes, lanes, lanes, pl.BlockSpec((1, LANES), const)],
        out_shape=[jax.ShapeDtypeStruct((N, D_MODEL), F32),
                   jax.ShapeDtypeStruct((N, LANES), jnp.int32),
                   jax.ShapeDtypeStruct((N, LANES), F32),
                   jax.ShapeDtypeStruct((N, LANES), jnp.int32),
                   jax.ShapeDtypeStruct((1, LANES), F32)],
        scratch_shapes=[pltpu.VMEM((1, LANES), F32)],
        compiler_params=_params(("arbitrary",)),
        name="router",
    )(x1, shift, scale, g, rw0, rw1, rb, tri)


INDEX_SLICE = ROW_MOVE_TILE * TOP_K


def _row_copy_out(h_ref, xs_hbm, sem, r, dst):
    return pltpu.make_async_copy(h_ref.at[pl.ds(r, 1)], xs_hbm.at[pl.ds(dst, 1)], sem)


def _zero_tile_copy(zero_ref, xs_hbm, sem, start):
    return pltpu.make_async_copy(zero_ref, xs_hbm.at[pl.ds(pl.multiple_of(start, MOE_TILE), MOE_TILE)], sem)


def _dispatch_kernel(pad_end_ref, padded_ref, dest_hbm, h_ref, xs_hbm, idx_smem, zero_ref, sem_idx, sem_rows, sem_zero):
    i = pl.program_id(0)
    fetch = pltpu.make_async_copy(dest_hbm.at[pl.ds(i * INDEX_SLICE, INDEX_SLICE)], idx_smem, sem_idx)
    fetch.start()

    @pl.when(i == 0)
    def _():
        zero_ref[...] = jnp.zeros_like(zero_ref)
        for e in range(N_EXPERTS):
            @pl.when(padded_ref[e] > 0)
            def _():
                _zero_tile_copy(zero_ref, xs_hbm, sem_zero, pad_end_ref[e] - MOE_TILE).start()
        for e in range(N_EXPERTS):
            @pl.when(padded_ref[e] > 0)
            def _():
                _zero_tile_copy(zero_ref, xs_hbm, sem_zero, 0).wait()

    fetch.wait()

    def start(r, carry):
        for kk in range(TOP_K):
            _row_copy_out(h_ref, xs_hbm, sem_rows, r, idx_smem[r * TOP_K + kk]).start()
        return carry

    lax.fori_loop(0, ROW_MOVE_TILE, start, 0, unroll=4)
    for kk in range(TOP_K):
        pltpu.make_async_copy(h_ref, xs_hbm.at[pl.ds(0, ROW_MOVE_TILE)], sem_rows).wait()


def _dispatch(pad_ends, padded, dest_flat, h2, n_rows):
    N = h2.shape[0]
    return pl.pallas_call(
        _dispatch_kernel,
        grid_spec=pltpu.PrefetchScalarGridSpec(
            num_scalar_prefetch=2,
            grid=(N // ROW_MOVE_TILE,),
            in_specs=[pl.BlockSpec(memory_space=pl.ANY),
                      pl.BlockSpec((ROW_MOVE_TILE, D_MODEL), lambda i, pe, pd: (i, 0))],
            out_specs=pl.BlockSpec(memory_space=pl.ANY),
            scratch_shapes=[pltpu.SMEM((INDEX_SLICE,), jnp.int32), pltpu.VMEM((MOE_TILE, D_MODEL), F32),
                            pltpu.SemaphoreType.DMA, pltpu.SemaphoreType.DMA, pltpu.SemaphoreType.DMA]),
        out_shape=jax.ShapeDtypeStruct((n_rows, D_MODEL), F32),
        compiler_params=_params(("arbitrary",)),
        name="moe_dispatch",
    )(pad_ends, padded, dest_flat, h2)


def _expert_kernel(te_ref, nu_ref, xs_ref, wg_ref, bg_ref, wl_ref, bl_ref, wd_ref, bd_ref, ys_ref):
    del te_ref

    @pl.when(pl.program_id(0) < nu_ref[0])
    def _():
        xb = xs_ref[...].astype(BF16)
        glu = jnp.minimum(_dot(xb, wg_ref[0]) + bg_ref[0], SWIGLU_LIMIT)
        lin = jnp.clip(_dot(xb, wl_ref[0]) + bl_ref[0], -SWIGLU_LIMIT, SWIGLU_LIMIT)
        act = glu * _sigmoid(SWIGLU_ALPHA * glu) * (lin + 1.0)
        ys_ref[...] = _dot(act.astype(BF16), wd_ref[0]) + bd_ref[0]


def _experts(tile_expert, n_used, xs, wg, bg, wl, bl, wd, bd):
    n_rows = xs.shape[0]
    n_tiles = n_rows // MOE_TILE
    wspec = pl.BlockSpec((1, D_MODEL, D_MODEL), lambda i, te, nu: (te[i], 0, 0))
    bspec = pl.BlockSpec((1, 1, D_MODEL), lambda i, te, nu: (te[i], 0, 0))
    rows = pl.BlockSpec((MOE_TILE, D_MODEL), lambda i, te, nu: (jnp.minimum(i, nu[0] - 1), 0))
    return pl.pallas_call(
        _expert_kernel,
        grid_spec=pltpu.PrefetchScalarGridSpec(
            num_scalar_prefetch=2,
            grid=(n_tiles,),
            in_specs=[rows, wspec, bspec, wspec, bspec, wspec, bspec],
            out_specs=rows),
        out_shape=jax.ShapeDtypeStruct((n_rows, D_MODEL), F32),
        compiler_params=_params(("arbitrary",)),
        name="moe_experts",
    )(tile_expert, n_used, xs, wg, bg, wl, bl, wd, bd)


def _row_copy_in(ys_hbm, buf_ref, sem, src, kk, r):
    return pltpu.make_async_copy(ys_hbm.at[pl.ds(src, 1)], buf_ref.at[kk, pl.ds(r, 1)], sem)


def _combine_kernel(dest_hbm, ys_hbm, gate_ref, x_ref, g2_ref, fg_ref, o_ref, idx_smem, buf_ref, sem_idx, sem_rows):
    i = pl.program_id(0)
    fetch = pltpu.make_async_copy(dest_hbm.at[pl.ds(i * INDEX_SLICE, INDEX_SLICE)], idx_smem, sem_idx)
    fetch.start()
    fetch.wait()

    def start(r, carry):
        for kk in range(TOP_K):
            _row_copy_in(ys_hbm, buf_ref, sem_rows, idx_smem[r * TOP_K + kk], kk, r).start()
        return carry

    lax.fori_loop(0, ROW_MOVE_TILE, start, 0, unroll=4)
    for kk in range(TOP_K):
        pltpu.make_async_copy(ys_hbm.at[pl.ds(0, ROW_MOVE_TILE)], buf_ref.at[kk], sem_rows).wait()

    gates = gate_ref[...]
    moe = gates[:, 0:1] * buf_ref[0]
    for kk in range(1, TOP_K):
        moe = moe + gates[:, kk:kk + 1] * buf_ref[kk]
    x = x_ref[...] + g2_ref[0] * moe
    ms = jnp.mean(x * x, axis=-1, keepdims=True)
    o_ref[...] = x * lax.rsqrt(ms + EPS) * fg_ref[...]


def _combine(dest_flat, ys, gates, x1, gate2, final_g, S):
    N = x1.shape[0]
    TM = ROW_MOVE_TILE
    tpb = S // TM
    row = lambda i: (i, 0)
    return pl.pallas_call(
        _combine_kernel,
        grid=(N // TM,),
        in_specs=[pl.BlockSpec(memory_space=pl.ANY),
                  pl.BlockSpec(memory_space=pl.ANY),
                  pl.BlockSpec((TM, LANES), row),
                  pl.BlockSpec((TM, D_MODEL), row),
                  pl.BlockSpec((1, 1, D_MODEL), lambda i: (i // tpb, 0, 0)),
                  pl.BlockSpec((1, D_MODEL), lambda i: (0, 0))],
        out_specs=pl.BlockSpec((TM, D_MODEL), row),
        out_shape=jax.ShapeDtypeStruct((N, D_MODEL), F32),
        scratch_shapes=[pltpu.SMEM((INDEX_SLICE,), jnp.int32), pltpu.VMEM((TOP_K, TM, D_MODEL), F32),
                        pltpu.SemaphoreType.DMA, pltpu.SemaphoreType.DMA],
        compiler_params=_params(("arbitrary",)),
        name="moe_combine",
    )(dest_flat, ys, gates, x1, gate2, final_g)


def _pad_lanes(a, offset=0):
    return jnp.pad(a, ((0, 0), (offset, LANES - offset - a.shape[1])))


def kernel(x, c, ada_w, ada_b, norm1_g, norm2_g, w_in, b_gate, conv_w, a_log, dt_bias, gdn_norm_g, w_branch_a,
           diff_lambda, diff_norm_g, w_branch_b, w_out, router_w, router_b, w_glu, b_glu, w_lin, b_lin, w_down,
           b_down, final_g):
    B, S, D = x.shape
    N = B * S
    assert D == D_MODEL and S % GROUP == 0 and ada_w.shape[0] == 1
    x2 = x.reshape(N, D)

    mod = _adaln(c, ada_w[0], ada_b[0])
    shift1, scale1, gate1, shift2, scale2, gate2 = [m.reshape(B, 1, D) for m in jnp.split(mod, 6, axis=-1)]

    wi = w_in[0]
    n_a = 4 * D
    n_small = 4 * N_HEADS
    w_big = jnp.concatenate([wi[:, :n_a], wi[:, n_a + n_small:]], axis=1).astype(BF16)
    w_small = _pad_lanes(wi[:, n_a:n_a + n_small]).astype(BF16)
    alog_row = _pad_lanes(a_log[0].reshape(1, -1), GATE_LANE0)
    dt_row = _pad_lanes(dt_bias[0].reshape(1, -1), GATE_LANE0)
    proj, beta, G, Gl, cdl = _inproj(x2, shift1, scale1, norm1_g, w_big, w_small, alog_row, dt_row, S)
    proj3 = proj.reshape(B, S, N_COL_BLOCKS * D)

    qkv = _gdn_conv(proj3, conv_w[0])
    r3 = lambda a: a.reshape(B, S, LANES)
    nC = S // GDN_CHUNK
    GT = jnp.transpose(r3(G)[:, :, GATE_LANE0:GATE_LANE0 + 2 * N_HEADS], (0, 2, 1)).reshape(B, 2 * N_HEADS, 1, S)
    cd = r3(cdl).reshape(B, nC, GDN_CHUNK, LANES)[:, :, 0, GATE_LANE0:GATE_LANE0 + 2 * N_HEADS]
    cd = jnp.transpose(cd.reshape(B, nC, 2, N_HEADS), (0, 2, 3, 1)).reshape(-1)
    u, w, qd, att, kdt = _gdn_prep(qkv, r3(beta), r3(G), r3(Gl), GT)
    oAf, oAb = _gdn_scan(cd, u, w, qd, att, kdt)

    half = DIFF_DH // 2
    inv_freq = ROPE_THETA ** (-jnp.arange(half, dtype=F32) / half)
    ang = jnp.arange(S, dtype=F32)[:, None] * inv_freq[None, :]
    cos_t = jnp.tile(jnp.cos(ang), (1, 4))
    sin_h = jnp.sin(ang)
    sin_t = jnp.tile(jnp.concatenate([-sin_h, sin_h], axis=1), (1, 2))
    qt, kr, vt = _rope(proj3, cos_t, sin_t)
    lam_a = jnp.pad(_pad_lanes(diff_lambda[0][0::2]), ((0, 6), (0, 0)))
    lam_b = jnp.pad(_pad_lanes(diff_lambda[0][1::2]), ((0, 6), (0, 0)))
    oB = _diff_attn(qt, kr, vt, lam_a, lam_b, diff_norm_g)

    x1 = _merge(oAf.reshape(N, D), oAb.reshape(N, D), proj, oB.reshape(N, D), b_gate, x2, gate1, gdn_norm_g,
                w_branch_a[0].astype(BF16), w_branch_b[0].astype(BF16), w_out[0].astype(BF16), S)

    rw = _pad_lanes(router_w[0])
    rw0 = rw.astype(BF16)
    rw1 = (rw - rw0.astype(F32)).astype(BF16)
    h2, idx, gates, rank, counts = _router(x1, shift2, scale2, norm2_g, rw0, rw1, _pad_lanes(router_b), S)
    cnt = counts[0, :N_EXPERTS].astype(jnp.int32)
    padded = (cnt + MOE_TILE - 1) // MOE_TILE * MOE_TILE
    pad_ends = jnp.cumsum(padded)
    pad_starts = pad_ends - padded
    dest = (pad_starts[idx[:, :TOP_K]] + rank[:, :TOP_K]).reshape(-1)
    n_tiles = -(-(N * TOP_K) // MOE_TILE) + N_EXPERTS
    tile_start = jnp.arange(n_tiles, dtype=jnp.int32) * MOE_TILE
    tile_expert = jnp.sum((tile_start[:, None] >= pad_ends[None, :]).astype(jnp.int32), axis=1)
    tile_expert = jnp.minimum(tile_expert, N_EXPERTS - 1)
    n_used = (pad_ends[N_EXPERTS - 1:] // MOE_TILE).astype(jnp.int32)
    xs = _dispatch(pad_ends.astype(jnp.int32), padded, dest, h2, n_tiles * MOE_TILE)
    ys = _experts(tile_expert, n_used, xs, w_glu[0].astype(BF16), b_glu[0][:, None, :], w_lin[0].astype(BF16),
                  b_lin[0][:, None, :], w_down[0].astype(BF16), b_down[0][:, None, :])
    out = _combine(dest, ys, gates, x1, gate2, final_g.reshape(1, D), S)
    return out.reshape(B, S, D)
```

```python
import functools
import math

import jax
import jax.numpy as jnp
from jax import lax
from jax.experimental import pallas as pl
from jax.experimental.pallas import tpu as pltpu

F32 = jnp.float32
BF16 = jnp.bfloat16

D_MODEL = 1024
EPS = 1e-6
N_HEADS = 8
HEAD_W = 128
GDN_CHUNK = 64
CONV_WIDTH = 5
DIFF_DH = 64
ROPE_THETA = 10000.0
LAM_INIT = 0.8 - 0.6 * math.exp(-0.3 * 0)
N_EXPERTS = 32
TOP_K = 4
SWIGLU_ALPHA = 1.702
SWIGLU_LIMIT = 7.0

LANES = 128
GROUP = 256
CHUNKS_PER_GROUP = GROUP // GDN_CHUNK
CHUNK_SHIFT = GDN_CHUNK.bit_length() - 1
MOE_TILE = 512
ROW_MOVE_TILE = 1024
NEG_INF = float("-inf")

COL_QA, COL_KA, COL_VA, COL_ZA, COL_QB, COL_KB, COL_VB, COL_GA, COL_GB = range(9)
N_COL_BLOCKS = 9


def _params(sem, vmem_mb=48):
    return pltpu.CompilerParams(dimension_semantics=sem, vmem_limit_bytes=vmem_mb * 1024 * 1024)


def _dot(a, b):
    return jnp.dot(a, b, preferred_element_type=F32)


def _dot_nt(a, b):
    return lax.dot_general(a, b, (((1,), (1,)), ((), ())), preferred_element_type=F32)


def _sigmoid(x):
    return 1.0 / (1.0 + jnp.exp(-x))


def _split3(x):
    a = x.astype(BF16)
    r = x - a.astype(F32)
    b = r.astype(BF16)
    c = (r - b.astype(F32)).astype(BF16)
    return a, b, c


def _adaln_kernel(c_ref, w_ref, b_ref, o_ref):
    c = c_ref[...]
    cond = c * _sigmoid(c)
    c0, c1, c2 = _split3(cond)
    w0, w1, w2 = _split3(w_ref[...])
    acc = _dot(c0, w0) + (_dot(c0, w1) + _dot(c1, w0)) + (_dot(c0, w2) + _dot(c1, w1) + _dot(c2, w0))
    o_ref[...] = acc + b_ref[...]


def _adaln(c, ada_w, ada_b):
    B = c.shape[0]
    n = ada_w.shape[1] // D_MODEL
    return pl.pallas_call(
        _adaln_kernel,
        grid=(n,),
        in_specs=[pl.BlockSpec((B, D_MODEL), lambda j: (0, 0)),
                  pl.BlockSpec((D_MODEL, D_MODEL), lambda j: (0, j)),
                  pl.BlockSpec((1, D_MODEL), lambda j: (0, j))],
        out_specs=pl.BlockSpec((B, D_MODEL), lambda j: (0, j)),
        out_shape=jax.ShapeDtypeStruct((B, n * D_MODEL), F32),
        compiler_params=_params(("parallel",)),
        name="adaln",
    )(c, ada_w, ada_b.reshape(1, -1))


def _inproj_kernel(x_ref, sh_ref, sc_ref, g_ref, w_ref, ws_ref, alog_ref, dt_ref,
                   o_ref, beta_ref, gcum_ref, glast_ref, cd_ref, h_ref):
    @pl.when(pl.program_id(1) == 0)
    def _():
        x = x_ref[...]
        ms = jnp.mean(x * x, axis=-1, keepdims=True)
        y = x * lax.rsqrt(ms + EPS) * g_ref[...]
        h = (y * (1.0 + sc_ref[0]) + sh_ref[0]).astype(BF16)
        h_ref[...] = h
        _gdn_gates(_dot(h, ws_ref[...]), alog_ref[...], dt_ref[...], beta_ref, gcum_ref, glast_ref, cd_ref)

    o_ref[...] = _dot(h_ref[...], w_ref[...]).astype(o_ref.dtype)


def _inproj(x2, shift, scale, g, w_big, w_small, alog_row, dt_row, S):
    N = x2.shape[0]
    TM = min(1024, S)
    tpb = S // TM
    const = lambda i, j: (0, 0)
    lanes = pl.BlockSpec((TM, LANES), lambda i, j: (i, 0))
    return pl.pallas_call(
        _inproj_kernel,
        grid=(N // TM, N_COL_BLOCKS),
        in_specs=[pl.BlockSpec((TM, D_MODEL), lambda i, j: (i, 0)),
                  pl.BlockSpec((1, 1, D_MODEL), lambda i, j: (i // tpb, 0, 0)),
                  pl.BlockSpec((1, 1, D_MODEL), lambda i, j: (i // tpb, 0, 0)),
                  pl.BlockSpec((1, D_MODEL), const),
                  pl.BlockSpec((D_MODEL, D_MODEL), lambda i, j: (0, j)),
                  pl.BlockSpec((D_MODEL, LANES), const),
                  pl.BlockSpec((1, LANES), const),
                  pl.BlockSpec((1, LANES), const)],
        out_specs=[pl.BlockSpec((TM, D_MODEL), lambda i, j: (i, j)), lanes, lanes, lanes, lanes],
        out_shape=[jax.ShapeDtypeStruct((N, N_COL_BLOCKS * D_MODEL), BF16)]
                  + [jax.ShapeDtypeStruct((N, LANES), F32)] * 4,
        scratch_shapes=[pltpu.VMEM((TM, D_MODEL), BF16)],
        compiler_params=_params(("parallel", "arbitrary")),
        name="inproj",
    )(x2, shift, scale, g, w_big, w_small, alog_row, dt_row)


HALO = 16


CONV_BLOCK = 256
CONV_PAD = (CONV_WIDTH - 1) // 2
CONV_TAPS = tuple(j for j in range(CONV_WIDTH) if j != CONV_PAD)


def _conv_shifts():
    r = jnp.arange(CONV_BLOCK)[:, None]
    c = jnp.arange(CONV_BLOCK)[None, :]
    return jnp.stack([c == r + (j - CONV_PAD) for j in CONV_TAPS]).astype(BF16)


def _conv_kernel(cur_ref, prev_ref, next_ref, w_ref, shift_ref, o_ref, ext_ref, *, TR):
    i = pl.program_id(1)
    g = pl.program_id(2)
    last = pl.num_programs(1) - 1
    ext_ref[8:8 + TR, :] = cur_ref[0].astype(F32)
    pv = prev_ref[0].astype(F32)[HALO - 8:HALO]
    nx = next_ref[0].astype(F32)[0:8]
    ext_ref[0:8, :] = jnp.where(i > 0, pv, 0.0)
    ext_ref[TR + 8:TR + 16, :] = jnp.where(i < last, nx, 0.0)
    ones = jnp.ones((HEAD_W, HEAD_W), BF16)
    qscale = jnp.where(g == 0, HEAD_W ** -0.5, 1.0)

    def edge_rows(row0):
        e = ext_ref[8 + row0 - CONV_PAD:16 + row0 - CONV_PAD, :] * w_ref[0:1, :]
        for j in range(1, CONV_WIDTH):
            e = e + ext_ref[8 + row0 - CONV_PAD + j:16 + row0 - CONV_PAD + j, :] * w_ref[j:j + 1, :]
        return e

    for blk in range(TR // CONV_BLOCK):
        r0 = blk * CONV_BLOCK
        rows = slice(r0, r0 + CONV_BLOCK)
        ub = cur_ref[0, rows, :]
        acc = ext_ref[8 + r0:8 + r0 + CONV_BLOCK, :] * w_ref[CONV_PAD:CONV_PAD + 1, :]
        for si, j in enumerate(CONV_TAPS):
            acc = acc + _dot(shift_ref[si], ub) * w_ref[j:j + 1, :]
        acc = jnp.concatenate([edge_rows(r0), acc[8:CONV_BLOCK - 8], edge_rows(r0 + CONV_BLOCK - 8)], axis=0)
        y = acc * _sigmoid(acc)

        for h in range(N_HEADS):
            cols = slice(h * HEAD_W, (h + 1) * HEAD_W)
            yh = y[:, cols]
            ss = _dot((yh * yh).astype(BF16), ones)
            scale = jnp.where(g < 2, lax.rsqrt(ss + EPS) * qscale, 1.0)
            o_ref[0, rows, cols] = (yh * scale).astype(o_ref.dtype)


def _gdn_conv(proj3, conv_w):
    B, S, _ = proj3.shape
    TR = min(512, S)
    nT = S // TR
    rb = TR // HALO
    nH = S // HALO
    return pl.pallas_call(
        functools.partial(_conv_kernel, TR=TR),
        grid=(B, nT, 3),
        in_specs=[pl.BlockSpec((1, TR, D_MODEL), lambda b, i, g: (b, i, g)),
                  pl.BlockSpec((1, HALO, D_MODEL), lambda b, i, g: (b, jnp.maximum(i * rb - 1, 0), g)),
                  pl.BlockSpec((1, HALO, D_MODEL), lambda b, i, g: (b, jnp.minimum((i + 1) * rb, nH - 1), g)),
                  pl.BlockSpec((CONV_WIDTH, D_MODEL), lambda b, i, g: (0, g)),
                  pl.BlockSpec((len(CONV_TAPS), CONV_BLOCK, CONV_BLOCK), lambda b, i, g: (0, 0, 0))],
        out_specs=pl.BlockSpec((1, TR, D_MODEL), lambda b, i, g: (b, i, g)),
        out_shape=jax.ShapeDtypeStruct((B, S, 3 * D_MODEL), BF16),
        scratch_shapes=[pltpu.VMEM((TR + 16, D_MODEL), F32)],
        compiler_params=_params(("parallel", "parallel", "parallel")),
        name="gdn_conv",
    )(proj3, proj3, proj3, conv_w, _conv_shifts())


GATE_LANE0 = 16


def _gdn_gates(x, alog, dt, beta_ref, g_ref, gl_ref, cd_ref):
    lane = lax.broadcasted_iota(jnp.int32, (GROUP, LANES), 1)
    r = lax.broadcasted_iota(jnp.int32, (GROUP, GROUP), 0)
    c = lax.broadcasted_iota(jnp.int32, (GROUP, GROUP), 1)
    same = (r >> CHUNK_SHIFT) == (c >> CHUNK_SHIFT)
    lower = jnp.where(same & (c <= r), 1.0, 0.0).astype(BF16)
    upper = jnp.where(same & (c >= r), 1.0, 0.0).astype(BF16)
    block = jnp.where(same, 1.0, 0.0).astype(BF16)
    beta_ref[...] = _sigmoid(x)
    for gi in range(x.shape[0] // GROUP):
        rows = slice(gi * GROUP, (gi + 1) * GROUP)
        z = x[rows] + dt
        softplus = jnp.maximum(z, 0.0) + jnp.log(1.0 + jnp.exp(-jnp.abs(z)))
        gd = -jnp.exp(alog) * softplus
        gd = jnp.where((lane >= GATE_LANE0) & (lane < GATE_LANE0 + 2 * N_HEADS), gd, 0.0)
        p0, p1, p2 = _split3(gd)
        g_fwd = _dot(lower, p0) + _dot(lower, p1) + _dot(lower, p2)
        g_bwd = _dot(upper, p0) + _dot(upper, p1) + _dot(upper, p2)
        tot = _dot(block, p0) + _dot(block, p1) + _dot(block, p2)
        G = jnp.where(lane < GATE_LANE0 + N_HEADS, g_fwd, g_bwd)
        g_ref[rows, :] = G
        gl_ref[rows, :] = tot - G
        cd_ref[rows, :] = jnp.exp(tot)


PREP_HEADS = 2
(MASK_STRICT_LO, MASK_STRICT_UP, MASK_INCL_LO, MASK_INCL_UP, MASK_EYE, MASK_BLK4, MASK_OFF0) = range(7)
N_MASKS = MASK_OFF0 + (CHUNK_SHIFT - 2)


def _prep_masks():
    r = jnp.arange(GROUP)[:, None]
    c = jnp.arange(GROUP)[None, :]
    same = (r >> CHUNK_SHIFT) == (c >> CHUNK_SHIFT)
    masks = [same & (c < r), same & (c > r), same & (c <= r), same & (c >= r), r == c, (r >> 2) == (c >> 2)]
    for shift in range(2, CHUNK_SHIFT):
        masks.append(((r >> shift) != (c >> shift)) & ((r >> (shift + 1)) == (c >> (shift + 1))))
    return jnp.stack(masks).astype(F32)


def _col(x, l, lane):
    return jnp.broadcast_to(jnp.sum(jnp.where(lane == l, x, 0.0), axis=1, keepdims=True), x.shape)


def _prep_kernel(q_ref, k_ref, v_ref, beta_ref, g_ref, gl_ref, gtf_ref, gtb_ref, mask_ref, bmask_ref,
                 u_ref, w_ref, qd_ref, at_ref, kdt_ref):
    hp = pl.program_id(1)
    lane = lax.broadcasted_iota(jnp.int32, (GROUP, LANES), 1)
    wide = lambda a: jnp.concatenate([a, a], axis=1)
    chains = [(hh, d) for hh in range(PREP_HEADS) for d in range(2)]
    p, rhs = {}, {}
    for hh in range(PREP_HEADS):
        cols = slice(hh * HEAD_W, (hh + 1) * HEAD_W)
        q = q_ref[0, :, cols]
        k = k_ref[0, :, cols]
        qf = q.astype(F32)
        kf = k.astype(F32)
        vf = v_ref[0, :, cols].astype(F32)
        kk = _dot_nt(k, k)
        qk = _dot_nt(q, k)
        for d in range(2):
            lb = d * N_HEADS + hp * PREP_HEADS + hh
            beta_c = _col(beta_ref[0], lb, lane)
            g_c = _col(g_ref[0], GATE_LANE0 + lb, lane)
            eg_c = jnp.exp(g_c)
            egl_c = jnp.exp(_col(gl_ref[0], GATE_LANE0 + lb, lane))
            g_r = (gtf_ref if d == 0 else gtb_ref)[0, hh]
            dec = jnp.exp(jnp.minimum(wide(g_c) - g_r, 0.0))
            p[hh, d] = (-(kk * wide(beta_c)) * dec * mask_ref[MASK_STRICT_LO + d]).astype(BF16)
            att = qk * dec * mask_ref[MASK_INCL_LO + d]
            rhs[hh, d] = jnp.concatenate([vf * beta_c, kf * (beta_c * eg_c)], axis=1).astype(BF16)
            qd_ref[0, d, :, cols] = (qf * eg_c).astype(qd_ref.dtype)
            kdt = (kf * egl_c).T
            for ci in range(CHUNKS_PER_GROUP):
                sl = slice(ci * GDN_CHUNK, (ci + 1) * GDN_CHUNK)
                at_ref[0, d, hh, ci] = att[sl, sl].astype(at_ref.dtype)
                kdt_ref[0, d, hh, ci] = kdt[:, sl].astype(kdt_ref.dtype)
    p4 = {ch: p[ch] * bmask_ref[0] for ch in chains}
    sq = {ch: _dot(p4[ch], p4[ch]).astype(BF16) for ch in chains}
    t = {ch: mask_ref[MASK_EYE] + p4[ch].astype(F32) for ch in chains}
    t = {ch: t[ch] + _dot(t[ch].astype(BF16), sq[ch]) for ch in chains}
    for lvl in range(CHUNK_SHIFT - 2):
        tb = {ch: t[ch].astype(BF16) for ch in chains}
        x = {ch: _dot(tb[ch], p[ch] * bmask_ref[1 + lvl]).astype(BF16) for ch in chains}
        t = {ch: t[ch] + _dot(x[ch], tb[ch]) for ch in chains}
    uw = {ch: _dot(t[ch].astype(BF16), rhs[ch]) for ch in chains}
    for hh, d in chains:
        cols = slice(hh * HEAD_W, (hh + 1) * HEAD_W)
        u_ref[0, d, :, cols] = uw[hh, d][:, :HEAD_W].astype(u_ref.dtype)
        w_ref[0, d, :, cols] = uw[hh, d][:, HEAD_W:].astype(w_ref.dtype)


def _gdn_prep(qkv, beta, G, Gl, GT):
    B, S, _ = qkv.shape
    nG = S // GROUP
    nC = S // GDN_CHUNK
    PW = PREP_HEADS * HEAD_W
    nP = N_HEADS // PREP_HEADS
    sm = pl.BlockSpec((1, GROUP, LANES), lambda b, h, g: (b, g, 0))
    big = pl.BlockSpec((1, 2, GROUP, PW), lambda b, h, g: (b, 0, g, h))
    masks = _prep_masks()
    return pl.pallas_call(
        _prep_kernel,
        grid=(B, nP, nG),
        in_specs=[pl.BlockSpec((1, GROUP, PW), lambda b, h, g: (b, g, h)),
                  pl.BlockSpec((1, GROUP, PW), lambda b, h, g: (b, g, nP + h)),
                  pl.BlockSpec((1, GROUP, PW), lambda b, h, g: (b, g, 2 * nP + h)),
                  sm, sm, sm,
                  pl.BlockSpec((1, PREP_HEADS, 1, GROUP), lambda b, h, g: (b, h, 0, g)),
                  pl.BlockSpec((1, PREP_HEADS, 1, GROUP), lambda b, h, g: (b, nP + h, 0, g)),
                  pl.BlockSpec((MASK_BLK4, GROUP, GROUP), lambda b, h, g: (0, 0, 0)),
                  pl.BlockSpec((N_MASKS - MASK_BLK4, GROUP, GROUP), lambda b, h, g: (0, 0, 0))],
        out_specs=[big, big, big,
                   pl.BlockSpec((1, 2, PREP_HEADS, CHUNKS_PER_GROUP, GDN_CHUNK, GDN_CHUNK),
                                lambda b, h, g: (b, 0, h, g, 0, 0)),
                   pl.BlockSpec((1, 2, PREP_HEADS, CHUNKS_PER_GROUP, HEAD_W, GDN_CHUNK),
                                lambda b, h, g: (b, 0, h, g, 0, 0))],
        out_shape=[jax.ShapeDtypeStruct((B, 2, S, D_MODEL), BF16)] * 3
                  + [jax.ShapeDtypeStruct((B, 2, N_HEADS, nC, GDN_CHUNK, GDN_CHUNK), BF16),
                     jax.ShapeDtypeStruct((B, 2, N_HEADS, nC, HEAD_W, GDN_CHUNK), BF16)],
        compiler_params=_params(("parallel", "parallel", "parallel")),
        name="gdn_prep",
    )(qkv, qkv, qkv, beta, G, Gl, GT, GT, masks[:MASK_BLK4], masks[MASK_BLK4:].astype(BF16))


def _scan_kernel(cd_ref, uf_ref, wf_ref, qdf_ref, atf_ref, kdtf_ref, ub_ref, wb_ref, qdb_ref, atb_ref, kdtb_ref,
                 of_ref, ob_ref, state_ref, *, nc, nC):
    b = pl.program_id(0)
    t = pl.program_id(1)
    nT = pl.num_programs(1)

    @pl.when(t == 0)
    def _():
        state_ref[...] = jnp.zeros_like(state_ref)

    dirs = ((uf_ref, wf_ref, qdf_ref, atf_ref, kdtf_ref, of_ref), (ub_ref, wb_ref, qdb_ref, atb_ref, kdtb_ref, ob_ref))

    def chunk(ci, carry):
        work = []
        for d, refs in enumerate(dirs):
            c = ci if d == 0 else nc - 1 - ci
            tt = t if d == 0 else nT - 1 - t
            row = pl.multiple_of(c * GDN_CHUNK, GDN_CHUNK)
            for h in range(N_HEADS):
                work.append((d, h, c, row, ((b * 2 + d) * N_HEADS + h) * nC + tt * nc + c, refs))
        s_old = [state_ref[d, h] for d, h, *_ in work]
        sb = [s.astype(BF16) for s in s_old]
        tile = lambda ref, row, h: ref[0, 0, pl.ds(row, GDN_CHUNK), h * HEAD_W:(h + 1) * HEAD_W]
        ws = [_dot(tile(refs[1], row, h), sb[i]) for i, (d, h, c, row, gi, refs) in enumerate(work)]
        qs = [_dot(tile(refs[2], row, h), sb[i]) for i, (d, h, c, row, gi, refs) in enumerate(work)]
        vb = [(tile(refs[0], row, h).astype(F32) - ws[i]).astype(BF16)
              for i, (d, h, c, row, gi, refs) in enumerate(work)]
        o = [qs[i] + _dot(refs[3][0, 0, h, c], vb[i]) for i, (d, h, c, row, gi, refs) in enumerate(work)]
        upd = [_dot(refs[4][0, 0, h, c], vb[i]) for i, (d, h, c, row, gi, refs) in enumerate(work)]
        for i, (d, h, c, row, gi, refs) in enumerate(work):
            state_ref[d, h] = s_old[i] * cd_ref[gi] + upd[i]
            refs[5][0, pl.ds(row, GDN_CHUNK), h * HEAD_W:(h + 1) * HEAD_W] = o[i].astype(of_ref.dtype)
        return carry

    lax.fori_loop(0, nc, chunk, 0)


def _gdn_scan(cd, u, w, qd, att, kdt):
    B, _, S, _ = u.shape
    TC = min(512, S)
    nT = S // TC
    nc = TC // GDN_CHUNK
    nC = S // GDN_CHUNK
    fwd = lambda b, t: t
    bwd = lambda b, t: nT - 1 - t

    def specs(d, tm):
        big = pl.BlockSpec((1, 1, TC, D_MODEL), lambda b, t: (b, d, tm(b, t), 0))
        return [big, big, big,
                pl.BlockSpec((1, 1, N_HEADS, nc, GDN_CHUNK, GDN_CHUNK), lambda b, t: (b, d, 0, tm(b, t), 0, 0)),
                pl.BlockSpec((1, 1, N_HEADS, nc, HEAD_W, GDN_CHUNK), lambda b, t: (b, d, 0, tm(b, t), 0, 0))]

    return pl.pallas_call(
        functools.partial(_scan_kernel, nc=nc, nC=nC),
        grid=(B, nT),
        in_specs=[pl.BlockSpec(memory_space=pltpu.SMEM)] + specs(0, fwd) + specs(1, bwd),
        out_specs=[pl.BlockSpec((1, TC, D_MODEL), lambda b, t: (b, t, 0)),
                   pl.BlockSpec((1, TC, D_MODEL), lambda b, t: (b, nT - 1 - t, 0))],
        out_shape=[jax.ShapeDtypeStruct((B, S, D_MODEL), BF16)] * 2,
        scratch_shapes=[pltpu.VMEM((2, N_HEADS, HEAD_W, HEAD_W), F32)],
        compiler_params=_params(("parallel", "arbitrary")),
        name="gdn_scan",
    )(cd, u, w, qd, att, kdt, u, w, qd, att, kdt)


ATT_TQ = 256
ATT_TK = 256
ATT_NQ = 4
ATT_VROWS = HEAD_W + 16
LOG2E = 1.4426950408889634


def _rope_kernel(q_ref, k_ref, v_ref, cos_ref, sin_ref, qt_ref, kr_ref, vt_ref, *, TR):
    cs = cos_ref[...]
    sn = sin_ref[...]
    lane = lax.broadcasted_iota(jnp.int32, cs.shape, 1)
    first_half = (lane & (DIFF_DH - 1)) < (DIFF_DH // 2)
    qscale = DIFF_DH ** -0.5 * LOG2E

    def rot(x):
        partner = jnp.where(first_half, pltpu.roll(x, HEAD_W - DIFF_DH // 2, 1), pltpu.roll(x, DIFF_DH // 2, 1))
        return x * cs + partner * sn

    for h in range(N_HEADS):
        cols = slice(h * HEAD_W, (h + 1) * HEAD_W)
        qr = rot(q_ref[0, :, cols].astype(F32)) * qscale
        kr_ref[0, :, cols] = rot(k_ref[0, :, cols].astype(F32)).astype(kr_ref.dtype)
        vf = v_ref[0, :, cols].astype(F32)
        for ci in range(TR // ATT_TK):
            rows = slice(ci * ATT_TK, (ci + 1) * ATT_TK)
            vt_ref[0, h, ci, 0:HEAD_W, :] = vf[rows].T.astype(vt_ref.dtype)
            vt_ref[0, h, ci, HEAD_W:ATT_VROWS, :] = jnp.ones((ATT_VROWS - HEAD_W, ATT_TK), vt_ref.dtype)
        for ci in range(TR // ATT_TQ):
            rows = slice(ci * ATT_TQ, (ci + 1) * ATT_TQ)
            qt_ref[0, h, ci] = qr[rows].T.astype(qt_ref.dtype)


def _rope(proj3, cos_t, sin_t):
    B, S, _ = proj3.shape
    TR = min(512, S)
    tab = pl.BlockSpec((TR, HEAD_W), lambda b, i: (i, 0))
    col = lambda cb: pl.BlockSpec((1, TR, D_MODEL), lambda b, i: (b, i, cb))
    return pl.pallas_call(
        functools.partial(_rope_kernel, TR=TR),
        grid=(B, S // TR),
        in_specs=[col(COL_QB), col(COL_KB), col(COL_VB), tab, tab],
        out_specs=[pl.BlockSpec((1, N_HEADS, TR // ATT_TQ, HEAD_W, ATT_TQ), lambda b, i: (b, 0, i, 0, 0)),
                   pl.BlockSpec((1, TR, D_MODEL), lambda b, i: (b, i, 0)),
                   pl.BlockSpec((1, N_HEADS, TR // ATT_TK, ATT_VROWS, ATT_TK), lambda b, i: (b, 0, i, 0, 0))],
        out_shape=[jax.ShapeDtypeStruct((B, N_HEADS, S // ATT_TQ, HEAD_W, ATT_TQ), BF16),
                   jax.ShapeDtypeStruct((B, S, D_MODEL), BF16),
                   jax.ShapeDtypeStruct((B, N_HEADS, S // ATT_TK, ATT_VROWS, ATT_TK), BF16)],
        compiler_params=_params(("parallel", "parallel")),
        name="rope",
    )(proj3, proj3, proj3, cos_t, sin_t)


def _attn_kernel(qt_ref, k_ref, vt_ref, la_ref, lb_ref, g_ref, o_ref, s_ref, acc_ref, *, n_chunks):
    row = lax.broadcasted_iota(jnp.int32, (HEAD_W, ATT_TQ), 0)
    qw = []
    for qb in range(ATT_NQ):
        qt = qt_ref[0, 0, qb]
        zero = jnp.zeros_like(qt)
        qw.append((jnp.where(row < DIFF_DH, qt, zero), jnp.where(row >= DIFF_DH, qt, zero)))
    chains = [(qb, comp) for qb in range(ATT_NQ) for comp in range(2)]

    def scores(j):
        kc = k_ref[0, pl.ds(pl.multiple_of(j * ATT_TK, ATT_TK), ATT_TK), :]
        return [_dot(kc, qw[qb][comp]) for qb, comp in chains]

    acc_ref[...] = jnp.zeros_like(acc_ref)
    for (qb, comp), s0 in zip(chains, scores(0)):
        s_ref[qb, comp] = s0

    def chunk(j, carry):
        s_next = scores(jnp.minimum(j + 1, n_chunks - 1))
        vt = vt_ref[0, 0, j]
        out = []
        for ci, (qb, comp) in enumerate(chains):
            m_prev = carry[ci]
            s = s_ref[qb, comp]
            m_new = jnp.maximum(m_prev, jnp.max(s, axis=0, keepdims=True))
            alpha = jnp.exp2(m_prev - m_new)
            p = jnp.exp2(s - m_new)
            out.append(m_new)
            acc_ref[qb, comp] = alpha * acc_ref[qb, comp] + _dot(vt, p.astype(BF16))
        for (qb, comp), sn in zip(chains, s_next):
            s_ref[qb, comp] = sn
        return tuple(out)

    neg = jnp.full((1, ATT_TQ), NEG_INF, F32)
    lax.fori_loop(0, n_chunks, chunk, (neg,) * len(chains), unroll=8)

    sums = jnp.sum(la_ref[...] * lb_ref[...], axis=1, keepdims=True)
    lrow = lax.broadcasted_iota(jnp.int32, sums.shape, 0)
    sign = jnp.where(lrow == 0, 1.0, jnp.where(lrow == 1, -1.0, 0.0))
    lam = jnp.sum(sign * jnp.exp(sums), axis=0, keepdims=True) + LAM_INIT
    for qb in range(ATT_NQ):
        l0 = acc_ref[qb, 0, HEAD_W:HEAD_W + 1, :]
        l1 = acc_ref[qb, 1, HEAD_W:HEAD_W + 1, :]
        ot = acc_ref[qb, 0, 0:HEAD_W, :] / l0 - lam * (acc_ref[qb, 1, 0:HEAD_W, :] / l1)
        ms = jnp.mean(ot * ot, axis=0, keepdims=True)
        y = (ot * lax.rsqrt(ms + EPS)).T * g_ref[...] * (1.0 - LAM_INIT)
        o_ref[0, qb * ATT_TQ:(qb + 1) * ATT_TQ, :] = y.astype(o_ref.dtype)


def _diff_attn(qt, kr, vt, lam_a, lam_b, norm_g):
    B, S, _ = kr.shape
    lam_spec = pl.BlockSpec((8, LANES), lambda b, h, qi: (0, 0))
    return pl.pallas_call(
        functools.partial(_attn_kernel, n_chunks=S // ATT_TK),
        grid=(B, N_HEADS, S // (ATT_NQ * ATT_TQ)),
        in_specs=[pl.BlockSpec((1, 1, ATT_NQ, HEAD_W, ATT_TQ), lambda b, h, qi: (b, h, qi, 0, 0)),
                  pl.BlockSpec((1, S, HEAD_W), lambda b, h, qi: (b, 0, h)),
                  pl.BlockSpec((1, 1, S // ATT_TK, ATT_VROWS, ATT_TK), lambda b, h, qi: (b, h, 0, 0, 0)),
                  lam_spec, lam_spec,
                  pl.BlockSpec((1, HEAD_W), lambda b, h, qi: (0, 0))],
        out_specs=pl.BlockSpec((1, ATT_NQ * ATT_TQ, HEAD_W), lambda b, h, qi: (b, qi, h)),
        out_shape=jax.ShapeDtypeStruct((B, S, D_MODEL), BF16),
        scratch_shapes=[pltpu.VMEM((ATT_NQ, 2, ATT_TK, ATT_TQ), F32), pltpu.VMEM((ATT_NQ, 2, ATT_VROWS, ATT_TQ), F32)],
        compiler_params=_params(("parallel", "parallel", "parallel")),
        name="diff_attn",
    )(qt, kr, vt, lam_a, lam_b, norm_g)


def _merge_kernel(of_ref, ob_ref, z_ref, oB_ref, ga_ref, gb_ref, bga_ref, bgb_ref, x_ref, g1_ref, gn_ref,
                  wa_ref, wb_ref, wo_ref, o_ref, ya_ref):
    oa = of_ref[...].astype(F32) + ob_ref[...].astype(F32)
    z = z_ref[...].astype(F32)
    gate = z * _sigmoid(z)
    for h in range(N_HEADS):
        cols = slice(h * HEAD_W, (h + 1) * HEAD_W)
        oh = oa[:, cols]
        ms = jnp.mean(oh * oh, axis=-1, keepdims=True)
        ya_ref[:, cols] = (oh * lax.rsqrt(ms + EPS) * gn_ref[...] * gate[:, cols]).astype(BF16)
    y_a = _dot(ya_ref[...], wa_ref[...])
    y_b = _dot(oB_ref[...], wb_ref[...])
    gate_a = _sigmoid(ga_ref[...].astype(F32) + bga_ref[...])
    gate_b = _sigmoid(gb_ref[...].astype(F32) + bgb_ref[...])
    mix = _dot((gate_a * y_a + gate_b * y_b).astype(BF16), wo_ref[...])
    o_ref[...] = x_ref[...] + g1_ref[0] * mix


def _merge(oAf, oAb, proj, oB, b_gate, x2, gate1, gn, wa, wb, wo, S):
    N = x2.shape[0]
    TM = min(512, S)
    tpb = S // TM
    row = lambda i: (i, 0)
    full = pl.BlockSpec((D_MODEL, D_MODEL), lambda i: (0, 0))
    return pl.pallas_call(
        _merge_kernel,
        grid=(N // TM,),
        in_specs=[pl.BlockSpec((TM, D_MODEL), row),
                  pl.BlockSpec((TM, D_MODEL), row),
                  pl.BlockSpec((TM, D_MODEL), lambda i: (i, COL_ZA)),
                  pl.BlockSpec((TM, D_MODEL), row),
                  pl.BlockSpec((TM, D_MODEL), lambda i: (i, COL_GA)),
                  pl.BlockSpec((TM, D_MODEL), lambda i: (i, COL_GB)),
                  pl.BlockSpec((1, D_MODEL), lambda i: (0, 0)),
                  pl.BlockSpec((1, D_MODEL), lambda i: (0, 1)),
                  pl.BlockSpec((TM, D_MODEL), row),
                  pl.BlockSpec((1, 1, D_MODEL), lambda i: (i // tpb, 0, 0)),
                  pl.BlockSpec((1, HEAD_W), lambda i: (0, 0)),
                  full, full, full],
        out_specs=pl.BlockSpec((TM, D_MODEL), row),
        out_shape=jax.ShapeDtypeStruct((N, D_MODEL), F32),
        scratch_shapes=[pltpu.VMEM((TM, D_MODEL), BF16)],
        compiler_params=_params(("parallel",)),
        name="merge",
    )(oAf, oAb, proj, oB, proj, proj, b_gate, b_gate, x2, gate1, gn, wa, wb, wo)


def _router_kernel(x_ref, sh_ref, sc_ref, g_ref, rw0_ref, rw1_ref, rb_ref, tri_ref,
                   h_ref, idx_ref, gate_ref, rank_ref, cnt_ref, base_ref):
    i = pl.program_id(0)

    @pl.when(i == 0)
    def _():
        base_ref[...] = jnp.zeros_like(base_ref)

    x = x_ref[...]
    ms = jnp.mean(x * x, axis=-1, keepdims=True)
    h = x * lax.rsqrt(ms + EPS) * g_ref[...] * (1.0 + sc_ref[0]) + sh_ref[0]
    h_ref[...] = h
    h0 = h.astype(BF16)
    h1 = (h - h0.astype(F32)).astype(BF16)
    logits = _dot(h0, rw0_ref[...]) + (_dot(h0, rw1_ref[...]) + _dot(h1, rw0_ref[...])) + rb_ref[...]
    lane = lax.broadcasted_iota(jnp.int32, logits.shape, 1)
    lane_f = lane.astype(F32)
    cur = jnp.where(lane < N_EXPERTS, logits, NEG_INF)
    vals, sel = [], []
    for _ in range(TOP_K):
        m = jnp.max(cur, axis=1, keepdims=True)
        ix = jnp.min(jnp.where(cur == m, lane_f, float(LANES)), axis=1, keepdims=True)
        hit = lane_f == ix
        vals.append(m)
        sel.append(hit)
        cur = jnp.where(hit, NEG_INF, cur)
    exps = [jnp.exp(v - vals[0]) for v in vals]
    den = exps[0] + exps[1] + exps[2] + exps[3]
    onehot = jnp.zeros(logits.shape, F32)
    for hit in sel:
        onehot = onehot + jnp.where(hit, 1.0, 0.0)
    before = _dot(tri_ref[...], onehot.astype(BF16)) + base_ref[...]
    idx_out = jnp.zeros(logits.shape, F32)
    gate_out = jnp.zeros(logits.shape, F32)
    rank_out = jnp.zeros(logits.shape, F32)
    for kk in range(TOP_K):
        slot = lane == kk
        e_id = jnp.sum(jnp.where(sel[kk], lane_f, 0.0), axis=1, keepdims=True)
        rk = jnp.sum(jnp.where(sel[kk], before, 0.0), axis=1, keepdims=True)
        idx_out = jnp.where(slot, e_id, idx_out)
        gate_out = jnp.where(slot, exps[kk] / den, gate_out)
        rank_out = jnp.where(slot, rk, rank_out)
    idx_ref[...] = idx_out.astype(jnp.int32)
    gate_ref[...] = gate_out
    rank_ref[...] = rank_out.astype(jnp.int32)
    base_ref[...] = base_ref[...] + jnp.sum(onehot, axis=0, keepdims=True)
    cnt_ref[...] = base_ref[...]


def _router(x1, shift, scale, g, rw0, rw1, rb, S):
    N = x1.shape[0]
    TM = min(512, S)
    tpb = S // TM
    r = jnp.arange(TM)
    tri = (r[None, :] < r[:, None]).astype(BF16)
    row = lambda i: (i, 0)
    const = lambda i: (0, 0)
    lanes = pl.BlockSpec((TM, LANES), row)
    return pl.pallas_call(
        _router_kernel,
        grid=(N // TM,),
        in_specs=[pl.BlockSpec((TM, D_MODEL), row),
                  pl.BlockSpec((1, 1, D_MODEL), lambda i: (i // tpb, 0, 0)),
                  pl.BlockSpec((1, 1, D_MODEL), lambda i: (i // tpb, 0, 0)),
                  pl.BlockSpec((1, D_MODEL), const),
                  pl.BlockSpec((D_MODEL, LANES), const),
                  pl.BlockSpec((D_MODEL, LANES), const),
                  pl.BlockSpec((1, LANES), const),
                  pl.BlockSpec((TM, TM), const)],
        out_specs=[pl.BlockSpec((TM, D_MODEL), row), lanes, lanes, lanes, pl.BlockSpec((1, LANES), const)],
        out_shape=[jax.ShapeDtypeStruct((N, D_MODEL), F32),
                   jax.ShapeDtypeStruct((N, LANES), jnp.int32),
                   jax.ShapeDtypeStruct((N, LANES), F32),
                   jax.ShapeDtypeStruct((N, LANES), jnp.int32),
                   jax.ShapeDtypeStruct((1, LANES), F32)],
        scratch_shapes=[pltpu.VMEM((1, LANES), F32)],
        compiler_params=_params(("arbitrary",)),
        name="router",
    )(x1, shift, scale, g, rw0, rw1, rb, tri)


INDEX_SLICE = ROW_MOVE_TILE * TOP_K


def _row_copy_out(h_ref, xs_hbm, sem, r, dst):
    return pltpu.make_async_copy(h_ref.at[pl.ds(r, 1)], xs_hbm.at[pl.ds(dst, 1)], sem)


def _zero_tile_copy(zero_ref, xs_hbm, sem, start):
    return pltpu.make_async_copy(zero_ref, xs_hbm.at[pl.ds(pl.multiple_of(start, MOE_TILE), MOE_TILE)], sem)


def _dispatch_kernel(pad_end_ref, padded_ref, dest_hbm, h_ref, xs_hbm, idx_smem, zero_ref, sem_idx, sem_rows, sem_zero):
    i = pl.program_id(0)
    fetch = pltpu.make_async_copy(dest_hbm.at[pl.ds(i * INDEX_SLICE, INDEX_SLICE)], idx_smem, sem_idx)
    fetch.start()

    @pl.when(i == 0)
    def _():
        zero_ref[...] = jnp.zeros_like(zero_ref)
        for e in range(N_EXPERTS):
            @pl.when(padded_ref[e] > 0)
            def _():
                _zero_tile_copy(zero_ref, xs_hbm, sem_zero, pad_end_ref[e] - MOE_TILE).start()
        for e in range(N_EXPERTS):
            @pl.when(padded_ref[e] > 0)
            def _():
                _zero_tile_copy(zero_ref, xs_hbm, sem_zero, 0).wait()

    fetch.wait()

    def start(r, carry):
        for kk in range(TOP_K):
            _row_copy_out(h_ref, xs_hbm, sem_rows, r, idx_smem[r * TOP_K + kk]).start()
        return carry

    lax.fori_loop(0, ROW_MOVE_TILE, start, 0, unroll=4)
    for kk in range(TOP_K):
        pltpu.make_async_copy(h_ref, xs_hbm.at[pl.ds(0, ROW_MOVE_TILE)], sem_rows).wait()


def _dispatch(pad_ends, padded, dest_flat, h2, n_rows):
    N = h2.shape[0]
    return pl.pallas_call(
        _dispatch_kernel,
        grid_spec=pltpu.PrefetchScalarGridSpec(
            num_scalar_prefetch=2,
            grid=(N // ROW_MOVE_TILE,),
            in_specs=[pl.BlockSpec(memory_space=pl.ANY),
                      pl.BlockSpec((ROW_MOVE_TILE, D_MODEL), lambda i, pe, pd: (i, 0))],
            out_specs=pl.BlockSpec(memory_space=pl.ANY),
            scratch_shapes=[pltpu.SMEM((INDEX_SLICE,), jnp.int32), pltpu.VMEM((MOE_TILE, D_MODEL), F32),
                            pltpu.SemaphoreType.DMA, pltpu.SemaphoreType.DMA, pltpu.SemaphoreType.DMA]),
        out_shape=jax.ShapeDtypeStruct((n_rows, D_MODEL), F32),
        compiler_params=_params(("arbitrary",)),
        name="moe_dispatch",
    )(pad_ends, padded, dest_flat, h2)


def _expert_kernel(te_ref, nu_ref, xs_ref, wg_ref, bg_ref, wl_ref, bl_ref, wd_ref, bd_ref, ys_ref):
    del te_ref

    @pl.when(pl.program_id(0) < nu_ref[0])
    def _():
        xb = xs_ref[...].astype(BF16)
        glu = jnp.minimum(_dot(xb, wg_ref[0].astype(BF16)) + bg_ref[0], SWIGLU_LIMIT)
        lin = jnp.clip(_dot(xb, wl_ref[0].astype(BF16)) + bl_ref[0], -SWIGLU_LIMIT, SWIGLU_LIMIT)
        act = glu * _sigmoid(SWIGLU_ALPHA * glu) * (lin + 1.0)
        ys_ref[...] = _dot(act.astype(BF16), wd_ref[0].astype(BF16)) + bd_ref[0]


def _experts(tile_expert, n_used, xs, wg, bg, wl, bl, wd, bd):
    n_rows = xs.shape[0]
    n_tiles = n_rows // MOE_TILE
    wspec = pl.BlockSpec((1, D_MODEL, D_MODEL), lambda i, te, nu: (te[i], 0, 0))
    bspec = pl.BlockSpec((1, 1, D_MODEL), lambda i, te, nu: (te[i], 0, 0))
    rows = pl.BlockSpec((MOE_TILE, D_MODEL), lambda i, te, nu: (jnp.minimum(i, nu[0] - 1), 0))
    return pl.pallas_call(
        _expert_kernel,
        grid_spec=pltpu.PrefetchScalarGridSpec(
            num_scalar_prefetch=2,
            grid=(n_tiles,),
            in_specs=[rows, wspec, bspec, wspec, bspec, wspec, bspec],
            out_specs=rows),
        out_shape=jax.ShapeDtypeStruct((n_rows, D_MODEL), F32),
        compiler_params=_params(("arbitrary",)),
        name="moe_experts",
    )(tile_expert, n_used, xs, wg, bg, wl, bl, wd, bd)


def _row_copy_in(ys_hbm, buf_ref, sem, src, kk, r):
    return pltpu.make_async_copy(ys_hbm.at[pl.ds(src, 1)], buf_ref.at[kk, pl.ds(r, 1)], sem)


def _combine_kernel(dest_hbm, ys_hbm, gate_ref, x_ref, g2_ref, fg_ref, o_ref, idx_smem, buf_ref, sem_idx, sem_rows):
    i = pl.program_id(0)
    fetch = pltpu.make_async_copy(dest_hbm.at[pl.ds(i * INDEX_SLICE, INDEX_SLICE)], idx_smem, sem_idx)
    fetch.start()
    fetch.wait()

    def start(r, carry):
        for kk in range(TOP_K):
            _row_copy_in(ys_hbm, buf_ref, sem_rows, idx_smem[r * TOP_K + kk], kk, r).start()
        return carry

    lax.fori_loop(0, ROW_MOVE_TILE, start, 0, unroll=4)
    for kk in range(TOP_K):
        pltpu.make_async_copy(ys_hbm.at[pl.ds(0, ROW_MOVE_TILE)], buf_ref.at[kk], sem_rows).wait()

    gates = gate_ref[...]
    moe = gates[:, 0:1] * buf_ref[0]
    for kk in range(1, TOP_K):
        moe = moe + gates[:, kk:kk + 1] * buf_ref[kk]
    x = x_ref[...] + g2_ref[0] * moe
    ms = jnp.mean(x * x, axis=-1, keepdims=True)
    o_ref[...] = x * lax.rsqrt(ms + EPS) * fg_ref[...]


def _combine(dest_flat, ys, gates, x1, gate2, final_g, S):
    N = x1.shape[0]
    TM = ROW_MOVE_TILE
    tpb = S // TM
    row = lambda i: (i, 0)
    return pl.pallas_call(
        _combine_kernel,
        grid=(N // TM,),
        in_specs=[pl.BlockSpec(memory_space=pl.ANY),
                  pl.BlockSpec(memory_space=pl.ANY),
                  pl.BlockSpec((TM, LANES), row),
                  pl.BlockSpec((TM, D_MODEL), row),
                  pl.BlockSpec((1, 1, D_MODEL), lambda i: (i // tpb, 0, 0)),
                  pl.BlockSpec((1, D_MODEL), lambda i: (0, 0))],
        out_specs=pl.BlockSpec((TM, D_MODEL), row),
        out_shape=jax.ShapeDtypeStruct((N, D_MODEL), F32),
        scratch_shapes=[pltpu.SMEM((INDEX_SLICE,), jnp.int32), pltpu.VMEM((TOP_K, TM, D_MODEL), F32),
                        pltpu.SemaphoreType.DMA, pltpu.SemaphoreType.DMA],
        compiler_params=_params(("arbitrary",)),
        name="moe_combine",
    )(dest_flat, ys, gates, x1, gate2, final_g)


def _pad_lanes(a, offset=0):
    return jnp.pad(a, ((0, 0), (offset, LANES - offset - a.shape[1])))


def kernel(x, c, ada_w, ada_b, norm1_g, norm2_g, w_in, b_gate, conv_w, a_log, dt_bias, gdn_norm_g, w_branch_a,
           diff_lambda, diff_norm_g, w_branch_b, w_out, router_w, router_b, w_glu, b_glu, w_lin, b_lin, w_down,
           b_down, final_g):
    B, S, D = x.shape
    N = B * S
    assert D == D_MODEL and S % GROUP == 0 and ada_w.shape[0] == 1
    x2 = x.reshape(N, D)

    mod = _adaln(c, ada_w[0], ada_b[0])
    shift1, scale1, gate1, shift2, scale2, gate2 = [m.reshape(B, 1, D) for m in jnp.split(mod, 6, axis=-1)]

    wi = w_in[0]
    n_a = 4 * D
    n_small = 4 * N_HEADS
    w_big = jnp.concatenate([wi[:, :n_a], wi[:, n_a + n_small:]], axis=1).astype(BF16)
    w_small = _pad_lanes(wi[:, n_a:n_a + n_small]).astype(BF16)
    alog_row = _pad_lanes(a_log[0].reshape(1, -1), GATE_LANE0)
    dt_row = _pad_lanes(dt_bias[0].reshape(1, -1), GATE_LANE0)
    proj, beta, G, Gl, cdl = _inproj(x2, shift1, scale1, norm1_g, w_big, w_small, alog_row, dt_row, S)
    proj3 = proj.reshape(B, S, N_COL_BLOCKS * D)

    qkv = _gdn_conv(proj3, conv_w[0])
    r3 = lambda a: a.reshape(B, S, LANES)
    nC = S // GDN_CHUNK
    GT = jnp.transpose(r3(G)[:, :, GATE_LANE0:GATE_LANE0 + 2 * N_HEADS], (0, 2, 1)).reshape(B, 2 * N_HEADS, 1, S)
    cd = r3(cdl).reshape(B, nC, GDN_CHUNK, LANES)[:, :, 0, GATE_LANE0:GATE_LANE0 + 2 * N_HEADS]
    cd = jnp.transpose(cd.reshape(B, nC, 2, N_HEADS), (0, 2, 3, 1)).reshape(-1)
    u, w, qd, att, kdt = _gdn_prep(qkv, r3(beta), r3(G), r3(Gl), GT)
    oAf, oAb = _gdn_scan(cd, u, w, qd, att, kdt)

    half = DIFF_DH // 2
    inv_freq = ROPE_THETA ** (-jnp.arange(half, dtype=F32) / half)
    ang = jnp.arange(S, dtype=F32)[:, None] * inv_freq[None, :]
    cos_t = jnp.tile(jnp.cos(ang), (1, 4))
    sin_h = jnp.sin(ang)
    sin_t = jnp.tile(jnp.concatenate([-sin_h, sin_h], axis=1), (1, 2))
    qt, kr, vt = _rope(proj3, cos_t, sin_t)
    lam_a = jnp.pad(_pad_lanes(diff_lambda[0][0::2]), ((0, 6), (0, 0)))
    lam_b = jnp.pad(_pad_lanes(diff_lambda[0][1::2]), ((0, 6), (0, 0)))
    oB = _diff_attn(qt, kr, vt, lam_a, lam_b, diff_norm_g)

    x1 = _merge(oAf.reshape(N, D), oAb.reshape(N, D), proj, oB.reshape(N, D), b_gate, x2, gate1, gdn_norm_g,
                w_branch_a[0].astype(BF16), w_branch_b[0].astype(BF16), w_out[0].astype(BF16), S)

    rw = _pad_lanes(router_w[0])
    rw0 = rw.astype(BF16)
    rw1 = (rw - rw0.astype(F32)).astype(BF16)
    h2, idx, gates, rank, counts = _router(x1, shift2, scale2, norm2_g, rw0, rw1, _pad_lanes(router_b), S)
    cnt = counts[0, :N_EXPERTS].astype(jnp.int32)
    padded = (cnt + MOE_TILE - 1) // MOE_TILE * MOE_TILE
    pad_ends = jnp.cumsum(padded)
    pad_starts = pad_ends - padded
    dest = (pad_starts[idx[:, :TOP_K]] + rank[:, :TOP_K]).reshape(-1)
    n_tiles = -(-(N * TOP_K) // MOE_TILE) + N_EXPERTS
    tile_start = jnp.arange(n_tiles, dtype=jnp.int32) * MOE_TILE
    tile_expert = jnp.sum((tile_start[:, None] >= pad_ends[None, :]).astype(jnp.int32), axis=1)
    tile_expert = jnp.minimum(tile_expert, N_EXPERTS - 1)
    n_used = (pad_ends[N_EXPERTS - 1:] // MOE_TILE).astype(jnp.int32)
    xs = _dispatch(pad_ends.astype(jnp.int32), padded, dest, h2, n_tiles * MOE_TILE)
    ys = _experts(tile_expert, n_used, xs, w_glu[0], b_glu[0][:, None, :], w_lin[0], b_lin[0][:, None, :],
                  w_down[0], b_down[0][:, None, :])
    out = _combine(dest, ys, gates, x1, gate2, final_g.reshape(1, D), S)
    return out.reshape(B, S, D)
```

```python
import functools
import math

import jax
import jax.numpy as jnp
from jax import lax
from jax.experimental import pallas as pl
from jax.experimental.pallas import tpu as pltpu

F32 = jnp.float32
BF16 = jnp.bfloat16

D_MODEL = 1024
EPS = 1e-6
N_HEADS = 8
HEAD_W = 128
GDN_CHUNK = 64
CONV_WIDTH = 5
DIFF_DH = 64
ROPE_THETA = 10000.0
LAM_INIT = 0.8 - 0.6 * math.exp(-0.3 * 0)
N_EXPERTS = 32
TOP_K = 4
SWIGLU_ALPHA = 1.702
SWIGLU_LIMIT = 7.0

LANES = 128
GROUP = 256
CHUNKS_PER_GROUP = GROUP // GDN_CHUNK
CHUNK_SHIFT = GDN_CHUNK.bit_length() - 1
MOE_TILE = 512
ROW_MOVE_TILE = 1024
NEG_INF = float("-inf")

COL_QA, COL_KA, COL_VA, COL_ZA, COL_QB, COL_KB, COL_VB, COL_GA, COL_GB = range(9)
N_COL_BLOCKS = 9


def _params(sem, vmem_mb=48):
    return pltpu.CompilerParams(dimension_semantics=sem, vmem_limit_bytes=vmem_mb * 1024 * 1024)


def _dot(a, b):
    return jnp.dot(a, b, preferred_element_type=F32)


def _dot_nt(a, b):
    return lax.dot_general(a, b, (((1,), (1,)), ((), ())), preferred_element_type=F32)


def _sigmoid(x):
    return 1.0 / (1.0 + jnp.exp(-x))


def _split3(x):
    a = x.astype(BF16)
    r = x - a.astype(F32)
    b = r.astype(BF16)
    c = (r - b.astype(F32)).astype(BF16)
    return a, b, c


def _adaln_kernel(c_ref, w_ref, b_ref, o_ref):
    c = c_ref[...]
    cond = c * _sigmoid(c)
    c0, c1, c2 = _split3(cond)
    w0, w1, w2 = _split3(w_ref[...])
    acc = _dot(c0, w0) + (_dot(c0, w1) + _dot(c1, w0)) + (_dot(c0, w2) + _dot(c1, w1) + _dot(c2, w0))
    o_ref[...] = acc + b_ref[...]


def _adaln(c, ada_w, ada_b):
    B = c.shape[0]
    n = ada_w.shape[1] // D_MODEL
    return pl.pallas_call(
        _adaln_kernel,
        grid=(n,),
        in_specs=[pl.BlockSpec((B, D_MODEL), lambda j: (0, 0)),
                  pl.BlockSpec((D_MODEL, D_MODEL), lambda j: (0, j)),
                  pl.BlockSpec((1, D_MODEL), lambda j: (0, j))],
        out_specs=pl.BlockSpec((B, D_MODEL), lambda j: (0, j)),
        out_shape=jax.ShapeDtypeStruct((B, n * D_MODEL), F32),
        compiler_params=_params(("parallel",)),
        name="adaln",
    )(c, ada_w, ada_b.reshape(1, -1))


def _inproj_kernel(x_ref, sh_ref, sc_ref, g_ref, w_ref, ws_ref, alog_ref, dt_ref,
                   o_ref, beta_ref, gcum_ref, glast_ref, cd_ref, h_ref):
    @pl.when(pl.program_id(1) == 0)
    def _():
        x = x_ref[...]
        ms = jnp.mean(x * x, axis=-1, keepdims=True)
        y = x * lax.rsqrt(ms + EPS) * g_ref[...]
        h = (y * (1.0 + sc_ref[0]) + sh_ref[0]).astype(BF16)
        h_ref[...] = h
        _gdn_gates(_dot(h, ws_ref[...]), alog_ref[...], dt_ref[...], beta_ref, gcum_ref, glast_ref, cd_ref)

    col = pl.multiple_of(pl.program_id(1) * D_MODEL, D_MODEL)
    o_ref[...] = _dot(h_ref[...], w_ref[:, pl.ds(col, D_MODEL)]).astype(o_ref.dtype)


def _inproj(x2, shift, scale, g, w_big, w_small, alog_row, dt_row, S):
    N = x2.shape[0]
    TM = min(1024, S)
    tpb = S // TM
    const = lambda i, j: (0, 0)
    lanes = pl.BlockSpec((TM, LANES), lambda i, j: (i, 0))
    return pl.pallas_call(
        _inproj_kernel,
        grid=(N // TM, N_COL_BLOCKS),
        in_specs=[pl.BlockSpec((TM, D_MODEL), lambda i, j: (i, 0)),
                  pl.BlockSpec((1, 1, D_MODEL), lambda i, j: (i // tpb, 0, 0)),
                  pl.BlockSpec((1, 1, D_MODEL), lambda i, j: (i // tpb, 0, 0)),
                  pl.BlockSpec((1, D_MODEL), const),
                  pl.BlockSpec((D_MODEL, N_COL_BLOCKS * D_MODEL), const, pipeline_mode=pl.Buffered(1)),
                  pl.BlockSpec((D_MODEL, LANES), const),
                  pl.BlockSpec((1, LANES), const),
                  pl.BlockSpec((1, LANES), const)],
        out_specs=[pl.BlockSpec((TM, D_MODEL), lambda i, j: (i, j)), lanes, lanes, lanes, lanes],
        out_shape=[jax.ShapeDtypeStruct((N, N_COL_BLOCKS * D_MODEL), BF16)]
                  + [jax.ShapeDtypeStruct((N, LANES), F32)] * 4,
        scratch_shapes=[pltpu.VMEM((TM, D_MODEL), BF16)],
        compiler_params=_params(("parallel", "arbitrary")),
        name="inproj",
    )(x2, shift, scale, g, w_big, w_small, alog_row, dt_row)


HALO = 16


CONV_BLOCK = 256
CONV_PAD = (CONV_WIDTH - 1) // 2
CONV_TAPS = tuple(j for j in range(CONV_WIDTH) if j != CONV_PAD)


def _conv_shifts():
    r = jnp.arange(CONV_BLOCK)[:, None]
    c = jnp.arange(CONV_BLOCK)[None, :]
    return jnp.stack([c == r + (j - CONV_PAD) for j in CONV_TAPS]).astype(BF16)


def _conv_kernel(cur_ref, prev_ref, next_ref, w_ref, shift_ref, o_ref, ext_ref, *, TR):
    i = pl.program_id(1)
    g = pl.program_id(2)
    last = pl.num_programs(1) - 1
    ext_ref[8:8 + TR, :] = cur_ref[0].astype(F32)
    pv = prev_ref[0].astype(F32)[HALO - 8:HALO]
    nx = next_ref[0].astype(F32)[0:8]
    ext_ref[0:8, :] = jnp.where(i > 0, pv, 0.0)
    ext_ref[TR + 8:TR + 16, :] = jnp.where(i < last, nx, 0.0)
    ones = jnp.ones((HEAD_W, HEAD_W), BF16)
    qscale = jnp.where(g == 0, HEAD_W ** -0.5, 1.0)

    def edge_rows(row0):
        e = ext_ref[8 + row0 - CONV_PAD:16 + row0 - CONV_PAD, :] * w_ref[0:1, :]
        for j in range(1, CONV_WIDTH):
            e = e + ext_ref[8 + row0 - CONV_PAD + j:16 + row0 - CONV_PAD + j, :] * w_ref[j:j + 1, :]
        return e

    for blk in range(TR // CONV_BLOCK):
        r0 = blk * CONV_BLOCK
        rows = slice(r0, r0 + CONV_BLOCK)
        ub = cur_ref[0, rows, :]
        acc = ext_ref[8 + r0:8 + r0 + CONV_BLOCK, :] * w_ref[CONV_PAD:CONV_PAD + 1, :]
        for si, j in enumerate(CONV_TAPS):
            acc = acc + _dot(shift_ref[si], ub) * w_ref[j:j + 1, :]
        acc = jnp.concatenate([edge_rows(r0), acc[8:CONV_BLOCK - 8], edge_rows(r0 + CONV_BLOCK - 8)], axis=0)
        y = acc * _sigmoid(acc)

        for h in range(N_HEADS):
            cols = slice(h * HEAD_W, (h + 1) * HEAD_W)
            yh = y[:, cols]
            ss = _dot((yh * yh).astype(BF16), ones)
            scale = jnp.where(g < 2, lax.rsqrt(ss + EPS) * qscale, 1.0)
            o_ref[0, rows, cols] = (yh * scale).astype(o_ref.dtype)


def _gdn_conv(proj3, conv_w):
    B, S, _ = proj3.shape
    TR = min(512, S)
    nT = S // TR
    rb = TR // HALO
    nH = S // HALO
    return pl.pallas_call(
        functools.partial(_conv_kernel, TR=TR),
        grid=(B, nT, 3),
        in_specs=[pl.BlockSpec((1, TR, D_MODEL), lambda b, i, g: (b, i, g)),
                  pl.BlockSpec((1, HALO, D_MODEL), lambda b, i, g: (b, jnp.maximum(i * rb - 1, 0), g)),
                  pl.BlockSpec((1, HALO, D_MODEL), lambda b, i, g: (b, jnp.minimum((i + 1) * rb, nH - 1), g)),
                  pl.BlockSpec((CONV_WIDTH, D_MODEL), lambda b, i, g: (0, g)),
                  pl.BlockSpec((len(CONV_TAPS), CONV_BLOCK, CONV_BLOCK), lambda b, i, g: (0, 0, 0))],
        out_specs=pl.BlockSpec((1, TR, D_MODEL), lambda b, i, g: (b, i, g)),
        out_shape=jax.ShapeDtypeStruct((B, S, 3 * D_MODEL), BF16),
        scratch_shapes=[pltpu.VMEM((TR + 16, D_MODEL), F32)],
        compiler_params=_params(("parallel", "parallel", "parallel")),
        name="gdn_conv",
    )(proj3, proj3, proj3, conv_w, _conv_shifts())


GATE_LANE0 = 16


def _gdn_gates(x, alog, dt, beta_ref, g_ref, gl_ref, cd_ref):
    lane = lax.broadcasted_iota(jnp.int32, (GROUP, LANES), 1)
    r = lax.broadcasted_iota(jnp.int32, (GROUP, GROUP), 0)
    c = lax.broadcasted_iota(jnp.int32, (GROUP, GROUP), 1)
    same = (r >> CHUNK_SHIFT) == (c >> CHUNK_SHIFT)
    lower = jnp.where(same & (c <= r), 1.0, 0.0).astype(BF16)
    upper = jnp.where(same & (c >= r), 1.0, 0.0).astype(BF16)
    block = jnp.where(same, 1.0, 0.0).astype(BF16)
    beta_ref[...] = _sigmoid(x)
    for gi in range(x.shape[0] // GROUP):
        rows = slice(gi * GROUP, (gi + 1) * GROUP)
        z = x[rows] + dt
        softplus = jnp.maximum(z, 0.0) + jnp.log(1.0 + jnp.exp(-jnp.abs(z)))
        gd = -jnp.exp(alog) * softplus
        gd = jnp.where((lane >= GATE_LANE0) & (lane < GATE_LANE0 + 2 * N_HEADS), gd, 0.0)
        p0, p1, p2 = _split3(gd)
        g_fwd = _dot(lower, p0) + _dot(lower, p1) + _dot(lower, p2)
        g_bwd = _dot(upper, p0) + _dot(upper, p1) + _dot(upper, p2)
        tot = _dot(block, p0) + _dot(block, p1) + _dot(block, p2)
        G = jnp.where(lane < GATE_LANE0 + N_HEADS, g_fwd, g_bwd)
        g_ref[rows, :] = G
        gl_ref[rows, :] = tot - G
        cd_ref[rows, :] = jnp.exp(tot)


PREP_HEADS = 4
(MASK_STRICT_LO, MASK_STRICT_UP, MASK_INCL_LO, MASK_INCL_UP, MASK_EYE, MASK_BLK4, MASK_OFF0) = range(7)
N_MASKS = MASK_OFF0 + (CHUNK_SHIFT - 2)


def _prep_masks():
    r = jnp.arange(GROUP)[:, None]
    c = jnp.arange(GROUP)[None, :]
    same = (r >> CHUNK_SHIFT) == (c >> CHUNK_SHIFT)
    masks = [same & (c < r), same & (c > r), same & (c <= r), same & (c >= r), r == c, (r >> 2) == (c >> 2)]
    for shift in range(2, CHUNK_SHIFT):
        masks.append(((r >> shift) != (c >> shift)) & ((r >> (shift + 1)) == (c >> (shift + 1))))
    return jnp.stack(masks).astype(F32)


def _col(x, l, lane):
    return jnp.broadcast_to(jnp.sum(jnp.where(lane == l, x, 0.0), axis=1, keepdims=True), x.shape)


def _prep_kernel(q_ref, k_ref, v_ref, beta_ref, g_ref, gl_ref, gtf_ref, gtb_ref, mask_ref, bmask_ref,
                 u_ref, w_ref, qd_ref, at_ref, kdt_ref):
    hp = pl.program_id(1)
    lane = lax.broadcasted_iota(jnp.int32, (GROUP, LANES), 1)
    wide = lambda a: jnp.concatenate([a, a], axis=1)
    chains = [(hh, d) for hh in range(PREP_HEADS) for d in range(2)]
    p, rhs = {}, {}
    for hh in range(PREP_HEADS):
        cols = slice(hh * HEAD_W, (hh + 1) * HEAD_W)
        q = q_ref[0, :, cols]
        k = k_ref[0, :, cols]
        qf = q.astype(F32)
        kf = k.astype(F32)
        vf = v_ref[0, :, cols].astype(F32)
        kk = _dot_nt(k, k)
        qk = _dot_nt(q, k)
        for d in range(2):
            lb = d * N_HEADS + hp * PREP_HEADS + hh
            beta_c = _col(beta_ref[0], lb, lane)
            g_c = _col(g_ref[0], GATE_LANE0 + lb, lane)
            eg_c = jnp.exp(g_c)
            egl_c = jnp.exp(_col(gl_ref[0], GATE_LANE0 + lb, lane))
            g_r = (gtf_ref if d == 0 else gtb_ref)[0, hh]
            dec = jnp.exp(jnp.minimum(wide(g_c) - g_r, 0.0))
            p[hh, d] = (-(kk * wide(beta_c)) * dec * mask_ref[MASK_STRICT_LO + d]).astype(BF16)
            att = qk * dec * mask_ref[MASK_INCL_LO + d]
            rhs[hh, d] = jnp.concatenate([vf * beta_c, kf * (beta_c * eg_c)], axis=1).astype(BF16)
            qd_ref[0, d, :, cols] = (qf * eg_c).astype(qd_ref.dtype)
            kdt = (kf * egl_c).T
            for ci in range(CHUNKS_PER_GROUP):
                sl = slice(ci * GDN_CHUNK, (ci + 1) * GDN_CHUNK)
                at_ref[0, d, hh, ci] = att[sl, sl].astype(at_ref.dtype)
                kdt_ref[0, d, hh, ci] = kdt[:, sl].astype(kdt_ref.dtype)
    p4 = {ch: p[ch] * bmask_ref[0] for ch in chains}
    sq = {ch: _dot(p4[ch], p4[ch]).astype(BF16) for ch in chains}
    tb = {ch: bmask_ref[N_MASKS - MASK_BLK4] + p4[ch] for ch in chains}
    tb = {ch: (tb[ch].astype(F32) + _dot(tb[ch], sq[ch])).astype(BF16) for ch in chains}
    for lvl in range(CHUNK_SHIFT - 2):
        x = {ch: _dot(tb[ch], p[ch] * bmask_ref[1 + lvl]).astype(BF16) for ch in chains}
        tb = {ch: (tb[ch].astype(F32) + _dot(x[ch], tb[ch])).astype(BF16) for ch in chains}
    uw = {ch: _dot(tb[ch], rhs[ch]) for ch in chains}
    for hh, d in chains:
        cols = slice(hh * HEAD_W, (hh + 1) * HEAD_W)
        u_ref[0, d, :, cols] = uw[hh, d][:, :HEAD_W].astype(u_ref.dtype)
        w_ref[0, d, :, cols] = uw[hh, d][:, HEAD_W:].astype(w_ref.dtype)


def _gdn_prep(qkv, beta, G, Gl, GT):
    B, S, _ = qkv.shape
    nG = S // GROUP
    nC = S // GDN_CHUNK
    PW = PREP_HEADS * HEAD_W
    nP = N_HEADS // PREP_HEADS
    sm = pl.BlockSpec((1, GROUP, LANES), lambda b, h, g: (b, g, 0))
    big = pl.BlockSpec((1, 2, GROUP, PW), lambda b, h, g: (b, 0, g, h))
    masks = _prep_masks()
    return pl.pallas_call(
        _prep_kernel,
        grid=(B, nP, nG),
        in_specs=[pl.BlockSpec((1, GROUP, PW), lambda b, h, g: (b, g, h)),
                  pl.BlockSpec((1, GROUP, PW), lambda b, h, g: (b, g, nP + h)),
                  pl.BlockSpec((1, GROUP, PW), lambda b, h, g: (b, g, 2 * nP + h)),
                  sm, sm, sm,
                  pl.BlockSpec((1, PREP_HEADS, 1, GROUP), lambda b, h, g: (b, h, 0, g)),
                  pl.BlockSpec((1, PREP_HEADS, 1, GROUP), lambda b, h, g: (b, nP + h, 0, g)),
                  pl.BlockSpec((MASK_EYE, GROUP, GROUP), lambda b, h, g: (0, 0, 0)),
                  pl.BlockSpec((N_MASKS - MASK_BLK4 + 1, GROUP, GROUP), lambda b, h, g: (0, 0, 0))],
        out_specs=[big, big, big,
                   pl.BlockSpec((1, 2, PREP_HEADS, CHUNKS_PER_GROUP, GDN_CHUNK, GDN_CHUNK),
                                lambda b, h, g: (b, 0, h, g, 0, 0)),
                   pl.BlockSpec((1, 2, PREP_HEADS, CHUNKS_PER_GROUP, HEAD_W, GDN_CHUNK),
                                lambda b, h, g: (b, 0, h, g, 0, 0))],
        out_shape=[jax.ShapeDtypeStruct((B, 2, S, D_MODEL), BF16)] * 3
                  + [jax.ShapeDtypeStruct((B, 2, N_HEADS, nC, GDN_CHUNK, GDN_CHUNK), BF16),
                     jax.ShapeDtypeStruct((B, 2, N_HEADS, nC, HEAD_W, GDN_CHUNK), BF16)],
        compiler_params=_params(("parallel", "parallel", "parallel")),
        name="gdn_prep",
    )(qkv, qkv, qkv, beta, G, Gl, GT, GT, masks[:MASK_EYE],
      jnp.concatenate([masks[MASK_BLK4:], masks[MASK_EYE:MASK_EYE + 1]]).astype(BF16))


def _scan_kernel(cd_ref, uf_ref, wf_ref, qdf_ref, atf_ref, kdtf_ref, ub_ref, wb_ref, qdb_ref, atb_ref, kdtb_ref,
                 of_ref, ob_ref, state_ref, *, nc, nC):
    b = pl.program_id(0)
    t = pl.program_id(1)
    nT = pl.num_programs(1)

    @pl.when(t == 0)
    def _():
        state_ref[...] = jnp.zeros_like(state_ref)

    dirs = ((uf_ref, wf_ref, qdf_ref, atf_ref, kdtf_ref, of_ref), (ub_ref, wb_ref, qdb_ref, atb_ref, kdtb_ref, ob_ref))

    def chunk(ci, carry):
        work = []
        for d, refs in enumerate(dirs):
            c = ci if d == 0 else nc - 1 - ci
            tt = t if d == 0 else nT - 1 - t
            row = pl.multiple_of(c * GDN_CHUNK, GDN_CHUNK)
            for h in range(N_HEADS):
                work.append((d, h, c, row, ((b * 2 + d) * N_HEADS + h) * nC + tt * nc + c, refs))
        s_old = [state_ref[d, h] for d, h, *_ in work]
        sb = [s.astype(BF16) for s in s_old]
        tile = lambda ref, row, h: ref[0, 0, pl.ds(row, GDN_CHUNK), h * HEAD_W:(h + 1) * HEAD_W]
        ws = [_dot(tile(refs[1], row, h), sb[i]) for i, (d, h, c, row, gi, refs) in enumerate(work)]
        qs = [_dot(tile(refs[2], row, h), sb[i]) for i, (d, h, c, row, gi, refs) in enumerate(work)]
        vb = [(tile(refs[0], row, h).astype(F32) - ws[i]).astype(BF16)
              for i, (d, h, c, row, gi, refs) in enumerate(work)]
        o = [qs[i] + _dot(refs[3][0, 0, h, c], vb[i]) for i, (d, h, c, row, gi, refs) in enumerate(work)]
        upd = [_dot(refs[4][0, 0, h, c], vb[i]) for i, (d, h, c, row, gi, refs) in enumerate(work)]
        for i, (d, h, c, row, gi, refs) in enumerate(work):
            state_ref[d, h] = s_old[i] * cd_ref[gi] + upd[i]
            refs[5][0, pl.ds(row, GDN_CHUNK), h * HEAD_W:(h + 1) * HEAD_W] = o[i].astype(of_ref.dtype)
        return carry

    lax.fori_loop(0, nc, chunk, 0)


def _gdn_scan(cd, u, w, qd, att, kdt):
    B, _, S, _ = u.shape
    TC = min(512, S)
    nT = S // TC
    nc = TC // GDN_CHUNK
    nC = S // GDN_CHUNK
    fwd = lambda b, t: t
    bwd = lambda b, t: nT - 1 - t

    def specs(d, tm):
        big = pl.BlockSpec((1, 1, TC, D_MODEL), lambda b, t: (b, d, tm(b, t), 0))
        return [big, big, big,
                pl.BlockSpec((1, 1, N_HEADS, nc, GDN_CHUNK, GDN_CHUNK), lambda b, t: (b, d, 0, tm(b, t), 0, 0)),
                pl.BlockSpec((1, 1, N_HEADS, nc, HEAD_W, GDN_CHUNK), lambda b, t: (b, d, 0, tm(b, t), 0, 0))]

    return pl.pallas_call(
        functools.partial(_scan_kernel, nc=nc, nC=nC),
        grid=(B, nT),
        in_specs=[pl.BlockSpec(memory_space=pltpu.SMEM)] + specs(0, fwd) + specs(1, bwd),
        out_specs=[pl.BlockSpec((1, TC, D_MODEL), lambda b, t: (b, t, 0)),
                   pl.BlockSpec((1, TC, D_MODEL), lambda b, t: (b, nT - 1 - t, 0))],
        out_shape=[jax.ShapeDtypeStruct((B, S, D_MODEL), BF16)] * 2,
        scratch_shapes=[pltpu.VMEM((2, N_HEADS, HEAD_W, HEAD_W), F32)],
        compiler_params=_params(("parallel", "arbitrary")),
        name="gdn_scan",
    )(cd, u, w, qd, att, kdt, u, w, qd, att, kdt)


ATT_TQ = 256
ATT_TK = 256
ATT_NQ = 4
ATT_VROWS = HEAD_W + 16
LOG2E = 1.4426950408889634


def _rope_kernel(q_ref, k_ref, v_ref, cos_ref, sin_ref, qt_ref, kr_ref, vt_ref, *, TR):
    cs = cos_ref[...]
    sn = sin_ref[...]
    lane = lax.broadcasted_iota(jnp.int32, cs.shape, 1)
    first_half = (lane & (DIFF_DH - 1)) < (DIFF_DH // 2)
    qscale = DIFF_DH ** -0.5 * LOG2E

    def rot(x):
        partner = jnp.where(first_half, pltpu.roll(x, HEAD_W - DIFF_DH // 2, 1), pltpu.roll(x, DIFF_DH // 2, 1))
        return x * cs + partner * sn

    for h in range(N_HEADS):
        cols = slice(h * HEAD_W, (h + 1) * HEAD_W)
        qr = rot(q_ref[0, :, cols].astype(F32)) * qscale
        kr_ref[0, :, cols] = rot(k_ref[0, :, cols].astype(F32)).astype(kr_ref.dtype)
        vf = v_ref[0, :, cols].astype(F32)
        for ci in range(TR // ATT_TK):
            rows = slice(ci * ATT_TK, (ci + 1) * ATT_TK)
            vt_ref[0, h, ci, 0:HEAD_W, :] = vf[rows].T.astype(vt_ref.dtype)
            vt_ref[0, h, ci, HEAD_W:ATT_VROWS, :] = jnp.ones((ATT_VROWS - HEAD_W, ATT_TK), vt_ref.dtype)
        for ci in range(TR // ATT_TQ):
            rows = slice(ci * ATT_TQ, (ci + 1) * ATT_TQ)
            qt_ref[0, h, ci] = qr[rows].T.astype(qt_ref.dtype)


def _rope(proj3, cos_t, sin_t):
    B, S, _ = proj3.shape
    TR = min(512, S)
    tab = pl.BlockSpec((TR, HEAD_W), lambda b, i: (i, 0))
    col = lambda cb: pl.BlockSpec((1, TR, D_MODEL), lambda b, i: (b, i, cb))
    return pl.pallas_call(
        functools.partial(_rope_kernel, TR=TR),
        grid=(B, S // TR),
        in_specs=[col(COL_QB), col(COL_KB), col(COL_VB), tab, tab],
        out_specs=[pl.BlockSpec((1, N_HEADS, TR // ATT_TQ, HEAD_W, ATT_TQ), lambda b, i: (b, 0, i, 0, 0)),
                   pl.BlockSpec((1, TR, D_MODEL), lambda b, i: (b, i, 0)),
                   pl.BlockSpec((1, N_HEADS, TR // ATT_TK, ATT_VROWS, ATT_TK), lambda b, i: (b, 0, i, 0, 0))],
        out_shape=[jax.ShapeDtypeStruct((B, N_HEADS, S // ATT_TQ, HEAD_W, ATT_TQ), BF16),
                   jax.ShapeDtypeStruct((B, S, D_MODEL), BF16),
                   jax.ShapeDtypeStruct((B, N_HEADS, S // ATT_TK, ATT_VROWS, ATT_TK), BF16)],
        compiler_params=_params(("parallel", "parallel")),
        name="rope",
    )(proj3, proj3, proj3, cos_t, sin_t)


def _attn_kernel(qt_ref, k_ref, vt_ref, la_ref, lb_ref, g_ref, o_ref, s_ref, acc_ref, *, n_chunks):
    row = lax.broadcasted_iota(jnp.int32, (HEAD_W, ATT_TQ), 0)
    qw = []
    for qb in range(ATT_NQ):
        qt = qt_ref[0, 0, qb]
        zero = jnp.zeros_like(qt)
        qw.append((jnp.where(row < DIFF_DH, qt, zero), jnp.where(row >= DIFF_DH, qt, zero)))
    chains = [(qb, comp) for qb in range(ATT_NQ) for comp in range(2)]

    def scores(j):
        kc = k_ref[0, pl.ds(pl.multiple_of(j * ATT_TK, ATT_TK), ATT_TK), :]
        return [_dot(kc, qw[qb][comp]) for qb, comp in chains]

    acc_ref[...] = jnp.zeros_like(acc_ref)
    for (qb, comp), s0 in zip(chains, scores(0)):
        s_ref[qb, comp] = s0

    def chunk(j, carry):
        s_next = scores(jnp.minimum(j + 1, n_chunks - 1))
        vt = vt_ref[0, 0, j]
        out = []
        for ci, (qb, comp) in enumerate(chains):
            m_prev = carry[ci]
            s = s_ref[qb, comp]
            m_new = jnp.maximum(m_prev, jnp.max(s, axis=0, keepdims=True))
            alpha = jnp.exp2(m_prev - m_new)
            p = jnp.exp2(s - m_new)
            out.append(m_new)
            acc_ref[qb, comp] = alpha * acc_ref[qb, comp] + _dot(vt, p.astype(BF16))
        for (qb, comp), sn in zip(chains, s_next):
            s_ref[qb, comp] = sn
        return tuple(out)

    neg = jnp.full((1, ATT_TQ), NEG_INF, F32)
    lax.fori_loop(0, n_chunks, chunk, (neg,) * len(chains), unroll=8)

    sums = jnp.sum(la_ref[...] * lb_ref[...], axis=1, keepdims=True)
    lrow = lax.broadcasted_iota(jnp.int32, sums.shape, 0)
    sign = jnp.where(lrow == 0, 1.0, jnp.where(lrow == 1, -1.0, 0.0))
    lam = jnp.sum(sign * jnp.exp(sums), axis=0, keepdims=True) + LAM_INIT
    for qb in range(ATT_NQ):
        l0 = acc_ref[qb, 0, HEAD_W:HEAD_W + 1, :]
        l1 = acc_ref[qb, 1, HEAD_W:HEAD_W + 1, :]
        ot = acc_ref[qb, 0, 0:HEAD_W, :] / l0 - lam * (acc_ref[qb, 1, 0:HEAD_W, :] / l1)
        ms = jnp.mean(ot * ot, axis=0, keepdims=True)
        y = (ot * lax.rsqrt(ms + EPS)).T * g_ref[...] * (1.0 - LAM_INIT)
        o_ref[0, qb * ATT_TQ:(qb + 1) * ATT_TQ, :] = y.astype(o_ref.dtype)


def _diff_attn(qt, kr, vt, lam_a, lam_b, norm_g):
    B, S, _ = kr.shape
    lam_spec = pl.BlockSpec((8, LANES), lambda b, h, qi: (0, 0))
    return pl.pallas_call(
        functools.partial(_attn_kernel, n_chunks=S // ATT_TK),
        grid=(B, N_HEADS, S // (ATT_NQ * ATT_TQ)),
        in_specs=[pl.BlockSpec((1, 1, ATT_NQ, HEAD_W, ATT_TQ), lambda b, h, qi: (b, h, qi, 0, 0)),
                  pl.BlockSpec((1, S, HEAD_W), lambda b, h, qi: (b, 0, h)),
                  pl.BlockSpec((1, 1, S // ATT_TK, ATT_VROWS, ATT_TK), lambda b, h, qi: (b, h, 0, 0, 0)),
                  lam_spec, lam_spec,
                  pl.BlockSpec((1, HEAD_W), lambda b, h, qi: (0, 0))],
        out_specs=pl.BlockSpec((1, ATT_NQ * ATT_TQ, HEAD_W), lambda b, h, qi: (b, qi, h)),
        out_shape=jax.ShapeDtypeStruct((B, S, D_MODEL), BF16),
        scratch_shapes=[pltpu.VMEM((ATT_NQ, 2, ATT_TK, ATT_TQ), F32), pltpu.VMEM((ATT_NQ, 2, ATT_VROWS, ATT_TQ), F32)],
        compiler_params=_params(("parallel", "parallel", "parallel")),
        name="diff_attn",
    )(qt, kr, vt, lam_a, lam_b, norm_g)


def _merge_kernel(of_ref, ob_ref, z_ref, oB_ref, ga_ref, gb_ref, bga_ref, bgb_ref, x_ref, g1_ref, gn_ref,
                  wa_ref, wb_ref, wo_ref, o_ref, ya_ref):
    oa = of_ref[...].astype(F32) + ob_ref[...].astype(F32)
    z = z_ref[...].astype(F32)
    gate = z * _sigmoid(z)
    for h in range(N_HEADS):
        cols = slice(h * HEAD_W, (h + 1) * HEAD_W)
        oh = oa[:, cols]
        ms = jnp.mean(oh * oh, axis=-1, keepdims=True)
        ya_ref[:, cols] = (oh * lax.rsqrt(ms + EPS) * gn_ref[...] * gate[:, cols]).astype(BF16)
    y_a = _dot(ya_ref[...], wa_ref[...])
    y_b = _dot(oB_ref[...], wb_ref[...])
    gate_a = _sigmoid(ga_ref[...].astype(F32) + bga_ref[...])
    gate_b = _sigmoid(gb_ref[...].astype(F32) + bgb_ref[...])
    mix = _dot((gate_a * y_a + gate_b * y_b).astype(BF16), wo_ref[...])
    o_ref[...] = x_ref[...] + g1_ref[0] * mix


def _merge(oAf, oAb, proj, oB, b_gate, x2, gate1, gn, wa, wb, wo, S):
    N = x2.shape[0]
    TM = min(512, S)
    tpb = S // TM
    row = lambda i: (i, 0)
    full = pl.BlockSpec((D_MODEL, D_MODEL), lambda i: (0, 0))
    return pl.pallas_call(
        _merge_kernel,
        grid=(N // TM,),
        in_specs=[pl.BlockSpec((TM, D_MODEL), row),
                  pl.BlockSpec((TM, D_MODEL), row),
                  pl.BlockSpec((TM, D_MODEL), lambda i: (i, COL_ZA)),
                  pl.BlockSpec((TM, D_MODEL), row),
                  pl.BlockSpec((TM, D_MODEL), lambda i: (i, COL_GA)),
                  pl.BlockSpec((TM, D_MODEL), lambda i: (i, COL_GB)),
                  pl.BlockSpec((1, D_MODEL), lambda i: (0, 0)),
                  pl.BlockSpec((1, D_MODEL), lambda i: (0, 1)),
                  pl.BlockSpec((TM, D_MODEL), row),
                  pl.BlockSpec((1, 1, D_MODEL), lambda i: (i // tpb, 0, 0)),
                  pl.BlockSpec((1, HEAD_W), lambda i: (0, 0)),
                  full, full, full],
        out_specs=pl.BlockSpec((TM, D_MODEL), row),
        out_shape=jax.ShapeDtypeStruct((N, D_MODEL), F32),
        scratch_shapes=[pltpu.VMEM((TM, D_MODEL), BF16)],
        compiler_params=_params(("parallel",)),
        name="merge",
    )(oAf, oAb, proj, oB, proj, proj, b_gate, b_gate, x2, gate1, gn, wa, wb, wo)


def _router_kernel(x_ref, sh_ref, sc_ref, g_ref, rw0_ref, rw1_ref, rb_ref, tri_ref,
                   h_ref, idx_ref, gate_ref, rank_ref, cnt_ref, base_ref):
    i = pl.program_id(0)

    @pl.when(i == 0)
    def _():
        base_ref[...] = jnp.zeros_like(base_ref)

    x = x_ref[...]
    ms = jnp.mean(x * x, axis=-1, keepdims=True)
    h = x * lax.rsqrt(ms + EPS) * g_ref[...] * (1.0 + sc_ref[0]) + sh_ref[0]
    h_ref[...] = h
    h0 = h.astype(BF16)
    h1 = (h - h0.astype(F32)).astype(BF16)
    logits = _dot(h0, rw0_ref[...]) + (_dot(h0, rw1_ref[...]) + _dot(h1, rw0_ref[...])) + rb_ref[...]
    lane = lax.broadcasted_iota(jnp.int32, logits.shape, 1)
    lane_f = lane.astype(F32)
    cur = jnp.where(lane < N_EXPERTS, logits, NEG_INF)
    vals, sel = [], []
    for _ in range(TOP_K):
        m = jnp.max(cur, axis=1, keepdims=True)
        ix = jnp.min(jnp.where(cur == m, lane_f, float(LANES)), axis=1, keepdims=True)
        hit = lane_f == ix
        vals.append(m)
        sel.append(hit)
        cur = jnp.where(hit, NEG_INF, cur)
    exps = [jnp.exp(v - vals[0]) for v in vals]
    den = exps[0] + exps[1] + exps[2] + exps[3]
    onehot = jnp.zeros(logits.shape, F32)
    for hit in sel:
        onehot = onehot + jnp.where(hit, 1.0, 0.0)
    before = _dot(tri_ref[...], onehot.astype(BF16)) + base_ref[...]
    idx_out = jnp.zeros(logits.shape, F32)
    gate_out = jnp.zeros(logits.shape, F32)
    rank_out = jnp.zeros(logits.shape, F32)
    for kk in range(TOP_K):
        slot = lane == kk
        e_id = jnp.sum(jnp.where(sel[kk], lane_f, 0.0), axis=1, keepdims=True)
        rk = jnp.sum(jnp.where(sel[kk], before, 0.0), axis=1, keepdims=True)
        idx_out = jnp.where(slot, e_id, idx_out)
        gate_out = jnp.where(slot, exps[kk] / den, gate_out)
        rank_out = jnp.where(slot, rk, rank_out)
    idx_ref[...] = idx_out.astype(jnp.int32)
    gate_ref[...] = gate_out
    rank_ref[...] = rank_out.astype(jnp.int32)
    base_ref[...] = base_ref[...] + jnp.sum(onehot, axis=0, keepdims=True)
    cnt_ref[...] = base_ref[...]


def _router(x1, shift, scale, g, rw0, rw1, rb, S):
    N = x1.shape[0]
    TM = min(512, S)
    tpb = S // TM
    r = jnp.arange(TM)
    tri = (r[None, :] < r[:, None]).astype(BF16)
    row = lambda i: (i, 0)
    const = lambda i: (0, 0)
    lanes = pl.BlockSpec((TM, LANES), row)
    return pl.pallas_call(
        _router_kernel,
        grid=(N // TM,),
        in_specs=[pl.BlockSpec((TM, D_MODEL), row),
                  pl.BlockSpec((1, 1, D_MODEL), lambda i: (i // tpb, 0, 0)),
                  pl.BlockSpec((1, 1, D_MODEL), lambda i: (i // tpb, 0, 0)),
                  pl.BlockSpec((1, D_MODEL), const),
                  pl.BlockSpec((D_MODEL, LANES), const),
                  pl.BlockSpec((D_MODEL, LANES), const),
                  pl.BlockSpec((1, LANES), const),
                  pl.BlockSpec((TM, TM), const)],
        out_specs=[pl.BlockSpec((TM, D_MODEL), row), lanes, lanes, lanes, pl.BlockSpec((1, LANES), const)],
        out_shape=[jax.ShapeDtypeStruct((N, D_MODEL), F32),
                   jax.ShapeDtypeStruct((N, LANES), jnp.int32),
                   jax.ShapeDtypeStruct((N, LANES), F32),
                   jax.ShapeDtypeStruct((N, LANES), jnp.int32),
                   jax.ShapeDtypeStruct((1, LANES), F32)],
        scratch_shapes=[pltpu.VMEM((1, LANES), F32)],
        compiler_params=_params(("arbitrary",)),
        name="router",
    )(x1, shift, scale, g, rw0, rw1, rb, tri)


INDEX_SLICE = ROW_MOVE_TILE * TOP_K


def _row_copy_out(h_ref, xs_hbm, sem, r, dst):
    return pltpu.make_async_copy(h_ref.at[pl.ds(r, 1)], xs_hbm.at[pl.ds(dst, 1)], sem)


def _zero_tile_copy(zero_ref, xs_hbm, sem, start):
    return pltpu.make_async_copy(zero_ref, xs_hbm.at[pl.ds(pl.multiple_of(start, MOE_TILE), MOE_TILE)], sem)


def _dispatch_kernel(pad_end_ref, padded_ref, dest_hbm, h_ref, xs_hbm, idx_smem, zero_ref, sem_idx, sem_rows, sem_zero):
    i = pl.program_id(0)
    fetch = pltpu.make_async_copy(dest_hbm.at[pl.ds(i * INDEX_SLICE, INDEX_SLICE)], idx_smem, sem_idx)
    fetch.start()

    @pl.when(i == 0)
    def _():
        zero_ref[...] = jnp.zeros_like(zero_ref)
        for e in range(N_EXPERTS):
            @pl.when(padded_ref[e] > 0)
            def _():
                _zero_tile_copy(zero_ref, xs_hbm, sem_zero, pad_end_ref[e] - MOE_TILE).start()
        for e in range(N_EXPERTS):
            @pl.when(padded_ref[e] > 0)
            def _():
                _zero_tile_copy(zero_ref, xs_hbm, sem_zero, 0).wait()

    fetch.wait()

    def start(r, carry):
        for kk in range(TOP_K):
            _row_copy_out(h_ref, xs_hbm, sem_rows, r, idx_smem[r * TOP_K + kk]).start()
        return carry

    lax.fori_loop(0, ROW_MOVE_TILE, start, 0, unroll=4)
    for kk in range(TOP_K):
        pltpu.make_async_copy(h_ref, xs_hbm.at[pl.ds(0, ROW_MOVE_TILE)], sem_rows).wait()


def _dispatch(pad_ends, padded, dest_flat, h2, n_rows):
    N = h2.shape[0]
    return pl.pallas_call(
        _dispatch_kernel,
        grid_spec=pltpu.PrefetchScalarGridSpec(
            num_scalar_prefetch=2,
            grid=(N // ROW_MOVE_TILE,),
            in_specs=[pl.BlockSpec(memory_space=pl.ANY),
                      pl.BlockSpec((ROW_MOVE_TILE, D_MODEL), lambda i, pe, pd: (i, 0))],
            out_specs=pl.BlockSpec(memory_space=pl.ANY),
            scratch_shapes=[pltpu.SMEM((INDEX_SLICE,), jnp.int32), pltpu.VMEM((MOE_TILE, D_MODEL), F32),
                            pltpu.SemaphoreType.DMA, pltpu.SemaphoreType.DMA, pltpu.SemaphoreType.DMA]),
        out_shape=jax.ShapeDtypeStruct((n_rows, D_MODEL), F32),
        compiler_params=_params(("arbitrary",)),
        name="moe_dispatch",
    )(pad_ends, padded, dest_flat, h2)


def _expert_kernel(te_ref, nu_ref, xs_ref, wg_ref, bg_ref, wl_ref, bl_ref, wd_ref, bd_ref, ys_ref):
    del te_ref

    @pl.when(pl.program_id(0) < nu_ref[0])
    def _():
        xb = xs_ref[...].astype(BF16)
        glu = jnp.minimum(_dot(xb, wg_ref[0].astype(BF16)) + bg_ref[0], SWIGLU_LIMIT)
        lin = jnp.clip(_dot(xb, wl_ref[0].astype(BF16)) + bl_ref[0], -SWIGLU_LIMIT, SWIGLU_LIMIT)
        act = glu * _sigmoid(SWIGLU_ALPHA * glu) * (lin + 1.0)
        ys_ref[...] = _dot(act.astype(BF16), wd_ref[0].astype(BF16)) + bd_ref[0]


def _experts(tile_expert, n_used, xs, wg, bg, wl, bl, wd, bd):
    n_rows = xs.shape[0]
    n_tiles = n_rows // MOE_TILE
    wspec = pl.BlockSpec((1, D_MODEL, D_MODEL), lambda i, te, nu: (te[i], 0, 0))
    bspec = pl.BlockSpec((1, 1, D_MODEL), lambda i, te, nu: (te[i], 0, 0))
    rows = pl.BlockSpec((MOE_TILE, D_MODEL), lambda i, te, nu: (jnp.minimum(i, nu[0] - 1), 0))
    return pl.pallas_call(
        _expert_kernel,
        grid_spec=pltpu.PrefetchScalarGridSpec(
            num_scalar_prefetch=2,
            grid=(n_tiles,),
            in_specs=[rows, wspec, bspec, wspec, bspec, wspec, bspec],
            out_specs=rows),
        out_shape=jax.ShapeDtypeStruct((n_rows, D_MODEL), F32),
        compiler_params=_params(("arbitrary",)),
        name="moe_experts",
    )(tile_expert, n_used, xs, wg, bg, wl, bl, wd, bd)


def _row_copy_in(ys_hbm, buf_ref, sem, src, kk, r):
    return pltpu.make_async_copy(ys_hbm.at[pl.ds(src, 1)], buf_ref.at[kk, pl.ds(r, 1)], sem)


def _combine_kernel(dest_hbm, ys_hbm, gate_ref, x_ref, g2_ref, fg_ref, o_ref, idx_smem, buf_ref, sem_idx, sem_rows):
    i = pl.program_id(0)
    fetch = pltpu.make_async_copy(dest_hbm.at[pl.ds(i * INDEX_SLICE, INDEX_SLICE)], idx_smem, sem_idx)
    fetch.start()
    fetch.wait()

    def start(r, carry):
        for kk in range(TOP_K):
            _row_copy_in(ys_hbm, buf_ref, sem_rows, idx_smem[r * TOP_K + kk], kk, r).start()
        return carry

    lax.fori_loop(0, ROW_MOVE_TILE, start, 0, unroll=4)
    for kk in range(TOP_K):
        pltpu.make_async_copy(ys_hbm.at[pl.ds(0, ROW_MOVE_TILE)], buf_ref.at[kk], sem_rows).wait()

    gates = gate_ref[...]
    moe = gates[:, 0:1] * buf_ref[0]
    for kk in range(1, TOP_K):
        moe = moe + gates[:, kk:kk + 1] * buf_ref[kk]
    x = x_ref[...] + g2_ref[0] * moe
    ms = jnp.mean(x * x, axis=-1, keepdims=True)
    o_ref[...] = x * lax.rsqrt(ms + EPS) * fg_ref[...]


def _combine(dest_flat, ys, gates, x1, gate2, final_g, S):
    N = x1.shape[0]
    TM = ROW_MOVE_TILE
    tpb = S // TM
    row = lambda i: (i, 0)
    return pl.pallas_call(
        _combine_kernel,
        grid=(N // TM,),
        in_specs=[pl.BlockSpec(memory_space=pl.ANY),
                  pl.BlockSpec(memory_space=pl.ANY),
                  pl.BlockSpec((TM, LANES), row),
                  pl.BlockSpec((TM, D_MODEL), row),
                  pl.BlockSpec((1, 1, D_MODEL), lambda i: (i // tpb, 0, 0)),
                  pl.BlockSpec((1, D_MODEL), lambda i: (0, 0))],
        out_specs=pl.BlockSpec((TM, D_MODEL), row),
        out_shape=jax.ShapeDtypeStruct((N, D_MODEL), F32),
        scratch_shapes=[pltpu.SMEM((INDEX_SLICE,), jnp.int32), pltpu.VMEM((TOP_K, TM, D_MODEL), F32),
                        pltpu.SemaphoreType.DMA, pltpu.SemaphoreType.DMA],
        compiler_params=_params(("arbitrary",)),
        name="moe_combine",
    )(dest_flat, ys, gates, x1, gate2, final_g)


def _pad_lanes(a, offset=0):
    return jnp.pad(a, ((0, 0), (offset, LANES - offset - a.shape[1])))


def kernel(x, c, ada_w, ada_b, norm1_g, norm2_g, w_in, b_gate, conv_w, a_log, dt_bias, gdn_norm_g, w_branch_a,
           diff_lambda, diff_norm_g, w_branch_b, w_out, router_w, router_b, w_glu, b_glu, w_lin, b_lin, w_down,
           b_down, final_g):
    B, S, D = x.shape
    N = B * S
    assert D == D_MODEL and S % GROUP == 0 and ada_w.shape[0] == 1
    x2 = x.reshape(N, D)

    mod = _adaln(c, ada_w[0], ada_b[0])
    shift1, scale1, gate1, shift2, scale2, gate2 = [m.reshape(B, 1, D) for m in jnp.split(mod, 6, axis=-1)]

    wi = w_in[0]
    n_a = 4 * D
    n_small = 4 * N_HEADS
    w_big = jnp.concatenate([wi[:, :n_a], wi[:, n_a + n_small:]], axis=1).astype(BF16)
    w_small = _pad_lanes(wi[:, n_a:n_a + n_small]).astype(BF16)
    alog_row = _pad_lanes(a_log[0].reshape(1, -1), GATE_LANE0)
    dt_row = _pad_lanes(dt_bias[0].reshape(1, -1), GATE_LANE0)
    proj, beta, G, Gl, cdl = _inproj(x2, shift1, scale1, norm1_g, w_big, w_small, alog_row, dt_row, S)
    proj3 = proj.reshape(B, S, N_COL_BLOCKS * D)

    qkv = _gdn_conv(proj3, conv_w[0])
    r3 = lambda a: a.reshape(B, S, LANES)
    nC = S // GDN_CHUNK
    GT = jnp.transpose(r3(G)[:, :, GATE_LANE0:GATE_LANE0 + 2 * N_HEADS], (0, 2, 1)).reshape(B, 2 * N_HEADS, 1, S)
    cd = r3(cdl).reshape(B, nC, GDN_CHUNK, LANES)[:, :, 0, GATE_LANE0:GATE_LANE0 + 2 * N_HEADS]
    cd = jnp.transpose(cd.reshape(B, nC, 2, N_HEADS), (0, 2, 3, 1)).reshape(-1)
    u, w, qd, att, kdt = _gdn_prep(qkv, r3(beta), r3(G), r3(Gl), GT)
    oAf, oAb = _gdn_scan(cd, u, w, qd, att, kdt)

    half = DIFF_DH // 2
    inv_freq = ROPE_THETA ** (-jnp.arange(half, dtype=F32) / half)
    ang = jnp.arange(S, dtype=F32)[:, None] * inv_freq[None, :]
    cos_t = jnp.tile(jnp.cos(ang), (1, 4))
    sin_h = jnp.sin(ang)
    sin_t = jnp.tile(jnp.concatenate([-sin_h, sin_h], axis=1), (1, 2))
    qt, kr, vt = _rope(proj3, cos_t, sin_t)
    lam_a = jnp.pad(_pad_lanes(diff_lambda[0][0::2]), ((0, 6), (0, 0)))
    lam_b = jnp.pad(_pad_lanes(diff_lambda[0][1::2]), ((0, 6), (0, 0)))
    oB = _diff_attn(qt, kr, vt, lam_a, lam_b, diff_norm_g)

    x1 = _merge(oAf.reshape(N, D), oAb.reshape(N, D), proj, oB.reshape(N, D), b_gate, x2, gate1, gdn_norm_g,
                w_branch_a[0].astype(BF16), w_branch_b[0].astype(BF16), w_out[0].astype(BF16), S)

    rw = _pad_lanes(router_w[0])
    rw0 = rw.astype(BF16)
    rw1 = (rw - rw0.astype(F32)).astype(BF16)
    h2, idx, gates, rank, counts = _router(x1, shift2, scale2, norm2_g, rw0, rw1, _pad_lanes(router_b), S)
    cnt = counts[0, :N_EXPERTS].astype(jnp.int32)
    padded = (cnt + MOE_TILE - 1) // MOE_TILE * MOE_TILE
    pad_ends = jnp.cumsum(padded)
    pad_starts = pad_ends - padded
    dest = (pad_starts[idx[:, :TOP_K]] + rank[:, :TOP_K]).reshape(-1)
    n_tiles = -(-(N * TOP_K) // MOE_TILE) + N_EXPERTS
    tile_start = jnp.arange(n_tiles, dtype=jnp.int32) * MOE_TILE
    tile_expert = jnp.sum((tile_start[:, None] >= pad_ends[None, :]).astype(jnp.int32), axis=1)
    tile_expert = jnp.minimum(tile_expert, N_EXPERTS - 1)
    n_used = (pad_ends[N_EXPERTS - 1:] // MOE_TILE).astype(jnp.int32)
    xs = _dispatch(pad_ends.astype(jnp.int32), padded, dest, h2, n_tiles * MOE_TILE)
    ys = _experts(tile_expert, n_used, xs, w_glu[0], b_glu[0][:, None, :], w_lin[0], b_lin[0][:, None, :],
                  w_down[0], b_down[0][:, None, :])
    out = _combine(dest, ys, gates, x1, gate2, final_g.reshape(1, D), S)
    return out.reshape(B, S, D)
```

```python
import functools
import math

import jax
import jax.numpy as jnp
from jax import lax
from jax.experimental import pallas as pl
from jax.experimental.pallas import tpu as pltpu

F32 = jnp.float32
BF16 = jnp.bfloat16

D_MODEL = 1024
EPS = 1e-6
N_HEADS = 8
HEAD_W = 128
GDN_CHUNK = 64
CONV_WIDTH = 5
DIFF_DH = 64
ROPE_THETA = 10000.0
LAM_INIT = 0.8 - 0.6 * math.exp(-0.3 * 0)
N_EXPERTS = 32
TOP_K = 4
SWIGLU_ALPHA = 1.702
SWIGLU_LIMIT = 7.0

LANES = 128
GROUP = 256
CHUNKS_PER_GROUP = GROUP // GDN_CHUNK
CHUNK_SHIFT = GDN_CHUNK.bit_length() - 1
MOE_TILE = 512
ROW_MOVE_TILE = 1024
NEG_INF = float("-inf")

COL_QA, COL_KA, COL_VA, COL_ZA, COL_QB, COL_KB, COL_VB, COL_GA, COL_GB = range(9)
N_COL_BLOCKS = 9


def _params(sem, vmem_mb=48):
    return pltpu.CompilerParams(dimension_semantics=sem, vmem_limit_bytes=vmem_mb * 1024 * 1024)


def _dot(a, b):
    return jnp.dot(a, b, preferred_element_type=F32)


def _dot_nt(a, b):
    return lax.dot_general(a, b, (((1,), (1,)), ((), ())), preferred_element_type=F32)


def _sigmoid(x):
    return 1.0 / (1.0 + jnp.exp(-x))


def _split3(x):
    a = x.astype(BF16)
    r = x - a.astype(F32)
    b = r.astype(BF16)
    c = (r - b.astype(F32)).astype(BF16)
    return a, b, c


def _adaln_kernel(c_ref, w_ref, b_ref, o_ref):
    c = c_ref[...]
    cond = c * _sigmoid(c)
    c0, c1, c2 = _split3(cond)
    w0, w1, w2 = _split3(w_ref[...])
    acc = _dot(c0, w0) + (_dot(c0, w1) + _dot(c1, w0)) + (_dot(c0, w2) + _dot(c1, w1) + _dot(c2, w0))
    o_ref[...] = acc + b_ref[...]


def _adaln(c, ada_w, ada_b):
    B = c.shape[0]
    n = ada_w.shape[1] // D_MODEL
    return pl.pallas_call(
        _adaln_kernel,
        grid=(n,),
        in_specs=[pl.BlockSpec((B, D_MODEL), lambda j: (0, 0)),
                  pl.BlockSpec((D_MODEL, D_MODEL), lambda j: (0, j)),
                  pl.BlockSpec((1, D_MODEL), lambda j: (0, j))],
        out_specs=pl.BlockSpec((B, D_MODEL), lambda j: (0, j)),
        out_shape=jax.ShapeDtypeStruct((B, n * D_MODEL), F32),
        compiler_params=_params(("parallel",)),
        name="adaln",
    )(c, ada_w, ada_b.reshape(1, -1))


def _inproj_kernel(x_ref, sh_ref, sc_ref, g_ref, w_ref, ws_ref, alog_ref, dt_ref,
                   o_ref, beta_ref, gcum_ref, glast_ref, cd_ref, h_ref):
    @pl.when(pl.program_id(1) == 0)
    def _():
        x = x_ref[...]
        ms = jnp.mean(x * x, axis=-1, keepdims=True)
        y = x * lax.rsqrt(ms + EPS) * g_ref[...]
        h = (y * (1.0 + sc_ref[0]) + sh_ref[0]).astype(BF16)
        h_ref[...] = h
        _gdn_gates(_dot(h, ws_ref[...]), alog_ref[...], dt_ref[...], beta_ref, gcum_ref, glast_ref, cd_ref)

    col = pl.multiple_of(pl.program_id(1) * D_MODEL, D_MODEL)
    o_ref[...] = _dot(h_ref[...], w_ref[:, pl.ds(col, D_MODEL)]).astype(o_ref.dtype)


def _inproj(x2, shift, scale, g, w_big, w_small, alog_row, dt_row, S):
    N = x2.shape[0]
    TM = min(1024, S)
    tpb = S // TM
    const = lambda i, j: (0, 0)
    lanes = pl.BlockSpec((TM, LANES), lambda i, j: (i, 0))
    return pl.pallas_call(
        _inproj_kernel,
        grid=(N // TM, N_COL_BLOCKS),
        in_specs=[pl.BlockSpec((TM, D_MODEL), lambda i, j: (i, 0)),
                  pl.BlockSpec((1, 1, D_MODEL), lambda i, j: (i // tpb, 0, 0)),
                  pl.BlockSpec((1, 1, D_MODEL), lambda i, j: (i // tpb, 0, 0)),
                  pl.BlockSpec((1, D_MODEL), const),
                  pl.BlockSpec((D_MODEL, N_COL_BLOCKS * D_MODEL), const, pipeline_mode=pl.Buffered(1)),
                  pl.BlockSpec((D_MODEL, LANES), const),
                  pl.BlockSpec((1, LANES), const),
                  pl.BlockSpec((1, LANES), const)],
        out_specs=[pl.BlockSpec((TM, D_MODEL), lambda i, j: (i, j)), lanes, lanes, lanes, lanes],
        out_shape=[jax.ShapeDtypeStruct((N, N_COL_BLOCKS * D_MODEL), BF16)]
                  + [jax.ShapeDtypeStruct((N, LANES), F32)] * 4,
        scratch_shapes=[pltpu.VMEM((TM, D_MODEL), BF16)],
        compiler_params=_params(("parallel", "arbitrary")),
        name="inproj",
    )(x2, shift, scale, g, w_big, w_small, alog_row, dt_row)


HALO = 16


CONV_BLOCK = 256
CONV_PAD = (CONV_WIDTH - 1) // 2
CONV_TAPS = tuple(j for j in range(CONV_WIDTH) if j != CONV_PAD)


def _conv_shifts():
    r = jnp.arange(CONV_BLOCK)[:, None]
    c = jnp.arange(CONV_BLOCK)[None, :]
    return jnp.stack([c == r + (j - CONV_PAD) for j in CONV_TAPS]).astype(BF16)


def _conv_kernel(cur_ref, prev_ref, next_ref, w_ref, shift_ref, o_ref, ext_ref, *, TR):
    i = pl.program_id(1)
    g = pl.program_id(2)
    last = pl.num_programs(1) - 1
    ext_ref[8:8 + TR, :] = cur_ref[0].astype(F32)
    pv = prev_ref[0].astype(F32)[HALO - 8:HALO]
    nx = next_ref[0].astype(F32)[0:8]
    ext_ref[0:8, :] = jnp.where(i > 0, pv, 0.0)
    ext_ref[TR + 8:TR + 16, :] = jnp.where(i < last, nx, 0.0)
    ones = jnp.ones((HEAD_W, HEAD_W), BF16)
    qscale = jnp.where(g == 0, HEAD_W ** -0.5, 1.0)

    def edge_rows(row0):
        e = ext_ref[8 + row0 - CONV_PAD:16 + row0 - CONV_PAD, :] * w_ref[0:1, :]
        for j in range(1, CONV_WIDTH):
            e = e + ext_ref[8 + row0 - CONV_PAD + j:16 + row0 - CONV_PAD + j, :] * w_ref[j:j + 1, :]
        return e

    for blk in range(TR // CONV_BLOCK):
        r0 = blk * CONV_BLOCK
        rows = slice(r0, r0 + CONV_BLOCK)
        ub = cur_ref[0, rows, :]
        acc = ext_ref[8 + r0:8 + r0 + CONV_BLOCK, :] * w_ref[CONV_PAD:CONV_PAD + 1, :]
        for si, j in enumerate(CONV_TAPS):
            acc = acc + _dot(shift_ref[si], ub) * w_ref[j:j + 1, :]
        acc = jnp.concatenate([edge_rows(r0), acc[8:CONV_BLOCK - 8], edge_rows(r0 + CONV_BLOCK - 8)], axis=0)
        y = acc * _sigmoid(acc)

        for h in range(N_HEADS):
            cols = slice(h * HEAD_W, (h + 1) * HEAD_W)
            yh = y[:, cols]
            ss = _dot((yh * yh).astype(BF16), ones)
            scale = jnp.where(g < 2, lax.rsqrt(ss + EPS) * qscale, 1.0)
            o_ref[0, rows, cols] = (yh * scale).astype(o_ref.dtype)


def _gdn_conv(proj3, conv_w):
    B, S, _ = proj3.shape
    TR = min(512, S)
    nT = S // TR
    rb = TR // HALO
    nH = S // HALO
    return pl.pallas_call(
        functools.partial(_conv_kernel, TR=TR),
        grid=(B, nT, 3),
        in_specs=[pl.BlockSpec((1, TR, D_MODEL), lambda b, i, g: (b, i, g)),
                  pl.BlockSpec((1, HALO, D_MODEL), lambda b, i, g: (b, jnp.maximum(i * rb - 1, 0), g)),
                  pl.BlockSpec((1, HALO, D_MODEL), lambda b, i, g: (b, jnp.minimum((i + 1) * rb, nH - 1), g)),
                  pl.BlockSpec((CONV_WIDTH, D_MODEL), lambda b, i, g: (0, g)),
                  pl.BlockSpec((len(CONV_TAPS), CONV_BLOCK, CONV_BLOCK), lambda b, i, g: (0, 0, 0))],
        out_specs=pl.BlockSpec((1, TR, D_MODEL), lambda b, i, g: (b, i, g)),
        out_shape=jax.ShapeDtypeStruct((B, S, 3 * D_MODEL), BF16),
        scratch_shapes=[pltpu.VMEM((TR + 16, D_MODEL), F32)],
        compiler_params=_params(("parallel", "parallel", "parallel")),
        name="gdn_conv",
    )(proj3, proj3, proj3, conv_w, _conv_shifts())


GATE_LANE0 = 16


def _gdn_gates(x, alog, dt, beta_ref, g_ref, gl_ref, cd_ref):
    lane = lax.broadcasted_iota(jnp.int32, (GROUP, LANES), 1)
    r = lax.broadcasted_iota(jnp.int32, (GROUP, GROUP), 0)
    c = lax.broadcasted_iota(jnp.int32, (GROUP, GROUP), 1)
    same = (r >> CHUNK_SHIFT) == (c >> CHUNK_SHIFT)
    lower = jnp.where(same & (c <= r), 1.0, 0.0).astype(BF16)
    upper = jnp.where(same & (c >= r), 1.0, 0.0).astype(BF16)
    block = jnp.where(same, 1.0, 0.0).astype(BF16)
    beta_ref[...] = _sigmoid(x)
    for gi in range(x.shape[0] // GROUP):
        rows = slice(gi * GROUP, (gi + 1) * GROUP)
        z = x[rows] + dt
        softplus = jnp.maximum(z, 0.0) + jnp.log(1.0 + jnp.exp(-jnp.abs(z)))
        gd = -jnp.exp(alog) * softplus
        gd = jnp.where((lane >= GATE_LANE0) & (lane < GATE_LANE0 + 2 * N_HEADS), gd, 0.0)
        p0, p1, p2 = _split3(gd)
        g_fwd = _dot(lower, p0) + _dot(lower, p1) + _dot(lower, p2)
        g_bwd = _dot(upper, p0) + _dot(upper, p1) + _dot(upper, p2)
        tot = _dot(block, p0) + _dot(block, p1) + _dot(block, p2)
        G = jnp.where(lane < GATE_LANE0 + N_HEADS, g_fwd, g_bwd)
        g_ref[rows, :] = G
        gl_ref[rows, :] = tot - G
        cd_ref[rows, :] = jnp.exp(tot)


PREP_HEADS = 4
(MASK_STRICT_LO, MASK_STRICT_UP, MASK_INCL_LO, MASK_INCL_UP, MASK_EYE, MASK_BLK4, MASK_OFF0) = range(7)
N_MASKS = MASK_OFF0 + (CHUNK_SHIFT - 2)


def _prep_masks():
    r = jnp.arange(GROUP)[:, None]
    c = jnp.arange(GROUP)[None, :]
    same = (r >> CHUNK_SHIFT) == (c >> CHUNK_SHIFT)
    masks = [same & (c < r), same & (c > r), same & (c <= r), same & (c >= r), r == c, (r >> 2) == (c >> 2)]
    for shift in range(2, CHUNK_SHIFT):
        masks.append(((r >> shift) != (c >> shift)) & ((r >> (shift + 1)) == (c >> (shift + 1))))
    return jnp.stack(masks).astype(F32)


def _col(x, l, lane):
    return jnp.broadcast_to(jnp.sum(jnp.where(lane == l, x, 0.0), axis=1, keepdims=True), x.shape)


def _prep_kernel(q_ref, k_ref, v_ref, beta_ref, g_ref, gl_ref, gtf_ref, gtb_ref, mask_ref, bmask_ref,
                 u_ref, w_ref, qd_ref, at_ref, kdt_ref):
    hp = pl.program_id(1)
    lane = lax.broadcasted_iota(jnp.int32, (GROUP, LANES), 1)
    wide = lambda a: jnp.concatenate([a, a], axis=1)
    chains = [(hh, d) for hh in range(PREP_HEADS) for d in range(2)]
    p, rhs = {}, {}
    for hh in range(PREP_HEADS):
        cols = slice(hh * HEAD_W, (hh + 1) * HEAD_W)
        q = q_ref[0, :, cols]
        k = k_ref[0, :, cols]
        qf = q.astype(F32)
        kf = k.astype(F32)
        vf = v_ref[0, :, cols].astype(F32)
        kk = _dot_nt(k, k)
        qk = _dot_nt(q, k)
        for d in range(2):
            lb = d * N_HEADS + hp * PREP_HEADS + hh
            beta_c = _col(beta_ref[0], lb, lane)
            g_c = _col(g_ref[0], GATE_LANE0 + lb, lane)
            eg_c = jnp.exp(g_c)
            egl_c = jnp.exp(_col(gl_ref[0], GATE_LANE0 + lb, lane))
            g_r = (gtf_ref if d == 0 else gtb_ref)[0, hh]
            dec = jnp.exp(jnp.minimum(wide(g_c) - g_r, 0.0))
            p[hh, d] = (-(kk * wide(beta_c)) * dec * mask_ref[MASK_STRICT_LO + d]).astype(BF16)
            att = qk * dec * mask_ref[MASK_INCL_LO + d]
            rhs[hh, d] = jnp.concatenate([vf * beta_c, kf * (beta_c * eg_c)], axis=1).astype(BF16)
            qd_ref[0, d, :, cols] = (qf * eg_c).astype(qd_ref.dtype)
            kdt = (kf * egl_c).T
            for ci in range(CHUNKS_PER_GROUP):
                sl = slice(ci * GDN_CHUNK, (ci + 1) * GDN_CHUNK)
                at_ref[0, d, hh, ci] = att[sl, sl].astype(at_ref.dtype)
                kdt_ref[0, d, hh, ci] = kdt[:, sl].astype(kdt_ref.dtype)
    p4 = {ch: p[ch] * bmask_ref[0] for ch in chains}
    sq = {ch: _dot(p4[ch], p4[ch]).astype(BF16) for ch in chains}
    tb = {ch: bmask_ref[N_MASKS - MASK_BLK4] + p4[ch] for ch in chains}
    tb = {ch: (tb[ch].astype(F32) + _dot(tb[ch], sq[ch])).astype(BF16) for ch in chains}
    for lvl in range(CHUNK_SHIFT - 2):
        x = {ch: _dot(tb[ch], p[ch] * bmask_ref[1 + lvl]).astype(BF16) for ch in chains}
        tb = {ch: (tb[ch].astype(F32) + _dot(x[ch], tb[ch])).astype(BF16) for ch in chains}
    uw = {ch: _dot(tb[ch], rhs[ch]) for ch in chains}
    for hh, d in chains:
        cols = slice(hh * HEAD_W, (hh + 1) * HEAD_W)
        u_ref[0, d, :, cols] = uw[hh, d][:, :HEAD_W].astype(u_ref.dtype)
        w_ref[0, d, :, cols] = uw[hh, d][:, HEAD_W:].astype(w_ref.dtype)


def _gdn_prep(qkv, beta, G, Gl, GT):
    B, S, _ = qkv.shape
    nG = S // GROUP
    nC = S // GDN_CHUNK
    PW = PREP_HEADS * HEAD_W
    nP = N_HEADS // PREP_HEADS
    sm = pl.BlockSpec((1, GROUP, LANES), lambda b, h, g: (b, g, 0))
    big = pl.BlockSpec((1, 2, GROUP, PW), lambda b, h, g: (b, 0, g, h))
    masks = _prep_masks()
    return pl.pallas_call(
        _prep_kernel,
        grid=(B, nP, nG),
        in_specs=[pl.BlockSpec((1, GROUP, PW), lambda b, h, g: (b, g, h)),
                  pl.BlockSpec((1, GROUP, PW), lambda b, h, g: (b, g, nP + h)),
                  pl.BlockSpec((1, GROUP, PW), lambda b, h, g: (b, g, 2 * nP + h)),
                  sm, sm, sm,
                  pl.BlockSpec((1, PREP_HEADS, 1, GROUP), lambda b, h, g: (b, h, 0, g)),
                  pl.BlockSpec((1, PREP_HEADS, 1, GROUP), lambda b, h, g: (b, nP + h, 0, g)),
                  pl.BlockSpec((MASK_EYE, GROUP, GROUP), lambda b, h, g: (0, 0, 0)),
                  pl.BlockSpec((N_MASKS - MASK_BLK4 + 1, GROUP, GROUP), lambda b, h, g: (0, 0, 0))],
        out_specs=[big, big, big,
                   pl.BlockSpec((1, 2, PREP_HEADS, CHUNKS_PER_GROUP, GDN_CHUNK, GDN_CHUNK),
                                lambda b, h, g: (b, 0, h, g, 0, 0)),
                   pl.BlockSpec((1, 2, PREP_HEADS, CHUNKS_PER_GROUP, HEAD_W, GDN_CHUNK),
                                lambda b, h, g: (b, 0, h, g, 0, 0))],
        out_shape=[jax.ShapeDtypeStruct((B, 2, S, D_MODEL), BF16)] * 3
                  + [jax.ShapeDtypeStruct((B, 2, N_HEADS, nC, GDN_CHUNK, GDN_CHUNK), BF16),
                     jax.ShapeDtypeStruct((B, 2, N_HEADS, nC, HEAD_W, GDN_CHUNK), BF16)],
        compiler_params=_params(("parallel", "parallel", "parallel")),
        name="gdn_prep",
    )(qkv, qkv, qkv, beta, G, Gl, GT, GT, masks[:MASK_EYE],
      jnp.concatenate([masks[MASK_BLK4:], masks[MASK_EYE:MASK_EYE + 1]]).astype(BF16))


def _scan_kernel(cd_ref, uf_ref, wf_ref, qdf_ref, atf_ref, kdtf_ref, ub_ref, wb_ref, qdb_ref, atb_ref, kdtb_ref,
                 of_ref, ob_ref, state_ref, *, nc, nC):
    b = pl.program_id(0)
    t = pl.program_id(1)
    nT = pl.num_programs(1)

    @pl.when(t == 0)
    def _():
        state_ref[...] = jnp.zeros_like(state_ref)

    dirs = ((uf_ref, wf_ref, qdf_ref, atf_ref, kdtf_ref, of_ref), (ub_ref, wb_ref, qdb_ref, atb_ref, kdtb_ref, ob_ref))

    def chunk(ci, carry):
        work = []
        for d, refs in enumerate(dirs):
            c = ci if d == 0 else nc - 1 - ci
            tt = t if d == 0 else nT - 1 - t
            row = pl.multiple_of(c * GDN_CHUNK, GDN_CHUNK)
            for h in range(N_HEADS):
                work.append((d, h, c, row, ((b * 2 + d) * N_HEADS + h) * nC + tt * nc + c, refs))
        s_old = [state_ref[d, h] for d, h, *_ in work]
        sb = [s.astype(BF16) for s in s_old]
        tile = lambda ref, row, h: ref[0, 0, pl.ds(row, GDN_CHUNK), h * HEAD_W:(h + 1) * HEAD_W]
        ws = [_dot(tile(refs[1], row, h), sb[i]) for i, (d, h, c, row, gi, refs) in enumerate(work)]
        qs = [_dot(tile(refs[2], row, h), sb[i]) for i, (d, h, c, row, gi, refs) in enumerate(work)]
        vb = [(tile(refs[0], row, h).astype(F32) - ws[i]).astype(BF16)
              for i, (d, h, c, row, gi, refs) in enumerate(work)]
        o = [qs[i] + _dot(refs[3][0, 0, h, c], vb[i]) for i, (d, h, c, row, gi, refs) in enumerate(work)]
        upd = [_dot(refs[4][0, 0, h, c], vb[i]) for i, (d, h, c, row, gi, refs) in enumerate(work)]
        for i, (d, h, c, row, gi, refs) in enumerate(work):
            state_ref[d, h] = s_old[i] * cd_ref[gi] + upd[i]
            refs[5][0, pl.ds(row, GDN_CHUNK), h * HEAD_W:(h + 1) * HEAD_W] = o[i].astype(of_ref.dtype)
        return carry

    lax.fori_loop(0, nc, chunk, 0)


def _gdn_scan(cd, u, w, qd, att, kdt):
    B, _, S, _ = u.shape
    TC = min(512, S)
    nT = S // TC
    nc = TC // GDN_CHUNK
    nC = S // GDN_CHUNK
    fwd = lambda b, t: t
    bwd = lambda b, t: nT - 1 - t

    def specs(d, tm):
        big = pl.BlockSpec((1, 1, TC, D_MODEL), lambda b, t: (b, d, tm(b, t), 0))
        return [big, big, big,
                pl.BlockSpec((1, 1, N_HEADS, nc, GDN_CHUNK, GDN_CHUNK), lambda b, t: (b, d, 0, tm(b, t), 0, 0)),
                pl.BlockSpec((1, 1, N_HEADS, nc, HEAD_W, GDN_CHUNK), lambda b, t: (b, d, 0, tm(b, t), 0, 0))]

    return pl.pallas_call(
        functools.partial(_scan_kernel, nc=nc, nC=nC),
        grid=(B, nT),
        in_specs=[pl.BlockSpec(memory_space=pltpu.SMEM)] + specs(0, fwd) + specs(1, bwd),
        out_specs=[pl.BlockSpec((1, TC, D_MODEL), lambda b, t: (b, t, 0)),
                   pl.BlockSpec((1, TC, D_MODEL), lambda b, t: (b, nT - 1 - t, 0))],
        out_shape=[jax.ShapeDtypeStruct((B, S, D_MODEL), BF16)] * 2,
        scratch_shapes=[pltpu.VMEM((2, N_HEADS, HEAD_W, HEAD_W), F32)],
        compiler_params=_params(("parallel", "arbitrary")),
        name="gdn_scan",
    )(cd, u, w, qd, att, kdt, u, w, qd, att, kdt)


ATT_TQ = 256
ATT_TK = 256
ATT_NQ = 4
ATT_VROWS = HEAD_W + 16
LOG2E = 1.4426950408889634


def _rope_kernel(q_ref, k_ref, v_ref, cos_ref, sin_ref, qt_ref, kr_ref, vt_ref, *, TR):
    cs = cos_ref[...]
    sn = sin_ref[...]
    lane = lax.broadcasted_iota(jnp.int32, cs.shape, 1)
    first_half = (lane & (DIFF_DH - 1)) < (DIFF_DH // 2)
    qscale = DIFF_DH ** -0.5 * LOG2E

    def rot(x):
        partner = jnp.where(first_half, pltpu.roll(x, HEAD_W - DIFF_DH // 2, 1), pltpu.roll(x, DIFF_DH // 2, 1))
        return x * cs + partner * sn

    for h in range(N_HEADS):
        cols = slice(h * HEAD_W, (h + 1) * HEAD_W)
        qr = rot(q_ref[0, :, cols].astype(F32)) * qscale
        kr_ref[0, :, cols] = rot(k_ref[0, :, cols].astype(F32)).astype(kr_ref.dtype)
        vf = v_ref[0, :, cols].astype(F32)
        for ci in range(TR // ATT_TK):
            rows = slice(ci * ATT_TK, (ci + 1) * ATT_TK)
            vt_ref[0, h, ci, 0:HEAD_W, :] = vf[rows].T.astype(vt_ref.dtype)
            vt_ref[0, h, ci, HEAD_W:ATT_VROWS, :] = jnp.ones((ATT_VROWS - HEAD_W, ATT_TK), vt_ref.dtype)
        for ci in range(TR // ATT_TQ):
            rows = slice(ci * ATT_TQ, (ci + 1) * ATT_TQ)
            qt_ref[0, h, ci] = qr[rows].T.astype(qt_ref.dtype)


def _rope(proj3, cos_t, sin_t):
    B, S, _ = proj3.shape
    TR = min(512, S)
    tab = pl.BlockSpec((TR, HEAD_W), lambda b, i: (i, 0))
    col = lambda cb: pl.BlockSpec((1, TR, D_MODEL), lambda b, i: (b, i, cb))
    return pl.pallas_call(
        functools.partial(_rope_kernel, TR=TR),
        grid=(B, S // TR),
        in_specs=[col(COL_QB), col(COL_KB), col(COL_VB), tab, tab],
        out_specs=[pl.BlockSpec((1, N_HEADS, TR // ATT_TQ, HEAD_W, ATT_TQ), lambda b, i: (b, 0, i, 0, 0)),
                   pl.BlockSpec((1, TR, D_MODEL), lambda b, i: (b, i, 0)),
                   pl.BlockSpec((1, N_HEADS, TR // ATT_TK, ATT_VROWS, ATT_TK), lambda b, i: (b, 0, i, 0, 0))],
        out_shape=[jax.ShapeDtypeStruct((B, N_HEADS, S // ATT_TQ, HEAD_W, ATT_TQ), BF16),
                   jax.ShapeDtypeStruct((B, S, D_MODEL), BF16),
                   jax.ShapeDtypeStruct((B, N_HEADS, S // ATT_TK, ATT_VROWS, ATT_TK), BF16)],
        compiler_params=_params(("parallel", "parallel")),
        name="rope",
    )(proj3, proj3, proj3, cos_t, sin_t)


def _attn_kernel(qt_ref, k_ref, vt_ref, la_ref, lb_ref, g_ref, o_ref, s_ref, acc_ref, *, n_chunks):
    row = lax.broadcasted_iota(jnp.int32, (HEAD_W, ATT_TQ), 0)
    qw = []
    for qb in range(ATT_NQ):
        qt = qt_ref[0, 0, qb]
        zero = jnp.zeros_like(qt)
        qw.append((jnp.where(row < DIFF_DH, qt, zero), jnp.where(row >= DIFF_DH, qt, zero)))
    chains = [(qb, comp) for qb in range(ATT_NQ) for comp in range(2)]

    def scores(j):
        kc = k_ref[0, pl.ds(pl.multiple_of(j * ATT_TK, ATT_TK), ATT_TK), :]
        return [_dot(kc, qw[qb][comp]) for qb, comp in chains]

    acc_ref[...] = jnp.zeros_like(acc_ref)
    for (qb, comp), s0 in zip(chains, scores(0)):
        s_ref[qb, comp] = s0

    def chunk(j, carry):
        s_next = scores(jnp.minimum(j + 1, n_chunks - 1))
        vt = vt_ref[0, 0, j]
        out = []
        for ci, (qb, comp) in enumerate(chains):
            m_prev = carry[ci]
            s = s_ref[qb, comp]
            m_new = jnp.maximum(m_prev, jnp.max(s, axis=0, keepdims=True))
            alpha = jnp.exp2(m_prev - m_new)
            p = jnp.exp2(s - m_new)
            out.append(m_new)
            acc_ref[qb, comp] = alpha * acc_ref[qb, comp] + _dot(vt, p.astype(BF16))
        for (qb, comp), sn in zip(chains, s_next):
            s_ref[qb, comp] = sn
        return tuple(out)

    neg = jnp.full((1, ATT_TQ), NEG_INF, F32)
    lax.fori_loop(0, n_chunks, chunk, (neg,) * len(chains), unroll=8)

    sums = jnp.sum(la_ref[...] * lb_ref[...], axis=1, keepdims=True)
    lrow = lax.broadcasted_iota(jnp.int32, sums.shape, 0)
    sign = jnp.where(lrow == 0, 1.0, jnp.where(lrow == 1, -1.0, 0.0))
    lam = jnp.sum(sign * jnp.exp(sums), axis=0, keepdims=True) + LAM_INIT
    for qb in range(ATT_NQ):
        l0 = acc_ref[qb, 0, HEAD_W:HEAD_W + 1, :]
        l1 = acc_ref[qb, 1, HEAD_W:HEAD_W + 1, :]
        ot = acc_ref[qb, 0, 0:HEAD_W, :] / l0 - lam * (acc_ref[qb, 1, 0:HEAD_W, :] / l1)
        ms = jnp.mean(ot * ot, axis=0, keepdims=True)
        y = (ot * lax.rsqrt(ms + EPS)).T * g_ref[...] * (1.0 - LAM_INIT)
        o_ref[0, qb * ATT_TQ:(qb + 1) * ATT_TQ, :] = y.astype(o_ref.dtype)


def _diff_attn(qt, kr, vt, lam_a, lam_b, norm_g):
    B, S, _ = kr.shape
    lam_spec = pl.BlockSpec((8, LANES), lambda b, h, qi: (0, 0))
    return pl.pallas_call(
        functools.partial(_attn_kernel, n_chunks=S // ATT_TK),
        grid=(B, N_HEADS, S // (ATT_NQ * ATT_TQ)),
        in_specs=[pl.BlockSpec((1, 1, ATT_NQ, HEAD_W, ATT_TQ), lambda b, h, qi: (b, h, qi, 0, 0)),
                  pl.BlockSpec((1, S, HEAD_W), lambda b, h, qi: (b, 0, h)),
                  pl.BlockSpec((1, 1, S // ATT_TK, ATT_VROWS, ATT_TK), lambda b, h, qi: (b, h, 0, 0, 0)),
                  lam_spec, lam_spec,
                  pl.BlockSpec((1, HEAD_W), lambda b, h, qi: (0, 0))],
        out_specs=pl.BlockSpec((1, ATT_NQ * ATT_TQ, HEAD_W), lambda b, h, qi: (b, qi, h)),
        out_shape=jax.ShapeDtypeStruct((B, S, D_MODEL), BF16),
        scratch_shapes=[pltpu.VMEM((ATT_NQ, 2, ATT_TK, ATT_TQ), F32), pltpu.VMEM((ATT_NQ, 2, ATT_VROWS, ATT_TQ), F32)],
        compiler_params=_params(("parallel", "parallel", "parallel")),
        name="diff_attn",
    )(qt, kr, vt, lam_a, lam_b, norm_g)


def _merge_kernel(of_ref, ob_ref, z_ref, oB_ref, ga_ref, gb_ref, bga_ref, bgb_ref, x_ref, g1_ref, gn_ref,
                  wa_ref, wb_ref, wo_ref, o_ref, ya_ref):
    oa = of_ref[...].astype(F32) + ob_ref[...].astype(F32)
    z = z_ref[...].astype(F32)
    gate = z * _sigmoid(z)
    for h in range(N_HEADS):
        cols = slice(h * HEAD_W, (h + 1) * HEAD_W)
        oh = oa[:, cols]
        ms = jnp.mean(oh * oh, axis=-1, keepdims=True)
        ya_ref[:, cols] = (oh * lax.rsqrt(ms + EPS) * gn_ref[...] * gate[:, cols]).astype(BF16)
    y_a = _dot(ya_ref[...], wa_ref[...])
    y_b = _dot(oB_ref[...], wb_ref[...])
    gate_a = _sigmoid(ga_ref[...].astype(F32) + bga_ref[...])
    gate_b = _sigmoid(gb_ref[...].astype(F32) + bgb_ref[...])
    mix = _dot((gate_a * y_a + gate_b * y_b).astype(BF16), wo_ref[...])
    o_ref[...] = x_ref[...] + g1_ref[0] * mix


def _merge(oAf, oAb, proj, oB, b_gate, x2, gate1, gn, wa, wb, wo, S):
    N = x2.shape[0]
    TM = min(512, S)
    tpb = S // TM
    row = lambda i: (i, 0)
    full = pl.BlockSpec((D_MODEL, D_MODEL), lambda i: (0, 0))
    return pl.pallas_call(
        _merge_kernel,
        grid=(N // TM,),
        in_specs=[pl.BlockSpec((TM, D_MODEL), row),
                  pl.BlockSpec((TM, D_MODEL), row),
                  pl.BlockSpec((TM, D_MODEL), lambda i: (i, COL_ZA)),
                  pl.BlockSpec((TM, D_MODEL), row),
                  pl.BlockSpec((TM, D_MODEL), lambda i: (i, COL_GA)),
                  pl.BlockSpec((TM, D_MODEL), lambda i: (i, COL_GB)),
                  pl.BlockSpec((1, D_MODEL), lambda i: (0, 0)),
                  pl.BlockSpec((1, D_MODEL), lambda i: (0, 1)),
                  pl.BlockSpec((TM, D_MODEL), row),
                  pl.BlockSpec((1, 1, D_MODEL), lambda i: (i // tpb, 0, 0)),
                  pl.BlockSpec((1, HEAD_W), lambda i: (0, 0)),
                  full, full, full],
        out_specs=pl.BlockSpec((TM, D_MODEL), row),
        out_shape=jax.ShapeDtypeStruct((N, D_MODEL), F32),
        scratch_shapes=[pltpu.VMEM((TM, D_MODEL), BF16)],
        compiler_params=_params(("parallel",)),
        name="merge",
    )(oAf, oAb, proj, oB, proj, proj, b_gate, b_gate, x2, gate1, gn, wa, wb, wo)


def _router_kernel(x_ref, sh_ref, sc_ref, g_ref, rw0_ref, rw1_ref, rb_ref, tri_ref,
                   h_ref, idx_ref, gate_ref, rank_ref, cnt_ref, base_ref):
    i = pl.program_id(0)

    @pl.when(i == 0)
    def _():
        base_ref[...] = jnp.zeros_like(base_ref)

    x = x_ref[...]
    ms = jnp.mean(x * x, axis=-1, keepdims=True)
    h = x * lax.rsqrt(ms + EPS) * g_ref[...] * (1.0 + sc_ref[0]) + sh_ref[0]
    h_ref[...] = h
    h0 = h.astype(BF16)
    h1 = (h - h0.astype(F32)).astype(BF16)
    logits = _dot(h0, rw0_ref[...]) + (_dot(h0, rw1_ref[...]) + _dot(h1, rw0_ref[...])) + rb_ref[...]
    lane = lax.broadcasted_iota(jnp.int32, logits.shape, 1)
    lane_f = lane.astype(F32)
    cur = jnp.where(lane < N_EXPERTS, logits, NEG_INF)
    vals, sel = [], []
    for _ in range(TOP_K):
        m = jnp.max(cur, axis=1, keepdims=True)
        ix = jnp.min(jnp.where(cur == m, lane_f, float(LANES)), axis=1, keepdims=True)
        hit = lane_f == ix
        vals.append(m)
        sel.append(hit)
        cur = jnp.where(hit, NEG_INF, cur)
    exps = [jnp.exp(v - vals[0]) for v in vals]
    den = exps[0] + exps[1] + exps[2] + exps[3]
    onehot = jnp.zeros(logits.shape, F32)
    for hit in sel:
        onehot = onehot + jnp.where(hit, 1.0, 0.0)
    before = _dot(tri_ref[...], onehot.astype(BF16)) + base_ref[...]
    idx_out = jnp.zeros(logits.shape, F32)
    gate_out = jnp.zeros(logits.shape, F32)
    rank_out = jnp.zeros(logits.shape, F32)
    for kk in range(TOP_K):
        slot = lane == kk
        e_id = jnp.sum(jnp.where(sel[kk], lane_f, 0.0), axis=1, keepdims=True)
        rk = jnp.sum(jnp.where(sel[kk], before, 0.0), axis=1, keepdims=True)
        idx_out = jnp.where(slot, e_id, idx_out)
        gate_out = jnp.where(slot, exps[kk] / den, gate_out)
        rank_out = jnp.where(slot, rk, rank_out)
    idx_ref[...] = idx_out.astype(jnp.int32)
    gate_ref[...] = gate_out
    rank_ref[...] = rank_out.astype(jnp.int32)
    base_ref[...] = base_ref[...] + jnp.sum(onehot, axis=0, keepdims=True)
    cnt_ref[...] = base_ref[...]


def _router(x1, shift, scale, g, rw0, rw1, rb, S):
    N = x1.shape[0]
    TM = min(512, S)
    tpb = S // TM
    r = jnp.arange(TM)
    tri = (r[None, :] < r[:, None]).astype(BF16)
    row = lambda i: (i, 0)
    const = lambda i: (0, 0)
    lanes = pl.BlockSpec((TM, LANES), row)
    return pl.pallas_call(
        _router_kernel,
        grid=(N // TM,),
        in_specs=[pl.BlockSpec((TM, D_MODEL), row),
                  pl.BlockSpec((1, 1, D_MODEL), lambda i: (i // tpb, 0, 0)),
                  pl.BlockSpec((1, 1, D_MODEL), lambda i: (i // tpb, 0, 0)),
                  pl.BlockSpec((1, D_MODEL), const),
                  pl.BlockSpec((D_MODEL, LANES), const),
                  pl.BlockSpec((D_MODEL, LANES), const),
                  pl.BlockSpec((1, LANES), const),
                  pl.BlockSpec((TM, TM), const)],
        out_specs=[pl.BlockSpec((TM, D_MODEL), row), lanes, lanes, lanes, pl.BlockSpec((1, LANES), const)],
        out_shape=[jax.ShapeDtypeStruct((N, D_MODEL), F32),
                   jax.ShapeDtypeStruct((N, LANES), jnp.int32),
                   jax.ShapeDtypeStruct((N, LANES), F32),
                   jax.ShapeDtypeStruct((N, LANES), jnp.int32),
                   jax.ShapeDtypeStruct((1, LANES), F32)],
        scratch_shapes=[pltpu.VMEM((1, LANES), F32)],
        compiler_params=_params(("arbitrary",)),
        name="router",
    )(x1, shift, scale, g, rw0, rw1, rb, tri)


INDEX_SLICE = ROW_MOVE_TILE * TOP_K


def _row_copy_out(h_ref, xs_hbm, sem, r, dst):
    return pltpu.make_async_copy(h_ref.at[pl.ds(r, 1)], xs_hbm.at[pl.ds(dst, 1)], sem)


def _zero_tile_copy(zero_ref, xs_hbm, sem, start):
    return pltpu.make_async_copy(zero_ref, xs_hbm.at[pl.ds(pl.multiple_of(start, MOE_TILE), MOE_TILE)], sem)


def _dispatch_kernel(pad_end_ref, padded_ref, dest_hbm, h_ref, xs_hbm, idx_smem, zero_ref, sem_idx, sem_rows, sem_zero):
    i = pl.program_id(0)
    fetch = pltpu.make_async_copy(dest_hbm.at[pl.ds(i * INDEX_SLICE, INDEX_SLICE)], idx_smem, sem_idx)
    fetch.start()

    @pl.when(i == 0)
    def _():
        zero_ref[...] = jnp.zeros_like(zero_ref)
        for e in range(N_EXPERTS):
            @pl.when(padded_ref[e] > 0)
            def _():
                _zero_tile_copy(zero_ref, xs_hbm, sem_zero, pad_end_ref[e] - MOE_TILE).start()
        for e in range(N_EXPERTS):
            @pl.when(padded_ref[e] > 0)
            def _():
                _zero_tile_copy(zero_ref, xs_hbm, sem_zero, 0).wait()

    fetch.wait()

    def start(r, carry):
        for kk in range(TOP_K):
            _row_copy_out(h_ref, xs_hbm, sem_rows, r, idx_smem[r * TOP_K + kk]).start(priority=kk % 2)
        return carry

    lax.fori_loop(0, ROW_MOVE_TILE, start, 0, unroll=4)
    for kk in range(TOP_K):
        pltpu.make_async_copy(h_ref, xs_hbm.at[pl.ds(0, ROW_MOVE_TILE)], sem_rows).wait()


def _dispatch(pad_ends, padded, dest_flat, h2, n_rows):
    N = h2.shape[0]
    return pl.pallas_call(
        _dispatch_kernel,
        grid_spec=pltpu.PrefetchScalarGridSpec(
            num_scalar_prefetch=2,
            grid=(N // ROW_MOVE_TILE,),
            in_specs=[pl.BlockSpec(memory_space=pl.ANY),
                      pl.BlockSpec((ROW_MOVE_TILE, D_MODEL), lambda i, pe, pd: (i, 0))],
            out_specs=pl.BlockSpec(memory_space=pl.ANY),
            scratch_shapes=[pltpu.SMEM((INDEX_SLICE,), jnp.int32), pltpu.VMEM((MOE_TILE, D_MODEL), F32),
                            pltpu.SemaphoreType.DMA, pltpu.SemaphoreType.DMA, pltpu.SemaphoreType.DMA]),
        out_shape=jax.ShapeDtypeStruct((n_rows, D_MODEL), F32),
        compiler_params=_params(("arbitrary",)),
        name="moe_dispatch",
    )(pad_ends, padded, dest_flat, h2)


def _expert_kernel(te_ref, nu_ref, xs_ref, wg_ref, bg_ref, wl_ref, bl_ref, wd_ref, bd_ref, ys_ref):
    del te_ref

    @pl.when(pl.program_id(0) < nu_ref[0])
    def _():
        xb = xs_ref[...].astype(BF16)
        glu = jnp.minimum(_dot(xb, wg_ref[0].astype(BF16)) + bg_ref[0], SWIGLU_LIMIT)
        lin = jnp.clip(_dot(xb, wl_ref[0].astype(BF16)) + bl_ref[0], -SWIGLU_LIMIT, SWIGLU_LIMIT)
        act = glu * _sigmoid(SWIGLU_ALPHA * glu) * (lin + 1.0)
        ys_ref[...] = _dot(act.astype(BF16), wd_ref[0].astype(BF16)) + bd_ref[0]


def _experts(tile_expert, n_used, xs, wg, bg, wl, bl, wd, bd):
    n_rows = xs.shape[0]
    n_tiles = n_rows // MOE_TILE
    wspec = pl.BlockSpec((1, D_MODEL, D_MODEL), lambda i, te, nu: (te[i], 0, 0))
    bspec = pl.BlockSpec((1, 1, D_MODEL), lambda i, te, nu: (te[i], 0, 0))
    rows = pl.BlockSpec((MOE_TILE, D_MODEL), lambda i, te, nu: (jnp.minimum(i, nu[0] - 1), 0))
    return pl.pallas_call(
        _expert_kernel,
        grid_spec=pltpu.PrefetchScalarGridSpec(
            num_scalar_prefetch=2,
            grid=(n_tiles,),
            in_specs=[rows, wspec, bspec, wspec, bspec, wspec, bspec],
            out_specs=rows),
        out_shape=jax.ShapeDtypeStruct((n_rows, D_MODEL), F32),
        compiler_params=_params(("arbitrary",)),
        name="moe_experts",
    )(tile_expert, n_used, xs, wg, bg, wl, bl, wd, bd)


def _row_copy_in(ys_hbm, buf_ref, sem, src, kk, r):
    return pltpu.make_async_copy(ys_hbm.at[pl.ds(src, 1)], buf_ref.at[kk, pl.ds(r, 1)], sem)


def _combine_kernel(dest_hbm, ys_hbm, gate_ref, x_ref, g2_ref, fg_ref, o_ref, idx_smem, buf_ref, sem_idx, sem_rows):
    i = pl.program_id(0)
    fetch = pltpu.make_async_copy(dest_hbm.at[pl.ds(i * INDEX_SLICE, INDEX_SLICE)], idx_smem, sem_idx)
    fetch.start()
    fetch.wait()

    def start(r, carry):
        for kk in range(TOP_K):
            _row_copy_in(ys_hbm, buf_ref, sem_rows, idx_smem[r * TOP_K + kk], kk, r).start(priority=kk % 2)
        return carry

    lax.fori_loop(0, ROW_MOVE_TILE, start, 0, unroll=4)
    for kk in range(TOP_K):
        pltpu.make_async_copy(ys_hbm.at[pl.ds(0, ROW_MOVE_TILE)], buf_ref.at[kk], sem_rows).wait()

    gates = gate_ref[...]
    moe = gates[:, 0:1] * buf_ref[0]
    for kk in range(1, TOP_K):
        moe = moe + gates[:, kk:kk + 1] * buf_ref[kk]
    x = x_ref[...] + g2_ref[0] * moe
    ms = jnp.mean(x * x, axis=-1, keepdims=True)
    o_ref[...] = x * lax.rsqrt(ms + EPS) * fg_ref[...]


def _combine(dest_flat, ys, gates, x1, gate2, final_g, S):
    N = x1.shape[0]
    TM = ROW_MOVE_TILE
    tpb = S // TM
    row = lambda i: (i, 0)
    return pl.pallas_call(
        _combine_kernel,
        grid=(N // TM,),
        in_specs=[pl.BlockSpec(memory_space=pl.ANY),
                  pl.BlockSpec(memory_space=pl.ANY),
                  pl.BlockSpec((TM, LANES), row),
                  pl.BlockSpec((TM, D_MODEL), row),
                  pl.BlockSpec((1, 1, D_MODEL), lambda i: (i // tpb, 0, 0)),
                  pl.BlockSpec((1, D_MODEL), lambda i: (0, 0))],
        out_specs=pl.BlockSpec((TM, D_MODEL), row),
        out_shape=jax.ShapeDtypeStruct((N, D_MODEL), F32),
        scratch_shapes=[pltpu.SMEM((INDEX_SLICE,), jnp.int32), pltpu.VMEM((TOP_K, TM, D_MODEL), F32),
                        pltpu.SemaphoreType.DMA, pltpu.SemaphoreType.DMA],
        compiler_params=_params(("arbitrary",)),
        name="moe_combine",
    )(dest_flat, ys, gates, x1, gate2, final_g)


def _pad_lanes(a, offset=0):
    return jnp.pad(a, ((0, 0), (offset, LANES - offset - a.shape[1])))


def kernel(x, c, ada_w, ada_b, norm1_g, norm2_g, w_in, b_gate, conv_w, a_log, dt_bias, gdn_norm_g, w_branch_a,
           diff_lambda, diff_norm_g, w_branch_b, w_out, router_w, router_b, w_glu, b_glu, w_lin, b_lin, w_down,
           b_down, final_g):
    B, S, D = x.shape
    N = B * S
    assert D == D_MODEL and S % GROUP == 0 and ada_w.shape[0] == 1
    x2 = x.reshape(N, D)

    mod = _adaln(c, ada_w[0], ada_b[0])
    shift1, scale1, gate1, shift2, scale2, gate2 = [m.reshape(B, 1, D) for m in jnp.split(mod, 6, axis=-1)]

    wi = w_in[0]
    n_a = 4 * D
    n_small = 4 * N_HEADS
    w_big = jnp.concatenate([wi[:, :n_a], wi[:, n_a + n_small:]], axis=1).astype(BF16)
    w_small = _pad_lanes(wi[:, n_a:n_a + n_small]).astype(BF16)
    alog_row = _pad_lanes(a_log[0].reshape(1, -1), GATE_LANE0)
    dt_row = _pad_lanes(dt_bias[0].reshape(1, -1), GATE_LANE0)
    proj, beta, G, Gl, cdl = _inproj(x2, shift1, scale1, norm1_g, w_big, w_small, alog_row, dt_row, S)
    proj3 = proj.reshape(B, S, N_COL_BLOCKS * D)

    qkv = _gdn_conv(proj3, conv_w[0])
    r3 = lambda a: a.reshape(B, S, LANES)
    nC = S // GDN_CHUNK
    GT = jnp.transpose(r3(G)[:, :, GATE_LANE0:GATE_LANE0 + 2 * N_HEADS], (0, 2, 1)).reshape(B, 2 * N_HEADS, 1, S)
    cd = r3(cdl).reshape(B, nC, GDN_CHUNK, LANES)[:, :, 0, GATE_LANE0:GATE_LANE0 + 2 * N_HEADS]
    cd = jnp.transpose(cd.reshape(B, nC, 2, N_HEADS), (0, 2, 3, 1)).reshape(-1)
    u, w, qd, att, kdt = _gdn_prep(qkv, r3(beta), r3(G), r3(Gl), GT)
    oAf, oAb = _gdn_scan(cd, u, w, qd, att, kdt)

    half = DIFF_DH // 2
    inv_freq = ROPE_THETA ** (-jnp.arange(half, dtype=F32) / half)
    ang = jnp.arange(S, dtype=F32)[:, None] * inv_freq[None, :]
    cos_t = jnp.tile(jnp.cos(ang), (1, 4))
    sin_h = jnp.sin(ang)
    sin_t = jnp.tile(jnp.concatenate([-sin_h, sin_h], axis=1), (1, 2))
    qt, kr, vt = _rope(proj3, cos_t, sin_t)
    lam_a = jnp.pad(_pad_lanes(diff_lambda[0][0::2]), ((0, 6), (0, 0)))
    lam_b = jnp.pad(_pad_lanes(diff_lambda[0][1::2]), ((0, 6), (0, 0)))
    oB = _diff_attn(qt, kr, vt, lam_a, lam_b, diff_norm_g)

    x1 = _merge(oAf.reshape(N, D), oAb.reshape(N, D), proj, oB.reshape(N, D), b_gate, x2, gate1, gdn_norm_g,
                w_branch_a[0].astype(BF16), w_branch_b[0].astype(BF16), w_out[0].astype(BF16), S)

    rw = _pad_lanes(router_w[0])
    rw0 = rw.astype(BF16)
    rw1 = (rw - rw0.astype(F32)).astype(BF16)
    h2, idx, gates, rank, counts = _router(x1, shift2, scale2, norm2_g, rw0, rw1, _pad_lanes(router_b), S)
    cnt = counts[0, :N_EXPERTS].astype(jnp.int32)
    padded = (cnt + MOE_TILE - 1) // MOE_TILE * MOE_TILE
    pad_ends = jnp.cumsum(padded)
    pad_starts = pad_ends - padded
    dest = (pad_starts[idx[:, :TOP_K]] + rank[:, :TOP_K]).reshape(-1)
    n_tiles = -(-(N * TOP_K) // MOE_TILE) + N_EXPERTS
    tile_start = jnp.arange(n_tiles, dtype=jnp.int32) * MOE_TILE
    tile_expert = jnp.sum((tile_start[:, None] >= pad_ends[None, :]).astype(jnp.int32), axis=1)
    tile_expert = jnp.minimum(tile_expert, N_EXPERTS - 1)
    n_used = (pad_ends[N_EXPERTS - 1:] // MOE_TILE).astype(jnp.int32)
    xs = _dispatch(pad_ends.astype(jnp.int32), padded, dest, h2, n_tiles * MOE_TILE)
    ys = _experts(tile_expert, n_used, xs, w_glu[0], b_glu[0][:, None, :], w_lin[0], b_lin[0][:, None, :],
                  w_down[0], b_down[0][:, None, :])
    out = _combine(dest, ys, gates, x1, gate2, final_g.reshape(1, D), S)
    return out.reshape(B, S, D)
```

```python
import functools
import math

import jax
import jax.numpy as jnp
from jax import lax
from jax.experimental import pallas as pl
from jax.experimental.pallas import tpu as pltpu

F32 = jnp.float32
BF16 = jnp.bfloat16

D_MODEL = 1024
EPS = 1e-6
N_HEADS = 8
HEAD_W = 128
GDN_CHUNK = 64
CONV_WIDTH = 5
DIFF_DH = 64
ROPE_THETA = 10000.0
LAM_INIT = 0.8 - 0.6 * math.exp(-0.3 * 0)
N_EXPERTS = 32
TOP_K = 4
SWIGLU_ALPHA = 1.702
SWIGLU_LIMIT = 7.0

LANES = 128
GROUP = 256
CHUNKS_PER_GROUP = GROUP // GDN_CHUNK
CHUNK_SHIFT = GDN_CHUNK.bit_length() - 1
MOE_TILE = 512
ROW_MOVE_TILE = 1024
NEG_INF = float("-inf")

COL_QA, COL_KA, COL_VA, COL_ZA, COL_QB, COL_KB, COL_VB, COL_GA, COL_GB = range(9)
N_COL_BLOCKS = 9


def _params(sem, vmem_mb=48):
    return pltpu.CompilerParams(dimension_semantics=sem, vmem_limit_bytes=vmem_mb * 1024 * 1024)


def _dot(a, b):
    return jnp.dot(a, b, preferred_element_type=F32)


def _dot_nt(a, b):
    return lax.dot_general(a, b, (((1,), (1,)), ((), ())), preferred_element_type=F32)


def _sigmoid(x):
    return 1.0 / (1.0 + jnp.exp(-x))


def _split3(x):
    a = x.astype(BF16)
    r = x - a.astype(F32)
    b = r.astype(BF16)
    c = (r - b.astype(F32)).astype(BF16)
    return a, b, c


def _adaln_kernel(c_ref, w_ref, b_ref, o_ref):
    c = c_ref[...]
    cond = c * _sigmoid(c)
    c0, c1, c2 = _split3(cond)
    w0, w1, w2 = _split3(w_ref[...])
    acc = _dot(c0, w0) + (_dot(c0, w1) + _dot(c1, w0)) + (_dot(c0, w2) + _dot(c1, w1) + _dot(c2, w0))
    o_ref[...] = acc + b_ref[...]


def _adaln(c, ada_w, ada_b):
    B = c.shape[0]
    n = ada_w.shape[1] // D_MODEL
    return pl.pallas_call(
        _adaln_kernel,
        grid=(n,),
        in_specs=[pl.BlockSpec((B, D_MODEL), lambda j: (0, 0)),
                  pl.BlockSpec((D_MODEL, D_MODEL), lambda j: (0, j)),
                  pl.BlockSpec((1, D_MODEL), lambda j: (0, j))],
        out_specs=pl.BlockSpec((B, D_MODEL), lambda j: (0, j)),
        out_shape=jax.ShapeDtypeStruct((B, n * D_MODEL), F32),
        compiler_params=_params(("parallel",)),
        name="adaln",
    )(c, ada_w, ada_b.reshape(1, -1))


def _inproj_kernel(x_ref, sh_ref, sc_ref, g_ref, w_ref, ws_ref, alog_ref, dt_ref,
                   o_ref, beta_ref, gcum_ref, glast_ref, cd_ref, h_ref):
    @pl.when(pl.program_id(1) == 0)
    def _():
        x = x_ref[...]
        ms = jnp.mean(x * x, axis=-1, keepdims=True)
        y = x * lax.rsqrt(ms + EPS) * g_ref[...]
        h = (y * (1.0 + sc_ref[0]) + sh_ref[0]).astype(BF16)
        h_ref[...] = h
        _gdn_gates(_dot(h, ws_ref[...]), alog_ref[...], dt_ref[...], beta_ref, gcum_ref, glast_ref, cd_ref)

    col = pl.multiple_of(pl.program_id(1) * D_MODEL, D_MODEL)
    o_ref[...] = _dot(h_ref[...], w_ref[:, pl.ds(col, D_MODEL)]).astype(o_ref.dtype)


def _inproj(x2, shift, scale, g, w_big, w_small, alog_row, dt_row, S):
    N = x2.shape[0]
    TM = min(1024, S)
    tpb = S // TM
    const = lambda i, j: (0, 0)
    lanes = pl.BlockSpec((TM, LANES), lambda i, j: (i, 0))
    return pl.pallas_call(
        _inproj_kernel,
        grid=(N // TM, N_COL_BLOCKS),
        in_specs=[pl.BlockSpec((TM, D_MODEL), lambda i, j: (i, 0)),
                  pl.BlockSpec((1, 1, D_MODEL), lambda i, j: (i // tpb, 0, 0)),
                  pl.BlockSpec((1, 1, D_MODEL), lambda i, j: (i // tpb, 0, 0)),
                  pl.BlockSpec((1, D_MODEL), const),
                  pl.BlockSpec((D_MODEL, N_COL_BLOCKS * D_MODEL), const, pipeline_mode=pl.Buffered(1)),
                  pl.BlockSpec((D_MODEL, LANES), const),
                  pl.BlockSpec((1, LANES), const),
                  pl.BlockSpec((1, LANES), const)],
        out_specs=[pl.BlockSpec((TM, D_MODEL), lambda i, j: (i, j)), lanes, lanes, lanes, lanes],
        out_shape=[jax.ShapeDtypeStruct((N, N_COL_BLOCKS * D_MODEL), BF16)]
                  + [jax.ShapeDtypeStruct((N, LANES), F32)] * 4,
        scratch_shapes=[pltpu.VMEM((TM, D_MODEL), BF16)],
        compiler_params=_params(("parallel", "arbitrary")),
        name="inproj",
    )(x2, shift, scale, g, w_big, w_small, alog_row, dt_row)


HALO = 16


CONV_BLOCK = 256
CONV_PAD = (CONV_WIDTH - 1) // 2
CONV_TAPS = tuple(j for j in range(CONV_WIDTH) if j != CONV_PAD)


def _conv_shifts():
    r = jnp.arange(CONV_BLOCK)[:, None]
    c = jnp.arange(CONV_BLOCK)[None, :]
    return jnp.stack([c == r + (j - CONV_PAD) for j in CONV_TAPS]).astype(BF16)


def _conv_kernel(cur_ref, prev_ref, next_ref, w_ref, shift_ref, o_ref, ext_ref, *, TR):
    i = pl.program_id(1)
    g = pl.program_id(2)
    last = pl.num_programs(1) - 1
    ext_ref[8:8 + TR, :] = cur_ref[0].astype(F32)
    pv = prev_ref[0].astype(F32)[HALO - 8:HALO]
    nx = next_ref[0].astype(F32)[0:8]
    ext_ref[0:8, :] = jnp.where(i > 0, pv, 0.0)
    ext_ref[TR + 8:TR + 16, :] = jnp.where(i < last, nx, 0.0)
    ones = jnp.ones((HEAD_W, HEAD_W), BF16)
    qscale = jnp.where(g == 0, HEAD_W ** -0.5, 1.0)

    def edge_rows(row0):
        e = ext_ref[8 + row0 - CONV_PAD:16 + row0 - CONV_PAD, :] * w_ref[0:1, :]
        for j in range(1, CONV_WIDTH):
            e = e + ext_ref[8 + row0 - CONV_PAD + j:16 + row0 - CONV_PAD + j, :] * w_ref[j:j + 1, :]
        return e

    for blk in range(TR // CONV_BLOCK):
        r0 = blk * CONV_BLOCK
        rows = slice(r0, r0 + CONV_BLOCK)
        ub = cur_ref[0, rows, :]
        acc = ext_ref[8 + r0:8 + r0 + CONV_BLOCK, :] * w_ref[CONV_PAD:CONV_PAD + 1, :]
        for si, j in enumerate(CONV_TAPS):
            acc = acc + _dot(shift_ref[si], ub) * w_ref[j:j + 1, :]
        acc = jnp.concatenate([edge_rows(r0), acc[8:CONV_BLOCK - 8], edge_rows(r0 + CONV_BLOCK - 8)], axis=0)
        y = acc * _sigmoid(acc)

        for h in range(N_HEADS):
            cols = slice(h * HEAD_W, (h + 1) * HEAD_W)
            yh = y[:, cols]
            ss = _dot((yh * yh).astype(BF16), ones)
            scale = jnp.where(g < 2, lax.rsqrt(ss + EPS) * qscale, 1.0)
            o_ref[0, rows, cols] = (yh * scale).astype(o_ref.dtype)


def _gdn_conv(proj3, conv_w):
    B, S, _ = proj3.shape
    TR = min(512, S)
    nT = S // TR
    rb = TR // HALO
    nH = S // HALO
    return pl.pallas_call(
        functools.partial(_conv_kernel, TR=TR),
        grid=(B, nT, 3),
        in_specs=[pl.BlockSpec((1, TR, D_MODEL), lambda b, i, g: (b, i, g)),
                  pl.BlockSpec((1, HALO, D_MODEL), lambda b, i, g: (b, jnp.maximum(i * rb - 1, 0), g)),
                  pl.BlockSpec((1, HALO, D_MODEL), lambda b, i, g: (b, jnp.minimum((i + 1) * rb, nH - 1), g)),
                  pl.BlockSpec((CONV_WIDTH, D_MODEL), lambda b, i, g: (0, g)),
                  pl.BlockSpec((len(CONV_TAPS), CONV_BLOCK, CONV_BLOCK), lambda b, i, g: (0, 0, 0))],
        out_specs=pl.BlockSpec((1, TR, D_MODEL), lambda b, i, g: (b, i, g)),
        out_shape=jax.ShapeDtypeStruct((B, S, 3 * D_MODEL), BF16),
        scratch_shapes=[pltpu.VMEM((TR + 16, D_MODEL), F32)],
        compiler_params=_params(("parallel", "parallel", "parallel")),
        name="gdn_conv",
    )(proj3, proj3, proj3, conv_w, _conv_shifts())


GATE_LANE0 = 16


def _gdn_gates(x, alog, dt, beta_ref, g_ref, gl_ref, cd_ref):
    lane = lax.broadcasted_iota(jnp.int32, (GROUP, LANES), 1)
    r = lax.broadcasted_iota(jnp.int32, (GROUP, GROUP), 0)
    c = lax.broadcasted_iota(jnp.int32, (GROUP, GROUP), 1)
    same = (r >> CHUNK_SHIFT) == (c >> CHUNK_SHIFT)
    lower = jnp.where(same & (c <= r), 1.0, 0.0).astype(BF16)
    upper = jnp.where(same & (c >= r), 1.0, 0.0).astype(BF16)
    block = jnp.where(same, 1.0, 0.0).astype(BF16)
    beta_ref[...] = _sigmoid(x)
    for gi in range(x.shape[0] // GROUP):
        rows = slice(gi * GROUP, (gi + 1) * GROUP)
        z = x[rows] + dt
        softplus = jnp.maximum(z, 0.0) + jnp.log(1.0 + jnp.exp(-jnp.abs(z)))
        gd = -jnp.exp(alog) * softplus
        gd = jnp.where((lane >= GATE_LANE0) & (lane < GATE_LANE0 + 2 * N_HEADS), gd, 0.0)
        p0, p1, p2 = _split3(gd)
        g_fwd = _dot(lower, p0) + _dot(lower, p1) + _dot(lower, p2)
        g_bwd = _dot(upper, p0) + _dot(upper, p1) + _dot(upper, p2)
        tot = _dot(block, p0) + _dot(block, p1) + _dot(block, p2)
        G = jnp.where(lane < GATE_LANE0 + N_HEADS, g_fwd, g_bwd)
        g_ref[rows, :] = G
        gl_ref[rows, :] = tot - G
        cd_ref[rows, :] = jnp.exp(tot)


PREP_HEADS = 4
(MASK_STRICT_LO, MASK_STRICT_UP, MASK_INCL_LO, MASK_INCL_UP, MASK_EYE, MASK_BLK4, MASK_OFF0) = range(7)
N_MASKS = MASK_OFF0 + (CHUNK_SHIFT - 2)


def _prep_masks():
    r = jnp.arange(GROUP)[:, None]
    c = jnp.arange(GROUP)[None, :]
    same = (r >> CHUNK_SHIFT) == (c >> CHUNK_SHIFT)
    masks = [same & (c < r), same & (c > r), same & (c <= r), same & (c >= r), r == c, (r >> 2) == (c >> 2)]
    for shift in range(2, CHUNK_SHIFT):
        masks.append(((r >> shift) != (c >> shift)) & ((r >> (shift + 1)) == (c >> (shift + 1))))
    return jnp.stack(masks).astype(F32)


def _col(x, l, lane):
    return jnp.broadcast_to(jnp.sum(jnp.where(lane == l, x, 0.0), axis=1, keepdims=True), x.shape)


def _prep_kernel(q_ref, k_ref, v_ref, beta_ref, g_ref, gl_ref, gtf_ref, gtb_ref, mask_ref, bmask_ref,
                 u_ref, w_ref, qd_ref, at_ref, kdt_ref):
    hp = pl.program_id(1)
    lane = lax.broadcasted_iota(jnp.int32, (GROUP, LANES), 1)
    wide = lambda a: jnp.concatenate([a, a], axis=1)
    chains = [(hh, d) for hh in range(PREP_HEADS) for d in range(2)]
    p, rhs = {}, {}
    for hh in range(PREP_HEADS):
        cols = slice(hh * HEAD_W, (hh + 1) * HEAD_W)
        q = q_ref[0, :, cols]
        k = k_ref[0, :, cols]
        qf = q.astype(F32)
        kf = k.astype(F32)
        vf = v_ref[0, :, cols].astype(F32)
        kk = _dot_nt(k, k)
        qk = _dot_nt(q, k)
        for d in range(2):
            lb = d * N_HEADS + hp * PREP_HEADS + hh
            beta_c = _col(beta_ref[0], lb, lane)
            g_c = _col(g_ref[0], GATE_LANE0 + lb, lane)
            eg_c = jnp.exp(g_c)
            egl_c = jnp.exp(_col(gl_ref[0], GATE_LANE0 + lb, lane))
            g_r = (gtf_ref if d == 0 else gtb_ref)[0, hh]
            dec = jnp.exp(jnp.minimum(wide(g_c) - g_r, 0.0))
            p[hh, d] = (-(kk * wide(beta_c)) * dec * mask_ref[MASK_STRICT_LO + d]).astype(BF16)
            att = qk * dec * mask_ref[MASK_INCL_LO + d]
            rhs[hh, d] = jnp.concatenate([vf * beta_c, kf * (beta_c * eg_c)], axis=1).astype(BF16)
            qd_ref[0, d, :, cols] = (qf * eg_c).astype(qd_ref.dtype)
            kdt = (kf * egl_c).T
            for ci in range(CHUNKS_PER_GROUP):
                sl = slice(ci * GDN_CHUNK, (ci + 1) * GDN_CHUNK)
                at_ref[0, d, hh, ci] = att[sl, sl].astype(at_ref.dtype)
                kdt_ref[0, d, hh, ci] = kdt[:, sl].astype(kdt_ref.dtype)
    p4 = {ch: p[ch] * bmask_ref[0] for ch in chains}
    sq = {ch: _dot(p4[ch], p4[ch]).astype(BF16) for ch in chains}
    tb = {ch: bmask_ref[N_MASKS - MASK_BLK4] + p4[ch] for ch in chains}
    tb = {ch: (tb[ch].astype(F32) + _dot(tb[ch], sq[ch])).astype(BF16) for ch in chains}
    for lvl in range(CHUNK_SHIFT - 2):
        x = {ch: _dot(tb[ch], p[ch] * bmask_ref[1 + lvl]).astype(BF16) for ch in chains}
        tb = {ch: (tb[ch].astype(F32) + _dot(x[ch], tb[ch])).astype(BF16) for ch in chains}
    uw = {ch: _dot(tb[ch], rhs[ch]) for ch in chains}
    for hh, d in chains:
        cols = slice(hh * HEAD_W, (hh + 1) * HEAD_W)
        u_ref[0, d, :, cols] = uw[hh, d][:, :HEAD_W].astype(u_ref.dtype)
        w_ref[0, d, :, cols] = uw[hh, d][:, HEAD_W:].astype(w_ref.dtype)


def _gdn_prep(qkv, beta, G, Gl, GT):
    B, S, _ = qkv.shape
    nG = S // GROUP
    nC = S // GDN_CHUNK
    PW = PREP_HEADS * HEAD_W
    nP = N_HEADS // PREP_HEADS
    sm = pl.BlockSpec((1, GROUP, LANES), lambda b, h, g: (b, g, 0))
    big = pl.BlockSpec((1, 2, GROUP, PW), lambda b, h, g: (b, 0, g, h))
    masks = _prep_masks()
    return pl.pallas_call(
        _prep_kernel,
        grid=(B, nP, nG),
        in_specs=[pl.BlockSpec((1, GROUP, PW), lambda b, h, g: (b, g, h)),
                  pl.BlockSpec((1, GROUP, PW), lambda b, h, g: (b, g, nP + h)),
                  pl.BlockSpec((1, GROUP, PW), lambda b, h, g: (b, g, 2 * nP + h)),
                  sm, sm, sm,
                  pl.BlockSpec((1, PREP_HEADS, 1, GROUP), lambda b, h, g: (b, h, 0, g)),
                  pl.BlockSpec((1, PREP_HEADS, 1, GROUP), lambda b, h, g: (b, nP + h, 0, g)),
                  pl.BlockSpec((MASK_EYE, GROUP, GROUP), lambda b, h, g: (0, 0, 0)),
                  pl.BlockSpec((N_MASKS - MASK_BLK4 + 1, GROUP, GROUP), lambda b, h, g: (0, 0, 0))],
        out_specs=[big, big, big,
                   pl.BlockSpec((1, 2, PREP_HEADS, CHUNKS_PER_GROUP, GDN_CHUNK, GDN_CHUNK),
                                lambda b, h, g: (b, 0, h, g, 0, 0)),
                   pl.BlockSpec((1, 2, PREP_HEADS, CHUNKS_PER_GROUP, HEAD_W, GDN_CHUNK),
                                lambda b, h, g: (b, 0, h, g, 0, 0))],
        out_shape=[jax.ShapeDtypeStruct((B, 2, S, D_MODEL), BF16)] * 3
                  + [jax.ShapeDtypeStruct((B, 2, N_HEADS, nC, GDN_CHUNK, GDN_CHUNK), BF16),
                     jax.ShapeDtypeStruct((B, 2, N_HEADS, nC, HEAD_W, GDN_CHUNK), BF16)],
        compiler_params=_params(("parallel", "parallel", "parallel")),
        name="gdn_prep",
    )(qkv, qkv, qkv, beta, G, Gl, GT, GT, masks[:MASK_EYE],
      jnp.concatenate([masks[MASK_BLK4:], masks[MASK_EYE:MASK_EYE + 1]]).astype(BF16))


def _scan_kernel(cd_ref, uf_ref, wf_ref, qdf_ref, atf_ref, kdtf_ref, ub_ref, wb_ref, qdb_ref, atb_ref, kdtb_ref,
                 of_ref, ob_ref, state_ref, *, nc, nC):
    b = pl.program_id(0)
    t = pl.program_id(1)
    nT = pl.num_programs(1)

    @pl.when(t == 0)
    def _():
        state_ref[...] = jnp.zeros_like(state_ref)

    dirs = ((uf_ref, wf_ref, qdf_ref, atf_ref, kdtf_ref, of_ref), (ub_ref, wb_ref, qdb_ref, atb_ref, kdtb_ref, ob_ref))

    def chunk(ci, carry):
        work = []
        for d, refs in enumerate(dirs):
            c = ci if d == 0 else nc - 1 - ci
            tt = t if d == 0 else nT - 1 - t
            row = pl.multiple_of(c * GDN_CHUNK, GDN_CHUNK)
            for h in range(N_HEADS):
                work.append((d, h, c, row, ((b * 2 + d) * N_HEADS + h) * nC + tt * nc + c, refs))
        s_old = [state_ref[d, h] for d, h, *_ in work]
        sb = [s.astype(BF16) for s in s_old]
        tile = lambda ref, row, h: ref[0, 0, pl.ds(row, GDN_CHUNK), h * HEAD_W:(h + 1) * HEAD_W]
        ws = [_dot(tile(refs[1], row, h), sb[i]) for i, (d, h, c, row, gi, refs) in enumerate(work)]
        qs = [_dot(tile(refs[2], row, h), sb[i]) for i, (d, h, c, row, gi, refs) in enumerate(work)]
        vb = [(tile(refs[0], row, h).astype(F32) - ws[i]).astype(BF16)
              for i, (d, h, c, row, gi, refs) in enumerate(work)]
        o = [qs[i] + _dot(refs[3][0, 0, h, c], vb[i]) for i, (d, h, c, row, gi, refs) in enumerate(work)]
        upd = [_dot(refs[4][0, 0, h, c], vb[i]) for i, (d, h, c, row, gi, refs) in enumerate(work)]
        for i, (d, h, c, row, gi, refs) in enumerate(work):
            state_ref[d, h] = s_old[i] * cd_ref[gi] + upd[i]
            refs[5][0, pl.ds(row, GDN_CHUNK), h * HEAD_W:(h + 1) * HEAD_W] = o[i].astype(of_ref.dtype)
        return carry

    lax.fori_loop(0, nc, chunk, 0)


def _gdn_scan(cd, u, w, qd, att, kdt):
    B, _, S, _ = u.shape
    TC = min(512, S)
    nT = S // TC
    nc = TC // GDN_CHUNK
    nC = S // GDN_CHUNK
    fwd = lambda b, t: t
    bwd = lambda b, t: nT - 1 - t

    def specs(d, tm):
        big = pl.BlockSpec((1, 1, TC, D_MODEL), lambda b, t: (b, d, tm(b, t), 0))
        return [big, big, big,
                pl.BlockSpec((1, 1, N_HEADS, nc, GDN_CHUNK, GDN_CHUNK), lambda b, t: (b, d, 0, tm(b, t), 0, 0)),
                pl.BlockSpec((1, 1, N_HEADS, nc, HEAD_W, GDN_CHUNK), lambda b, t: (b, d, 0, tm(b, t), 0, 0))]

    return pl.pallas_call(
        functools.partial(_scan_kernel, nc=nc, nC=nC),
        grid=(B, nT),
        in_specs=[pl.BlockSpec(memory_space=pltpu.SMEM)] + specs(0, fwd) + specs(1, bwd),
        out_specs=[pl.BlockSpec((1, TC, D_MODEL), lambda b, t: (b, t, 0)),
                   pl.BlockSpec((1, TC, D_MODEL), lambda b, t: (b, nT - 1 - t, 0))],
        out_shape=[jax.ShapeDtypeStruct((B, S, D_MODEL), BF16)] * 2,
        scratch_shapes=[pltpu.VMEM((2, N_HEADS, HEAD_W, HEAD_W), F32)],
        compiler_params=_params(("parallel", "arbitrary")),
        name="gdn_scan",
    )(cd, u, w, qd, att, kdt, u, w, qd, att, kdt)


ATT_TQ = 256
ATT_TK = 256
ATT_NQ = 4
ATT_VROWS = HEAD_W + 16
LOG2E = 1.4426950408889634


def _rope_kernel(q_ref, k_ref, v_ref, cos_ref, sin_ref, qt_ref, kr_ref, vt_ref, *, TR):
    cs = cos_ref[...]
    sn = sin_ref[...]
    lane = lax.broadcasted_iota(jnp.int32, cs.shape, 1)
    first_half = (lane & (DIFF_DH - 1)) < (DIFF_DH // 2)
    qscale = DIFF_DH ** -0.5 * LOG2E

    def rot(x):
        partner = jnp.where(first_half, pltpu.roll(x, HEAD_W - DIFF_DH // 2, 1), pltpu.roll(x, DIFF_DH // 2, 1))
        return x * cs + partner * sn

    for h in range(N_HEADS):
        cols = slice(h * HEAD_W, (h + 1) * HEAD_W)
        qr = rot(q_ref[0, :, cols].astype(F32)) * qscale
        kr_ref[0, :, cols] = rot(k_ref[0, :, cols].astype(F32)).astype(kr_ref.dtype)
        vf = v_ref[0, :, cols].astype(F32)
        for ci in range(TR // ATT_TK):
            rows = slice(ci * ATT_TK, (ci + 1) * ATT_TK)
            vt_ref[0, h, ci, 0:HEAD_W, :] = vf[rows].T.astype(vt_ref.dtype)
            vt_ref[0, h, ci, HEAD_W:ATT_VROWS, :] = jnp.ones((ATT_VROWS - HEAD_W, ATT_TK), vt_ref.dtype)
        for ci in range(TR // ATT_TQ):
            rows = slice(ci * ATT_TQ, (ci + 1) * ATT_TQ)
            qt_ref[0, h, ci] = qr[rows].T.astype(qt_ref.dtype)


def _rope(proj3, cos_t, sin_t):
    B, S, _ = proj3.shape
    TR = min(512, S)
    tab = pl.BlockSpec((TR, HEAD_W), lambda b, i: (i, 0))
    col = lambda cb: pl.BlockSpec((1, TR, D_MODEL), lambda b, i: (b, i, cb))
    return pl.pallas_call(
        functools.partial(_rope_kernel, TR=TR),
        grid=(B, S // TR),
        in_specs=[col(COL_QB), col(COL_KB), col(COL_VB), tab, tab],
        out_specs=[pl.BlockSpec((1, N_HEADS, TR // ATT_TQ, HEAD_W, ATT_TQ), lambda b, i: (b, 0, i, 0, 0)),
                   pl.BlockSpec((1, TR, D_MODEL), lambda b, i: (b, i, 0)),
                   pl.BlockSpec((1, N_HEADS, TR // ATT_TK, ATT_VROWS, ATT_TK), lambda b, i: (b, 0, i, 0, 0))],
        out_shape=[jax.ShapeDtypeStruct((B, N_HEADS, S // ATT_TQ, HEAD_W, ATT_TQ), BF16),
                   jax.ShapeDtypeStruct((B, S, D_MODEL), BF16),
                   jax.ShapeDtypeStruct((B, N_HEADS, S // ATT_TK, ATT_VROWS, ATT_TK), BF16)],
        compiler_params=_params(("parallel", "parallel")),
        name="rope",
    )(proj3, proj3, proj3, cos_t, sin_t)


def _attn_kernel(qt_ref, k_ref, vt_ref, la_ref, lb_ref, g_ref, o_ref, s_ref, acc_ref, *, n_chunks):
    row = lax.broadcasted_iota(jnp.int32, (HEAD_W, ATT_TQ), 0)
    qw = []
    for qb in range(ATT_NQ):
        qt = qt_ref[0, 0, qb]
        zero = jnp.zeros_like(qt)
        qw.append((jnp.where(row < DIFF_DH, qt, zero), jnp.where(row >= DIFF_DH, qt, zero)))
    chains = [(qb, comp) for qb in range(ATT_NQ) for comp in range(2)]

    def scores(j):
        kc = k_ref[0, pl.ds(pl.multiple_of(j * ATT_TK, ATT_TK), ATT_TK), :]
        return [_dot(kc, qw[qb][comp]) for qb, comp in chains]

    acc_ref[...] = jnp.zeros_like(acc_ref)
    for (qb, comp), s0 in zip(chains, scores(0)):
        s_ref[qb, comp] = s0

    def chunk(j, carry):
        s_next = scores(jnp.minimum(j + 1, n_chunks - 1))
        vt = vt_ref[0, 0, j]
        out = []
        for ci, (qb, comp) in enumerate(chains):
            m_prev = carry[ci]
            s = s_ref[qb, comp]
            m_new = jnp.maximum(m_prev, jnp.max(s, axis=0, keepdims=True))
            alpha = jnp.exp2(m_prev - m_new)
            p = jnp.exp2(s - m_new)
            out.append(m_new)
            acc_ref[qb, comp] = alpha * acc_ref[qb, comp] + _dot(vt, p.astype(BF16))
        for (qb, comp), sn in zip(chains, s_next):
            s_ref[qb, comp] = sn
        return tuple(out)

    neg = jnp.full((1, ATT_TQ), NEG_INF, F32)
    lax.fori_loop(0, n_chunks, chunk, (neg,) * len(chains), unroll=8)

    sums = jnp.sum(la_ref[...] * lb_ref[...], axis=1, keepdims=True)
    lrow = lax.broadcasted_iota(jnp.int32, sums.shape, 0)
    sign = jnp.where(lrow == 0, 1.0, jnp.where(lrow == 1, -1.0, 0.0))
    lam = jnp.sum(sign * jnp.exp(sums), axis=0, keepdims=True) + LAM_INIT
    for qb in range(ATT_NQ):
        l0 = acc_ref[qb, 0, HEAD_W:HEAD_W + 1, :]
        l1 = acc_ref[qb, 1, HEAD_W:HEAD_W + 1, :]
        ot = acc_ref[qb, 0, 0:HEAD_W, :] / l0 - lam * (acc_ref[qb, 1, 0:HEAD_W, :] / l1)
        ms = jnp.mean(ot * ot, axis=0, keepdims=True)
        y = (ot * lax.rsqrt(ms + EPS)).T * g_ref[...] * (1.0 - LAM_INIT)
        o_ref[0, qb * ATT_TQ:(qb + 1) * ATT_TQ, :] = y.astype(o_ref.dtype)


def _diff_attn(qt, kr, vt, lam_a, lam_b, norm_g):
    B, S, _ = kr.shape
    lam_spec = pl.BlockSpec((8, LANES), lambda b, h, qi: (0, 0))
    return pl.pallas_call(
        functools.partial(_attn_kernel, n_chunks=S // ATT_TK),
        grid=(B, N_HEADS, S // (ATT_NQ * ATT_TQ)),
        in_specs=[pl.BlockSpec((1, 1, ATT_NQ, HEAD_W, ATT_TQ), lambda b, h, qi: (b, h, qi, 0, 0)),
                  pl.BlockSpec((1, S, HEAD_W), lambda b, h, qi: (b, 0, h)),
                  pl.BlockSpec((1, 1, S // ATT_TK, ATT_VROWS, ATT_TK), lambda b, h, qi: (b, h, 0, 0, 0)),
                  lam_spec, lam_spec,
                  pl.BlockSpec((1, HEAD_W), lambda b, h, qi: (0, 0))],
        out_specs=pl.BlockSpec((1, ATT_NQ * ATT_TQ, HEAD_W), lambda b, h, qi: (b, qi, h)),
        out_shape=jax.ShapeDtypeStruct((B, S, D_MODEL), BF16),
        scratch_shapes=[pltpu.VMEM((ATT_NQ, 2, ATT_TK, ATT_TQ), F32), pltpu.VMEM((ATT_NQ, 2, ATT_VROWS, ATT_TQ), F32)],
        compiler_params=_params(("parallel", "parallel", "parallel")),
        name="diff_attn",
    )(qt, kr, vt, lam_a, lam_b, norm_g)


def _merge_kernel(of_ref, ob_ref, z_ref, oB_ref, ga_ref, gb_ref, bga_ref, bgb_ref, x_ref, g1_ref, gn_ref,
                  wa_ref, wb_ref, wo_ref, o_ref, ya_ref):
    oa = of_ref[...].astype(F32) + ob_ref[...].astype(F32)
    z = z_ref[...].astype(F32)
    gate = z * _sigmoid(z)
    for h in range(N_HEADS):
        cols = slice(h * HEAD_W, (h + 1) * HEAD_W)
        oh = oa[:, cols]
        ms = jnp.mean(oh * oh, axis=-1, keepdims=True)
        ya_ref[:, cols] = (oh * lax.rsqrt(ms + EPS) * gn_ref[...] * gate[:, cols]).astype(BF16)
    y_a = _dot(ya_ref[...], wa_ref[...])
    y_b = _dot(oB_ref[...], wb_ref[...])
    gate_a = _sigmoid(ga_ref[...].astype(F32) + bga_ref[...])
    gate_b = _sigmoid(gb_ref[...].astype(F32) + bgb_ref[...])
    mix = _dot((gate_a * y_a + gate_b * y_b).astype(BF16), wo_ref[...])
    o_ref[...] = x_ref[...] + g1_ref[0] * mix


def _merge(oAf, oAb, proj, oB, b_gate, x2, gate1, gn, wa, wb, wo, S):
    N = x2.shape[0]
    TM = min(512, S)
    tpb = S // TM
    row = lambda i: (i, 0)
    full = pl.BlockSpec((D_MODEL, D_MODEL), lambda i: (0, 0))
    return pl.pallas_call(
        _merge_kernel,
        grid=(N // TM,),
        in_specs=[pl.BlockSpec((TM, D_MODEL), row),
                  pl.BlockSpec((TM, D_MODEL), row),
                  pl.BlockSpec((TM, D_MODEL), lambda i: (i, COL_ZA)),
                  pl.BlockSpec((TM, D_MODEL), row),
                  pl.BlockSpec((TM, D_MODEL), lambda i: (i, COL_GA)),
                  pl.BlockSpec((TM, D_MODEL), lambda i: (i, COL_GB)),
                  pl.BlockSpec((1, D_MODEL), lambda i: (0, 0)),
                  pl.BlockSpec((1, D_MODEL), lambda i: (0, 1)),
                  pl.BlockSpec((TM, D_MODEL), row),
                  pl.BlockSpec((1, 1, D_MODEL), lambda i: (i // tpb, 0, 0)),
                  pl.BlockSpec((1, HEAD_W), lambda i: (0, 0)),
                  full, full, full],
        out_specs=pl.BlockSpec((TM, D_MODEL), row),
        out_shape=jax.ShapeDtypeStruct((N, D_MODEL), F32),
        scratch_shapes=[pltpu.VMEM((TM, D_MODEL), BF16)],
        compiler_params=_params(("parallel",)),
        name="merge",
    )(oAf, oAb, proj, oB, proj, proj, b_gate, b_gate, x2, gate1, gn, wa, wb, wo)


def _router_kernel(x_ref, sh_ref, sc_ref, g_ref, rw0_ref, rw1_ref, rb_ref, tri_ref,
                   h_ref, idx_ref, gate_ref, rank_ref, cnt_ref, base_ref):
    i = pl.program_id(0)

    @pl.when(i == 0)
    def _():
        base_ref[...] = jnp.zeros_like(base_ref)

    x = x_ref[...]
    ms = jnp.mean(x * x, axis=-1, keepdims=True)
    h = x * lax.rsqrt(ms + EPS) * g_ref[...] * (1.0 + sc_ref[0]) + sh_ref[0]
    h_ref[...] = h
    h0 = h.astype(BF16)
    h1 = (h - h0.astype(F32)).astype(BF16)
    logits = _dot(h0, rw0_ref[...]) + (_dot(h0, rw1_ref[...]) + _dot(h1, rw0_ref[...])) + rb_ref[...]
    lane = lax.broadcasted_iota(jnp.int32, logits.shape, 1)
    lane_f = lane.astype(F32)
    cur = jnp.where(lane < N_EXPERTS, logits, NEG_INF)
    vals, sel = [], []
    for _ in range(TOP_K):
        m = jnp.max(cur, axis=1, keepdims=True)
        ix = jnp.min(jnp.where(cur == m, lane_f, float(LANES)), axis=1, keepdims=True)
        hit = lane_f == ix
        vals.append(m)
        sel.append(hit)
        cur = jnp.where(hit, NEG_INF, cur)
    exps = [jnp.exp(v - vals[0]) for v in vals]
    den = exps[0] + exps[1] + exps[2] + exps[3]
    onehot = jnp.zeros(logits.shape, F32)
    for hit in sel:
        onehot = onehot + jnp.where(hit, 1.0, 0.0)
    before = _dot(tri_ref[...], onehot.astype(BF16)) + base_ref[...]
    idx_out = jnp.zeros(logits.shape, F32)
    gate_out = jnp.zeros(logits.shape, F32)
    rank_out = jnp.zeros(logits.shape, F32)
    for kk in range(TOP_K):
        slot = lane == kk
        e_id = jnp.sum(jnp.where(sel[kk], lane_f, 0.0), axis=1, keepdims=True)
        rk = jnp.sum(jnp.where(sel[kk], before, 0.0), axis=1, keepdims=True)
        idx_out = jnp.where(slot, e_id, idx_out)
        gate_out = jnp.where(slot, exps[kk] / den, gate_out)
        rank_out = jnp.where(slot, rk, rank_out)
    idx_ref[...] = idx_out.astype(jnp.int32)
    gate_ref[...] = gate_out
    rank_ref[...] = rank_out.astype(jnp.int32)
    base_ref[...] = base_ref[...] + jnp.sum(onehot, axis=0, keepdims=True)
    cnt_ref[...] = base_ref[...]


def _router(x1, shift, scale, g, rw0, rw1, rb, S):
    N = x1.shape[0]
    TM = min(512, S)
    tpb = S // TM
    r = jnp.arange(TM)
    tri = (r[None, :] < r[:, None]).astype(BF16)
    row = lambda i: (i, 0)
    const = lambda i: (0, 0)
    lanes = pl.BlockSpec((TM, LANES), row)
    return pl.pallas_call(
        _router_kernel,
        grid=(N // TM,),
        in_specs=[pl.BlockSpec((TM, D_MODEL), row),
                  pl.BlockSpec((1, 1, D_MODEL), lambda i: (i // tpb, 0, 0)),
                  pl.BlockSpec((1, 1, D_MODEL), lambda i: (i // tpb, 0, 0)),
                  pl.BlockSpec((1, D_MODEL), const),
                  pl.BlockSpec((D_MODEL, LANES), const),
                  pl.BlockSpec((D_MODEL, LANES), const),
                  pl.BlockSpec((1, LANES), const),
                  pl.BlockSpec((TM, TM), const)],
        out_specs=[pl.BlockSpec((TM, D_MODEL), row), lanes, lanes, lanes, pl.BlockSpec((1, LANES), const)],
        out_shape=[jax.ShapeDtypeStruct((N, D_MODEL), F32),
                   jax.ShapeDtypeStruct((N, LANES), jnp.int32),
                   jax.ShapeDtypeStruct((N, LANES), F32),
                   jax.ShapeDtypeStruct((N, LANES), jnp.int32),
                   jax.ShapeDtypeStruct((1, LANES), F32)],
        scratch_shapes=[pltpu.VMEM((1, LANES), F32)],
        compiler_params=_params(("arbitrary",)),
        name="router",
    )(x1, shift, scale, g, rw0, rw1, rb, tri)


INDEX_SLICE = ROW_MOVE_TILE * TOP_K


def _row_copy_out(h_ref, xs_hbm, sem, r, dst):
    return pltpu.make_async_copy(h_ref.at[pl.ds(r, 1)], xs_hbm.at[pl.ds(dst, 1)], sem)


def _zero_tile_copy(zero_ref, xs_hbm, sem, start):
    return pltpu.make_async_copy(zero_ref, xs_hbm.at[pl.ds(pl.multiple_of(start, MOE_TILE), MOE_TILE)], sem)


def _dispatch_kernel(pad_end_ref, padded_ref, dest_hbm, h_ref, xs_hbm, idx_smem, zero_ref, sem_idx, sem_rows, sem_zero):
    i = pl.program_id(0)
    fetch = pltpu.make_async_copy(dest_hbm.at[pl.ds(i * INDEX_SLICE, INDEX_SLICE)], idx_smem, sem_idx)
    fetch.start()

    @pl.when(i == 0)
    def _():
        zero_ref[...] = jnp.zeros_like(zero_ref)
        for e in range(N_EXPERTS):
            @pl.when(padded_ref[e] > 0)
            def _():
                _zero_tile_copy(zero_ref, xs_hbm, sem_zero, pad_end_ref[e] - MOE_TILE).start()
        for e in range(N_EXPERTS):
            @pl.when(padded_ref[e] > 0)
            def _():
                _zero_tile_copy(zero_ref, xs_hbm, sem_zero, 0).wait()

    fetch.wait()

    def start(r, carry):
        for kk in range(TOP_K):
            _row_copy_out(h_ref, xs_hbm, sem_rows, r, idx_smem[r * TOP_K + kk]).start(priority=kk % 2)
        return carry

    lax.fori_loop(0, ROW_MOVE_TILE, start, 0, unroll=8)
    for kk in range(TOP_K):
        pltpu.make_async_copy(h_ref, xs_hbm.at[pl.ds(0, ROW_MOVE_TILE)], sem_rows).wait()


def _dispatch(pad_ends, padded, dest_flat, h2, n_rows):
    N = h2.shape[0]
    return pl.pallas_call(
        _dispatch_kernel,
        grid_spec=pltpu.PrefetchScalarGridSpec(
            num_scalar_prefetch=2,
            grid=(N // ROW_MOVE_TILE,),
            in_specs=[pl.BlockSpec(memory_space=pl.ANY),
                      pl.BlockSpec((ROW_MOVE_TILE, D_MODEL), lambda i, pe, pd: (i, 0))],
            out_specs=pl.BlockSpec(memory_space=pl.ANY),
            scratch_shapes=[pltpu.SMEM((INDEX_SLICE,), jnp.int32), pltpu.VMEM((MOE_TILE, D_MODEL), F32),
                            pltpu.SemaphoreType.DMA, pltpu.SemaphoreType.DMA, pltpu.SemaphoreType.DMA]),
        out_shape=jax.ShapeDtypeStruct((n_rows, D_MODEL), F32),
        compiler_params=_params(("arbitrary",)),
        name="moe_dispatch",
    )(pad_ends, padded, dest_flat, h2)


def _expert_kernel(te_ref, nu_ref, xs_ref, wg_ref, bg_ref, wl_ref, bl_ref, wd_ref, bd_ref, ys_ref):
    del te_ref

    @pl.when(pl.program_id(0) < nu_ref[0])
    def _():
        xb = xs_ref[...].astype(BF16)
        glu = jnp.minimum(_dot(xb, wg_ref[0].astype(BF16)) + bg_ref[0], SWIGLU_LIMIT)
        lin = jnp.clip(_dot(xb, wl_ref[0].astype(BF16)) + bl_ref[0], -SWIGLU_LIMIT, SWIGLU_LIMIT)
        act = glu * _sigmoid(SWIGLU_ALPHA * glu) * (lin + 1.0)
        ys_ref[...] = _dot(act.astype(BF16), wd_ref[0].astype(BF16)) + bd_ref[0]


def _experts(tile_expert, n_used, xs, wg, bg, wl, bl, wd, bd):
    n_rows = xs.shape[0]
    n_tiles = n_rows // MOE_TILE
    wspec = pl.BlockSpec((1, D_MODEL, D_MODEL), lambda i, te, nu: (te[i], 0, 0))
    bspec = pl.BlockSpec((1, 1, D_MODEL), lambda i, te, nu: (te[i], 0, 0))
    rows = pl.BlockSpec((MOE_TILE, D_MODEL), lambda i, te, nu: (jnp.minimum(i, nu[0] - 1), 0))
    return pl.pallas_call(
        _expert_kernel,
        grid_spec=pltpu.PrefetchScalarGridSpec(
            num_scalar_prefetch=2,
            grid=(n_tiles,),
            in_specs=[rows, wspec, bspec, wspec, bspec, wspec, bspec],
            out_specs=rows),
        out_shape=jax.ShapeDtypeStruct((n_rows, D_MODEL), F32),
        compiler_params=_params(("arbitrary",)),
        name="moe_experts",
    )(tile_expert, n_used, xs, wg, bg, wl, bl, wd, bd)


def _row_copy_in(ys_hbm, buf_ref, sem, src, kk, r):
    return pltpu.make_async_copy(ys_hbm.at[pl.ds(src, 1)], buf_ref.at[kk, pl.ds(r, 1)], sem)


def _combine_kernel(dest_hbm, ys_hbm, gate_ref, x_ref, g2_ref, fg_ref, o_ref, idx_smem, buf_ref, sem_idx, sem_rows):
    i = pl.program_id(0)
    fetch = pltpu.make_async_copy(dest_hbm.at[pl.ds(i * INDEX_SLICE, INDEX_SLICE)], idx_smem, sem_idx)
    fetch.start()
    fetch.wait()

    def start(r, carry):
        for kk in range(TOP_K):
            _row_copy_in(ys_hbm, buf_ref, sem_rows, idx_smem[r * TOP_K + kk], kk, r).start(priority=kk % 2)
        return carry

    lax.fori_loop(0, ROW_MOVE_TILE, start, 0, unroll=8)
    for kk in range(TOP_K):
        pltpu.make_async_copy(ys_hbm.at[pl.ds(0, ROW_MOVE_TILE)], buf_ref.at[kk], sem_rows).wait()

    gates = gate_ref[...]
    moe = gates[:, 0:1] * buf_ref[0]
    for kk in range(1, TOP_K):
        moe = moe + gates[:, kk:kk + 1] * buf_ref[kk]
    x = x_ref[...] + g2_ref[0] * moe
    ms = jnp.mean(x * x, axis=-1, keepdims=True)
    o_ref[...] = x * lax.rsqrt(ms + EPS) * fg_ref[...]


def _combine(dest_flat, ys, gates, x1, gate2, final_g, S):
    N = x1.shape[0]
    TM = ROW_MOVE_TILE
    tpb = S // TM
    row = lambda i: (i, 0)
    return pl.pallas_call(
        _combine_kernel,
        grid=(N // TM,),
        in_specs=[pl.BlockSpec(memory_space=pl.ANY),
                  pl.BlockSpec(memory_space=pl.ANY),
                  pl.BlockSpec((TM, LANES), row),
                  pl.BlockSpec((TM, D_MODEL), row),
                  pl.BlockSpec((1, 1, D_MODEL), lambda i: (i // tpb, 0, 0)),
                  pl.BlockSpec((1, D_MODEL), lambda i: (0, 0))],
        out_specs=pl.BlockSpec((TM, D_MODEL), row),
        out_shape=jax.ShapeDtypeStruct((N, D_MODEL), F32),
        scratch_shapes=[pltpu.SMEM((INDEX_SLICE,), jnp.int32), pltpu.VMEM((TOP_K, TM, D_MODEL), F32),
                        pltpu.SemaphoreType.DMA, pltpu.SemaphoreType.DMA],
        compiler_params=_params(("arbitrary",)),
        name="moe_combine",
    )(dest_flat, ys, gates, x1, gate2, final_g)


def _pad_lanes(a, offset=0):
    return jnp.pad(a, ((0, 0), (offset, LANES - offset - a.shape[1])))


def kernel(x, c, ada_w, ada_b, norm1_g, norm2_g, w_in, b_gate, conv_w, a_log, dt_bias, gdn_norm_g, w_branch_a,
           diff_lambda, diff_norm_g, w_branch_b, w_out, router_w, router_b, w_glu, b_glu, w_lin, b_lin, w_down,
           b_down, final_g):
    B, S, D = x.shape
    N = B * S
    assert D == D_MODEL and S % GROUP == 0 and ada_w.shape[0] == 1
    x2 = x.reshape(N, D)

    mod = _adaln(c, ada_w[0], ada_b[0])
    shift1, scale1, gate1, shift2, scale2, gate2 = [m.reshape(B, 1, D) for m in jnp.split(mod, 6, axis=-1)]

    wi = w_in[0]
    n_a = 4 * D
    n_small = 4 * N_HEADS
    w_big = jnp.concatenate([wi[:, :n_a], wi[:, n_a + n_small:]], axis=1).astype(BF16)
    w_small = _pad_lanes(wi[:, n_a:n_a + n_small]).astype(BF16)
    alog_row = _pad_lanes(a_log[0].reshape(1, -1), GATE_LANE0)
    dt_row = _pad_lanes(dt_bias[0].reshape(1, -1), GATE_LANE0)
    proj, beta, G, Gl, cdl = _inproj(x2, shift1, scale1, norm1_g, w_big, w_small, alog_row, dt_row, S)
    proj3 = proj.reshape(B, S, N_COL_BLOCKS * D)

    qkv = _gdn_conv(proj3, conv_w[0])
    r3 = lambda a: a.reshape(B, S, LANES)
    nC = S // GDN_CHUNK
    GT = jnp.transpose(r3(G)[:, :, GATE_LANE0:GATE_LANE0 + 2 * N_HEADS], (0, 2, 1)).reshape(B, 2 * N_HEADS, 1, S)
    cd = r3(cdl).reshape(B, nC, GDN_CHUNK, LANES)[:, :, 0, GATE_LANE0:GATE_LANE0 + 2 * N_HEADS]
    cd = jnp.transpose(cd.reshape(B, nC, 2, N_HEADS), (0, 2, 3, 1)).reshape(-1)
    u, w, qd, att, kdt = _gdn_prep(qkv, r3(beta), r3(G), r3(Gl), GT)
    oAf, oAb = _gdn_scan(cd, u, w, qd, att, kdt)

    half = DIFF_DH // 2
    inv_freq = ROPE_THETA ** (-jnp.arange(half, dtype=F32) / half)
    ang = jnp.arange(S, dtype=F32)[:, None] * inv_freq[None, :]
    cos_t = jnp.tile(jnp.cos(ang), (1, 4))
    sin_h = jnp.sin(ang)
    sin_t = jnp.tile(jnp.concatenate([-sin_h, sin_h], axis=1), (1, 2))
    qt, kr, vt = _rope(proj3, cos_t, sin_t)
    lam_a = jnp.pad(_pad_lanes(diff_lambda[0][0::2]), ((0, 6), (0, 0)))
    lam_b = jnp.pad(_pad_lanes(diff_lambda[0][1::2]), ((0, 6), (0, 0)))
    oB = _diff_attn(qt, kr, vt, lam_a, lam_b, diff_norm_g)

    x1 = _merge(oAf.reshape(N, D), oAb.reshape(N, D), proj, oB.reshape(N, D), b_gate, x2, gate1, gdn_norm_g,
                w_branch_a[0].astype(BF16), w_branch_b[0].astype(BF16), w_out[0].astype(BF16), S)

    rw = _pad_lanes(router_w[0])
    rw0 = rw.astype(BF16)
    rw1 = (rw - rw0.astype(F32)).astype(BF16)
    h2, idx, gates, rank, counts = _router(x1, shift2, scale2, norm2_g, rw0, rw1, _pad_lanes(router_b), S)
    cnt = counts[0, :N_EXPERTS].astype(jnp.int32)
    padded = (cnt + MOE_TILE - 1) // MOE_TILE * MOE_TILE
    pad_ends = jnp.cumsum(padded)
    pad_starts = pad_ends - padded
    dest = (pad_starts[idx[:, :TOP_K]] + rank[:, :TOP_K]).reshape(-1)
    n_tiles = -(-(N * TOP_K) // MOE_TILE) + N_EXPERTS
    tile_start = jnp.arange(n_tiles, dtype=jnp.int32) * MOE_TILE
    tile_expert = jnp.sum((tile_start[:, None] >= pad_ends[None, :]).astype(jnp.int32), axis=1)
    tile_expert = jnp.minimum(tile_expert, N_EXPERTS - 1)
    n_used = (pad_ends[N_EXPERTS - 1:] // MOE_TILE).astype(jnp.int32)
    xs = _dispatch(pad_ends.astype(jnp.int32), padded, dest, h2, n_tiles * MOE_TILE)
    ys = _experts(tile_expert, n_used, xs, w_glu[0], b_glu[0][:, None, :], w_lin[0], b_lin[0][:, None, :],
                  w_down[0], b_down[0][:, None, :])
    out = _combine(dest, ys, gates, x1, gate2, final_g.reshape(1, D), S)
    return out.reshape(B, S, D)
```

```python
import functools
import math

import jax
import jax.numpy as jnp
from jax import lax
from jax.experimental import pallas as pl
from jax.experimental.pallas import tpu as pltpu

F32 = jnp.float32
BF16 = jnp.bfloat16

D_MODEL = 1024
EPS = 1e-6
N_HEADS = 8
HEAD_W = 128
GDN_CHUNK = 64
CONV_WIDTH = 5
DIFF_DH = 64
ROPE_THETA = 10000.0
LAM_INIT = 0.8 - 0.6 * math.exp(-0.3 * 0)
N_EXPERTS = 32
TOP_K = 4
SWIGLU_ALPHA = 1.702
SWIGLU_LIMIT = 7.0

LANES = 128
GROUP = 256
CHUNKS_PER_GROUP = GROUP // GDN_CHUNK
CHUNK_SHIFT = GDN_CHUNK.bit_length() - 1
MOE_TILE = 512
ROW_MOVE_TILE = 1024
NEG_INF = float("-inf")

COL_QA, COL_KA, COL_VA, COL_ZA, COL_QB, COL_KB, COL_VB, COL_GA, COL_GB = range(9)
N_COL_BLOCKS = 9


def _params(sem, vmem_mb=48):
    return pltpu.CompilerParams(dimension_semantics=sem, vmem_limit_bytes=vmem_mb * 1024 * 1024)


def _dot(a, b):
    return jnp.dot(a, b, preferred_element_type=F32)


def _dot_nt(a, b):
    return lax.dot_general(a, b, (((1,), (1,)), ((), ())), preferred_element_type=F32)


def _sigmoid(x):
    return 1.0 / (1.0 + jnp.exp(-x))


def _split3(x):
    a = x.astype(BF16)
    r = x - a.astype(F32)
    b = r.astype(BF16)
    c = (r - b.astype(F32)).astype(BF16)
    return a, b, c


def _adaln_kernel(c_ref, w_ref, b_ref, o_ref):
    c = c_ref[...]
    cond = c * _sigmoid(c)
    c0, c1, c2 = _split3(cond)
    w0, w1, w2 = _split3(w_ref[...])
    acc = _dot(c0, w0) + (_dot(c0, w1) + _dot(c1, w0)) + (_dot(c0, w2) + _dot(c1, w1) + _dot(c2, w0))
    o_ref[...] = acc + b_ref[...]


def _adaln(c, ada_w, ada_b):
    B = c.shape[0]
    n = ada_w.shape[1] // D_MODEL
    return pl.pallas_call(
        _adaln_kernel,
        grid=(n,),
        in_specs=[pl.BlockSpec((B, D_MODEL), lambda j: (0, 0)),
                  pl.BlockSpec((D_MODEL, D_MODEL), lambda j: (0, j)),
                  pl.BlockSpec((1, D_MODEL), lambda j: (0, j))],
        out_specs=pl.BlockSpec((B, D_MODEL), lambda j: (0, j)),
        out_shape=jax.ShapeDtypeStruct((B, n * D_MODEL), F32),
        compiler_params=_params(("parallel",)),
        name="adaln",
    )(c, ada_w, ada_b.reshape(1, -1))


def _inproj_kernel(x_ref, sh_ref, sc_ref, g_ref, w_ref, ws_ref, alog_ref, dt_ref,
                   o_ref, beta_ref, gcum_ref, glast_ref, cd_ref, h_ref):
    @pl.when(pl.program_id(1) == 0)
    def _():
        x = x_ref[...]
        ms = jnp.mean(x * x, axis=-1, keepdims=True)
        y = x * lax.rsqrt(ms + EPS) * g_ref[...]
        h = (y * (1.0 + sc_ref[0]) + sh_ref[0]).astype(BF16)
        h_ref[...] = h
        _gdn_gates(_dot(h, ws_ref[...]), alog_ref[...], dt_ref[...], beta_ref, gcum_ref, glast_ref, cd_ref)

    col = pl.multiple_of(pl.program_id(1) * D_MODEL, D_MODEL)
    o_ref[...] = _dot(h_ref[...], w_ref[:, pl.ds(col, D_MODEL)]).astype(o_ref.dtype)


def _inproj(x2, shift, scale, g, w_big, w_small, alog_row, dt_row, S):
    N = x2.shape[0]
    TM = min(1024, S)
    tpb = S // TM
    const = lambda i, j: (0, 0)
    lanes = pl.BlockSpec((TM, LANES), lambda i, j: (i, 0))
    return pl.pallas_call(
        _inproj_kernel,
        grid=(N // TM, N_COL_BLOCKS),
        in_specs=[pl.BlockSpec((TM, D_MODEL), lambda i, j: (i, 0)),
                  pl.BlockSpec((1, 1, D_MODEL), lambda i, j: (i // tpb, 0, 0)),
                  pl.BlockSpec((1, 1, D_MODEL), lambda i, j: (i // tpb, 0, 0)),
                  pl.BlockSpec((1, D_MODEL), const),
                  pl.BlockSpec((D_MODEL, N_COL_BLOCKS * D_MODEL), const, pipeline_mode=pl.Buffered(1)),
                  pl.BlockSpec((D_MODEL, LANES), const),
                  pl.BlockSpec((1, LANES), const),
                  pl.BlockSpec((1, LANES), const)],
        out_specs=[pl.BlockSpec((TM, D_MODEL), lambda i, j: (i, j)), lanes, lanes, lanes, lanes],
        out_shape=[jax.ShapeDtypeStruct((N, N_COL_BLOCKS * D_MODEL), BF16)]
                  + [jax.ShapeDtypeStruct((N, LANES), F32)] * 4,
        scratch_shapes=[pltpu.VMEM((TM, D_MODEL), BF16)],
        compiler_params=_params(("parallel", "arbitrary")),
        name="inproj",
    )(x2, shift, scale, g, w_big, w_small, alog_row, dt_row)


HALO = 16


CONV_BLOCK = 256
CONV_PAD = (CONV_WIDTH - 1) // 2
CONV_TAPS = tuple(j for j in range(CONV_WIDTH) if j != CONV_PAD)


def _conv_shifts():
    r = jnp.arange(CONV_BLOCK)[:, None]
    c = jnp.arange(CONV_BLOCK)[None, :]
    return jnp.stack([c == r + (j - CONV_PAD) for j in CONV_TAPS]).astype(BF16)


def _conv_kernel(cur_ref, prev_ref, next_ref, w_ref, shift_ref, o_ref, ext_ref, *, TR):
    i = pl.program_id(1)
    g = pl.program_id(2)
    last = pl.num_programs(1) - 1
    ext_ref[8:8 + TR, :] = cur_ref[0].astype(F32)
    pv = prev_ref[0].astype(F32)[HALO - 8:HALO]
    nx = next_ref[0].astype(F32)[0:8]
    ext_ref[0:8, :] = jnp.where(i > 0, pv, 0.0)
    ext_ref[TR + 8:TR + 16, :] = jnp.where(i < last, nx, 0.0)
    ones = jnp.ones((HEAD_W, HEAD_W), BF16)
    qscale = jnp.where(g == 0, HEAD_W ** -0.5, 1.0)

    def edge_rows(row0):
        e = ext_ref[8 + row0 - CONV_PAD:16 + row0 - CONV_PAD, :] * w_ref[0:1, :]
        for j in range(1, CONV_WIDTH):
            e = e + ext_ref[8 + row0 - CONV_PAD + j:16 + row0 - CONV_PAD + j, :] * w_ref[j:j + 1, :]
        return e

    for blk in range(TR // CONV_BLOCK):
        r0 = blk * CONV_BLOCK
        rows = slice(r0, r0 + CONV_BLOCK)
        ub = cur_ref[0, rows, :]
        acc = ext_ref[8 + r0:8 + r0 + CONV_BLOCK, :] * w_ref[CONV_PAD:CONV_PAD + 1, :]
        for si, j in enumerate(CONV_TAPS):
            acc = acc + _dot(shift_ref[si], ub) * w_ref[j:j + 1, :]
        acc = jnp.concatenate([edge_rows(r0), acc[8:CONV_BLOCK - 8], edge_rows(r0 + CONV_BLOCK - 8)], axis=0)
        y = acc * _sigmoid(acc)

        for h in range(N_HEADS):
            cols = slice(h * HEAD_W, (h + 1) * HEAD_W)
            yh = y[:, cols]
            ss = _dot((yh * yh).astype(BF16), ones)
            scale = jnp.where(g < 2, lax.rsqrt(ss + EPS) * qscale, 1.0)
            o_ref[0, rows, cols] = (yh * scale).astype(o_ref.dtype)


def _gdn_conv(proj3, conv_w):
    B, S, _ = proj3.shape
    TR = min(512, S)
    nT = S // TR
    rb = TR // HALO
    nH = S // HALO
    return pl.pallas_call(
        functools.partial(_conv_kernel, TR=TR),
        grid=(B, nT, 3),
        in_specs=[pl.BlockSpec((1, TR, D_MODEL), lambda b, i, g: (b, i, g)),
                  pl.BlockSpec((1, HALO, D_MODEL), lambda b, i, g: (b, jnp.maximum(i * rb - 1, 0), g)),
                  pl.BlockSpec((1, HALO, D_MODEL), lambda b, i, g: (b, jnp.minimum((i + 1) * rb, nH - 1), g)),
                  pl.BlockSpec((CONV_WIDTH, D_MODEL), lambda b, i, g: (0, g)),
                  pl.BlockSpec((len(CONV_TAPS), CONV_BLOCK, CONV_BLOCK), lambda b, i, g: (0, 0, 0))],
        out_specs=pl.BlockSpec((1, TR, D_MODEL), lambda b, i, g: (b, i, g)),
        out_shape=jax.ShapeDtypeStruct((B, S, 3 * D_MODEL), BF16),
        scratch_shapes=[pltpu.VMEM((TR + 16, D_MODEL), F32)],
        compiler_params=_params(("parallel", "parallel", "parallel")),
        name="gdn_conv",
    )(proj3, proj3, proj3, conv_w, _conv_shifts())


GATE_LANE0 = 16


def _gdn_gates(x, alog, dt, beta_ref, g_ref, gl_ref, cd_ref):
    lane = lax.broadcasted_iota(jnp.int32, (GROUP, LANES), 1)
    r = lax.broadcasted_iota(jnp.int32, (GROUP, GROUP), 0)
    c = lax.broadcasted_iota(jnp.int32, (GROUP, GROUP), 1)
    same = (r >> CHUNK_SHIFT) == (c >> CHUNK_SHIFT)
    lower = jnp.where(same & (c <= r), 1.0, 0.0).astype(BF16)
    upper = jnp.where(same & (c >= r), 1.0, 0.0).astype(BF16)
    block = jnp.where(same, 1.0, 0.0).astype(BF16)
    beta_ref[...] = _sigmoid(x)
    for gi in range(x.shape[0] // GROUP):
        rows = slice(gi * GROUP, (gi + 1) * GROUP)
        z = x[rows] + dt
        softplus = jnp.maximum(z, 0.0) + jnp.log(1.0 + jnp.exp(-jnp.abs(z)))
        gd = -jnp.exp(alog) * softplus
        gd = jnp.where((lane >= GATE_LANE0) & (lane < GATE_LANE0 + 2 * N_HEADS), gd, 0.0)
        p0, p1, p2 = _split3(gd)
        g_fwd = _dot(lower, p0) + _dot(lower, p1) + _dot(lower, p2)
        g_bwd = _dot(upper, p0) + _dot(upper, p1) + _dot(upper, p2)
        tot = _dot(block, p0) + _dot(block, p1) + _dot(block, p2)
        G = jnp.where(lane < GATE_LANE0 + N_HEADS, g_fwd, g_bwd)
        g_ref[rows, :] = G
        gl_ref[rows, :] = tot - G
        cd_ref[rows, :] = jnp.exp(tot)


PREP_HEADS = 4
(MASK_STRICT_LO, MASK_STRICT_UP, MASK_INCL_LO, MASK_INCL_UP, MASK_EYE, MASK_BLK4, MASK_OFF0) = range(7)
N_MASKS = MASK_OFF0 + (CHUNK_SHIFT - 2)


def _prep_masks():
    r = jnp.arange(GROUP)[:, None]
    c = jnp.arange(GROUP)[None, :]
    same = (r >> CHUNK_SHIFT) == (c >> CHUNK_SHIFT)
    masks = [same & (c < r), same & (c > r), same & (c <= r), same & (c >= r), r == c, (r >> 2) == (c >> 2)]
    for shift in range(2, CHUNK_SHIFT):
        masks.append(((r >> shift) != (c >> shift)) & ((r >> (shift + 1)) == (c >> (shift + 1))))
    return jnp.stack(masks).astype(F32)


def _col(x, l, lane):
    return jnp.broadcast_to(jnp.sum(jnp.where(lane == l, x, 0.0), axis=1, keepdims=True), x.shape)


def _prep_kernel(q_ref, k_ref, v_ref, beta_ref, g_ref, gl_ref, gtf_ref, gtb_ref, mask_ref, bmask_ref,
                 u_ref, w_ref, qd_ref, at_ref, kdt_ref):
    hp = pl.program_id(1)
    lane = lax.broadcasted_iota(jnp.int32, (GROUP, LANES), 1)
    wide = lambda a: jnp.concatenate([a, a], axis=1)
    chains = [(hh, d) for hh in range(PREP_HEADS) for d in range(2)]
    p, rhs = {}, {}
    for hh in range(PREP_HEADS):
        cols = slice(hh * HEAD_W, (hh + 1) * HEAD_W)
        q = q_ref[0, :, cols]
        k = k_ref[0, :, cols]
        qf = q.astype(F32)
        kf = k.astype(F32)
        vf = v_ref[0, :, cols].astype(F32)
        kk = _dot_nt(k, k)
        qk = _dot_nt(q, k)
        for d in range(2):
            lb = d * N_HEADS + hp * PREP_HEADS + hh
            beta_c = _col(beta_ref[0], lb, lane)
            g_c = _col(g_ref[0], GATE_LANE0 + lb, lane)
            eg_c = jnp.exp(g_c)
            egl_c = jnp.exp(_col(gl_ref[0], GATE_LANE0 + lb, lane))
            g_r = (gtf_ref if d == 0 else gtb_ref)[0, hh]
            dec = jnp.exp(jnp.minimum(wide(g_c) - g_r, 0.0))
            p[hh, d] = (-(kk * wide(beta_c)) * dec * mask_ref[MASK_STRICT_LO + d]).astype(BF16)
            att = qk * dec * mask_ref[MASK_INCL_LO + d]
            rhs[hh, d] = jnp.concatenate([vf * beta_c, kf * (beta_c * eg_c)], axis=1).astype(BF16)
            qd_ref[0, d, :, cols] = (qf * eg_c).astype(qd_ref.dtype)
            kdt = (kf * egl_c).T
            for ci in range(CHUNKS_PER_GROUP):
                sl = slice(ci * GDN_CHUNK, (ci + 1) * GDN_CHUNK)
                at_ref[0, d, hh, ci] = att[sl, sl].astype(at_ref.dtype)
                kdt_ref[0, d, hh, ci] = kdt[:, sl].astype(kdt_ref.dtype)
    p4 = {ch: p[ch] * bmask_ref[0] for ch in chains}
    sq = {ch: _dot(p4[ch], p4[ch]).astype(BF16) for ch in chains}
    tb = {ch: bmask_ref[N_MASKS - MASK_BLK4] + p4[ch] for ch in chains}
    tb = {ch: (tb[ch].astype(F32) + _dot(tb[ch], sq[ch])).astype(BF16) for ch in chains}
    for lvl in range(CHUNK_SHIFT - 2):
        x = {ch: _dot(tb[ch], p[ch] * bmask_ref[1 + lvl]).astype(BF16) for ch in chains}
        tb = {ch: (tb[ch].astype(F32) + _dot(x[ch], tb[ch])).astype(BF16) for ch in chains}
    uw = {ch: _dot(tb[ch], rhs[ch]) for ch in chains}
    for hh, d in chains:
        cols = slice(hh * HEAD_W, (hh + 1) * HEAD_W)
        u_ref[0, d, :, cols] = uw[hh, d][:, :HEAD_W].astype(u_ref.dtype)
        w_ref[0, d, :, cols] = uw[hh, d][:, HEAD_W:].astype(w_ref.dtype)


def _gdn_prep(qkv, beta, G, Gl, GT):
    B, S, _ = qkv.shape
    nG = S // GROUP
    nC = S // GDN_CHUNK
    PW = PREP_HEADS * HEAD_W
    nP = N_HEADS // PREP_HEADS
    sm = pl.BlockSpec((1, GROUP, LANES), lambda b, h, g: (b, g, 0))
    big = pl.BlockSpec((1, 2, GROUP, PW), lambda b, h, g: (b, 0, g, h))
    masks = _prep_masks()
    return pl.pallas_call(
        _prep_kernel,
        grid=(B, nP, nG),
        in_specs=[pl.BlockSpec((1, GROUP, PW), lambda b, h, g: (b, g, h)),
                  pl.BlockSpec((1, GROUP, PW), lambda b, h, g: (b, g, nP + h)),
                  pl.BlockSpec((1, GROUP, PW), lambda b, h, g: (b, g, 2 * nP + h)),
                  sm, sm, sm,
                  pl.BlockSpec((1, PREP_HEADS, 1, GROUP), lambda b, h, g: (b, h, 0, g)),
                  pl.BlockSpec((1, PREP_HEADS, 1, GROUP), lambda b, h, g: (b, nP + h, 0, g)),
                  pl.BlockSpec((MASK_EYE, GROUP, GROUP), lambda b, h, g: (0, 0, 0)),
                  pl.BlockSpec((N_MASKS - MASK_BLK4 + 1, GROUP, GROUP), lambda b, h, g: (0, 0, 0))],
        out_specs=[big, big, big,
                   pl.BlockSpec((1, 2, PREP_HEADS, CHUNKS_PER_GROUP, GDN_CHUNK, GDN_CHUNK),
                                lambda b, h, g: (b, 0, h, g, 0, 0)),
                   pl.BlockSpec((1, 2, PREP_HEADS, CHUNKS_PER_GROUP, HEAD_W, GDN_CHUNK),
                                lambda b, h, g: (b, 0, h, g, 0, 0))],
        out_shape=[jax.ShapeDtypeStruct((B, 2, S, D_MODEL), BF16)] * 3
                  + [jax.ShapeDtypeStruct((B, 2, N_HEADS, nC, GDN_CHUNK, GDN_CHUNK), BF16),
                     jax.ShapeDtypeStruct((B, 2, N_HEADS, nC, HEAD_W, GDN_CHUNK), BF16)],
        compiler_params=_params(("parallel", "parallel", "parallel")),
        name="gdn_prep",
    )(qkv, qkv, qkv, beta, G, Gl, GT, GT, masks[:MASK_EYE],
      jnp.concatenate([masks[MASK_BLK4:], masks[MASK_EYE:MASK_EYE + 1]]).astype(BF16))


def _scan_kernel(cd_ref, uf_ref, wf_ref, qdf_ref, atf_ref, kdtf_ref, ub_ref, wb_ref, qdb_ref, atb_ref, kdtb_ref,
                 of_ref, ob_ref, state_ref, *, nc, nC):
    b = pl.program_id(0)
    t = pl.program_id(1)
    nT = pl.num_programs(1)

    @pl.when(t == 0)
    def _():
        state_ref[...] = jnp.zeros_like(state_ref)

    dirs = ((uf_ref, wf_ref, qdf_ref, atf_ref, kdtf_ref, of_ref), (ub_ref, wb_ref, qdb_ref, atb_ref, kdtb_ref, ob_ref))

    def chunk(ci, carry):
        work = []
        for d, refs in enumerate(dirs):
            c = ci if d == 0 else nc - 1 - ci
            tt = t if d == 0 else nT - 1 - t
            row = pl.multiple_of(c * GDN_CHUNK, GDN_CHUNK)
            for h in range(N_HEADS):
                work.append((d, h, c, row, ((b * 2 + d) * N_HEADS + h) * nC + tt * nc + c, refs))
        s_old = [state_ref[d, h] for d, h, *_ in work]
        sb = [s.astype(BF16) for s in s_old]
        tile = lambda ref, row, h: ref[0, 0, pl.ds(row, GDN_CHUNK), h * HEAD_W:(h + 1) * HEAD_W]
        ws = [_dot(tile(refs[1], row, h), sb[i]) for i, (d, h, c, row, gi, refs) in enumerate(work)]
        qs = [_dot(tile(refs[2], row, h), sb[i]) for i, (d, h, c, row, gi, refs) in enumerate(work)]
        vb = [(tile(refs[0], row, h).astype(F32) - ws[i]).astype(BF16)
              for i, (d, h, c, row, gi, refs) in enumerate(work)]
        o = [qs[i] + _dot(refs[3][0, 0, h, c], vb[i]) for i, (d, h, c, row, gi, refs) in enumerate(work)]
        upd = [_dot(refs[4][0, 0, h, c], vb[i]) for i, (d, h, c, row, gi, refs) in enumerate(work)]
        for i, (d, h, c, row, gi, refs) in enumerate(work):
            state_ref[d, h] = s_old[i] * cd_ref[gi] + upd[i]
            refs[5][0, pl.ds(row, GDN_CHUNK), h * HEAD_W:(h + 1) * HEAD_W] = o[i].astype(of_ref.dtype)
        return carry

    lax.fori_loop(0, nc, chunk, 0)


def _gdn_scan(cd, u, w, qd, att, kdt):
    B, _, S, _ = u.shape
    TC = min(512, S)
    nT = S // TC
    nc = TC // GDN_CHUNK
    nC = S // GDN_CHUNK
    fwd = lambda b, t: t
    bwd = lambda b, t: nT - 1 - t

    def specs(d, tm):
        big = pl.BlockSpec((1, 1, TC, D_MODEL), lambda b, t: (b, d, tm(b, t), 0))
        return [big, big, big,
                pl.BlockSpec((1, 1, N_HEADS, nc, GDN_CHUNK, GDN_CHUNK), lambda b, t: (b, d, 0, tm(b, t), 0, 0)),
                pl.BlockSpec((1, 1, N_HEADS, nc, HEAD_W, GDN_CHUNK), lambda b, t: (b, d, 0, tm(b, t), 0, 0))]

    return pl.pallas_call(
        functools.partial(_scan_kernel, nc=nc, nC=nC),
        grid=(B, nT),
        in_specs=[pl.BlockSpec(memory_space=pltpu.SMEM)] + specs(0, fwd) + specs(1, bwd),
        out_specs=[pl.BlockSpec((1, TC, D_MODEL), lambda b, t: (b, t, 0)),
                   pl.BlockSpec((1, TC, D_MODEL), lambda b, t: (b, nT - 1 - t, 0))],
        out_shape=[jax.ShapeDtypeStruct((B, S, D_MODEL), BF16)] * 2,
        scratch_shapes=[pltpu.VMEM((2, N_HEADS, HEAD_W, HEAD_W), F32)],
        compiler_params=_params(("parallel", "arbitrary")),
        name="gdn_scan",
    )(cd, u, w, qd, att, kdt, u, w, qd, att, kdt)


ATT_TQ = 256
ATT_TK = 256
ATT_NQ = 4
ATT_VROWS = HEAD_W + 16
LOG2E = 1.4426950408889634


def _rope_kernel(q_ref, k_ref, v_ref, cos_ref, sin_ref, qt_ref, kr_ref, vt_ref, *, TR):
    cs = cos_ref[...]
    sn = sin_ref[...]
    lane = lax.broadcasted_iota(jnp.int32, cs.shape, 1)
    first_half = (lane & (DIFF_DH - 1)) < (DIFF_DH // 2)
    qscale = DIFF_DH ** -0.5 * LOG2E

    def rot(x):
        partner = jnp.where(first_half, pltpu.roll(x, HEAD_W - DIFF_DH // 2, 1), pltpu.roll(x, DIFF_DH // 2, 1))
        return x * cs + partner * sn

    for h in range(N_HEADS):
        cols = slice(h * HEAD_W, (h + 1) * HEAD_W)
        qr = rot(q_ref[0, :, cols].astype(F32)) * qscale
        kr_ref[0, :, cols] = rot(k_ref[0, :, cols].astype(F32)).astype(kr_ref.dtype)
        vf = v_ref[0, :, cols].astype(F32)
        for ci in range(TR // ATT_TK):
            rows = slice(ci * ATT_TK, (ci + 1) * ATT_TK)
            vt_ref[0, h, ci, 0:HEAD_W, :] = vf[rows].T.astype(vt_ref.dtype)
            vt_ref[0, h, ci, HEAD_W:ATT_VROWS, :] = jnp.ones((ATT_VROWS - HEAD_W, ATT_TK), vt_ref.dtype)
        for ci in range(TR // ATT_TQ):
            rows = slice(ci * ATT_TQ, (ci + 1) * ATT_TQ)
            qt_ref[0, h, ci] = qr[rows].T.astype(qt_ref.dtype)


def _rope(proj3, cos_t, sin_t):
    B, S, _ = proj3.shape
    TR = min(512, S)
    tab = pl.BlockSpec((TR, HEAD_W), lambda b, i: (i, 0))
    col = lambda cb: pl.BlockSpec((1, TR, D_MODEL), lambda b, i: (b, i, cb))
    return pl.pallas_call(
        functools.partial(_rope_kernel, TR=TR),
        grid=(B, S // TR),
        in_specs=[col(COL_QB), col(COL_KB), col(COL_VB), tab, tab],
        out_specs=[pl.BlockSpec((1, N_HEADS, TR // ATT_TQ, HEAD_W, ATT_TQ), lambda b, i: (b, 0, i, 0, 0)),
                   pl.BlockSpec((1, TR, D_MODEL), lambda b, i: (b, i, 0)),
                   pl.BlockSpec((1, N_HEADS, TR // ATT_TK, ATT_VROWS, ATT_TK), lambda b, i: (b, 0, i, 0, 0))],
        out_shape=[jax.ShapeDtypeStruct((B, N_HEADS, S // ATT_TQ, HEAD_W, ATT_TQ), BF16),
                   jax.ShapeDtypeStruct((B, S, D_MODEL), BF16),
                   jax.ShapeDtypeStruct((B, N_HEADS, S // ATT_TK, ATT_VROWS, ATT_TK), BF16)],
        compiler_params=_params(("parallel", "parallel")),
        name="rope",
    )(proj3, proj3, proj3, cos_t, sin_t)


def _attn_kernel(qt_ref, k_ref, vt_ref, la_ref, lb_ref, g_ref, o_ref, s_ref, acc_ref, *, n_chunks):
    row = lax.broadcasted_iota(jnp.int32, (HEAD_W, ATT_TQ), 0)
    qw = []
    for qb in range(ATT_NQ):
        qt = qt_ref[0, 0, qb]
        zero = jnp.zeros_like(qt)
        qw.append((jnp.where(row < DIFF_DH, qt, zero), jnp.where(row >= DIFF_DH, qt, zero)))
    chains = [(qb, comp) for qb in range(ATT_NQ) for comp in range(2)]

    def scores(j):
        kc = k_ref[0, pl.ds(pl.multiple_of(j * ATT_TK, ATT_TK), ATT_TK), :]
        return [_dot(kc, qw[qb][comp]) for qb, comp in chains]

    acc_ref[...] = jnp.zeros_like(acc_ref)
    for (qb, comp), s0 in zip(chains, scores(0)):
        s_ref[qb, comp] = s0

    def chunk(j, carry):
        s_next = scores(jnp.minimum(j + 1, n_chunks - 1))
        vt = vt_ref[0, 0, j]
        out = []
        for ci, (qb, comp) in enumerate(chains):
            m_prev = carry[ci]
            s = s_ref[qb, comp]
            m_new = jnp.maximum(m_prev, jnp.max(s, axis=0, keepdims=True))
            alpha = jnp.exp2(m_prev - m_new)
            p = jnp.exp2(s - m_new)
            out.append(m_new)
            acc_ref[qb, comp] = alpha * acc_ref[qb, comp] + _dot(vt, p.astype(BF16))
        for (qb, comp), sn in zip(chains, s_next):
            s_ref[qb, comp] = sn
        return tuple(out)

    neg = jnp.full((1, ATT_TQ), NEG_INF, F32)
    lax.fori_loop(0, n_chunks, chunk, (neg,) * len(chains), unroll=8)

    sums = jnp.sum(la_ref[...] * lb_ref[...], axis=1, keepdims=True)
    lrow = lax.broadcasted_iota(jnp.int32, sums.shape, 0)
    sign = jnp.where(lrow == 0, 1.0, jnp.where(lrow == 1, -1.0, 0.0))
    lam = jnp.sum(sign * jnp.exp(sums), axis=0, keepdims=True) + LAM_INIT
    for qb in range(ATT_NQ):
        l0 = acc_ref[qb, 0, HEAD_W:HEAD_W + 1, :]
        l1 = acc_ref[qb, 1, HEAD_W:HEAD_W + 1, :]
        ot = acc_ref[qb, 0, 0:HEAD_W, :] / l0 - lam * (acc_ref[qb, 1, 0:HEAD_W, :] / l1)
        ms = jnp.mean(ot * ot, axis=0, keepdims=True)
        y = (ot * lax.rsqrt(ms + EPS)).T * g_ref[...] * (1.0 - LAM_INIT)
        o_ref[0, qb * ATT_TQ:(qb + 1) * ATT_TQ, :] = y.astype(o_ref.dtype)


def _diff_attn(qt, kr, vt, lam_a, lam_b, norm_g):
    B, S, _ = kr.shape
    lam_spec = pl.BlockSpec((8, LANES), lambda b, h, qi: (0, 0))
    return pl.pallas_call(
        functools.partial(_attn_kernel, n_chunks=S // ATT_TK),
        grid=(B, N_HEADS, S // (ATT_NQ * ATT_TQ)),
        in_specs=[pl.BlockSpec((1, 1, ATT_NQ, HEAD_W, ATT_TQ), lambda b, h, qi: (b, h, qi, 0, 0)),
                  pl.BlockSpec((1, S, HEAD_W), lambda b, h, qi: (b, 0, h)),
                  pl.BlockSpec((1, 1, S // ATT_TK, ATT_VROWS, ATT_TK), lambda b, h, qi: (b, h, 0, 0, 0)),
                  lam_spec, lam_spec,
                  pl.BlockSpec((1, HEAD_W), lambda b, h, qi: (0, 0))],
        out_specs=pl.BlockSpec((1, ATT_NQ * ATT_TQ, HEAD_W), lambda b, h, qi: (b, qi, h)),
        out_shape=jax.ShapeDtypeStruct((B, S, D_MODEL), BF16),
        scratch_shapes=[pltpu.VMEM((ATT_NQ, 2, ATT_TK, ATT_TQ), F32), pltpu.VMEM((ATT_NQ, 2, ATT_VROWS, ATT_TQ), F32)],
        compiler_params=_params(("parallel", "parallel", "parallel")),
        name="diff_attn",
    )(qt, kr, vt, lam_a, lam_b, norm_g)


def _merge_kernel(of_ref, ob_ref, z_ref, oB_ref, ga_ref, gb_ref, bga_ref, bgb_ref, x_ref, g1_ref, gn_ref,
                  wa_ref, wb_ref, wo_ref, o_ref, ya_ref):
    oa = of_ref[...].astype(F32) + ob_ref[...].astype(F32)
    z = z_ref[...].astype(F32)
    gate = z * _sigmoid(z)
    for h in range(N_HEADS):
        cols = slice(h * HEAD_W, (h + 1) * HEAD_W)
        oh = oa[:, cols]
        ms = jnp.mean(oh * oh, axis=-1, keepdims=True)
        ya_ref[:, cols] = (oh * lax.rsqrt(ms + EPS) * gn_ref[...] * gate[:, cols]).astype(BF16)
    y_a = _dot(ya_ref[...], wa_ref[...])
    y_b = _dot(oB_ref[...], wb_ref[...])
    gate_a = _sigmoid(ga_ref[...].astype(F32) + bga_ref[...])
    gate_b = _sigmoid(gb_ref[...].astype(F32) + bgb_ref[...])
    mix = _dot((gate_a * y_a + gate_b * y_b).astype(BF16), wo_ref[...])
    o_ref[...] = x_ref[...] + g1_ref[0] * mix


def _merge(oAf, oAb, proj, oB, b_gate, x2, gate1, gn, wa, wb, wo, S):
    N = x2.shape[0]
    TM = min(512, S)
    tpb = S // TM
    row = lambda i: (i, 0)
    full = pl.BlockSpec((D_MODEL, D_MODEL), lambda i: (0, 0))
    return pl.pallas_call(
        _merge_kernel,
        grid=(N // TM,),
        in_specs=[pl.BlockSpec((TM, D_MODEL), row),
                  pl.BlockSpec((TM, D_MODEL), row),
                  pl.BlockSpec((TM, D_MODEL), lambda i: (i, COL_ZA)),
                  pl.BlockSpec((TM, D_MODEL), row),
                  pl.BlockSpec((TM, D_MODEL), lambda i: (i, COL_GA)),
                  pl.BlockSpec((TM, D_MODEL), lambda i: (i, COL_GB)),
                  pl.BlockSpec((1, D_MODEL), lambda i: (0, 0)),
                  pl.BlockSpec((1, D_MODEL), lambda i: (0, 1)),
                  pl.BlockSpec((TM, D_MODEL), row),
                  pl.BlockSpec((1, 1, D_MODEL), lambda i: (i // tpb, 0, 0)),
                  pl.BlockSpec((1, HEAD_W), lambda i: (0, 0)),
                  full, full, full],
        out_specs=pl.BlockSpec((TM, D_MODEL), row),
        out_shape=jax.ShapeDtypeStruct((N, D_MODEL), F32),
        scratch_shapes=[pltpu.VMEM((TM, D_MODEL), BF16)],
        compiler_params=_params(("parallel",)),
        name="merge",
    )(oAf, oAb, proj, oB, proj, proj, b_gate, b_gate, x2, gate1, gn, wa, wb, wo)


def _router_kernel(x_ref, sh_ref, sc_ref, g_ref, rw0_ref, rw1_ref, rb_ref, tri_ref,
                   h_ref, idx_ref, gate_ref, rank_ref, cnt_ref, base_ref):
    i = pl.program_id(0)

    @pl.when(i == 0)
    def _():
        base_ref[...] = jnp.zeros_like(base_ref)

    x = x_ref[...]
    ms = jnp.mean(x * x, axis=-1, keepdims=True)
    h = x * lax.rsqrt(ms + EPS) * g_ref[...] * (1.0 + sc_ref[0]) + sh_ref[0]
    h_ref[...] = h
    h0 = h.astype(BF16)
    h1 = (h - h0.astype(F32)).astype(BF16)
    logits = _dot(h0, rw0_ref[...]) + (_dot(h0, rw1_ref[...]) + _dot(h1, rw0_ref[...])) + rb_ref[...]
    lane = lax.broadcasted_iota(jnp.int32, logits.shape, 1)
    lane_f = lane.astype(F32)
    cur = jnp.where(lane < N_EXPERTS, logits, NEG_INF)
    vals, sel = [], []
    for _ in range(TOP_K):
        m = jnp.max(cur, axis=1, keepdims=True)
        ix = jnp.min(jnp.where(cur == m, lane_f, float(LANES)), axis=1, keepdims=True)
        hit = lane_f == ix
        vals.append(m)
        sel.append(hit)
        cur = jnp.where(hit, NEG_INF, cur)
    exps = [jnp.exp(v - vals[0]) for v in vals]
    den = exps[0] + exps[1] + exps[2] + exps[3]
    onehot = jnp.zeros(logits.shape, F32)
    for hit in sel:
        onehot = onehot + jnp.where(hit, 1.0, 0.0)
    before = _dot(tri_ref[...], onehot.astype(BF16)) + base_ref[...]
    idx_out = jnp.zeros(logits.shape, F32)
    gate_out = jnp.zeros(logits.shape, F32)
    rank_out = jnp.zeros(logits.shape, F32)
    for kk in range(TOP_K):
        slot = lane == kk
        e_id = jnp.sum(jnp.where(sel[kk], lane_f, 0.0), axis=1, keepdims=True)
        rk = jnp.sum(jnp.where(sel[kk], before, 0.0), axis=1, keepdims=True)
        idx_out = jnp.where(slot, e_id, idx_out)
        gate_out = jnp.where(slot, exps[kk] / den, gate_out)
        rank_out = jnp.where(slot, rk, rank_out)
    idx_ref[...] = idx_out.astype(jnp.int32)
    gate_ref[...] = gate_out
    rank_ref[...] = rank_out.astype(jnp.int32)
    base_ref[...] = base_ref[...] + jnp.sum(onehot, axis=0, keepdims=True)
    cnt_ref[...] = base_ref[...]


def _router(x1, shift, scale, g, rw0, rw1, rb, S):
    N = x1.shape[0]
    TM = min(512, S)
    tpb = S // TM
    r = jnp.arange(TM)
    tri = (r[None, :] < r[:, None]).astype(BF16)
    row = lambda i: (i, 0)
    const = lambda i: (0, 0)
    lanes = pl.BlockSpec((TM, LANES), row)
    return pl.pallas_call(
        _router_kernel,
        grid=(N // TM,),
        in_specs=[pl.BlockSpec((TM, D_MODEL), row),
                  pl.BlockSpec((1, 1, D_MODEL), lambda i: (i // tpb, 0, 0)),
                  pl.BlockSpec((1, 1, D_MODEL), lambda i: (i // tpb, 0, 0)),
                  pl.BlockSpec((1, D_MODEL), const),
                  pl.BlockSpec((D_MODEL, LANES), const),
                  pl.BlockSpec((D_MODEL, LANES), const),
                  pl.BlockSpec((1, LANES), const),
                  pl.BlockSpec((TM, TM), const)],
        out_specs=[pl.BlockSpec((TM, D_MODEL), row), lanes, lanes, lanes, pl.BlockSpec((1, LANES), const)],
        out_shape=[jax.ShapeDtypeStruct((N, D_MODEL), F32),
                   jax.ShapeDtypeStruct((N, LANES), jnp.int32),
                   jax.ShapeDtypeStruct((N, LANES), F32),
                   jax.ShapeDtypeStruct((N, LANES), jnp.int32),
                   jax.ShapeDtypeStruct((1, LANES), F32)],
        scratch_shapes=[pltpu.VMEM((1, LANES), F32)],
        compiler_params=_params(("arbitrary",)),
        name="router",
    )(x1, shift, scale, g, rw0, rw1, rb, tri)


INDEX_SLICE = ROW_MOVE_TILE * TOP_K


def _row_copy_out(h_ref, xs_hbm, sem, r, dst):
    return pltpu.make_async_copy(h_ref.at[pl.ds(r, 1)], xs_hbm.at[pl.ds(dst, 1)], sem)


def _zero_tile_copy(zero_ref, xs_hbm, sem, start):
    return pltpu.make_async_copy(zero_ref, xs_hbm.at[pl.ds(pl.multiple_of(start, MOE_TILE), MOE_TILE)], sem)


def _dispatch_kernel(pad_end_ref, padded_ref, dest_hbm, h_ref, xs_hbm, idx_smem, zero_ref, sem_idx, sem_rows, sem_zero):
    i = pl.program_id(0)
    fetch = pltpu.make_async_copy(dest_hbm.at[pl.ds(i * INDEX_SLICE, INDEX_SLICE)], idx_smem, sem_idx)
    fetch.start()

    @pl.when(i == 0)
    def _():
        zero_ref[...] = jnp.zeros_like(zero_ref)
        for e in range(N_EXPERTS):
            @pl.when(padded_ref[e] > 0)
            def _():
                _zero_tile_copy(zero_ref, xs_hbm, sem_zero, pad_end_ref[e] - MOE_TILE).start()
        for e in range(N_EXPERTS):
            @pl.when(padded_ref[e] > 0)
            def _():
                _zero_tile_copy(zero_ref, xs_hbm, sem_zero, 0).wait()

    fetch.wait()

    def start(r, carry):
        for kk in range(TOP_K):
            _row_copy_out(h_ref, xs_hbm, sem_rows, r, idx_smem[r * TOP_K + kk]).start(priority=kk % 2)
        return carry

    lax.fori_loop(0, ROW_MOVE_TILE, start, 0, unroll=8)
    for kk in range(TOP_K):
        pltpu.make_async_copy(h_ref, xs_hbm.at[pl.ds(0, ROW_MOVE_TILE)], sem_rows).wait()


def _dispatch(pad_ends, padded, dest_flat, h2, n_rows):
    N = h2.shape[0]
    return pl.pallas_call(
        _dispatch_kernel,
        grid_spec=pltpu.PrefetchScalarGridSpec(
            num_scalar_prefetch=2,
            grid=(N // ROW_MOVE_TILE,),
            in_specs=[pl.BlockSpec(memory_space=pl.ANY),
                      pl.BlockSpec((ROW_MOVE_TILE, D_MODEL), lambda i, pe, pd: (i, 0))],
            out_specs=pl.BlockSpec(memory_space=pl.ANY),
            scratch_shapes=[pltpu.SMEM((INDEX_SLICE,), jnp.int32), pltpu.VMEM((MOE_TILE, D_MODEL), F32),
                            pltpu.SemaphoreType.DMA, pltpu.SemaphoreType.DMA, pltpu.SemaphoreType.DMA]),
        out_shape=jax.ShapeDtypeStruct((n_rows, D_MODEL), F32),
        compiler_params=_params(("arbitrary",)),
        name="moe_dispatch",
    )(pad_ends, padded, dest_flat, h2)


def _expert_kernel(te_ref, nu_ref, xs_ref, wg_ref, bg_ref, wl_ref, bl_ref, wd_ref, bd_ref, ys_ref):
    del te_ref

    @pl.when(pl.program_id(0) < nu_ref[0])
    def _():
        xb = xs_ref[...].astype(BF16)
        glu = jnp.minimum(_dot(xb, wg_ref[0].astype(BF16)) + bg_ref[0], SWIGLU_LIMIT)
        lin = jnp.clip(_dot(xb, wl_ref[0].astype(BF16)) + bl_ref[0], -SWIGLU_LIMIT, SWIGLU_LIMIT)
        act = glu * _sigmoid(SWIGLU_ALPHA * glu) * (lin + 1.0)
        ys_ref[...] = _dot(act.astype(BF16), wd_ref[0].astype(BF16)) + bd_ref[0]


def _experts(tile_expert, n_used, xs, wg, bg, wl, bl, wd, bd):
    n_rows = xs.shape[0]
    n_tiles = n_rows // MOE_TILE
    wspec = pl.BlockSpec((1, D_MODEL, D_MODEL), lambda i, te, nu: (te[i], 0, 0))
    bspec = pl.BlockSpec((1, 1, D_MODEL), lambda i, te, nu: (te[i], 0, 0))
    rows = pl.BlockSpec((MOE_TILE, D_MODEL), lambda i, te, nu: (jnp.minimum(i, nu[0] - 1), 0))
    return pl.pallas_call(
        _expert_kernel,
        grid_spec=pltpu.PrefetchScalarGridSpec(
            num_scalar_prefetch=2,
            grid=(n_tiles,),
            in_specs=[rows, wspec, bspec, wspec, bspec, wspec, bspec],
            out_specs=rows),
        out_shape=jax.ShapeDtypeStruct((n_rows, D_MODEL), F32),
        compiler_params=_params(("arbitrary",)),
        name="moe_experts",
    )(tile_expert, n_used, xs, wg, bg, wl, bl, wd, bd)


COMBINE_TILE = 512
COMBINE_SLICE = COMBINE_TILE * TOP_K


def _combine_kernel(dest_hbm, ys_hbm, gate_ref, x_ref, g2_ref, fg_ref, o_ref, idx_smem, buf_ref, sem_idx, sem_rows):
    i = pl.program_id(0)
    n = pl.num_programs(0)
    cur = i % 2
    nxt = 1 - cur

    def index_fetch(step, slot):
        return pltpu.make_async_copy(dest_hbm.at[pl.ds(step * COMBINE_SLICE, COMBINE_SLICE)], idx_smem.at[slot],
                                     sem_idx.at[slot])

    def issue_gathers(slot):
        def start(r, carry):
            for kk in range(TOP_K):
                pltpu.make_async_copy(ys_hbm.at[pl.ds(idx_smem[slot, r * TOP_K + kk], 1)],
                                      buf_ref.at[slot, kk, pl.ds(r, 1)], sem_rows.at[slot]).start(priority=kk % 2)
            return carry

        lax.fori_loop(0, COMBINE_TILE, start, 0, unroll=8)

    @pl.when(i == 0)
    def _():
        index_fetch(0, 0).start()
        index_fetch(0, 0).wait()
        issue_gathers(0)

        @pl.when(n > 1)
        def _():
            index_fetch(1, 1).start()

    @pl.when(i + 1 < n)
    def _():
        index_fetch(i + 1, nxt).wait()
        issue_gathers(nxt)

        @pl.when(i + 2 < n)
        def _():
            index_fetch(i + 2, cur).start()

    for kk in range(TOP_K):
        pltpu.make_async_copy(ys_hbm.at[pl.ds(0, COMBINE_TILE)], buf_ref.at[cur, kk], sem_rows.at[cur]).wait()

    gates = gate_ref[...]
    moe = gates[:, 0:1] * buf_ref[cur, 0]
    for kk in range(1, TOP_K):
        moe = moe + gates[:, kk:kk + 1] * buf_ref[cur, kk]
    x = x_ref[...] + g2_ref[0] * moe
    ms = jnp.mean(x * x, axis=-1, keepdims=True)
    o_ref[...] = x * lax.rsqrt(ms + EPS) * fg_ref[...]


def _combine(dest_flat, ys, gates, x1, gate2, final_g, S):
    N = x1.shape[0]
    TM = COMBINE_TILE
    tpb = S // TM
    row = lambda i: (i, 0)
    return pl.pallas_call(
        _combine_kernel,
        grid=(N // TM,),
        in_specs=[pl.BlockSpec(memory_space=pl.ANY),
                  pl.BlockSpec(memory_space=pl.ANY),
                  pl.BlockSpec((TM, LANES), row),
                  pl.BlockSpec((TM, D_MODEL), row),
                  pl.BlockSpec((1, 1, D_MODEL), lambda i: (i // tpb, 0, 0)),
                  pl.BlockSpec((1, D_MODEL), lambda i: (0, 0))],
        out_specs=pl.BlockSpec((TM, D_MODEL), row),
        out_shape=jax.ShapeDtypeStruct((N, D_MODEL), F32),
        scratch_shapes=[pltpu.SMEM((2, COMBINE_SLICE), jnp.int32), pltpu.VMEM((2, TOP_K, TM, D_MODEL), F32),
                        pltpu.SemaphoreType.DMA((2,)), pltpu.SemaphoreType.DMA((2,))],
        compiler_params=_params(("arbitrary",)),
        name="moe_combine",
    )(dest_flat, ys, gates, x1, gate2, final_g)


def _pad_lanes(a, offset=0):
    return jnp.pad(a, ((0, 0), (offset, LANES - offset - a.shape[1])))


def kernel(x, c, ada_w, ada_b, norm1_g, norm2_g, w_in, b_gate, conv_w, a_log, dt_bias, gdn_norm_g, w_branch_a,
           diff_lambda, diff_norm_g, w_branch_b, w_out, router_w, router_b, w_glu, b_glu, w_lin, b_lin, w_down,
           b_down, final_g):
    B, S, D = x.shape
    N = B * S
    assert D == D_MODEL and S % GROUP == 0 and ada_w.shape[0] == 1
    x2 = x.reshape(N, D)

    mod = _adaln(c, ada_w[0], ada_b[0])
    shift1, scale1, gate1, shift2, scale2, gate2 = [m.reshape(B, 1, D) for m in jnp.split(mod, 6, axis=-1)]

    wi = w_in[0]
    n_a = 4 * D
    n_small = 4 * N_HEADS
    w_big = jnp.concatenate([wi[:, :n_a], wi[:, n_a + n_small:]], axis=1).astype(BF16)
    w_small = _pad_lanes(wi[:, n_a:n_a + n_small]).astype(BF16)
    alog_row = _pad_lanes(a_log[0].reshape(1, -1), GATE_LANE0)
    dt_row = _pad_lanes(dt_bias[0].reshape(1, -1), GATE_LANE0)
    proj, beta, G, Gl, cdl = _inproj(x2, shift1, scale1, norm1_g, w_big, w_small, alog_row, dt_row, S)
    proj3 = proj.reshape(B, S, N_COL_BLOCKS * D)

    qkv = _gdn_conv(proj3, conv_w[0])
    r3 = lambda a: a.reshape(B, S, LANES)
    nC = S // GDN_CHUNK
    GT = jnp.transpose(r3(G)[:, :, GATE_LANE0:GATE_LANE0 + 2 * N_HEADS], (0, 2, 1)).reshape(B, 2 * N_HEADS, 1, S)
    cd = r3(cdl).reshape(B, nC, GDN_CHUNK, LANES)[:, :, 0, GATE_LANE0:GATE_LANE0 + 2 * N_HEADS]
    cd = jnp.transpose(cd.reshape(B, nC, 2, N_HEADS), (0, 2, 3, 1)).reshape(-1)
    u, w, qd, att, kdt = _gdn_prep(qkv, r3(beta), r3(G), r3(Gl), GT)
    oAf, oAb = _gdn_scan(cd, u, w, qd, att, kdt)

    half = DIFF_DH // 2
    inv_freq = ROPE_THETA ** (-jnp.arange(half, dtype=F32) / half)
    ang = jnp.arange(S, dtype=F32)[:, None] * inv_freq[None, :]
    cos_t = jnp.tile(jnp.cos(ang), (1, 4))
    sin_h = jnp.sin(ang)
    sin_t = jnp.tile(jnp.concatenate([-sin_h, sin_h], axis=1), (1, 2))
    qt, kr, vt = _rope(proj3, cos_t, sin_t)
    lam_a = jnp.pad(_pad_lanes(diff_lambda[0][0::2]), ((0, 6), (0, 0)))
    lam_b = jnp.pad(_pad_lanes(diff_lambda[0][1::2]), ((0, 6), (0, 0)))
    oB = _diff_attn(qt, kr, vt, lam_a, lam_b, diff_norm_g)

    x1 = _merge(oAf.reshape(N, D), oAb.reshape(N, D), proj, oB.reshape(N, D), b_gate, x2, gate1, gdn_norm_g,
                w_branch_a[0].astype(BF16), w_branch_b[0].astype(BF16), w_out[0].astype(BF16), S)

    rw = _pad_lanes(router_w[0])
    rw0 = rw.astype(BF16)
    rw1 = (rw - rw0.astype(F32)).astype(BF16)
    h2, idx, gates, rank, counts = _router(x1, shift2, scale2, norm2_g, rw0, rw1, _pad_lanes(router_b), S)
    cnt = counts[0, :N_EXPERTS].astype(jnp.int32)
    padded = (cnt + MOE_TILE - 1) // MOE_TILE * MOE_TILE
    pad_ends = jnp.cumsum(padded)
    pad_starts = pad_ends - padded
    dest = (pad_starts[idx[:, :TOP_K]] + rank[:, :TOP_K]).reshape(-1)
    n_tiles = -(-(N * TOP_K) // MOE_TILE) + N_EXPERTS
    tile_start = jnp.arange(n_tiles, dtype=jnp.int32) * MOE_TILE
    tile_expert = jnp.sum((tile_start[:, None] >= pad_ends[None, :]).astype(jnp.int32), axis=1)
    tile_expert = jnp.minimum(tile_expert, N_EXPERTS - 1)
    n_used = (pad_ends[N_EXPERTS - 1:] // MOE_TILE).astype(jnp.int32)
    xs = _dispatch(pad_ends.astype(jnp.int32), padded, dest, h2, n_tiles * MOE_TILE)
    ys = _experts(tile_expert, n_used, xs, w_glu[0], b_glu[0][:, None, :], w_lin[0], b_lin[0][:, None, :],
                  w_down[0], b_down[0][:, None, :])
    out = _combine(dest, ys, gates, x1, gate2, final_g.reshape(1, D), S)
    return out.reshape(B, S, D)
```

```python
import functools
import math

import jax
import jax.numpy as jnp
from jax import lax
from jax.experimental import pallas as pl
from jax.experimental.pallas import tpu as pltpu

F32 = jnp.float32
BF16 = jnp.bfloat16

D_MODEL = 1024
EPS = 1e-6
N_HEADS = 8
HEAD_W = 128
GDN_CHUNK = 64
CONV_WIDTH = 5
DIFF_DH = 64
ROPE_THETA = 10000.0
LAM_INIT = 0.8 - 0.6 * math.exp(-0.3 * 0)
N_EXPERTS = 32
TOP_K = 4
SWIGLU_ALPHA = 1.702
SWIGLU_LIMIT = 7.0

LANES = 128
GROUP = 256
CHUNKS_PER_GROUP = GROUP // GDN_CHUNK
CHUNK_SHIFT = GDN_CHUNK.bit_length() - 1
MOE_TILE = 512
ROW_MOVE_TILE = 1024
NEG_INF = float("-inf")

COL_QA, COL_KA, COL_VA, COL_ZA, COL_QB, COL_KB, COL_VB, COL_GA, COL_GB = range(9)
N_COL_BLOCKS = 9


def _params(sem, vmem_mb=48):
    return pltpu.CompilerParams(dimension_semantics=sem, vmem_limit_bytes=vmem_mb * 1024 * 1024)


def _dot(a, b):
    return jnp.dot(a, b, preferred_element_type=F32)


def _dot_nt(a, b):
    return lax.dot_general(a, b, (((1,), (1,)), ((), ())), preferred_element_type=F32)


def _sigmoid(x):
    return 1.0 / (1.0 + jnp.exp(-x))


def _split3(x):
    a = x.astype(BF16)
    r = x - a.astype(F32)
    b = r.astype(BF16)
    c = (r - b.astype(F32)).astype(BF16)
    return a, b, c


def _adaln_kernel(c_ref, w_ref, b_ref, o_ref):
    c = c_ref[...]
    cond = c * _sigmoid(c)
    c0, c1, c2 = _split3(cond)
    w0, w1, w2 = _split3(w_ref[...])
    acc = _dot(c0, w0) + (_dot(c0, w1) + _dot(c1, w0)) + (_dot(c0, w2) + _dot(c1, w1) + _dot(c2, w0))
    o_ref[...] = acc + b_ref[...]


def _adaln(c, ada_w, ada_b):
    B = c.shape[0]
    n = ada_w.shape[1] // D_MODEL
    return pl.pallas_call(
        _adaln_kernel,
        grid=(n,),
        in_specs=[pl.BlockSpec((B, D_MODEL), lambda j: (0, 0)),
                  pl.BlockSpec((D_MODEL, D_MODEL), lambda j: (0, j)),
                  pl.BlockSpec((1, D_MODEL), lambda j: (0, j))],
        out_specs=pl.BlockSpec((B, D_MODEL), lambda j: (0, j)),
        out_shape=jax.ShapeDtypeStruct((B, n * D_MODEL), F32),
        compiler_params=_params(("parallel",)),
        name="adaln",
    )(c, ada_w, ada_b.reshape(1, -1))


def _inproj_kernel(x_ref, sh_ref, sc_ref, g_ref, w_ref, ws_ref, alog_ref, dt_ref,
                   o_ref, beta_ref, gcum_ref, glast_ref, cd_ref, h_ref):
    @pl.when(pl.program_id(1) == 0)
    def _():
        x = x_ref[...]
        ms = jnp.mean(x * x, axis=-1, keepdims=True)
        y = x * lax.rsqrt(ms + EPS) * g_ref[...]
        h = (y * (1.0 + sc_ref[0]) + sh_ref[0]).astype(BF16)
        h_ref[...] = h
        _gdn_gates(_dot(h, ws_ref[...]), alog_ref[...], dt_ref[...], beta_ref, gcum_ref, glast_ref, cd_ref)

    col = pl.multiple_of(pl.program_id(1) * D_MODEL, D_MODEL)
    o_ref[...] = _dot(h_ref[...], w_ref[:, pl.ds(col, D_MODEL)]).astype(o_ref.dtype)


def _inproj(x2, shift, scale, g, w_big, w_small, alog_row, dt_row, S):
    N = x2.shape[0]
    TM = min(1024, S)
    tpb = S // TM
    const = lambda i, j: (0, 0)
    lanes = pl.BlockSpec((TM, LANES), lambda i, j: (i, 0))
    return pl.pallas_call(
        _inproj_kernel,
        grid=(N // TM, N_COL_BLOCKS),
        in_specs=[pl.BlockSpec((TM, D_MODEL), lambda i, j: (i, 0)),
                  pl.BlockSpec((1, 1, D_MODEL), lambda i, j: (i // tpb, 0, 0)),
                  pl.BlockSpec((1, 1, D_MODEL), lambda i, j: (i // tpb, 0, 0)),
                  pl.BlockSpec((1, D_MODEL), const),
                  pl.BlockSpec((D_MODEL, N_COL_BLOCKS * D_MODEL), const, pipeline_mode=pl.Buffered(1)),
                  pl.BlockSpec((D_MODEL, LANES), const),
                  pl.BlockSpec((1, LANES), const),
                  pl.BlockSpec((1, LANES), const)],
        out_specs=[pl.BlockSpec((TM, D_MODEL), lambda i, j: (i, j)), lanes, lanes, lanes, lanes],
        out_shape=[jax.ShapeDtypeStruct((N, N_COL_BLOCKS * D_MODEL), BF16)]
                  + [jax.ShapeDtypeStruct((N, LANES), F32)] * 4,
        scratch_shapes=[pltpu.VMEM((TM, D_MODEL), BF16)],
        compiler_params=_params(("parallel", "arbitrary")),
        name="inproj",
    )(x2, shift, scale, g, w_big, w_small, alog_row, dt_row)


HALO = 16


CONV_BLOCK = 256
CONV_PAD = (CONV_WIDTH - 1) // 2
CONV_TAPS = tuple(j for j in range(CONV_WIDTH) if j != CONV_PAD)


def _conv_shifts():
    r = jnp.arange(CONV_BLOCK)[:, None]
    c = jnp.arange(CONV_BLOCK)[None, :]
    return jnp.stack([c == r + (j - CONV_PAD) for j in CONV_TAPS]).astype(BF16)


def _conv_kernel(cur_ref, prev_ref, next_ref, w_ref, shift_ref, o_ref, ext_ref, *, TR):
    i = pl.program_id(1)
    g = pl.program_id(2)
    last = pl.num_programs(1) - 1
    ext_ref[8:8 + TR, :] = cur_ref[0].astype(F32)
    pv = prev_ref[0].astype(F32)[HALO - 8:HALO]
    nx = next_ref[0].astype(F32)[0:8]
    ext_ref[0:8, :] = jnp.where(i > 0, pv, 0.0)
    ext_ref[TR + 8:TR + 16, :] = jnp.where(i < last, nx, 0.0)
    ones = jnp.ones((HEAD_W, HEAD_W), BF16)
    qscale = jnp.where(g == 0, HEAD_W ** -0.5, 1.0)

    def edge_rows(row0):
        e = ext_ref[8 + row0 - CONV_PAD:16 + row0 - CONV_PAD, :] * w_ref[0:1, :]
        for j in range(1, CONV_WIDTH):
            e = e + ext_ref[8 + row0 - CONV_PAD + j:16 + row0 - CONV_PAD + j, :] * w_ref[j:j + 1, :]
        return e

    for blk in range(TR // CONV_BLOCK):
        r0 = blk * CONV_BLOCK
        rows = slice(r0, r0 + CONV_BLOCK)
        ub = cur_ref[0, rows, :]
        acc = ext_ref[8 + r0:8 + r0 + CONV_BLOCK, :] * w_ref[CONV_PAD:CONV_PAD + 1, :]
        for si, j in enumerate(CONV_TAPS):
            acc = acc + _dot(shift_ref[si], ub) * w_ref[j:j + 1, :]
        acc = jnp.concatenate([edge_rows(r0), acc[8:CONV_BLOCK - 8], edge_rows(r0 + CONV_BLOCK - 8)], axis=0)
        y = acc * _sigmoid(acc)

        for h in range(N_HEADS):
            cols = slice(h * HEAD_W, (h + 1) * HEAD_W)
            yh = y[:, cols]
            ss = _dot((yh * yh).astype(BF16), ones)
            scale = jnp.where(g < 2, lax.rsqrt(ss + EPS) * qscale, 1.0)
            o_ref[0, rows, cols] = (yh * scale).astype(o_ref.dtype)


def _gdn_conv(proj3, conv_w):
    B, S, _ = proj3.shape
    TR = min(512, S)
    nT = S // TR
    rb = TR // HALO
    nH = S // HALO
    return pl.pallas_call(
        functools.partial(_conv_kernel, TR=TR),
        grid=(B, nT, 3),
        in_specs=[pl.BlockSpec((1, TR, D_MODEL), lambda b, i, g: (b, i, g)),
                  pl.BlockSpec((1, HALO, D_MODEL), lambda b, i, g: (b, jnp.maximum(i * rb - 1, 0), g)),
                  pl.BlockSpec((1, HALO, D_MODEL), lambda b, i, g: (b, jnp.minimum((i + 1) * rb, nH - 1), g)),
                  pl.BlockSpec((CONV_WIDTH, D_MODEL), lambda b, i, g: (0, g)),
                  pl.BlockSpec((len(CONV_TAPS), CONV_BLOCK, CONV_BLOCK), lambda b, i, g: (0, 0, 0))],
        out_specs=pl.BlockSpec((1, TR, D_MODEL), lambda b, i, g: (b, i, g)),
        out_shape=jax.ShapeDtypeStruct((B, S, 3 * D_MODEL), BF16),
        scratch_shapes=[pltpu.VMEM((TR + 16, D_MODEL), F32)],
        compiler_params=_params(("parallel", "parallel", "parallel")),
        name="gdn_conv",
    )(proj3, proj3, proj3, conv_w, _conv_shifts())


GATE_LANE0 = 16


def _gdn_gates(x, alog, dt, beta_ref, g_ref, gl_ref, cd_ref):
    lane = lax.broadcasted_iota(jnp.int32, (GROUP, LANES), 1)
    r = lax.broadcasted_iota(jnp.int32, (GROUP, GROUP), 0)
    c = lax.broadcasted_iota(jnp.int32, (GROUP, GROUP), 1)
    same = (r >> CHUNK_SHIFT) == (c >> CHUNK_SHIFT)
    lower = jnp.where(same & (c <= r), 1.0, 0.0).astype(BF16)
    upper = jnp.where(same & (c >= r), 1.0, 0.0).astype(BF16)
    block = jnp.where(same, 1.0, 0.0).astype(BF16)
    beta_ref[...] = _sigmoid(x)
    for gi in range(x.shape[0] // GROUP):
        rows = slice(gi * GROUP, (gi + 1) * GROUP)
        z = x[rows] + dt
        softplus = jnp.maximum(z, 0.0) + jnp.log(1.0 + jnp.exp(-jnp.abs(z)))
        gd = -jnp.exp(alog) * softplus
        gd = jnp.where((lane >= GATE_LANE0) & (lane < GATE_LANE0 + 2 * N_HEADS), gd, 0.0)
        p0, p1, p2 = _split3(gd)
        g_fwd = _dot(lower, p0) + _dot(lower, p1) + _dot(lower, p2)
        g_bwd = _dot(upper, p0) + _dot(upper, p1) + _dot(upper, p2)
        tot = _dot(block, p0) + _dot(block, p1) + _dot(block, p2)
        G = jnp.where(lane < GATE_LANE0 + N_HEADS, g_fwd, g_bwd)
        g_ref[rows, :] = G
        gl_ref[rows, :] = tot - G
        cd_ref[rows, :] = jnp.exp(tot)


PREP_HEADS = 4
(MASK_STRICT_LO, MASK_STRICT_UP, MASK_INCL_LO, MASK_INCL_UP, MASK_EYE, MASK_BLK4, MASK_OFF0) = range(7)
N_MASKS = MASK_OFF0 + (CHUNK_SHIFT - 2)


def _prep_masks():
    r = jnp.arange(GROUP)[:, None]
    c = jnp.arange(GROUP)[None, :]
    same = (r >> CHUNK_SHIFT) == (c >> CHUNK_SHIFT)
    masks = [same & (c < r), same & (c > r), same & (c <= r), same & (c >= r), r == c, (r >> 2) == (c >> 2)]
    for shift in range(2, CHUNK_SHIFT):
        masks.append(((r >> shift) != (c >> shift)) & ((r >> (shift + 1)) == (c >> (shift + 1))))
    return jnp.stack(masks).astype(F32)


def _col(x, l, lane):
    return jnp.broadcast_to(jnp.sum(jnp.where(lane == l, x, 0.0), axis=1, keepdims=True), x.shape)


def _prep_kernel(q_ref, k_ref, v_ref, beta_ref, g_ref, gl_ref, gtf_ref, gtb_ref, mask_ref, bmask_ref,
                 u_ref, w_ref, qd_ref, at_ref, kdt_ref):
    hp = pl.program_id(1)
    lane = lax.broadcasted_iota(jnp.int32, (GROUP, LANES), 1)
    wide = lambda a: jnp.concatenate([a, a], axis=1)
    chains = [(hh, d) for hh in range(PREP_HEADS) for d in range(2)]
    p, rhs = {}, {}
    for hh in range(PREP_HEADS):
        cols = slice(hh * HEAD_W, (hh + 1) * HEAD_W)
        q = q_ref[0, :, cols]
        k = k_ref[0, :, cols]
        qf = q.astype(F32)
        kf = k.astype(F32)
        vf = v_ref[0, :, cols].astype(F32)
        kk = _dot_nt(k, k)
        qk = _dot_nt(q, k)
        for d in range(2):
            lb = d * N_HEADS + hp * PREP_HEADS + hh
            beta_c = _col(beta_ref[0], lb, lane)
            g_c = _col(g_ref[0], GATE_LANE0 + lb, lane)
            eg_c = jnp.exp(g_c)
            egl_c = jnp.exp(_col(gl_ref[0], GATE_LANE0 + lb, lane))
            g_r = (gtf_ref if d == 0 else gtb_ref)[0, hh]
            dec = jnp.exp(jnp.minimum(wide(g_c) - g_r, 0.0))
            p[hh, d] = (-(kk * wide(beta_c)) * dec * mask_ref[MASK_STRICT_LO + d]).astype(BF16)
            att = qk * dec * mask_ref[MASK_INCL_LO + d]
            rhs[hh, d] = jnp.concatenate([vf * beta_c, kf * (beta_c * eg_c)], axis=1).astype(BF16)
            qd_ref[0, d, :, cols] = (qf * eg_c).astype(qd_ref.dtype)
            kdt = (kf * egl_c).T
            for ci in range(CHUNKS_PER_GROUP):
                sl = slice(ci * GDN_CHUNK, (ci + 1) * GDN_CHUNK)
                at_ref[0, d, hh, ci] = att[sl, sl].astype(at_ref.dtype)
                kdt_ref[0, d, hh, ci] = kdt[:, sl].astype(kdt_ref.dtype)
    p4 = {ch: p[ch] * bmask_ref[0] for ch in chains}
    sq = {ch: _dot(p4[ch], p4[ch]).astype(BF16) for ch in chains}
    tb = {ch: bmask_ref[N_MASKS - MASK_BLK4] + p4[ch] for ch in chains}
    tb = {ch: (tb[ch].astype(F32) + _dot(tb[ch], sq[ch])).astype(BF16) for ch in chains}
    for lvl in range(CHUNK_SHIFT - 2):
        x = {ch: _dot(tb[ch], p[ch] * bmask_ref[1 + lvl]).astype(BF16) for ch in chains}
        tb = {ch: (tb[ch].astype(F32) + _dot(x[ch], tb[ch])).astype(BF16) for ch in chains}
    uw = {ch: _dot(tb[ch], rhs[ch]) for ch in chains}
    for hh, d in chains:
        cols = slice(hh * HEAD_W, (hh + 1) * HEAD_W)
        u_ref[0, d, :, cols] = uw[hh, d][:, :HEAD_W].astype(u_ref.dtype)
        w_ref[0, d, :, cols] = uw[hh, d][:, HEAD_W:].astype(w_ref.dtype)


def _gdn_prep(qkv, beta, G, Gl, GT):
    B, S, _ = qkv.shape
    nG = S // GROUP
    nC = S // GDN_CHUNK
    PW = PREP_HEADS * HEAD_W
    nP = N_HEADS // PREP_HEADS
    sm = pl.BlockSpec((1, GROUP, LANES), lambda b, h, g: (b, g, 0))
    big = pl.BlockSpec((1, 2, GROUP, PW), lambda b, h, g: (b, 0, g, h))
    masks = _prep_masks()
    return pl.pallas_call(
        _prep_kernel,
        grid=(B, nP, nG),
        in_specs=[pl.BlockSpec((1, GROUP, PW), lambda b, h, g: (b, g, h)),
                  pl.BlockSpec((1, GROUP, PW), lambda b, h, g: (b, g, nP + h)),
                  pl.BlockSpec((1, GROUP, PW), lambda b, h, g: (b, g, 2 * nP + h)),
                  sm, sm, sm,
                  pl.BlockSpec((1, PREP_HEADS, 1, GROUP), lambda b, h, g: (b, h, 0, g)),
                  pl.BlockSpec((1, PREP_HEADS, 1, GROUP), lambda b, h, g: (b, nP + h, 0, g)),
                  pl.BlockSpec((MASK_EYE, GROUP, GROUP), lambda b, h, g: (0, 0, 0)),
                  pl.BlockSpec((N_MASKS - MASK_BLK4 + 1, GROUP, GROUP), lambda b, h, g: (0, 0, 0))],
        out_specs=[big, big, big,
                   pl.BlockSpec((1, 2, PREP_HEADS, CHUNKS_PER_GROUP, GDN_CHUNK, GDN_CHUNK),
                                lambda b, h, g: (b, 0, h, g, 0, 0)),
                   pl.BlockSpec((1, 2, PREP_HEADS, CHUNKS_PER_GROUP, HEAD_W, GDN_CHUNK),
                                lambda b, h, g: (b, 0, h, g, 0, 0))],
        out_shape=[jax.ShapeDtypeStruct((B, 2, S, D_MODEL), BF16)] * 3
                  + [jax.ShapeDtypeStruct((B, 2, N_HEADS, nC, GDN_CHUNK, GDN_CHUNK), BF16),
                     jax.ShapeDtypeStruct((B, 2, N_HEADS, nC, HEAD_W, GDN_CHUNK), BF16)],
        compiler_params=_params(("parallel", "parallel", "parallel")),
        name="gdn_prep",
    )(qkv, qkv, qkv, beta, G, Gl, GT, GT, masks[:MASK_EYE],
      jnp.concatenate([masks[MASK_BLK4:], masks[MASK_EYE:MASK_EYE + 1]]).astype(BF16))


def _scan_kernel(cd_ref, uf_ref, wf_ref, qdf_ref, atf_ref, kdtf_ref, ub_ref, wb_ref, qdb_ref, atb_ref, kdtb_ref,
                 of_ref, ob_ref, state_ref, *, nc, nC):
    b = pl.program_id(0)
    t = pl.program_id(1)
    nT = pl.num_programs(1)

    @pl.when(t == 0)
    def _():
        state_ref[...] = jnp.zeros_like(state_ref)

    dirs = ((uf_ref, wf_ref, qdf_ref, atf_ref, kdtf_ref, of_ref), (ub_ref, wb_ref, qdb_ref, atb_ref, kdtb_ref, ob_ref))

    def chunk(ci, carry):
        work = []
        for d, refs in enumerate(dirs):
            c = ci if d == 0 else nc - 1 - ci
            tt = t if d == 0 else nT - 1 - t
            row = pl.multiple_of(c * GDN_CHUNK, GDN_CHUNK)
            for h in range(N_HEADS):
                work.append((d, h, c, row, ((b * 2 + d) * N_HEADS + h) * nC + tt * nc + c, refs))
        s_old = [state_ref[d, h] for d, h, *_ in work]
        sb = [s.astype(BF16) for s in s_old]
        tile = lambda ref, row, h: ref[0, 0, pl.ds(row, GDN_CHUNK), h * HEAD_W:(h + 1) * HEAD_W]
        ws = [_dot(tile(refs[1], row, h), sb[i]) for i, (d, h, c, row, gi, refs) in enumerate(work)]
        qs = [_dot(tile(refs[2], row, h), sb[i]) for i, (d, h, c, row, gi, refs) in enumerate(work)]
        vb = [(tile(refs[0], row, h).astype(F32) - ws[i]).astype(BF16)
              for i, (d, h, c, row, gi, refs) in enumerate(work)]
        o = [qs[i] + _dot(refs[3][0, 0, h, c], vb[i]) for i, (d, h, c, row, gi, refs) in enumerate(work)]
        upd = [_dot(refs[4][0, 0, h, c], vb[i]) for i, (d, h, c, row, gi, refs) in enumerate(work)]
        for i, (d, h, c, row, gi, refs) in enumerate(work):
            state_ref[d, h] = s_old[i] * cd_ref[gi] + upd[i]
            refs[5][0, pl.ds(row, GDN_CHUNK), h * HEAD_W:(h + 1) * HEAD_W] = o[i].astype(of_ref.dtype)
        return carry

    lax.fori_loop(0, nc, chunk, 0)


def _gdn_scan(cd, u, w, qd, att, kdt):
    B, _, S, _ = u.shape
    TC = min(512, S)
    nT = S // TC
    nc = TC // GDN_CHUNK
    nC = S // GDN_CHUNK
    fwd = lambda b, t: t
    bwd = lambda b, t: nT - 1 - t

    def specs(d, tm):
        big = pl.BlockSpec((1, 1, TC, D_MODEL), lambda b, t: (b, d, tm(b, t), 0))
        return [big, big, big,
                pl.BlockSpec((1, 1, N_HEADS, nc, GDN_CHUNK, GDN_CHUNK), lambda b, t: (b, d, 0, tm(b, t), 0, 0)),
                pl.BlockSpec((1, 1, N_HEADS, nc, HEAD_W, GDN_CHUNK), lambda b, t: (b, d, 0, tm(b, t), 0, 0))]

    return pl.pallas_call(
        functools.partial(_scan_kernel, nc=nc, nC=nC),
        grid=(B, nT),
        in_specs=[pl.BlockSpec(memory_space=pltpu.SMEM)] + specs(0, fwd) + specs(1, bwd),
        out_specs=[pl.BlockSpec((1, TC, D_MODEL), lambda b, t: (b, t, 0)),
                   pl.BlockSpec((1, TC, D_MODEL), lambda b, t: (b, nT - 1 - t, 0))],
        out_shape=[jax.ShapeDtypeStruct((B, S, D_MODEL), BF16)] * 2,
        scratch_shapes=[pltpu.VMEM((2, N_HEADS, HEAD_W, HEAD_W), F32)],
        compiler_params=_params(("parallel", "arbitrary")),
        name="gdn_scan",
    )(cd, u, w, qd, att, kdt, u, w, qd, att, kdt)


ATT_TQ = 256
ATT_TK = 256
ATT_NQ = 4
ATT_VROWS = HEAD_W + 16
LOG2E = 1.4426950408889634


def _rope_kernel(q_ref, k_ref, v_ref, cos_ref, sin_ref, qt_ref, kr_ref, vt_ref, *, TR):
    cs = cos_ref[...]
    sn = sin_ref[...]
    lane = lax.broadcasted_iota(jnp.int32, cs.shape, 1)
    first_half = (lane & (DIFF_DH - 1)) < (DIFF_DH // 2)
    qscale = DIFF_DH ** -0.5 * LOG2E

    def rot(x):
        partner = jnp.where(first_half, pltpu.roll(x, HEAD_W - DIFF_DH // 2, 1), pltpu.roll(x, DIFF_DH // 2, 1))
        return x * cs + partner * sn

    for h in range(N_HEADS):
        cols = slice(h * HEAD_W, (h + 1) * HEAD_W)
        qr = rot(q_ref[0, :, cols].astype(F32)) * qscale
        kr_ref[0, :, cols] = rot(k_ref[0, :, cols].astype(F32)).astype(kr_ref.dtype)
        vf = v_ref[0, :, cols].astype(F32)
        for ci in range(TR // ATT_TK):
            rows = slice(ci * ATT_TK, (ci + 1) * ATT_TK)
            vt_ref[0, h, ci, 0:HEAD_W, :] = vf[rows].T.astype(vt_ref.dtype)
            vt_ref[0, h, ci, HEAD_W:ATT_VROWS, :] = jnp.ones((ATT_VROWS - HEAD_W, ATT_TK), vt_ref.dtype)
        for ci in range(TR // ATT_TQ):
            rows = slice(ci * ATT_TQ, (ci + 1) * ATT_TQ)
            qt_ref[0, h, ci] = qr[rows].T.astype(qt_ref.dtype)


def _rope(proj3, cos_t, sin_t):
    B, S, _ = proj3.shape
    TR = min(512, S)
    tab = pl.BlockSpec((TR, HEAD_W), lambda b, i: (i, 0))
    col = lambda cb: pl.BlockSpec((1, TR, D_MODEL), lambda b, i: (b, i, cb))
    return pl.pallas_call(
        functools.partial(_rope_kernel, TR=TR),
        grid=(B, S // TR),
        in_specs=[col(COL_QB), col(COL_KB), col(COL_VB), tab, tab],
        out_specs=[pl.BlockSpec((1, N_HEADS, TR // ATT_TQ, HEAD_W, ATT_TQ), lambda b, i: (b, 0, i, 0, 0)),
                   pl.BlockSpec((1, TR, D_MODEL), lambda b, i: (b, i, 0)),
                   pl.BlockSpec((1, N_HEADS, TR // ATT_TK, ATT_VROWS, ATT_TK), lambda b, i: (b, 0, i, 0, 0))],
        out_shape=[jax.ShapeDtypeStruct((B, N_HEADS, S // ATT_TQ, HEAD_W, ATT_TQ), BF16),
                   jax.ShapeDtypeStruct((B, S, D_MODEL), BF16),
                   jax.ShapeDtypeStruct((B, N_HEADS, S // ATT_TK, ATT_VROWS, ATT_TK), BF16)],
        compiler_params=_params(("parallel", "parallel")),
        name="rope",
    )(proj3, proj3, proj3, cos_t, sin_t)


def _attn_kernel(qt_ref, k_ref, vt_ref, la_ref, lb_ref, g_ref, o_ref, s_ref, acc_ref, *, n_chunks):
    row = lax.broadcasted_iota(jnp.int32, (HEAD_W, ATT_TQ), 0)
    qw = []
    for qb in range(ATT_NQ):
        qt = qt_ref[0, 0, qb]
        zero = jnp.zeros_like(qt)
        qw.append((jnp.where(row < DIFF_DH, qt, zero), jnp.where(row >= DIFF_DH, qt, zero)))
    chains = [(qb, comp) for qb in range(ATT_NQ) for comp in range(2)]

    def scores(j):
        kc = k_ref[0, pl.ds(pl.multiple_of(j * ATT_TK, ATT_TK), ATT_TK), :]
        return [_dot(kc, qw[qb][comp]) for qb, comp in chains]

    acc_ref[...] = jnp.zeros_like(acc_ref)
    for (qb, comp), s0 in zip(chains, scores(0)):
        s_ref[qb, comp] = s0

    def chunk(j, carry):
        s_next = scores(jnp.minimum(j + 1, n_chunks - 1))
        vt = vt_ref[0, 0, j]
        out = []
        for ci, (qb, comp) in enumerate(chains):
            m_prev = carry[ci]
            s = s_ref[qb, comp]
            m_new = jnp.maximum(m_prev, jnp.max(s, axis=0, keepdims=True))
            alpha = jnp.exp2(m_prev - m_new)
            p = jnp.exp2(s - m_new)
            out.append(m_new)
            acc_ref[qb, comp] = alpha * acc_ref[qb, comp] + _dot(vt, p.astype(BF16))
        for (qb, comp), sn in zip(chains, s_next):
            s_ref[qb, comp] = sn
        return tuple(out)

    neg = jnp.full((1, ATT_TQ), NEG_INF, F32)
    lax.fori_loop(0, n_chunks, chunk, (neg,) * len(chains), unroll=8)

    sums = jnp.sum(la_ref[...] * lb_ref[...], axis=1, keepdims=True)
    lrow = lax.broadcasted_iota(jnp.int32, sums.shape, 0)
    sign = jnp.where(lrow == 0, 1.0, jnp.where(lrow == 1, -1.0, 0.0))
    lam = jnp.sum(sign * jnp.exp(sums), axis=0, keepdims=True) + LAM_INIT
    for qb in range(ATT_NQ):
        l0 = acc_ref[qb, 0, HEAD_W:HEAD_W + 1, :]
        l1 = acc_ref[qb, 1, HEAD_W:HEAD_W + 1, :]
        ot = acc_ref[qb, 0, 0:HEAD_W, :] / l0 - lam * (acc_ref[qb, 1, 0:HEAD_W, :] / l1)
        ms = jnp.mean(ot * ot, axis=0, keepdims=True)
        y = (ot * lax.rsqrt(ms + EPS)).T * g_ref[...] * (1.0 - LAM_INIT)
        o_ref[0, qb * ATT_TQ:(qb + 1) * ATT_TQ, :] = y.astype(o_ref.dtype)


def _diff_attn(qt, kr, vt, lam_a, lam_b, norm_g):
    B, S, _ = kr.shape
    lam_spec = pl.BlockSpec((8, LANES), lambda b, h, qi: (0, 0))
    return pl.pallas_call(
        functools.partial(_attn_kernel, n_chunks=S // ATT_TK),
        grid=(B, N_HEADS, S // (ATT_NQ * ATT_TQ)),
        in_specs=[pl.BlockSpec((1, 1, ATT_NQ, HEAD_W, ATT_TQ), lambda b, h, qi: (b, h, qi, 0, 0)),
                  pl.BlockSpec((1, S, HEAD_W), lambda b, h, qi: (b, 0, h)),
                  pl.BlockSpec((1, 1, S // ATT_TK, ATT_VROWS, ATT_TK), lambda b, h, qi: (b, h, 0, 0, 0)),
                  lam_spec, lam_spec,
                  pl.BlockSpec((1, HEAD_W), lambda b, h, qi: (0, 0))],
        out_specs=pl.BlockSpec((1, ATT_NQ * ATT_TQ, HEAD_W), lambda b, h, qi: (b, qi, h)),
        out_shape=jax.ShapeDtypeStruct((B, S, D_MODEL), BF16),
        scratch_shapes=[pltpu.VMEM((ATT_NQ, 2, ATT_TK, ATT_TQ), F32), pltpu.VMEM((ATT_NQ, 2, ATT_VROWS, ATT_TQ), F32)],
        compiler_params=_params(("parallel", "parallel", "parallel")),
        name="diff_attn",
    )(qt, kr, vt, lam_a, lam_b, norm_g)


def _merge_kernel(of_ref, ob_ref, z_ref, oB_ref, ga_ref, gb_ref, bga_ref, bgb_ref, x_ref, g1_ref, gn_ref,
                  wa_ref, wb_ref, wo_ref, o_ref, ya_ref):
    oa = of_ref[...].astype(F32) + ob_ref[...].astype(F32)
    z = z_ref[...].astype(F32)
    gate = z * _sigmoid(z)
    for h in range(N_HEADS):
        cols = slice(h * HEAD_W, (h + 1) * HEAD_W)
        oh = oa[:, cols]
        ms = jnp.mean(oh * oh, axis=-1, keepdims=True)
        ya_ref[:, cols] = (oh * lax.rsqrt(ms + EPS) * gn_ref[...] * gate[:, cols]).astype(BF16)
    y_a = _dot(ya_ref[...], wa_ref[...])
    y_b = _dot(oB_ref[...], wb_ref[...])
    gate_a = _sigmoid(ga_ref[...].astype(F32) + bga_ref[...])
    gate_b = _sigmoid(gb_ref[...].astype(F32) + bgb_ref[...])
    mix = _dot((gate_a * y_a + gate_b * y_b).astype(BF16), wo_ref[...])
    o_ref[...] = x_ref[...] + g1_ref[0] * mix


def _merge(oAf, oAb, proj, oB, b_gate, x2, gate1, gn, wa, wb, wo, S):
    N = x2.shape[0]
    TM = min(512, S)
    tpb = S // TM
    row = lambda i: (i, 0)
    full = pl.BlockSpec((D_MODEL, D_MODEL), lambda i: (0, 0))
    return pl.pallas_call(
        _merge_kernel,
        grid=(N // TM,),
        in_specs=[pl.BlockSpec((TM, D_MODEL), row),
                  pl.BlockSpec((TM, D_MODEL), row),
                  pl.BlockSpec((TM, D_MODEL), lambda i: (i, COL_ZA)),
                  pl.BlockSpec((TM, D_MODEL), row),
                  pl.BlockSpec((TM, D_MODEL), lambda i: (i, COL_GA)),
                  pl.BlockSpec((TM, D_MODEL), lambda i: (i, COL_GB)),
                  pl.BlockSpec((1, D_MODEL), lambda i: (0, 0)),
                  pl.BlockSpec((1, D_MODEL), lambda i: (0, 1)),
                  pl.BlockSpec((TM, D_MODEL), row),
                  pl.BlockSpec((1, 1, D_MODEL), lambda i: (i // tpb, 0, 0)),
                  pl.BlockSpec((1, HEAD_W), lambda i: (0, 0)),
                  full, full, full],
        out_specs=pl.BlockSpec((TM, D_MODEL), row),
        out_shape=jax.ShapeDtypeStruct((N, D_MODEL), F32),
        scratch_shapes=[pltpu.VMEM((TM, D_MODEL), BF16)],
        compiler_params=_params(("parallel",)),
        name="merge",
    )(oAf, oAb, proj, oB, proj, proj, b_gate, b_gate, x2, gate1, gn, wa, wb, wo)


def _router_kernel(x_ref, sh_ref, sc_ref, g_ref, rw0_ref, rw1_ref, rb_ref, tri_ref,
                   h_ref, idx_ref, gate_ref, rank_ref, cnt_ref, base_ref):
    i = pl.program_id(0)

    @pl.when(i == 0)
    def _():
        base_ref[...] = jnp.zeros_like(base_ref)

    x = x_ref[...]
    ms = jnp.mean(x * x, axis=-1, keepdims=True)
    h = x * lax.rsqrt(ms + EPS) * g_ref[...] * (1.0 + sc_ref[0]) + sh_ref[0]
    h_ref[...] = h
    h0 = h.astype(BF16)
    h1 = (h - h0.astype(F32)).astype(BF16)
    logits = _dot(h0, rw0_ref[...]) + (_dot(h0, rw1_ref[...]) + _dot(h1, rw0_ref[...])) + rb_ref[...]
    lane = lax.broadcasted_iota(jnp.int32, logits.shape, 1)
    lane_f = lane.astype(F32)
    cur = jnp.where(lane < N_EXPERTS, logits, NEG_INF)
    vals, sel = [], []
    for _ in range(TOP_K):
        m = jnp.max(cur, axis=1, keepdims=True)
        ix = jnp.min(jnp.where(cur == m, lane_f, float(LANES)), axis=1, keepdims=True)
        hit = lane_f == ix
        vals.append(m)
        sel.append(hit)
        cur = jnp.where(hit, NEG_INF, cur)
    exps = [jnp.exp(v - vals[0]) for v in vals]
    den = exps[0] + exps[1] + exps[2] + exps[3]
    onehot = jnp.zeros(logits.shape, F32)
    for hit in sel:
        onehot = onehot + jnp.where(hit, 1.0, 0.0)
    before = _dot(tri_ref[...], onehot.astype(BF16)) + base_ref[...]
    idx_out = jnp.zeros(logits.shape, F32)
    gate_out = jnp.zeros(logits.shape, F32)
    rank_out = jnp.zeros(logits.shape, F32)
    for kk in range(TOP_K):
        slot = lane == kk
        e_id = jnp.sum(jnp.where(sel[kk], lane_f, 0.0), axis=1, keepdims=True)
        rk = jnp.sum(jnp.where(sel[kk], before, 0.0), axis=1, keepdims=True)
        idx_out = jnp.where(slot, e_id, idx_out)
        gate_out = jnp.where(slot, exps[kk] / den, gate_out)
        rank_out = jnp.where(slot, rk, rank_out)
    idx_ref[...] = idx_out.astype(jnp.int32)
    gate_ref[...] = gate_out
    rank_ref[...] = rank_out.astype(jnp.int32)
    base_ref[...] = base_ref[...] + jnp.sum(onehot, axis=0, keepdims=True)
    cnt_ref[...] = base_ref[...]


def _router(x1, shift, scale, g, rw0, rw1, rb, S):
    N = x1.shape[0]
    TM = min(512, S)
    tpb = S // TM
    r = jnp.arange(TM)
    tri = (r[None, :] < r[:, None]).astype(BF16)
    row = lambda i: (i, 0)
    const = lambda i: (0, 0)
    lanes = pl.BlockSpec((TM, LANES), row)
    return pl.pallas_call(
        _router_kernel,
        grid=(N // TM,),
        in_specs=[pl.BlockSpec((TM, D_MODEL), row),
                  pl.BlockSpec((1, 1, D_MODEL), lambda i: (i // tpb, 0, 0)),
                  pl.BlockSpec((1, 1, D_MODEL), lambda i: (i // tpb, 0, 0)),
                  pl.BlockSpec((1, D_MODEL), const),
                  pl.BlockSpec((D_MODEL, LANES), const),
                  pl.BlockSpec((D_MODEL, LANES), const),
                  pl.BlockSpec((1, LANES), const),
                  pl.BlockSpec((TM, TM), const)],
        out_specs=[pl.BlockSpec((TM, D_MODEL), row), lanes, lanes, lanes, pl.BlockSpec((1, LANES), const)],
        out_shape=[jax.ShapeDtypeStruct((N, D_MODEL), F32),
                   jax.ShapeDtypeStruct((N, LANES), jnp.int32),
                   jax.ShapeDtypeStruct((N, LANES), F32),
                   jax.ShapeDtypeStruct((N, LANES), jnp.int32),
                   jax.ShapeDtypeStruct((1, LANES), F32)],
        scratch_shapes=[pltpu.VMEM((1, LANES), F32)],
        compiler_params=_params(("arbitrary",)),
        name="router",
    )(x1, shift, scale, g, rw0, rw1, rb, tri)


INDEX_SLICE = ROW_MOVE_TILE * TOP_K


def _row_copy_out(h_ref, xs_hbm, sem, r, dst):
    return pltpu.make_async_copy(h_ref.at[pl.ds(r, 1)], xs_hbm.at[pl.ds(dst, 1)], sem)


def _zero_tile_copy(zero_ref, xs_hbm, sem, start):
    return pltpu.make_async_copy(zero_ref, xs_hbm.at[pl.ds(pl.multiple_of(start, MOE_TILE), MOE_TILE)], sem)


def _dispatch_kernel(pad_end_ref, padded_ref, dest_hbm, h_ref, xs_hbm, idx_smem, zero_ref, sem_idx, sem_rows, sem_zero):
    i = pl.program_id(0)
    fetch = pltpu.make_async_copy(dest_hbm.at[pl.ds(i * INDEX_SLICE, INDEX_SLICE)], idx_smem, sem_idx)
    fetch.start()

    @pl.when(i == 0)
    def _():
        zero_ref[...] = jnp.zeros_like(zero_ref)
        for e in range(N_EXPERTS):
            @pl.when(padded_ref[e] > 0)
            def _():
                _zero_tile_copy(zero_ref, xs_hbm, sem_zero, pad_end_ref[e] - MOE_TILE).start()
        for e in range(N_EXPERTS):
            @pl.when(padded_ref[e] > 0)
            def _():
                _zero_tile_copy(zero_ref, xs_hbm, sem_zero, 0).wait()

    fetch.wait()

    def start(r, carry):
        for kk in range(TOP_K):
            _row_copy_out(h_ref, xs_hbm, sem_rows, r, idx_smem[r * TOP_K + kk]).start(priority=kk % 2)
        return carry

    lax.fori_loop(0, ROW_MOVE_TILE, start, 0, unroll=8)
    for kk in range(TOP_K):
        pltpu.make_async_copy(h_ref, xs_hbm.at[pl.ds(0, ROW_MOVE_TILE)], sem_rows).wait()


def _dispatch(pad_ends, padded, dest_flat, h2, n_rows):
    N = h2.shape[0]
    return pl.pallas_call(
        _dispatch_kernel,
        grid_spec=pltpu.PrefetchScalarGridSpec(
            num_scalar_prefetch=2,
            grid=(N // ROW_MOVE_TILE,),
            in_specs=[pl.BlockSpec(memory_space=pl.ANY),
                      pl.BlockSpec((ROW_MOVE_TILE, D_MODEL), lambda i, pe, pd: (i, 0))],
            out_specs=pl.BlockSpec(memory_space=pl.ANY),
            scratch_shapes=[pltpu.SMEM((INDEX_SLICE,), jnp.int32), pltpu.VMEM((MOE_TILE, D_MODEL), F32),
                            pltpu.SemaphoreType.DMA, pltpu.SemaphoreType.DMA, pltpu.SemaphoreType.DMA]),
        out_shape=jax.ShapeDtypeStruct((n_rows, D_MODEL), F32),
        compiler_params=_params(("arbitrary",)),
        name="moe_dispatch",
    )(pad_ends, padded, dest_flat, h2)


def _expert_kernel(te_ref, nu_ref, xs_ref, wg_ref, bg_ref, wl_ref, bl_ref, wd_ref, bd_ref, ys_ref):
    del te_ref

    @pl.when(pl.program_id(0) < nu_ref[0])
    def _():
        xb = xs_ref[...].astype(BF16)
        glu = jnp.minimum(_dot(xb, wg_ref[0].astype(BF16)) + bg_ref[0], SWIGLU_LIMIT)
        lin = jnp.clip(_dot(xb, wl_ref[0].astype(BF16)) + bl_ref[0], -SWIGLU_LIMIT, SWIGLU_LIMIT)
        act = glu * _sigmoid(SWIGLU_ALPHA * glu) * (lin + 1.0)
        ys_ref[...] = _dot(act.astype(BF16), wd_ref[0].astype(BF16)) + bd_ref[0]


def _experts(tile_expert, n_used, xs, wg, bg, wl, bl, wd, bd):
    n_rows = xs.shape[0]
    n_tiles = n_rows // MOE_TILE
    wspec = pl.BlockSpec((1, D_MODEL, D_MODEL), lambda i, te, nu: (te[i], 0, 0))
    bspec = pl.BlockSpec((1, 1, D_MODEL), lambda i, te, nu: (te[i], 0, 0))
    rows = pl.BlockSpec((MOE_TILE, D_MODEL), lambda i, te, nu: (jnp.minimum(i, nu[0] - 1), 0))
    return pl.pallas_call(
        _expert_kernel,
        grid_spec=pltpu.PrefetchScalarGridSpec(
            num_scalar_prefetch=2,
            grid=(n_tiles,),
            in_specs=[rows, wspec, bspec, wspec, bspec, wspec, bspec],
            out_specs=rows),
        out_shape=jax.ShapeDtypeStruct((n_rows, D_MODEL), F32),
        compiler_params=_params(("arbitrary",)),
        name="moe_experts",
    )(tile_expert, n_used, xs, wg, bg, wl, bl, wd, bd)


def _row_copy_in(ys_hbm, buf_ref, sem, src, kk, r):
    return pltpu.make_async_copy(ys_hbm.at[pl.ds(src, 1)], buf_ref.at[kk, pl.ds(r, 1)], sem)


def _combine_kernel(dest_hbm, ys_hbm, gate_ref, x_ref, g2_ref, fg_ref, o_ref, idx_smem, buf_ref, sem_idx, sem_rows):
    i = pl.program_id(0)
    fetch = pltpu.make_async_copy(dest_hbm.at[pl.ds(i * INDEX_SLICE, INDEX_SLICE)], idx_smem, sem_idx)
    fetch.start()
    fetch.wait()

    def start(r, carry):
        for kk in range(TOP_K):
            _row_copy_in(ys_hbm, buf_ref, sem_rows, idx_smem[r * TOP_K + kk], kk, r).start(priority=kk % 2)
        return carry

    lax.fori_loop(0, ROW_MOVE_TILE, start, 0, unroll=8)
    for kk in range(TOP_K):
        pltpu.make_async_copy(ys_hbm.at[pl.ds(0, ROW_MOVE_TILE)], buf_ref.at[kk], sem_rows).wait()

    gates = gate_ref[...]
    moe = gates[:, 0:1] * buf_ref[0]
    for kk in range(1, TOP_K):
        moe = moe + gates[:, kk:kk + 1] * buf_ref[kk]
    x = x_ref[...] + g2_ref[0] * moe
    ms = jnp.mean(x * x, axis=-1, keepdims=True)
    o_ref[...] = x * lax.rsqrt(ms + EPS) * fg_ref[...]


def _combine(dest_flat, ys, gates, x1, gate2, final_g, S):
    N = x1.shape[0]
    TM = ROW_MOVE_TILE
    tpb = S // TM
    row = lambda i: (i, 0)
    return pl.pallas_call(
        _combine_kernel,
        grid=(N // TM,),
        in_specs=[pl.BlockSpec(memory_space=pl.ANY),
                  pl.BlockSpec(memory_space=pl.ANY),
                  pl.BlockSpec((TM, LANES), row),
                  pl.BlockSpec((TM, D_MODEL), row),
                  pl.BlockSpec((1, 1, D_MODEL), lambda i: (i // tpb, 0, 0)),
                  pl.BlockSpec((1, D_MODEL), lambda i: (0, 0))],
        out_specs=pl.BlockSpec((TM, D_MODEL), row),
        out_shape=jax.ShapeDtypeStruct((N, D_MODEL), F32),
        scratch_shapes=[pltpu.SMEM((INDEX_SLICE,), jnp.int32), pltpu.VMEM((TOP_K, TM, D_MODEL), F32),
                        pltpu.SemaphoreType.DMA, pltpu.SemaphoreType.DMA],
        compiler_params=_params(("arbitrary",)),
        name="moe_combine",
    )(dest_flat, ys, gates, x1, gate2, final_g)


def _pad_lanes(a, offset=0):
    return jnp.pad(a, ((0, 0), (offset, LANES - offset - a.shape[1])))


def kernel(x, c, ada_w, ada_b, norm1_g, norm2_g, w_in, b_gate, conv_w, a_log, dt_bias, gdn_norm_g, w_branch_a,
           diff_lambda, diff_norm_g, w_branch_b, w_out, router_w, router_b, w_glu, b_glu, w_lin, b_lin, w_down,
           b_down, final_g):
    B, S, D = x.shape
    N = B * S
    assert D == D_MODEL and S % GROUP == 0 and ada_w.shape[0] == 1
    x2 = x.reshape(N, D)

    mod = _adaln(c, ada_w[0], ada_b[0])
    shift1, scale1, gate1, shift2, scale2, gate2 = [m.reshape(B, 1, D) for m in jnp.split(mod, 6, axis=-1)]

    wi = w_in[0]
    n_a = 4 * D
    n_small = 4 * N_HEADS
    w_big = jnp.concatenate([wi[:, :n_a], wi[:, n_a + n_small:]], axis=1).astype(BF16)
    w_small = _pad_lanes(wi[:, n_a:n_a + n_small]).astype(BF16)
    alog_row = _pad_lanes(a_log[0].reshape(1, -1), GATE_LANE0)
    dt_row = _pad_lanes(dt_bias[0].reshape(1, -1), GATE_LANE0)
    proj, beta, G, Gl, cdl = _inproj(x2, shift1, scale1, norm1_g, w_big, w_small, alog_row, dt_row, S)
    proj3 = proj.reshape(B, S, N_COL_BLOCKS * D)

    qkv = _gdn_conv(proj3, conv_w[0])
    r3 = lambda a: a.reshape(B, S, LANES)
    nC = S // GDN_CHUNK
    GT = jnp.transpose(r3(G)[:, :, GATE_LANE0:GATE_LANE0 + 2 * N_HEADS], (0, 2, 1)).reshape(B, 2 * N_HEADS, 1, S)
    cd = r3(cdl).reshape(B, nC, GDN_CHUNK, LANES)[:, :, 0, GATE_LANE0:GATE_LANE0 + 2 * N_HEADS]
    cd = jnp.transpose(cd.reshape(B, nC, 2, N_HEADS), (0, 2, 3, 1)).reshape(-1)
    u, w, qd, att, kdt = _gdn_prep(qkv, r3(beta), r3(G), r3(Gl), GT)
    oAf, oAb = _gdn_scan(cd, u, w, qd, att, kdt)

    half = DIFF_DH // 2
    inv_freq = ROPE_THETA ** (-jnp.arange(half, dtype=F32) / half)
    ang = jnp.arange(S, dtype=F32)[:, None] * inv_freq[None, :]
    cos_t = jnp.tile(jnp.cos(ang), (1, 4))
    sin_h = jnp.sin(ang)
    sin_t = jnp.tile(jnp.concatenate([-sin_h, sin_h], axis=1), (1, 2))
    qt, kr, vt = _rope(proj3, cos_t, sin_t)
    lam_a = jnp.pad(_pad_lanes(diff_lambda[0][0::2]), ((0, 6), (0, 0)))
    lam_b = jnp.pad(_pad_lanes(diff_lambda[0][1::2]), ((0, 6), (0, 0)))
    oB = _diff_attn(qt, kr, vt, lam_a, lam_b, diff_norm_g)

    x1 = _merge(oAf.reshape(N, D), oAb.reshape(N, D), proj, oB.reshape(N, D), b_gate, x2, gate1, gdn_norm_g,
                w_branch_a[0].astype(BF16), w_branch_b[0].astype(BF16), w_out[0].astype(BF16), S)

    rw = _pad_lanes(router_w[0])
    rw0 = rw.astype(BF16)
    rw1 = (rw - rw0.astype(F32)).astype(BF16)
    h2, idx, gates, rank, counts = _router(x1, shift2, scale2, norm2_g, rw0, rw1, _pad_lanes(router_b), S)
    cnt = counts[0, :N_EXPERTS].astype(jnp.int32)
    padded = (cnt + MOE_TILE - 1) // MOE_TILE * MOE_TILE
    pad_ends = jnp.cumsum(padded)
    pad_starts = pad_ends - padded
    dest = (pad_starts[idx[:, :TOP_K]] + rank[:, :TOP_K]).reshape(-1)
    n_tiles = -(-(N * TOP_K) // MOE_TILE) + N_EXPERTS
    tile_start = jnp.arange(n_tiles, dtype=jnp.int32) * MOE_TILE
    tile_expert = jnp.sum((tile_start[:, None] >= pad_ends[None, :]).astype(jnp.int32), axis=1)
    tile_expert = jnp.minimum(tile_expert, N_EXPERTS - 1)
    n_used = (pad_ends[N_EXPERTS - 1:] // MOE_TILE).astype(jnp.int32)
    xs = _dispatch(pad_ends.astype(jnp.int32), padded, dest, h2, n_tiles * MOE_TILE)
    ys = _experts(tile_expert, n_used, xs, w_glu[0], b_glu[0][:, None, :], w_lin[0], b_lin[0][:, None, :],
                  w_down[0], b_down[0][:, None, :])
    out = _combine(dest, ys, gates, x1, gate2, final_g.reshape(1, D), S)
    return out.reshape(B, S, D)
```

```python
import functools
import math

import jax
import jax.numpy as jnp
from jax import lax
from jax.experimental import pallas as pl
from jax.experimental.pallas import tpu as pltpu

F32 = jnp.float32
BF16 = jnp.bfloat16

D_MODEL = 1024
EPS = 1e-6
N_HEADS = 8
HEAD_W = 128
GDN_CHUNK = 64
CONV_WIDTH = 5
DIFF_DH = 64
ROPE_THETA = 10000.0
LAM_INIT = 0.8 - 0.6 * math.exp(-0.3 * 0)
N_EXPERTS = 32
TOP_K = 4
SWIGLU_ALPHA = 1.702
SWIGLU_LIMIT = 7.0

LANES = 128
GROUP = 256
CHUNKS_PER_GROUP = GROUP // GDN_CHUNK
CHUNK_SHIFT = GDN_CHUNK.bit_length() - 1
MOE_TILE = 512
ROW_MOVE_TILE = 1024
NEG_INF = float("-inf")

COL_QA, COL_KA, COL_VA, COL_ZA, COL_QB, COL_KB, COL_VB, COL_GA, COL_GB = range(9)
N_COL_BLOCKS = 9


def _params(sem, vmem_mb=48):
    return pltpu.CompilerParams(dimension_semantics=sem, vmem_limit_bytes=vmem_mb * 1024 * 1024)


def _dot(a, b):
    return jnp.dot(a, b, preferred_element_type=F32)


def _dot_nt(a, b):
    return lax.dot_general(a, b, (((1,), (1,)), ((), ())), preferred_element_type=F32)


def _sigmoid(x):
    return 1.0 / (1.0 + jnp.exp(-x))


def _split3(x):
    a = x.astype(BF16)
    r = x - a.astype(F32)
    b = r.astype(BF16)
    c = (r - b.astype(F32)).astype(BF16)
    return a, b, c


def _adaln_kernel(c_ref, w_ref, b_ref, o_ref):
    c = c_ref[...]
    cond = c * _sigmoid(c)
    c0, c1, c2 = _split3(cond)
    w0, w1, w2 = _split3(w_ref[...])
    acc = _dot(c0, w0) + (_dot(c0, w1) + _dot(c1, w0)) + (_dot(c0, w2) + _dot(c1, w1) + _dot(c2, w0))
    o_ref[...] = acc + b_ref[...]


def _adaln(c, ada_w, ada_b):
    B = c.shape[0]
    n = ada_w.shape[1] // D_MODEL
    return pl.pallas_call(
        _adaln_kernel,
        grid=(n,),
        in_specs=[pl.BlockSpec((B, D_MODEL), lambda j: (0, 0)),
                  pl.BlockSpec((D_MODEL, D_MODEL), lambda j: (0, j)),
                  pl.BlockSpec((1, D_MODEL), lambda j: (0, j))],
        out_specs=pl.BlockSpec((B, D_MODEL), lambda j: (0, j)),
        out_shape=jax.ShapeDtypeStruct((B, n * D_MODEL), F32),
        compiler_params=_params(("parallel",)),
        name="adaln",
    )(c, ada_w, ada_b.reshape(1, -1))


def _inproj_kernel(x_ref, sh_ref, sc_ref, g_ref, w_ref, ws_ref, alog_ref, dt_ref,
                   o_ref, beta_ref, gcum_ref, glast_ref, cd_ref, h_ref):
    @pl.when(pl.program_id(1) == 0)
    def _():
        x = x_ref[...]
        ms = jnp.mean(x * x, axis=-1, keepdims=True)
        y = x * lax.rsqrt(ms + EPS) * g_ref[...]
        h = (y * (1.0 + sc_ref[0]) + sh_ref[0]).astype(BF16)
        h_ref[...] = h
        _gdn_gates(_dot(h, ws_ref[...]), alog_ref[...], dt_ref[...], beta_ref, gcum_ref, glast_ref, cd_ref)

    col = pl.multiple_of(pl.program_id(1) * D_MODEL, D_MODEL)
    o_ref[...] = _dot(h_ref[...], w_ref[:, pl.ds(col, D_MODEL)]).astype(o_ref.dtype)


def _inproj(x2, shift, scale, g, w_big, w_small, alog_row, dt_row, S):
    N = x2.shape[0]
    TM = min(1024, S)
    tpb = S // TM
    const = lambda i, j: (0, 0)
    lanes = pl.BlockSpec((TM, LANES), lambda i, j: (i, 0))
    return pl.pallas_call(
        _inproj_kernel,
        grid=(N // TM, N_COL_BLOCKS),
        in_specs=[pl.BlockSpec((TM, D_MODEL), lambda i, j: (i, 0)),
                  pl.BlockSpec((1, 1, D_MODEL), lambda i, j: (i // tpb, 0, 0)),
                  pl.BlockSpec((1, 1, D_MODEL), lambda i, j: (i // tpb, 0, 0)),
                  pl.BlockSpec((1, D_MODEL), const),
                  pl.BlockSpec((D_MODEL, N_COL_BLOCKS * D_MODEL), const, pipeline_mode=pl.Buffered(1)),
                  pl.BlockSpec((D_MODEL, LANES), const),
                  pl.BlockSpec((1, LANES), const),
                  pl.BlockSpec((1, LANES), const)],
        out_specs=[pl.BlockSpec((TM, D_MODEL), lambda i, j: (i, j)), lanes, lanes, lanes, lanes],
        out_shape=[jax.ShapeDtypeStruct((N, N_COL_BLOCKS * D_MODEL), BF16)]
                  + [jax.ShapeDtypeStruct((N, LANES), F32)] * 4,
        scratch_shapes=[pltpu.VMEM((TM, D_MODEL), BF16)],
        compiler_params=_params(("parallel", "arbitrary")),
        name="inproj",
    )(x2, shift, scale, g, w_big, w_small, alog_row, dt_row)


HALO = 16


CONV_BLOCK = 256
CONV_PAD = (CONV_WIDTH - 1) // 2
CONV_TAPS = tuple(j for j in range(CONV_WIDTH) if j != CONV_PAD)


def _conv_shifts():
    r = jnp.arange(CONV_BLOCK)[:, None]
    c = jnp.arange(CONV_BLOCK)[None, :]
    return jnp.stack([c == r + (j - CONV_PAD) for j in CONV_TAPS]).astype(BF16)


def _conv_kernel(cur_ref, prev_ref, next_ref, w_ref, shift_ref, o_ref, ext_ref, *, TR):
    i = pl.program_id(1)
    g = pl.program_id(2)
    last = pl.num_programs(1) - 1
    ext_ref[8:8 + TR, :] = cur_ref[0].astype(F32)
    pv = prev_ref[0].astype(F32)[HALO - 8:HALO]
    nx = next_ref[0].astype(F32)[0:8]
    ext_ref[0:8, :] = jnp.where(i > 0, pv, 0.0)
    ext_ref[TR + 8:TR + 16, :] = jnp.where(i < last, nx, 0.0)
    ones = jnp.ones((HEAD_W, HEAD_W), BF16)
    qscale = jnp.where(g == 0, HEAD_W ** -0.5, 1.0)

    def edge_rows(row0):
        e = ext_ref[8 + row0 - CONV_PAD:16 + row0 - CONV_PAD, :] * w_ref[0:1, :]
        for j in range(1, CONV_WIDTH):
            e = e + ext_ref[8 + row0 - CONV_PAD + j:16 + row0 - CONV_PAD + j, :] * w_ref[j:j + 1, :]
        return e

    for blk in range(TR // CONV_BLOCK):
        r0 = blk * CONV_BLOCK
        rows = slice(r0, r0 + CONV_BLOCK)
        ub = cur_ref[0, rows, :]
        acc = ext_ref[8 + r0:8 + r0 + CONV_BLOCK, :] * w_ref[CONV_PAD:CONV_PAD + 1, :]
        for si, j in enumerate(CONV_TAPS):
            acc = acc + _dot(shift_ref[si], ub) * w_ref[j:j + 1, :]
        acc = jnp.concatenate([edge_rows(r0), acc[8:CONV_BLOCK - 8], edge_rows(r0 + CONV_BLOCK - 8)], axis=0)
        y = acc * _sigmoid(acc)

        for h in range(N_HEADS):
            cols = slice(h * HEAD_W, (h + 1) * HEAD_W)
            yh = y[:, cols]
            ss = _dot((yh * yh).astype(BF16), ones)
            scale = jnp.where(g < 2, lax.rsqrt(ss + EPS) * qscale, 1.0)
            o_ref[0, rows, cols] = (yh * scale).astype(o_ref.dtype)


def _gdn_conv(proj3, conv_w):
    B, S, _ = proj3.shape
    TR = min(512, S)
    nT = S // TR
    rb = TR // HALO
    nH = S // HALO
    return pl.pallas_call(
        functools.partial(_conv_kernel, TR=TR),
        grid=(B, nT, 3),
        in_specs=[pl.BlockSpec((1, TR, D_MODEL), lambda b, i, g: (b, i, g)),
                  pl.BlockSpec((1, HALO, D_MODEL), lambda b, i, g: (b, jnp.maximum(i * rb - 1, 0), g)),
                  pl.BlockSpec((1, HALO, D_MODEL), lambda b, i, g: (b, jnp.minimum((i + 1) * rb, nH - 1), g)),
                  pl.BlockSpec((CONV_WIDTH, D_MODEL), lambda b, i, g: (0, g)),
                  pl.BlockSpec((len(CONV_TAPS), CONV_BLOCK, CONV_BLOCK), lambda b, i, g: (0, 0, 0))],
        out_specs=pl.BlockSpec((1, TR, D_MODEL), lambda b, i, g: (b, i, g)),
        out_shape=jax.ShapeDtypeStruct((B, S, 3 * D_MODEL), BF16),
        scratch_shapes=[pltpu.VMEM((TR + 16, D_MODEL), F32)],
        compiler_params=_params(("parallel", "parallel", "parallel")),
        name="gdn_conv",
    )(proj3, proj3, proj3, conv_w, _conv_shifts())


GATE_LANE0 = 16


def _gdn_gates(x, alog, dt, beta_ref, g_ref, gl_ref, cd_ref):
    lane = lax.broadcasted_iota(jnp.int32, (GROUP, LANES), 1)
    r = lax.broadcasted_iota(jnp.int32, (GROUP, GROUP), 0)
    c = lax.broadcasted_iota(jnp.int32, (GROUP, GROUP), 1)
    same = (r >> CHUNK_SHIFT) == (c >> CHUNK_SHIFT)
    lower = jnp.where(same & (c <= r), 1.0, 0.0).astype(BF16)
    upper = jnp.where(same & (c >= r), 1.0, 0.0).astype(BF16)
    block = jnp.where(same, 1.0, 0.0).astype(BF16)
    beta_ref[...] = _sigmoid(x)
    for gi in range(x.shape[0] // GROUP):
        rows = slice(gi * GROUP, (gi + 1) * GROUP)
        z = x[rows] + dt
        softplus = jnp.maximum(z, 0.0) + jnp.log(1.0 + jnp.exp(-jnp.abs(z)))
        gd = -jnp.exp(alog) * softplus
        gd = jnp.where((lane >= GATE_LANE0) & (lane < GATE_LANE0 + 2 * N_HEADS), gd, 0.0)
        p0, p1, p2 = _split3(gd)
        g_fwd = _dot(lower, p0) + _dot(lower, p1) + _dot(lower, p2)
        g_bwd = _dot(upper, p0) + _dot(upper, p1) + _dot(upper, p2)
        tot = _dot(block, p0) + _dot(block, p1) + _dot(block, p2)
        G = jnp.where(lane < GATE_LANE0 + N_HEADS, g_fwd, g_bwd)
        g_ref[rows, :] = G
        gl_ref[rows, :] = tot - G
        cd_ref[rows, :] = jnp.exp(tot)


PREP_HEADS = 8
(MASK_STRICT_LO, MASK_STRICT_UP, MASK_INCL_LO, MASK_INCL_UP, MASK_EYE, MASK_BLK4, MASK_OFF0) = range(7)
N_MASKS = MASK_OFF0 + (CHUNK_SHIFT - 2)


def _prep_masks():
    r = jnp.arange(GROUP)[:, None]
    c = jnp.arange(GROUP)[None, :]
    same = (r >> CHUNK_SHIFT) == (c >> CHUNK_SHIFT)
    masks = [same & (c < r), same & (c > r), same & (c <= r), same & (c >= r), r == c, (r >> 2) == (c >> 2)]
    for shift in range(2, CHUNK_SHIFT):
        masks.append(((r >> shift) != (c >> shift)) & ((r >> (shift + 1)) == (c >> (shift + 1))))
    return jnp.stack(masks).astype(F32)


def _col(x, l, lane):
    return jnp.broadcast_to(jnp.sum(jnp.where(lane == l, x, 0.0), axis=1, keepdims=True), x.shape)


def _prep_kernel(q_ref, k_ref, v_ref, beta_ref, g_ref, gl_ref, gtf_ref, gtb_ref, mask_ref, bmask_ref,
                 u_ref, w_ref, qd_ref, at_ref, kdt_ref):
    hp = pl.program_id(1)
    lane = lax.broadcasted_iota(jnp.int32, (GROUP, LANES), 1)
    wide = lambda a: jnp.concatenate([a, a], axis=1)
    chains = [(hh, d) for hh in range(PREP_HEADS) for d in range(2)]
    p, rhs = {}, {}
    for hh in range(PREP_HEADS):
        cols = slice(hh * HEAD_W, (hh + 1) * HEAD_W)
        q = q_ref[0, :, cols]
        k = k_ref[0, :, cols]
        qf = q.astype(F32)
        kf = k.astype(F32)
        vf = v_ref[0, :, cols].astype(F32)
        kk = _dot_nt(k, k)
        qk = _dot_nt(q, k)
        for d in range(2):
            lb = d * N_HEADS + hp * PREP_HEADS + hh
            beta_c = _col(beta_ref[0], lb, lane)
            g_c = _col(g_ref[0], GATE_LANE0 + lb, lane)
            eg_c = jnp.exp(g_c)
            egl_c = jnp.exp(_col(gl_ref[0], GATE_LANE0 + lb, lane))
            g_r = (gtf_ref if d == 0 else gtb_ref)[0, hh]
            dec = jnp.exp(jnp.minimum(wide(g_c) - g_r, 0.0))
            p[hh, d] = (-(kk * wide(beta_c)) * dec * mask_ref[MASK_STRICT_LO + d]).astype(BF16)
            att = qk * dec * mask_ref[MASK_INCL_LO + d]
            rhs[hh, d] = jnp.concatenate([vf * beta_c, kf * (beta_c * eg_c)], axis=1).astype(BF16)
            qd_ref[0, d, :, cols] = (qf * eg_c).astype(qd_ref.dtype)
            kdt = (kf * egl_c).T
            for ci in range(CHUNKS_PER_GROUP):
                sl = slice(ci * GDN_CHUNK, (ci + 1) * GDN_CHUNK)
                at_ref[0, d, hh, ci] = att[sl, sl].astype(at_ref.dtype)
                kdt_ref[0, d, hh, ci] = kdt[:, sl].astype(kdt_ref.dtype)
    p4 = {ch: p[ch] * bmask_ref[0] for ch in chains}
    sq = {ch: _dot(p4[ch], p4[ch]).astype(BF16) for ch in chains}
    tb = {ch: bmask_ref[N_MASKS - MASK_BLK4] + p4[ch] for ch in chains}
    tb = {ch: (tb[ch].astype(F32) + _dot(tb[ch], sq[ch])).astype(BF16) for ch in chains}
    for lvl in range(CHUNK_SHIFT - 2):
        x = {ch: _dot(tb[ch], p[ch] * bmask_ref[1 + lvl]).astype(BF16) for ch in chains}
        tb = {ch: (tb[ch].astype(F32) + _dot(x[ch], tb[ch])).astype(BF16) for ch in chains}
    uw = {ch: _dot(tb[ch], rhs[ch]) for ch in chains}
    for hh, d in chains:
        cols = slice(hh * HEAD_W, (hh + 1) * HEAD_W)
        u_ref[0, d, :, cols] = uw[hh, d][:, :HEAD_W].astype(u_ref.dtype)
        w_ref[0, d, :, cols] = uw[hh, d][:, HEAD_W:].astype(w_ref.dtype)


def _gdn_prep(qkv, beta, G, Gl, GT):
    B, S, _ = qkv.shape
    nG = S // GROUP
    nC = S // GDN_CHUNK
    PW = PREP_HEADS * HEAD_W
    nP = N_HEADS // PREP_HEADS
    sm = pl.BlockSpec((1, GROUP, LANES), lambda b, h, g: (b, g, 0))
    big = pl.BlockSpec((1, 2, GROUP, PW), lambda b, h, g: (b, 0, g, h))
    masks = _prep_masks()
    return pl.pallas_call(
        _prep_kernel,
        grid=(B, nP, nG),
        in_specs=[pl.BlockSpec((1, GROUP, PW), lambda b, h, g: (b, g, h)),
                  pl.BlockSpec((1, GROUP, PW), lambda b, h, g: (b, g, nP + h)),
                  pl.BlockSpec((1, GROUP, PW), lambda b, h, g: (b, g, 2 * nP + h)),
                  sm, sm, sm,
                  pl.BlockSpec((1, PREP_HEADS, 1, GROUP), lambda b, h, g: (b, h, 0, g)),
                  pl.BlockSpec((1, PREP_HEADS, 1, GROUP), lambda b, h, g: (b, nP + h, 0, g)),
                  pl.BlockSpec((MASK_EYE, GROUP, GROUP), lambda b, h, g: (0, 0, 0)),
                  pl.BlockSpec((N_MASKS - MASK_BLK4 + 1, GROUP, GROUP), lambda b, h, g: (0, 0, 0))],
        out_specs=[big, big, big,
                   pl.BlockSpec((1, 2, PREP_HEADS, CHUNKS_PER_GROUP, GDN_CHUNK, GDN_CHUNK),
                                lambda b, h, g: (b, 0, h, g, 0, 0)),
                   pl.BlockSpec((1, 2, PREP_HEADS, CHUNKS_PER_GROUP, HEAD_W, GDN_CHUNK),
                                lambda b, h, g: (b, 0, h, g, 0, 0))],
        out_shape=[jax.ShapeDtypeStruct((B, 2, S, D_MODEL), BF16)] * 3
                  + [jax.ShapeDtypeStruct((B, 2, N_HEADS, nC, GDN_CHUNK, GDN_CHUNK), BF16),
                     jax.ShapeDtypeStruct((B, 2, N_HEADS, nC, HEAD_W, GDN_CHUNK), BF16)],
        compiler_params=_params(("parallel", "parallel", "parallel")),
        name="gdn_prep",
    )(qkv, qkv, qkv, beta, G, Gl, GT, GT, masks[:MASK_EYE],
      jnp.concatenate([masks[MASK_BLK4:], masks[MASK_EYE:MASK_EYE + 1]]).astype(BF16))


def _scan_kernel(cd_ref, uf_ref, wf_ref, qdf_ref, atf_ref, kdtf_ref, ub_ref, wb_ref, qdb_ref, atb_ref, kdtb_ref,
                 of_ref, ob_ref, state_ref, *, nc, nC):
    b = pl.program_id(0)
    t = pl.program_id(1)
    nT = pl.num_programs(1)

    @pl.when(t == 0)
    def _():
        state_ref[...] = jnp.zeros_like(state_ref)

    dirs = ((uf_ref, wf_ref, qdf_ref, atf_ref, kdtf_ref, of_ref), (ub_ref, wb_ref, qdb_ref, atb_ref, kdtb_ref, ob_ref))

    def chunk(ci, carry):
        work = []
        for d, refs in enumerate(dirs):
            c = ci if d == 0 else nc - 1 - ci
            tt = t if d == 0 else nT - 1 - t
            row = pl.multiple_of(c * GDN_CHUNK, GDN_CHUNK)
            for h in range(N_HEADS):
                work.append((d, h, c, row, ((b * 2 + d) * N_HEADS + h) * nC + tt * nc + c, refs))
        s_old = [state_ref[d, h] for d, h, *_ in work]
        sb = [s.astype(BF16) for s in s_old]
        tile = lambda ref, row, h: ref[0, 0, pl.ds(row, GDN_CHUNK), h * HEAD_W:(h + 1) * HEAD_W]
        ws = [_dot(tile(refs[1], row, h), sb[i]) for i, (d, h, c, row, gi, refs) in enumerate(work)]
        qs = [_dot(tile(refs[2], row, h), sb[i]) for i, (d, h, c, row, gi, refs) in enumerate(work)]
        vb = [(tile(refs[0], row, h).astype(F32) - ws[i]).astype(BF16)
              for i, (d, h, c, row, gi, refs) in enumerate(work)]
        o = [qs[i] + _dot(refs[3][0, 0, h, c], vb[i]) for i, (d, h, c, row, gi, refs) in enumerate(work)]
        upd = [_dot(refs[4][0, 0, h, c], vb[i]) for i, (d, h, c, row, gi, refs) in enumerate(work)]
        for i, (d, h, c, row, gi, refs) in enumerate(work):
            state_ref[d, h] = s_old[i] * cd_ref[gi] + upd[i]
            refs[5][0, pl.ds(row, GDN_CHUNK), h * HEAD_W:(h + 1) * HEAD_W] = o[i].astype(of_ref.dtype)
        return carry

    lax.fori_loop(0, nc, chunk, 0)


def _gdn_scan(cd, u, w, qd, att, kdt):
    B, _, S, _ = u.shape
    TC = min(512, S)
    nT = S // TC
    nc = TC // GDN_CHUNK
    nC = S // GDN_CHUNK
    fwd = lambda b, t: t
    bwd = lambda b, t: nT - 1 - t

    def specs(d, tm):
        big = pl.BlockSpec((1, 1, TC, D_MODEL), lambda b, t: (b, d, tm(b, t), 0))
        return [big, big, big,
                pl.BlockSpec((1, 1, N_HEADS, nc, GDN_CHUNK, GDN_CHUNK), lambda b, t: (b, d, 0, tm(b, t), 0, 0)),
                pl.BlockSpec((1, 1, N_HEADS, nc, HEAD_W, GDN_CHUNK), lambda b, t: (b, d, 0, tm(b, t), 0, 0))]

    return pl.pallas_call(
        functools.partial(_scan_kernel, nc=nc, nC=nC),
        grid=(B, nT),
        in_specs=[pl.BlockSpec(memory_space=pltpu.SMEM)] + specs(0, fwd) + specs(1, bwd),
        out_specs=[pl.BlockSpec((1, TC, D_MODEL), lambda b, t: (b, t, 0)),
                   pl.BlockSpec((1, TC, D_MODEL), lambda b, t: (b, nT - 1 - t, 0))],
        out_shape=[jax.ShapeDtypeStruct((B, S, D_MODEL), BF16)] * 2,
        scratch_shapes=[pltpu.VMEM((2, N_HEADS, HEAD_W, HEAD_W), F32)],
        compiler_params=_params(("parallel", "arbitrary")),
        name="gdn_scan",
    )(cd, u, w, qd, att, kdt, u, w, qd, att, kdt)


ATT_TQ = 256
ATT_TK = 256
ATT_NQ = 4
ATT_VROWS = HEAD_W + 16
LOG2E = 1.4426950408889634


def _rope_kernel(q_ref, k_ref, v_ref, cos_ref, sin_ref, qt_ref, kr_ref, vt_ref, *, TR):
    cs = cos_ref[...]
    sn = sin_ref[...]
    lane = lax.broadcasted_iota(jnp.int32, cs.shape, 1)
    first_half = (lane & (DIFF_DH - 1)) < (DIFF_DH // 2)
    qscale = DIFF_DH ** -0.5 * LOG2E

    def rot(x):
        partner = jnp.where(first_half, pltpu.roll(x, HEAD_W - DIFF_DH // 2, 1), pltpu.roll(x, DIFF_DH // 2, 1))
        return x * cs + partner * sn

    for h in range(N_HEADS):
        cols = slice(h * HEAD_W, (h + 1) * HEAD_W)
        qr = rot(q_ref[0, :, cols].astype(F32)) * qscale
        kr_ref[0, :, cols] = rot(k_ref[0, :, cols].astype(F32)).astype(kr_ref.dtype)
        vf = v_ref[0, :, cols].astype(F32)
        for ci in range(TR // ATT_TK):
            rows = slice(ci * ATT_TK, (ci + 1) * ATT_TK)
            vt_ref[0, h, ci, 0:HEAD_W, :] = vf[rows].T.astype(vt_ref.dtype)
            vt_ref[0, h, ci, HEAD_W:ATT_VROWS, :] = jnp.ones((ATT_VROWS - HEAD_W, ATT_TK), vt_ref.dtype)
        for ci in range(TR // ATT_TQ):
            rows = slice(ci * ATT_TQ, (ci + 1) * ATT_TQ)
            qt_ref[0, h, ci] = qr[rows].T.astype(qt_ref.dtype)


def _rope(proj3, cos_t, sin_t):
    B, S, _ = proj3.shape
    TR = min(512, S)
    tab = pl.BlockSpec((TR, HEAD_W), lambda b, i: (i, 0))
    col = lambda cb: pl.BlockSpec((1, TR, D_MODEL), lambda b, i: (b, i, cb))
    return pl.pallas_call(
        functools.partial(_rope_kernel, TR=TR),
        grid=(B, S // TR),
        in_specs=[col(COL_QB), col(COL_KB), col(COL_VB), tab, tab],
        out_specs=[pl.BlockSpec((1, N_HEADS, TR // ATT_TQ, HEAD_W, ATT_TQ), lambda b, i: (b, 0, i, 0, 0)),
                   pl.BlockSpec((1, TR, D_MODEL), lambda b, i: (b, i, 0)),
                   pl.BlockSpec((1, N_HEADS, TR // ATT_TK, ATT_VROWS, ATT_TK), lambda b, i: (b, 0, i, 0, 0))],
        out_shape=[jax.ShapeDtypeStruct((B, N_HEADS, S // ATT_TQ, HEAD_W, ATT_TQ), BF16),
                   jax.ShapeDtypeStruct((B, S, D_MODEL), BF16),
                   jax.ShapeDtypeStruct((B, N_HEADS, S // ATT_TK, ATT_VROWS, ATT_TK), BF16)],
        compiler_params=_params(("parallel", "parallel")),
        name="rope",
    )(proj3, proj3, proj3, cos_t, sin_t)


def _attn_kernel(qt_ref, k_ref, vt_ref, la_ref, lb_ref, g_ref, o_ref, s_ref, acc_ref, *, n_chunks):
    row = lax.broadcasted_iota(jnp.int32, (HEAD_W, ATT_TQ), 0)
    qw = []
    for qb in range(ATT_NQ):
        qt = qt_ref[0, 0, qb]
        zero = jnp.zeros_like(qt)
        qw.append((jnp.where(row < DIFF_DH, qt, zero), jnp.where(row >= DIFF_DH, qt, zero)))
    chains = [(qb, comp) for qb in range(ATT_NQ) for comp in range(2)]

    def scores(j):
        kc = k_ref[0, pl.ds(pl.multiple_of(j * ATT_TK, ATT_TK), ATT_TK), :]
        return [_dot(kc, qw[qb][comp]) for qb, comp in chains]

    acc_ref[...] = jnp.zeros_like(acc_ref)
    for (qb, comp), s0 in zip(chains, scores(0)):
        s_ref[qb, comp] = s0

    def chunk(j, carry):
        s_next = scores(jnp.minimum(j + 1, n_chunks - 1))
        vt = vt_ref[0, 0, j]
        out = []
        for ci, (qb, comp) in enumerate(chains):
            m_prev = carry[ci]
            s = s_ref[qb, comp]
            m_new = jnp.maximum(m_prev, jnp.max(s, axis=0, keepdims=True))
            alpha = jnp.exp2(m_prev - m_new)
            p = jnp.exp2(s - m_new)
            out.append(m_new)
            acc_ref[qb, comp] = alpha * acc_ref[qb, comp] + _dot(vt, p.astype(BF16))
        for (qb, comp), sn in zip(chains, s_next):
            s_ref[qb, comp] = sn
        return tuple(out)

    neg = jnp.full((1, ATT_TQ), NEG_INF, F32)
    lax.fori_loop(0, n_chunks, chunk, (neg,) * len(chains), unroll=8)

    sums = jnp.sum(la_ref[...] * lb_ref[...], axis=1, keepdims=True)
    lrow = lax.broadcasted_iota(jnp.int32, sums.shape, 0)
    sign = jnp.where(lrow == 0, 1.0, jnp.where(lrow == 1, -1.0, 0.0))
    lam = jnp.sum(sign * jnp.exp(sums), axis=0, keepdims=True) + LAM_INIT
    for qb in range(ATT_NQ):
        l0 = acc_ref[qb, 0, HEAD_W:HEAD_W + 1, :]
        l1 = acc_ref[qb, 1, HEAD_W:HEAD_W + 1, :]
        ot = acc_ref[qb, 0, 0:HEAD_W, :] / l0 - lam * (acc_ref[qb, 1, 0:HEAD_W, :] / l1)
        ms = jnp.mean(ot * ot, axis=0, keepdims=True)
        y = (ot * lax.rsqrt(ms + EPS)).T * g_ref[...] * (1.0 - LAM_INIT)
        o_ref[0, qb * ATT_TQ:(qb + 1) * ATT_TQ, :] = y.astype(o_ref.dtype)


def _diff_attn(qt, kr, vt, lam_a, lam_b, norm_g):
    B, S, _ = kr.shape
    lam_spec = pl.BlockSpec((8, LANES), lambda b, h, qi: (0, 0))
    return pl.pallas_call(
        functools.partial(_attn_kernel, n_chunks=S // ATT_TK),
        grid=(B, N_HEADS, S // (ATT_NQ * ATT_TQ)),
        in_specs=[pl.BlockSpec((1, 1, ATT_NQ, HEAD_W, ATT_TQ), lambda b, h, qi: (b, h, qi, 0, 0)),
                  pl.BlockSpec((1, S, HEAD_W), lambda b, h, qi: (b, 0, h)),
                  pl.BlockSpec((1, 1, S // ATT_TK, ATT_VROWS, ATT_TK), lambda b, h, qi: (b, h, 0, 0, 0)),
                  lam_spec, lam_spec,
                  pl.BlockSpec((1, HEAD_W), lambda b, h, qi: (0, 0))],
        out_specs=pl.BlockSpec((1, ATT_NQ * ATT_TQ, HEAD_W), lambda b, h, qi: (b, qi, h)),
        out_shape=jax.ShapeDtypeStruct((B, S, D_MODEL), BF16),
        scratch_shapes=[pltpu.VMEM((ATT_NQ, 2, ATT_TK, ATT_TQ), F32), pltpu.VMEM((ATT_NQ, 2, ATT_VROWS, ATT_TQ), F32)],
        compiler_params=_params(("parallel", "parallel", "parallel")),
        name="diff_attn",
    )(qt, kr, vt, lam_a, lam_b, norm_g)


def _merge_kernel(of_ref, ob_ref, z_ref, oB_ref, ga_ref, gb_ref, bga_ref, bgb_ref, x_ref, g1_ref, gn_ref,
                  wa_ref, wb_ref, wo_ref, o_ref, ya_ref):
    oa = of_ref[...].astype(F32) + ob_ref[...].astype(F32)
    z = z_ref[...].astype(F32)
    gate = z * _sigmoid(z)
    for h in range(N_HEADS):
        cols = slice(h * HEAD_W, (h + 1) * HEAD_W)
        oh = oa[:, cols]
        ms = jnp.mean(oh * oh, axis=-1, keepdims=True)
        ya_ref[:, cols] = (oh * lax.rsqrt(ms + EPS) * gn_ref[...] * gate[:, cols]).astype(BF16)
    y_a = _dot(ya_ref[...], wa_ref[...])
    y_b = _dot(oB_ref[...], wb_ref[...])
    gate_a = _sigmoid(ga_ref[...].astype(F32) + bga_ref[...])
    gate_b = _sigmoid(gb_ref[...].astype(F32) + bgb_ref[...])
    mix = _dot((gate_a * y_a + gate_b * y_b).astype(BF16), wo_ref[...])
    o_ref[...] = x_ref[...] + g1_ref[0] * mix


def _merge(oAf, oAb, proj, oB, b_gate, x2, gate1, gn, wa, wb, wo, S):
    N = x2.shape[0]
    TM = min(512, S)
    tpb = S // TM
    row = lambda i: (i, 0)
    full = pl.BlockSpec((D_MODEL, D_MODEL), lambda i: (0, 0))
    return pl.pallas_call(
        _merge_kernel,
        grid=(N // TM,),
        in_specs=[pl.BlockSpec((TM, D_MODEL), row),
                  pl.BlockSpec((TM, D_MODEL), row),
                  pl.BlockSpec((TM, D_MODEL), lambda i: (i, COL_ZA)),
                  pl.BlockSpec((TM, D_MODEL), row),
                  pl.BlockSpec((TM, D_MODEL), lambda i: (i, COL_GA)),
                  pl.BlockSpec((TM, D_MODEL), lambda i: (i, COL_GB)),
                  pl.BlockSpec((1, D_MODEL), lambda i: (0, 0)),
                  pl.BlockSpec((1, D_MODEL), lambda i: (0, 1)),
                  pl.BlockSpec((TM, D_MODEL), row),
                  pl.BlockSpec((1, 1, D_MODEL), lambda i: (i // tpb, 0, 0)),
                  pl.BlockSpec((1, HEAD_W), lambda i: (0, 0)),
                  full, full, full],
        out_specs=pl.BlockSpec((TM, D_MODEL), row),
        out_shape=jax.ShapeDtypeStruct((N, D_MODEL), F32),
        scratch_shapes=[pltpu.VMEM((TM, D_MODEL), BF16)],
        compiler_params=_params(("parallel",)),
        name="merge",
    )(oAf, oAb, proj, oB, proj, proj, b_gate, b_gate, x2, gate1, gn, wa, wb, wo)


def _router_kernel(x_ref, sh_ref, sc_ref, g_ref, rw0_ref, rw1_ref, rb_ref, tri_ref,
                   h_ref, idx_ref, gate_ref, rank_ref, cnt_ref, base_ref):
    i = pl.program_id(0)

    @pl.when(i == 0)
    def _():
        base_ref[...] = jnp.zeros_like(base_ref)

    x = x_ref[...]
    ms = jnp.mean(x * x, axis=-1, keepdims=True)
    h = x * lax.rsqrt(ms + EPS) * g_ref[...] * (1.0 + sc_ref[0]) + sh_ref[0]
    h_ref[...] = h
    h0 = h.astype(BF16)
    h1 = (h - h0.astype(F32)).astype(BF16)
    logits = _dot(h0, rw0_ref[...]) + (_dot(h0, rw1_ref[...]) + _dot(h1, rw0_ref[...])) + rb_ref[...]
    lane = lax.broadcasted_iota(jnp.int32, logits.shape, 1)
    lane_f = lane.astype(F32)
    cur = jnp.where(lane < N_EXPERTS, logits, NEG_INF)
    vals, sel = [], []
    for _ in range(TOP_K):
        m = jnp.max(cur, axis=1, keepdims=True)
        ix = jnp.min(jnp.where(cur == m, lane_f, float(LANES)), axis=1, keepdims=True)
        hit = lane_f == ix
        vals.append(m)
        sel.append(hit)
        cur = jnp.where(hit, NEG_INF, cur)
    exps = [jnp.exp(v - vals[0]) for v in vals]
    den = exps[0] + exps[1] + exps[2] + exps[3]
    onehot = jnp.zeros(logits.shape, F32)
    for hit in sel:
        onehot = onehot + jnp.where(hit, 1.0, 0.0)
    before = _dot(tri_ref[...], onehot.astype(BF16)) + base_ref[...]
    idx_out = jnp.zeros(logits.shape, F32)
    gate_out = jnp.zeros(logits.shape, F32)
    rank_out = jnp.zeros(logits.shape, F32)
    for kk in range(TOP_K):
        slot = lane == kk
        e_id = jnp.sum(jnp.where(sel[kk], lane_f, 0.0), axis=1, keepdims=True)
        rk = jnp.sum(jnp.where(sel[kk], before, 0.0), axis=1, keepdims=True)
        idx_out = jnp.where(slot, e_id, idx_out)
        gate_out = jnp.where(slot, exps[kk] / den, gate_out)
        rank_out = jnp.where(slot, rk, rank_out)
    idx_ref[...] = idx_out.astype(jnp.int32)
    gate_ref[...] = gate_out
    rank_ref[...] = rank_out.astype(jnp.int32)
    base_ref[...] = base_ref[...] + jnp.sum(onehot, axis=0, keepdims=True)
    cnt_ref[...] = base_ref[...]


def _router(x1, shift, scale, g, rw0, rw1, rb, S):
    N = x1.shape[0]
    TM = min(512, S)
    tpb = S // TM
    r = jnp.arange(TM)
    tri = (r[None, :] < r[:, None]).astype(BF16)
    row = lambda i: (i, 0)
    const = lambda i: (0, 0)
    lanes = pl.BlockSpec((TM, LANES), row)
    return pl.pallas_call(
        _router_kernel,
        grid=(N // TM,),
        in_specs=[pl.BlockSpec((TM, D_MODEL), row),
                  pl.BlockSpec((1, 1, D_MODEL), lambda i: (i // tpb, 0, 0)),
                  pl.BlockSpec((1, 1, D_MODEL), lambda i: (i // tpb, 0, 0)),
                  pl.BlockSpec((1, D_MODEL), const),
                  pl.BlockSpec((D_MODEL, LANES), const),
                  pl.BlockSpec((D_MODEL, LANES), const),
                  pl.BlockSpec((1, LANES), const),
                  pl.BlockSpec((TM, TM), const)],
        out_specs=[pl.BlockSpec((TM, D_MODEL), row), lanes, lanes, lanes, pl.BlockSpec((1, LANES), const)],
        out_shape=[jax.ShapeDtypeStruct((N, D_MODEL), F32),
                   jax.ShapeDtypeStruct((N, LANES), jnp.int32),
                   jax.ShapeDtypeStruct((N, LANES), F32),
                   jax.ShapeDtypeStruct((N, LANES), jnp.int32),
                   jax.ShapeDtypeStruct((1, LANES), F32)],
        scratch_shapes=[pltpu.VMEM((1, LANES), F32)],
        compiler_params=_params(("arbitrary",)),
        name="router",
    )(x1, shift, scale, g, rw0, rw1, rb, tri)


INDEX_SLICE = ROW_MOVE_TILE * TOP_K
DISPATCH_TILE = 2 * ROW_MOVE_TILE
DISPATCH_SLICE = DISPATCH_TILE * TOP_K


def _row_copy_out(h_ref, xs_hbm, sem, r, dst):
    return pltpu.make_async_copy(h_ref.at[pl.ds(r, 1)], xs_hbm.at[pl.ds(dst, 1)], sem)


def _zero_tile_copy(zero_ref, xs_hbm, sem, start):
    return pltpu.make_async_copy(zero_ref, xs_hbm.at[pl.ds(pl.multiple_of(start, MOE_TILE), MOE_TILE)], sem)


def _dispatch_kernel(pad_end_ref, padded_ref, dest_hbm, h_ref, xs_hbm, idx_smem, zero_ref, sem_idx, sem_rows, sem_zero):
    i = pl.program_id(0)
    fetch = pltpu.make_async_copy(dest_hbm.at[pl.ds(i * DISPATCH_SLICE, DISPATCH_SLICE)], idx_smem, sem_idx)
    fetch.start()

    @pl.when(i == 0)
    def _():
        zero_ref[...] = jnp.zeros_like(zero_ref)
        for e in range(N_EXPERTS):
            @pl.when(padded_ref[e] > 0)
            def _():
                _zero_tile_copy(zero_ref, xs_hbm, sem_zero, pad_end_ref[e] - MOE_TILE).start()
        for e in range(N_EXPERTS):
            @pl.when(padded_ref[e] > 0)
            def _():
                _zero_tile_copy(zero_ref, xs_hbm, sem_zero, 0).wait()

    fetch.wait()

    def start(r, carry):
        for kk in range(TOP_K):
            _row_copy_out(h_ref, xs_hbm, sem_rows, r, idx_smem[r * TOP_K + kk]).start(priority=kk % 2)
        return carry

    lax.fori_loop(0, DISPATCH_TILE, start, 0, unroll=8)
    for kk in range(TOP_K):
        pltpu.make_async_copy(h_ref, xs_hbm.at[pl.ds(0, DISPATCH_TILE)], sem_rows).wait()


def _dispatch(pad_ends, padded, dest_flat, h2, n_rows):
    N = h2.shape[0]
    return pl.pallas_call(
        _dispatch_kernel,
        grid_spec=pltpu.PrefetchScalarGridSpec(
            num_scalar_prefetch=2,
            grid=(N // DISPATCH_TILE,),
            in_specs=[pl.BlockSpec(memory_space=pl.ANY),
                      pl.BlockSpec((DISPATCH_TILE, D_MODEL), lambda i, pe, pd: (i, 0))],
            out_specs=pl.BlockSpec(memory_space=pl.ANY),
            scratch_shapes=[pltpu.SMEM((DISPATCH_SLICE,), jnp.int32), pltpu.VMEM((MOE_TILE, D_MODEL), F32),
                            pltpu.SemaphoreType.DMA, pltpu.SemaphoreType.DMA, pltpu.SemaphoreType.DMA]),
        out_shape=jax.ShapeDtypeStruct((n_rows, D_MODEL), F32),
        compiler_params=_params(("arbitrary",)),
        name="moe_dispatch",
    )(pad_ends, padded, dest_flat, h2)


def _expert_kernel(te_ref, nu_ref, xs_ref, wg_ref, bg_ref, wl_ref, bl_ref, wd_ref, bd_ref, ys_ref):
    del te_ref

    @pl.when(pl.program_id(0) < nu_ref[0])
    def _():
        xb = xs_ref[...].astype(BF16)
        glu = jnp.minimum(_dot(xb, wg_ref[0].astype(BF16)) + bg_ref[0], SWIGLU_LIMIT)
        lin = jnp.clip(_dot(xb, wl_ref[0].astype(BF16)) + bl_ref[0], -SWIGLU_LIMIT, SWIGLU_LIMIT)
        act = glu * _sigmoid(SWIGLU_ALPHA * glu) * (lin + 1.0)
        ys_ref[...] = _dot(act.astype(BF16), wd_ref[0].astype(BF16)) + bd_ref[0]


def _experts(tile_expert, n_used, xs, wg, bg, wl, bl, wd, bd):
    n_rows = xs.shape[0]
    n_tiles = n_rows // MOE_TILE
    wspec = pl.BlockSpec((1, D_MODEL, D_MODEL), lambda i, te, nu: (te[i], 0, 0))
    bspec = pl.BlockSpec((1, 1, D_MODEL), lambda i, te, nu: (te[i], 0, 0))
    rows = pl.BlockSpec((MOE_TILE, D_MODEL), lambda i, te, nu: (jnp.minimum(i, nu[0] - 1), 0))
    return pl.pallas_call(
        _expert_kernel,
        grid_spec=pltpu.PrefetchScalarGridSpec(
            num_scalar_prefetch=2,
            grid=(n_tiles,),
            in_specs=[rows, wspec, bspec, wspec, bspec, wspec, bspec],
            out_specs=rows),
        out_shape=jax.ShapeDtypeStruct((n_rows, D_MODEL), F32),
        compiler_params=_params(("arbitrary",)),
        name="moe_experts",
    )(tile_expert, n_used, xs, wg, bg, wl, bl, wd, bd)


def _row_copy_in(ys_hbm, buf_ref, sem, src, kk, r):
    return pltpu.make_async_copy(ys_hbm.at[pl.ds(src, 1)], buf_ref.at[kk, pl.ds(r, 1)], sem)


def _combine_kernel(dest_hbm, ys_hbm, gate_ref, x_ref, g2_ref, fg_ref, o_ref, idx_smem, buf_ref, sem_idx, sem_rows):
    i = pl.program_id(0)
    fetch = pltpu.make_async_copy(dest_hbm.at[pl.ds(i * INDEX_SLICE, INDEX_SLICE)], idx_smem, sem_idx)
    fetch.start()
    fetch.wait()

    def start(r, carry):
        for kk in range(TOP_K):
            _row_copy_in(ys_hbm, buf_ref, sem_rows, idx_smem[r * TOP_K + kk], kk, r).start(priority=kk % 2)
        return carry

    lax.fori_loop(0, ROW_MOVE_TILE, start, 0, unroll=8)
    for kk in range(TOP_K):
        pltpu.make_async_copy(ys_hbm.at[pl.ds(0, ROW_MOVE_TILE)], buf_ref.at[kk], sem_rows).wait()

    gates = gate_ref[...]
    moe = gates[:, 0:1] * buf_ref[0]
    for kk in range(1, TOP_K):
        moe = moe + gates[:, kk:kk + 1] * buf_ref[kk]
    x = x_ref[...] + g2_ref[0] * moe
    ms = jnp.mean(x * x, axis=-1, keepdims=True)
    o_ref[...] = x * lax.rsqrt(ms + EPS) * fg_ref[...]


def _combine(dest_flat, ys, gates, x1, gate2, final_g, S):
    N = x1.shape[0]
    TM = ROW_MOVE_TILE
    tpb = S // TM
    row = lambda i: (i, 0)
    return pl.pallas_call(
        _combine_kernel,
        grid=(N // TM,),
        in_specs=[pl.BlockSpec(memory_space=pl.ANY),
                  pl.BlockSpec(memory_space=pl.ANY),
                  pl.BlockSpec((TM, LANES), row),
                  pl.BlockSpec((TM, D_MODEL), row),
                  pl.BlockSpec((1, 1, D_MODEL), lambda i: (i // tpb, 0, 0)),
                  pl.BlockSpec((1, D_MODEL), lambda i: (0, 0))],
        out_specs=pl.BlockSpec((TM, D_MODEL), row),
        out_shape=jax.ShapeDtypeStruct((N, D_MODEL), F32),
        scratch_shapes=[pltpu.SMEM((INDEX_SLICE,), jnp.int32), pltpu.VMEM((TOP_K, TM, D_MODEL), F32),
                        pltpu.SemaphoreType.DMA, pltpu.SemaphoreType.DMA],
        compiler_params=_params(("arbitrary",)),
        name="moe_combine",
    )(dest_flat, ys, gates, x1, gate2, final_g)


def _pad_lanes(a, offset=0):
    return jnp.pad(a, ((0, 0), (offset, LANES - offset - a.shape[1])))


def kernel(x, c, ada_w, ada_b, norm1_g, norm2_g, w_in, b_gate, conv_w, a_log, dt_bias, gdn_norm_g, w_branch_a,
           diff_lambda, diff_norm_g, w_branch_b, w_out, router_w, router_b, w_glu, b_glu, w_lin, b_lin, w_down,
           b_down, final_g):
    B, S, D = x.shape
    N = B * S
    assert D == D_MODEL and S % GROUP == 0 and ada_w.shape[0] == 1
    x2 = x.reshape(N, D)

    mod = _adaln(c, ada_w[0], ada_b[0])
    shift1, scale1, gate1, shift2, scale2, gate2 = [m.reshape(B, 1, D) for m in jnp.split(mod, 6, axis=-1)]

    wi = w_in[0]
    n_a = 4 * D
    n_small = 4 * N_HEADS
    w_big = jnp.concatenate([wi[:, :n_a], wi[:, n_a + n_small:]], axis=1).astype(BF16)
    w_small = _pad_lanes(wi[:, n_a:n_a + n_small]).astype(BF16)
    alog_row = _pad_lanes(a_log[0].reshape(1, -1), GATE_LANE0)
    dt_row = _pad_lanes(dt_bias[0].reshape(1, -1), GATE_LANE0)
    proj, beta, G, Gl, cdl = _inproj(x2, shift1, scale1, norm1_g, w_big, w_small, alog_row, dt_row, S)
    proj3 = proj.reshape(B, S, N_COL_BLOCKS * D)

    qkv = _gdn_conv(proj3, conv_w[0])
    r3 = lambda a: a.reshape(B, S, LANES)
    nC = S // GDN_CHUNK
    GT = jnp.transpose(r3(G)[:, :, GATE_LANE0:GATE_LANE0 + 2 * N_HEADS], (0, 2, 1)).reshape(B, 2 * N_HEADS, 1, S)
    cd = r3(cdl).reshape(B, nC, GDN_CHUNK, LANES)[:, :, 0, GATE_LANE0:GATE_LANE0 + 2 * N_HEADS]
    cd = jnp.transpose(cd.reshape(B, nC, 2, N_HEADS), (0, 2, 3, 1)).reshape(-1)
    u, w, qd, att, kdt = _gdn_prep(qkv, r3(beta), r3(G), r3(Gl), GT)
    oAf, oAb = _gdn_scan(cd, u, w, qd, att, kdt)

    half = DIFF_DH // 2
    inv_freq = ROPE_THETA ** (-jnp.arange(half, dtype=F32) / half)
    ang = jnp.arange(S, dtype=F32)[:, None] * inv_freq[None, :]
    cos_t = jnp.tile(jnp.cos(ang), (1, 4))
    sin_h = jnp.sin(ang)
    sin_t = jnp.tile(jnp.concatenate([-sin_h, sin_h], axis=1), (1, 2))
    qt, kr, vt = _rope(proj3, cos_t, sin_t)
    lam_a = jnp.pad(_pad_lanes(diff_lambda[0][0::2]), ((0, 6), (0, 0)))
    lam_b = jnp.pad(_pad_lanes(diff_lambda[0][1::2]), ((0, 6), (0, 0)))
    oB = _diff_attn(qt, kr, vt, lam_a, lam_b, diff_norm_g)

    x1 = _merge(oAf.reshape(N, D), oAb.reshape(N, D), proj, oB.reshape(N, D), b_gate, x2, gate1, gdn_norm_g,
                w_branch_a[0].astype(BF16), w_branch_b[0].astype(BF16), w_out[0].astype(BF16), S)

    rw = _pad_lanes(router_w[0])
    rw0 = rw.astype(BF16)
    rw1 = (rw - rw0.astype(F32)).astype(BF16)
    h2, idx, gates, rank, counts = _router(x1, shift2, scale2, norm2_g, rw0, rw1, _pad_lanes(router_b), S)
    cnt = counts[0, :N_EXPERTS].astype(jnp.int32)
    padded = (cnt + MOE_TILE - 1) // MOE_TILE * MOE_TILE
    pad_ends = jnp.cumsum(padded)
    pad_starts = pad_ends - padded
    dest = (pad_starts[idx[:, :TOP_K]] + rank[:, :TOP_K]).reshape(-1)
    n_tiles = -(-(N * TOP_K) // MOE_TILE) + N_EXPERTS
    tile_start = jnp.arange(n_tiles, dtype=jnp.int32) * MOE_TILE
    tile_expert = jnp.sum((tile_start[:, None] >= pad_ends[None, :]).astype(jnp.int32), axis=1)
    tile_expert = jnp.minimum(tile_expert, N_EXPERTS - 1)
    n_used = (pad_ends[N_EXPERTS - 1:] // MOE_TILE).astype(jnp.int32)
    xs = _dispatch(pad_ends.astype(jnp.int32), padded, dest, h2, n_tiles * MOE_TILE)
    ys = _experts(tile_expert, n_used, xs, w_glu[0], b_glu[0][:, None, :], w_lin[0], b_lin[0][:, None, :],
                  w_down[0], b_down[0][:, None, :])
    out = _combine(dest, ys, gates, x1, gate2, final_g.reshape(1, D), S)
    return out.reshape(B, S, D)
```

```python
import functools
import math

import jax
import jax.numpy as jnp
from jax import lax
from jax.experimental import pallas as pl
from jax.experimental.pallas import tpu as pltpu

F32 = jnp.float32
BF16 = jnp.bfloat16

D_MODEL = 1024
EPS = 1e-6
N_HEADS = 8
HEAD_W = 128
GDN_CHUNK = 64
CONV_WIDTH = 5
DIFF_DH = 64
ROPE_THETA = 10000.0
LAM_INIT = 0.8 - 0.6 * math.exp(-0.3 * 0)
N_EXPERTS = 32
TOP_K = 4
SWIGLU_ALPHA = 1.702
SWIGLU_LIMIT = 7.0

LANES = 128
GROUP = 256
CHUNKS_PER_GROUP = GROUP // GDN_CHUNK
CHUNK_SHIFT = GDN_CHUNK.bit_length() - 1
MOE_TILE = 512
ROW_MOVE_TILE = 1024
NEG_INF = float("-inf")

COL_QA, COL_KA, COL_VA, COL_ZA, COL_QB, COL_KB, COL_VB, COL_GA, COL_GB = range(9)
N_COL_BLOCKS = 9


def _params(sem, vmem_mb=48):
    return pltpu.CompilerParams(dimension_semantics=sem, vmem_limit_bytes=vmem_mb * 1024 * 1024)


def _dot(a, b):
    return jnp.dot(a, b, preferred_element_type=F32)


def _dot_nt(a, b):
    return lax.dot_general(a, b, (((1,), (1,)), ((), ())), preferred_element_type=F32)


def _sigmoid(x):
    return 1.0 / (1.0 + jnp.exp(-x))


def _split3(x):
    a = x.astype(BF16)
    r = x - a.astype(F32)
    b = r.astype(BF16)
    c = (r - b.astype(F32)).astype(BF16)
    return a, b, c


def _adaln_kernel(c_ref, w_ref, b_ref, o_ref):
    c = c_ref[...]
    cond = c * _sigmoid(c)
    c0, c1, c2 = _split3(cond)
    w0, w1, w2 = _split3(w_ref[...])
    acc = _dot(c0, w0) + (_dot(c0, w1) + _dot(c1, w0)) + (_dot(c0, w2) + _dot(c1, w1) + _dot(c2, w0))
    o_ref[...] = acc + b_ref[...]


def _adaln(c, ada_w, ada_b):
    B = c.shape[0]
    n = ada_w.shape[1] // D_MODEL
    return pl.pallas_call(
        _adaln_kernel,
        grid=(n,),
        in_specs=[pl.BlockSpec((B, D_MODEL), lambda j: (0, 0)),
                  pl.BlockSpec((D_MODEL, D_MODEL), lambda j: (0, j)),
                  pl.BlockSpec((1, D_MODEL), lambda j: (0, j))],
        out_specs=pl.BlockSpec((B, D_MODEL), lambda j: (0, j)),
        out_shape=jax.ShapeDtypeStruct((B, n * D_MODEL), F32),
        compiler_params=_params(("parallel",)),
        name="adaln",
    )(c, ada_w, ada_b.reshape(1, -1))


def _inproj_kernel(x_ref, sh_ref, sc_ref, g_ref, w_ref, ws_ref, alog_ref, dt_ref,
                   o_ref, beta_ref, gcum_ref, glast_ref, cd_ref, h_ref):
    @pl.when(pl.program_id(1) == 0)
    def _():
        x = x_ref[...]
        ms = jnp.mean(x * x, axis=-1, keepdims=True)
        y = x * lax.rsqrt(ms + EPS) * g_ref[...]
        h = (y * (1.0 + sc_ref[0]) + sh_ref[0]).astype(BF16)
        h_ref[...] = h
        _gdn_gates(_dot(h, ws_ref[...]), alog_ref[...], dt_ref[...], beta_ref, gcum_ref, glast_ref, cd_ref)

    col = pl.multiple_of(pl.program_id(1) * D_MODEL, D_MODEL)
    o_ref[...] = _dot(h_ref[...], w_ref[:, pl.ds(col, D_MODEL)]).astype(o_ref.dtype)


def _inproj(x2, shift, scale, g, w_big, w_small, alog_row, dt_row, S):
    N = x2.shape[0]
    TM = min(1024, S)
    tpb = S // TM
    const = lambda i, j: (0, 0)
    lanes = pl.BlockSpec((TM, LANES), lambda i, j: (i, 0))
    return pl.pallas_call(
        _inproj_kernel,
        grid=(N // TM, N_COL_BLOCKS),
        in_specs=[pl.BlockSpec((TM, D_MODEL), lambda i, j: (i, 0)),
                  pl.BlockSpec((1, 1, D_MODEL), lambda i, j: (i // tpb, 0, 0)),
                  pl.BlockSpec((1, 1, D_MODEL), lambda i, j: (i // tpb, 0, 0)),
                  pl.BlockSpec((1, D_MODEL), const),
                  pl.BlockSpec((D_MODEL, N_COL_BLOCKS * D_MODEL), const, pipeline_mode=pl.Buffered(1)),
                  pl.BlockSpec((D_MODEL, LANES), const),
                  pl.BlockSpec((1, LANES), const),
                  pl.BlockSpec((1, LANES), const)],
        out_specs=[pl.BlockSpec((TM, D_MODEL), lambda i, j: (i, j)), lanes, lanes, lanes, lanes],
        out_shape=[jax.ShapeDtypeStruct((N, N_COL_BLOCKS * D_MODEL), BF16)]
                  + [jax.ShapeDtypeStruct((N, LANES), F32)] * 4,
        scratch_shapes=[pltpu.VMEM((TM, D_MODEL), BF16)],
        compiler_params=_params(("parallel", "arbitrary")),
        name="inproj",
    )(x2, shift, scale, g, w_big, w_small, alog_row, dt_row)


HALO = 16


CONV_BLOCK = 256
CONV_PAD = (CONV_WIDTH - 1) // 2
CONV_TAPS = tuple(j for j in range(CONV_WIDTH) if j != CONV_PAD)


def _conv_shifts():
    r = jnp.arange(CONV_BLOCK)[:, None]
    c = jnp.arange(CONV_BLOCK)[None, :]
    return jnp.stack([c == r + (j - CONV_PAD) for j in CONV_TAPS]).astype(BF16)


def _conv_kernel(cur_ref, prev_ref, next_ref, w_ref, shift_ref, o_ref, ext_ref, *, TR):
    i = pl.program_id(1)
    g = pl.program_id(2)
    last = pl.num_programs(1) - 1
    ext_ref[8:8 + TR, :] = cur_ref[0].astype(F32)
    pv = prev_ref[0].astype(F32)[HALO - 8:HALO]
    nx = next_ref[0].astype(F32)[0:8]
    ext_ref[0:8, :] = jnp.where(i > 0, pv, 0.0)
    ext_ref[TR + 8:TR + 16, :] = jnp.where(i < last, nx, 0.0)
    ones = jnp.ones((HEAD_W, HEAD_W), BF16)
    qscale = jnp.where(g == 0, HEAD_W ** -0.5, 1.0)

    def edge_rows(row0):
        e = ext_ref[8 + row0 - CONV_PAD:16 + row0 - CONV_PAD, :] * w_ref[0:1, :]
        for j in range(1, CONV_WIDTH):
            e = e + ext_ref[8 + row0 - CONV_PAD + j:16 + row0 - CONV_PAD + j, :] * w_ref[j:j + 1, :]
        return e

    for blk in range(TR // CONV_BLOCK):
        r0 = blk * CONV_BLOCK
        rows = slice(r0, r0 + CONV_BLOCK)
        ub = cur_ref[0, rows, :]
        acc = ext_ref[8 + r0:8 + r0 + CONV_BLOCK, :] * w_ref[CONV_PAD:CONV_PAD + 1, :]
        for si, j in enumerate(CONV_TAPS):
            acc = acc + _dot(shift_ref[si], ub) * w_ref[j:j + 1, :]
        acc = jnp.concatenate([edge_rows(r0), acc[8:CONV_BLOCK - 8], edge_rows(r0 + CONV_BLOCK - 8)], axis=0)
        y = acc * _sigmoid(acc)

        for h in range(N_HEADS):
            cols = slice(h * HEAD_W, (h + 1) * HEAD_W)
            yh = y[:, cols]
            ss = _dot((yh * yh).astype(BF16), ones)
            scale = jnp.where(g < 2, lax.rsqrt(ss + EPS) * qscale, 1.0)
            o_ref[0, rows, cols] = (yh * scale).astype(o_ref.dtype)


def _gdn_conv(proj3, conv_w):
    B, S, _ = proj3.shape
    TR = min(1024, S)
    nT = S // TR
    rb = TR // HALO
    nH = S // HALO
    return pl.pallas_call(
        functools.partial(_conv_kernel, TR=TR),
        grid=(B, nT, 3),
        in_specs=[pl.BlockSpec((1, TR, D_MODEL), lambda b, i, g: (b, i, g)),
                  pl.BlockSpec((1, HALO, D_MODEL), lambda b, i, g: (b, jnp.maximum(i * rb - 1, 0), g)),
                  pl.BlockSpec((1, HALO, D_MODEL), lambda b, i, g: (b, jnp.minimum((i + 1) * rb, nH - 1), g)),
                  pl.BlockSpec((CONV_WIDTH, D_MODEL), lambda b, i, g: (0, g)),
                  pl.BlockSpec((len(CONV_TAPS), CONV_BLOCK, CONV_BLOCK), lambda b, i, g: (0, 0, 0))],
        out_specs=pl.BlockSpec((1, TR, D_MODEL), lambda b, i, g: (b, i, g)),
        out_shape=jax.ShapeDtypeStruct((B, S, 3 * D_MODEL), BF16),
        scratch_shapes=[pltpu.VMEM((TR + 16, D_MODEL), F32)],
        compiler_params=_params(("parallel", "parallel", "parallel")),
        name="gdn_conv",
    )(proj3, proj3, proj3, conv_w, _conv_shifts())


GATE_LANE0 = 16


def _gdn_gates(x, alog, dt, beta_ref, g_ref, gl_ref, cd_ref):
    lane = lax.broadcasted_iota(jnp.int32, (GROUP, LANES), 1)
    r = lax.broadcasted_iota(jnp.int32, (GROUP, GROUP), 0)
    c = lax.broadcasted_iota(jnp.int32, (GROUP, GROUP), 1)
    same = (r >> CHUNK_SHIFT) == (c >> CHUNK_SHIFT)
    lower = jnp.where(same & (c <= r), 1.0, 0.0).astype(BF16)
    upper = jnp.where(same & (c >= r), 1.0, 0.0).astype(BF16)
    block = jnp.where(same, 1.0, 0.0).astype(BF16)
    beta_ref[...] = _sigmoid(x)
    for gi in range(x.shape[0] // GROUP):
        rows = slice(gi * GROUP, (gi + 1) * GROUP)
        z = x[rows] + dt
        softplus = jnp.maximum(z, 0.0) + jnp.log(1.0 + jnp.exp(-jnp.abs(z)))
        gd = -jnp.exp(alog) * softplus
        gd = jnp.where((lane >= GATE_LANE0) & (lane < GATE_LANE0 + 2 * N_HEADS), gd, 0.0)
        p0, p1, p2 = _split3(gd)
        g_fwd = _dot(lower, p0) + _dot(lower, p1) + _dot(lower, p2)
        g_bwd = _dot(upper, p0) + _dot(upper, p1) + _dot(upper, p2)
        tot = _dot(block, p0) + _dot(block, p1) + _dot(block, p2)
        G = jnp.where(lane < GATE_LANE0 + N_HEADS, g_fwd, g_bwd)
        g_ref[rows, :] = G
        gl_ref[rows, :] = tot - G
        cd_ref[rows, :] = jnp.exp(tot)


PREP_HEADS = 8
(MASK_STRICT_LO, MASK_STRICT_UP, MASK_INCL_LO, MASK_INCL_UP, MASK_EYE, MASK_BLK4, MASK_OFF0) = range(7)
N_MASKS = MASK_OFF0 + (CHUNK_SHIFT - 2)


def _prep_masks():
    r = jnp.arange(GROUP)[:, None]
    c = jnp.arange(GROUP)[None, :]
    same = (r >> CHUNK_SHIFT) == (c >> CHUNK_SHIFT)
    masks = [same & (c < r), same & (c > r), same & (c <= r), same & (c >= r), r == c, (r >> 2) == (c >> 2)]
    for shift in range(2, CHUNK_SHIFT):
        masks.append(((r >> shift) != (c >> shift)) & ((r >> (shift + 1)) == (c >> (shift + 1))))
    return jnp.stack(masks).astype(F32)


def _col(x, l, lane):
    return jnp.broadcast_to(jnp.sum(jnp.where(lane == l, x, 0.0), axis=1, keepdims=True), x.shape)


def _prep_kernel(q_ref, k_ref, v_ref, beta_ref, g_ref, gl_ref, gtf_ref, gtb_ref, mask_ref, bmask_ref,
                 u_ref, w_ref, qd_ref, at_ref, kdt_ref):
    hp = pl.program_id(1)
    lane = lax.broadcasted_iota(jnp.int32, (GROUP, LANES), 1)
    wide = lambda a: jnp.concatenate([a, a], axis=1)
    chains = [(hh, d) for hh in range(PREP_HEADS) for d in range(2)]
    p, rhs = {}, {}
    for hh in range(PREP_HEADS):
        cols = slice(hh * HEAD_W, (hh + 1) * HEAD_W)
        q = q_ref[0, :, cols]
        k = k_ref[0, :, cols]
        qf = q.astype(F32)
        kf = k.astype(F32)
        vf = v_ref[0, :, cols].astype(F32)
        kk = _dot_nt(k, k)
        qk = _dot_nt(q, k)
        for d in range(2):
            lb = d * N_HEADS + hp * PREP_HEADS + hh
            beta_c = _col(beta_ref[0], lb, lane)
            g_c = _col(g_ref[0], GATE_LANE0 + lb, lane)
            eg_c = jnp.exp(g_c)
            egl_c = jnp.exp(_col(gl_ref[0], GATE_LANE0 + lb, lane))
            g_r = (gtf_ref if d == 0 else gtb_ref)[0, hh]
            dec = jnp.exp(jnp.minimum(wide(g_c) - g_r, 0.0))
            p[hh, d] = (-(kk * wide(beta_c)) * dec * mask_ref[MASK_STRICT_LO + d]).astype(BF16)
            att = qk * dec * mask_ref[MASK_INCL_LO + d]
            rhs[hh, d] = jnp.concatenate([vf * beta_c, kf * (beta_c * eg_c)], axis=1).astype(BF16)
            qd_ref[0, d, :, cols] = (qf * eg_c).astype(qd_ref.dtype)
            kdt = (kf * egl_c).T
            for ci in range(CHUNKS_PER_GROUP):
                sl = slice(ci * GDN_CHUNK, (ci + 1) * GDN_CHUNK)
                at_ref[0, d, hh, ci] = att[sl, sl].astype(at_ref.dtype)
                kdt_ref[0, d, hh, ci] = kdt[:, sl].astype(kdt_ref.dtype)
    p4 = {ch: p[ch] * bmask_ref[0] for ch in chains}
    sq = {ch: _dot(p4[ch], p4[ch]).astype(BF16) for ch in chains}
    tb = {ch: bmask_ref[N_MASKS - MASK_BLK4] + p4[ch] for ch in chains}
    tb = {ch: (tb[ch].astype(F32) + _dot(tb[ch], sq[ch])).astype(BF16) for ch in chains}
    for lvl in range(CHUNK_SHIFT - 2):
        x = {ch: _dot(tb[ch], p[ch] * bmask_ref[1 + lvl]).astype(BF16) for ch in chains}
        tb = {ch: (tb[ch].astype(F32) + _dot(x[ch], tb[ch])).astype(BF16) for ch in chains}
    uw = {ch: _dot(tb[ch], rhs[ch]) for ch in chains}
    for hh, d in chains:
        cols = slice(hh * HEAD_W, (hh + 1) * HEAD_W)
        u_ref[0, d, :, cols] = uw[hh, d][:, :HEAD_W].astype(u_ref.dtype)
        w_ref[0, d, :, cols] = uw[hh, d][:, HEAD_W:].astype(w_ref.dtype)


def _gdn_prep(qkv, beta, G, Gl, GT):
    B, S, _ = qkv.shape
    nG = S // GROUP
    nC = S // GDN_CHUNK
    PW = PREP_HEADS * HEAD_W
    nP = N_HEADS // PREP_HEADS
    sm = pl.BlockSpec((1, GROUP, LANES), lambda b, h, g: (b, g, 0))
    big = pl.BlockSpec((1, 2, GROUP, PW), lambda b, h, g: (b, 0, g, h))
    masks = _prep_masks()
    return pl.pallas_call(
        _prep_kernel,
        grid=(B, nP, nG),
        in_specs=[pl.BlockSpec((1, GROUP, PW), lambda b, h, g: (b, g, h)),
                  pl.BlockSpec((1, GROUP, PW), lambda b, h, g: (b, g, nP + h)),
                  pl.BlockSpec((1, GROUP, PW), lambda b, h, g: (b, g, 2 * nP + h)),
                  sm, sm, sm,
                  pl.BlockSpec((1, PREP_HEADS, 1, GROUP), lambda b, h, g: (b, h, 0, g)),
                  pl.BlockSpec((1, PREP_HEADS, 1, GROUP), lambda b, h, g: (b, nP + h, 0, g)),
                  pl.BlockSpec((MASK_EYE, GROUP, GROUP), lambda b, h, g: (0, 0, 0)),
                  pl.BlockSpec((N_MASKS - MASK_BLK4 + 1, GROUP, GROUP), lambda b, h, g: (0, 0, 0))],
        out_specs=[big, big, big,
                   pl.BlockSpec((1, 2, PREP_HEADS, CHUNKS_PER_GROUP, GDN_CHUNK, GDN_CHUNK),
                                lambda b, h, g: (b, 0, h, g, 0, 0)),
                   pl.BlockSpec((1, 2, PREP_HEADS, CHUNKS_PER_GROUP, HEAD_W, GDN_CHUNK),
                                lambda b, h, g: (b, 0, h, g, 0, 0))],
        out_shape=[jax.ShapeDtypeStruct((B, 2, S, D_MODEL), BF16)] * 3
                  + [jax.ShapeDtypeStruct((B, 2, N_HEADS, nC, GDN_CHUNK, GDN_CHUNK), BF16),
                     jax.ShapeDtypeStruct((B, 2, N_HEADS, nC, HEAD_W, GDN_CHUNK), BF16)],
        compiler_params=_params(("parallel", "parallel", "parallel")),
        name="gdn_prep",
    )(qkv, qkv, qkv, beta, G, Gl, GT, GT, masks[:MASK_EYE],
      jnp.concatenate([masks[MASK_BLK4:], masks[MASK_EYE:MASK_EYE + 1]]).astype(BF16))


def _scan_kernel(cd_ref, uf_ref, wf_ref, qdf_ref, atf_ref, kdtf_ref, ub_ref, wb_ref, qdb_ref, atb_ref, kdtb_ref,
                 of_ref, ob_ref, state_ref, *, nc, nC):
    b = pl.program_id(0)
    t = pl.program_id(1)
    nT = pl.num_programs(1)

    @pl.when(t == 0)
    def _():
        state_ref[...] = jnp.zeros_like(state_ref)

    dirs = ((uf_ref, wf_ref, qdf_ref, atf_ref, kdtf_ref, of_ref), (ub_ref, wb_ref, qdb_ref, atb_ref, kdtb_ref, ob_ref))

    def chunk(ci, carry):
        work = []
        for d, refs in enumerate(dirs):
            c = ci if d == 0 else nc - 1 - ci
            tt = t if d == 0 else nT - 1 - t
            row = pl.multiple_of(c * GDN_CHUNK, GDN_CHUNK)
            for h in range(N_HEADS):
                work.append((d, h, c, row, ((b * 2 + d) * N_HEADS + h) * nC + tt * nc + c, refs))
        s_old = [state_ref[d, h] for d, h, *_ in work]
        sb = [s.astype(BF16) for s in s_old]
        tile = lambda ref, row, h: ref[0, 0, pl.ds(row, GDN_CHUNK), h * HEAD_W:(h + 1) * HEAD_W]
        ws = [_dot(tile(refs[1], row, h), sb[i]) for i, (d, h, c, row, gi, refs) in enumerate(work)]
        qs = [_dot(tile(refs[2], row, h), sb[i]) for i, (d, h, c, row, gi, refs) in enumerate(work)]
        vb = [(tile(refs[0], row, h).astype(F32) - ws[i]).astype(BF16)
              for i, (d, h, c, row, gi, refs) in enumerate(work)]
        o = [qs[i] + _dot(refs[3][0, 0, h, c], vb[i]) for i, (d, h, c, row, gi, refs) in enumerate(work)]
        upd = [_dot(refs[4][0, 0, h, c], vb[i]) for i, (d, h, c, row, gi, refs) in enumerate(work)]
        for i, (d, h, c, row, gi, refs) in enumerate(work):
            state_ref[d, h] = s_old[i] * cd_ref[gi] + upd[i]
            refs[5][0, pl.ds(row, GDN_CHUNK), h * HEAD_W:(h + 1) * HEAD_W] = o[i].astype(of_ref.dtype)
        return carry

    lax.fori_loop(0, nc, chunk, 0)


def _gdn_scan(cd, u, w, qd, att, kdt):
    B, _, S, _ = u.shape
    TC = min(512, S)
    nT = S // TC
    nc = TC // GDN_CHUNK
    nC = S // GDN_CHUNK
    fwd = lambda b, t: t
    bwd = lambda b, t: nT - 1 - t

    def specs(d, tm):
        big = pl.BlockSpec((1, 1, TC, D_MODEL), lambda b, t: (b, d, tm(b, t), 0))
        return [big, big, big,
                pl.BlockSpec((1, 1, N_HEADS, nc, GDN_CHUNK, GDN_CHUNK), lambda b, t: (b, d, 0, tm(b, t), 0, 0)),
                pl.BlockSpec((1, 1, N_HEADS, nc, HEAD_W, GDN_CHUNK), lambda b, t: (b, d, 0, tm(b, t), 0, 0))]

    return pl.pallas_call(
        functools.partial(_scan_kernel, nc=nc, nC=nC),
        grid=(B, nT),
        in_specs=[pl.BlockSpec(memory_space=pltpu.SMEM)] + specs(0, fwd) + specs(1, bwd),
        out_specs=[pl.BlockSpec((1, TC, D_MODEL), lambda b, t: (b, t, 0)),
                   pl.BlockSpec((1, TC, D_MODEL), lambda b, t: (b, nT - 1 - t, 0))],
        out_shape=[jax.ShapeDtypeStruct((B, S, D_MODEL), BF16)] * 2,
        scratch_shapes=[pltpu.VMEM((2, N_HEADS, HEAD_W, HEAD_W), F32)],
        compiler_params=_params(("parallel", "arbitrary")),
        name="gdn_scan",
    )(cd, u, w, qd, att, kdt, u, w, qd, att, kdt)


ATT_TQ = 256
ATT_TK = 256
ATT_NQ = 4
ATT_VROWS = HEAD_W + 16
LOG2E = 1.4426950408889634


def _rope_kernel(q_ref, k_ref, v_ref, cos_ref, sin_ref, qt_ref, kr_ref, vt_ref, *, TR):
    cs = cos_ref[...]
    sn = sin_ref[...]
    lane = lax.broadcasted_iota(jnp.int32, cs.shape, 1)
    first_half = (lane & (DIFF_DH - 1)) < (DIFF_DH // 2)
    qscale = DIFF_DH ** -0.5 * LOG2E

    def rot(x):
        partner = jnp.where(first_half, pltpu.roll(x, HEAD_W - DIFF_DH // 2, 1), pltpu.roll(x, DIFF_DH // 2, 1))
        return x * cs + partner * sn

    for h in range(N_HEADS):
        cols = slice(h * HEAD_W, (h + 1) * HEAD_W)
        qr = rot(q_ref[0, :, cols].astype(F32)) * qscale
        kr_ref[0, :, cols] = rot(k_ref[0, :, cols].astype(F32)).astype(kr_ref.dtype)
        vf = v_ref[0, :, cols].astype(F32)
        for ci in range(TR // ATT_TK):
            rows = slice(ci * ATT_TK, (ci + 1) * ATT_TK)
            vt_ref[0, h, ci, 0:HEAD_W, :] = vf[rows].T.astype(vt_ref.dtype)
            vt_ref[0, h, ci, HEAD_W:ATT_VROWS, :] = jnp.ones((ATT_VROWS - HEAD_W, ATT_TK), vt_ref.dtype)
        for ci in range(TR // ATT_TQ):
            rows = slice(ci * ATT_TQ, (ci + 1) * ATT_TQ)
            qt_ref[0, h, ci] = qr[rows].T.astype(qt_ref.dtype)


def _rope(proj3, cos_t, sin_t):
    B, S, _ = proj3.shape
    TR = min(512, S)
    tab = pl.BlockSpec((TR, HEAD_W), lambda b, i: (i, 0))
    col = lambda cb: pl.BlockSpec((1, TR, D_MODEL), lambda b, i: (b, i, cb))
    return pl.pallas_call(
        functools.partial(_rope_kernel, TR=TR),
        grid=(B, S // TR),
        in_specs=[col(COL_QB), col(COL_KB), col(COL_VB), tab, tab],
        out_specs=[pl.BlockSpec((1, N_HEADS, TR // ATT_TQ, HEAD_W, ATT_TQ), lambda b, i: (b, 0, i, 0, 0)),
                   pl.BlockSpec((1, TR, D_MODEL), lambda b, i: (b, i, 0)),
                   pl.BlockSpec((1, N_HEADS, TR // ATT_TK, ATT_VROWS, ATT_TK), lambda b, i: (b, 0, i, 0, 0))],
        out_shape=[jax.ShapeDtypeStruct((B, N_HEADS, S // ATT_TQ, HEAD_W, ATT_TQ), BF16),
                   jax.ShapeDtypeStruct((B, S, D_MODEL), BF16),
                   jax.ShapeDtypeStruct((B, N_HEADS, S // ATT_TK, ATT_VROWS, ATT_TK), BF16)],
        compiler_params=_params(("parallel", "parallel")),
        name="rope",
    )(proj3, proj3, proj3, cos_t, sin_t)


def _attn_kernel(qt_ref, k_ref, vt_ref, la_ref, lb_ref, g_ref, o_ref, s_ref, acc_ref, *, n_chunks):
    row = lax.broadcasted_iota(jnp.int32, (HEAD_W, ATT_TQ), 0)
    qw = []
    for qb in range(ATT_NQ):
        qt = qt_ref[0, 0, qb]
        zero = jnp.zeros_like(qt)
        qw.append((jnp.where(row < DIFF_DH, qt, zero), jnp.where(row >= DIFF_DH, qt, zero)))
    chains = [(qb, comp) for qb in range(ATT_NQ) for comp in range(2)]

    def scores(j):
        kc = k_ref[0, pl.ds(pl.multiple_of(j * ATT_TK, ATT_TK), ATT_TK), :]
        return [_dot(kc, qw[qb][comp]) for qb, comp in chains]

    acc_ref[...] = jnp.zeros_like(acc_ref)
    for (qb, comp), s0 in zip(chains, scores(0)):
        s_ref[qb, comp] = s0

    def chunk(j, carry):
        s_next = scores(jnp.minimum(j + 1, n_chunks - 1))
        vt = vt_ref[0, 0, j]
        out = []
        for ci, (qb, comp) in enumerate(chains):
            m_prev = carry[ci]
            s = s_ref[qb, comp]
            m_new = jnp.maximum(m_prev, jnp.max(s, axis=0, keepdims=True))
            alpha = jnp.exp2(m_prev - m_new)
            p = jnp.exp2(s - m_new)
            out.append(m_new)
            acc_ref[qb, comp] = alpha * acc_ref[qb, comp] + _dot(vt, p.astype(BF16))
        for (qb, comp), sn in zip(chains, s_next):
            s_ref[qb, comp] = sn
        return tuple(out)

    neg = jnp.full((1, ATT_TQ), NEG_INF, F32)
    lax.fori_loop(0, n_chunks, chunk, (neg,) * len(chains), unroll=8)

    sums = jnp.sum(la_ref[...] * lb_ref[...], axis=1, keepdims=True)
    lrow = lax.broadcasted_iota(jnp.int32, sums.shape, 0)
    sign = jnp.where(lrow == 0, 1.0, jnp.where(lrow == 1, -1.0, 0.0))
    lam = jnp.sum(sign * jnp.exp(sums), axis=0, keepdims=True) + LAM_INIT
    for qb in range(ATT_NQ):
        l0 = acc_ref[qb, 0, HEAD_W:HEAD_W + 1, :]
        l1 = acc_ref[qb, 1, HEAD_W:HEAD_W + 1, :]
        ot = acc_ref[qb, 0, 0:HEAD_W, :] / l0 - lam * (acc_ref[qb, 1, 0:HEAD_W, :] / l1)
        ms = jnp.mean(ot * ot, axis=0, keepdims=True)
        y = (ot * lax.rsqrt(ms + EPS)).T * g_ref[...] * (1.0 - LAM_INIT)
        o_ref[0, qb * ATT_TQ:(qb + 1) * ATT_TQ, :] = y.astype(o_ref.dtype)


def _diff_attn(qt, kr, vt, lam_a, lam_b, norm_g):
    B, S, _ = kr.shape
    lam_spec = pl.BlockSpec((8, LANES), lambda b, h, qi: (0, 0))
    return pl.pallas_call(
        functools.partial(_attn_kernel, n_chunks=S // ATT_TK),
        grid=(B, N_HEADS, S // (ATT_NQ * ATT_TQ)),
        in_specs=[pl.BlockSpec((1, 1, ATT_NQ, HEAD_W, ATT_TQ), lambda b, h, qi: (b, h, qi, 0, 0)),
                  pl.BlockSpec((1, S, HEAD_W), lambda b, h, qi: (b, 0, h)),
                  pl.BlockSpec((1, 1, S // ATT_TK, ATT_VROWS, ATT_TK), lambda b, h, qi: (b, h, 0, 0, 0)),
                  lam_spec, lam_spec,
                  pl.BlockSpec((1, HEAD_W), lambda b, h, qi: (0, 0))],
        out_specs=pl.BlockSpec((1, ATT_NQ * ATT_TQ, HEAD_W), lambda b, h, qi: (b, qi, h)),
        out_shape=jax.ShapeDtypeStruct((B, S, D_MODEL), BF16),
        scratch_shapes=[pltpu.VMEM((ATT_NQ, 2, ATT_TK, ATT_TQ), F32), pltpu.VMEM((ATT_NQ, 2, ATT_VROWS, ATT_TQ), F32)],
        compiler_params=_params(("parallel", "parallel", "parallel")),
        name="diff_attn",
    )(qt, kr, vt, lam_a, lam_b, norm_g)


def _merge_kernel(of_ref, ob_ref, z_ref, oB_ref, ga_ref, gb_ref, bga_ref, bgb_ref, x_ref, g1_ref, gn_ref,
                  wa_ref, wb_ref, wo_ref, o_ref, ya_ref):
    oa = of_ref[...].astype(F32) + ob_ref[...].astype(F32)
    z = z_ref[...].astype(F32)
    gate = z * _sigmoid(z)
    for h in range(N_HEADS):
        cols = slice(h * HEAD_W, (h + 1) * HEAD_W)
        oh = oa[:, cols]
        ms = jnp.mean(oh * oh, axis=-1, keepdims=True)
        ya_ref[:, cols] = (oh * lax.rsqrt(ms + EPS) * gn_ref[...] * gate[:, cols]).astype(BF16)
    y_a = _dot(ya_ref[...], wa_ref[...])
    y_b = _dot(oB_ref[...], wb_ref[...])
    gate_a = _sigmoid(ga_ref[...].astype(F32) + bga_ref[...])
    gate_b = _sigmoid(gb_ref[...].astype(F32) + bgb_ref[...])
    mix = _dot((gate_a * y_a + gate_b * y_b).astype(BF16), wo_ref[...])
    o_ref[...] = x_ref[...] + g1_ref[0] * mix


def _merge(oAf, oAb, proj, oB, b_gate, x2, gate1, gn, wa, wb, wo, S):
    N = x2.shape[0]
    TM = min(512, S)
    tpb = S // TM
    row = lambda i: (i, 0)
    full = pl.BlockSpec((D_MODEL, D_MODEL), lambda i: (0, 0))
    return pl.pallas_call(
        _merge_kernel,
        grid=(N // TM,),
        in_specs=[pl.BlockSpec((TM, D_MODEL), row),
                  pl.BlockSpec((TM, D_MODEL), row),
                  pl.BlockSpec((TM, D_MODEL), lambda i: (i, COL_ZA)),
                  pl.BlockSpec((TM, D_MODEL), row),
                  pl.BlockSpec((TM, D_MODEL), lambda i: (i, COL_GA)),
                  pl.BlockSpec((TM, D_MODEL), lambda i: (i, COL_GB)),
                  pl.BlockSpec((1, D_MODEL), lambda i: (0, 0)),
                  pl.BlockSpec((1, D_MODEL), lambda i: (0, 1)),
                  pl.BlockSpec((TM, D_MODEL), row),
                  pl.BlockSpec((1, 1, D_MODEL), lambda i: (i // tpb, 0, 0)),
                  pl.BlockSpec((1, HEAD_W), lambda i: (0, 0)),
                  full, full, full],
        out_specs=pl.BlockSpec((TM, D_MODEL), row),
        out_shape=jax.ShapeDtypeStruct((N, D_MODEL), F32),
        scratch_shapes=[pltpu.VMEM((TM, D_MODEL), BF16)],
        compiler_params=_params(("parallel",)),
        name="merge",
    )(oAf, oAb, proj, oB, proj, proj, b_gate, b_gate, x2, gate1, gn, wa, wb, wo)


def _router_kernel(x_ref, sh_ref, sc_ref, g_ref, rw0_ref, rw1_ref, rb_ref, tri_ref,
                   h_ref, idx_ref, gate_ref, rank_ref, cnt_ref, base_ref):
    i = pl.program_id(0)

    @pl.when(i == 0)
    def _():
        base_ref[...] = jnp.zeros_like(base_ref)

    x = x_ref[...]
    ms = jnp.mean(x * x, axis=-1, keepdims=True)
    h = x * lax.rsqrt(ms + EPS) * g_ref[...] * (1.0 + sc_ref[0]) + sh_ref[0]
    h_ref[...] = h
    h0 = h.astype(BF16)
    h1 = (h - h0.astype(F32)).astype(BF16)
    logits = _dot(h0, rw0_ref[...]) + (_dot(h0, rw1_ref[...]) + _dot(h1, rw0_ref[...])) + rb_ref[...]
    lane = lax.broadcasted_iota(jnp.int32, logits.shape, 1)
    lane_f = lane.astype(F32)
    cur = jnp.where(lane < N_EXPERTS, logits, NEG_INF)
    vals, sel = [], []
    for _ in range(TOP_K):
        m = jnp.max(cur, axis=1, keepdims=True)
        ix = jnp.min(jnp.where(cur == m, lane_f, float(LANES)), axis=1, keepdims=True)
        hit = lane_f == ix
        vals.append(m)
        sel.append(hit)
        cur = jnp.where(hit, NEG_INF, cur)
    exps = [jnp.exp(v - vals[0]) for v in vals]
    den = exps[0] + exps[1] + exps[2] + exps[3]
    onehot = jnp.zeros(logits.shape, F32)
    for hit in sel:
        onehot = onehot + jnp.where(hit, 1.0, 0.0)
    before = _dot(tri_ref[...], onehot.astype(BF16)) + base_ref[...]
    idx_out = jnp.zeros(logits.shape, F32)
    gate_out = jnp.zeros(logits.shape, F32)
    rank_out = jnp.zeros(logits.shape, F32)
    for kk in range(TOP_K):
        slot = lane == kk
        e_id = jnp.sum(jnp.where(sel[kk], lane_f, 0.0), axis=1, keepdims=True)
        rk = jnp.sum(jnp.where(sel[kk], before, 0.0), axis=1, keepdims=True)
        idx_out = jnp.where(slot, e_id, idx_out)
        gate_out = jnp.where(slot, exps[kk] / den, gate_out)
        rank_out = jnp.where(slot, rk, rank_out)
    idx_ref[...] = idx_out.astype(jnp.int32)
    gate_ref[...] = gate_out
    rank_ref[...] = rank_out.astype(jnp.int32)
    base_ref[...] = base_ref[...] + jnp.sum(onehot, axis=0, keepdims=True)
    cnt_ref[...] = base_ref[...]


def _router(x1, shift, scale, g, rw0, rw1, rb, S):
    N = x1.shape[0]
    TM = min(512, S)
    tpb = S // TM
    r = jnp.arange(TM)
    tri = (r[None, :] < r[:, None]).astype(BF16)
    row = lambda i: (i, 0)
    const = lambda i: (0, 0)
    lanes = pl.BlockSpec((TM, LANES), row)
    return pl.pallas_call(
        _router_kernel,
        grid=(N // TM,),
        in_specs=[pl.BlockSpec((TM, D_MODEL), row),
                  pl.BlockSpec((1, 1, D_MODEL), lambda i: (i // tpb, 0, 0)),
                  pl.BlockSpec((1, 1, D_MODEL), lambda i: (i // tpb, 0, 0)),
                  pl.BlockSpec((1, D_MODEL), const),
                  pl.BlockSpec((D_MODEL, LANES), const),
                  pl.BlockSpec((D_MODEL, LANES), const),
                  pl.BlockSpec((1, LANES), const),
                  pl.BlockSpec((TM, TM), const)],
        out_specs=[pl.BlockSpec((TM, D_MODEL), row), lanes, lanes, lanes, pl.BlockSpec((1, LANES), const)],
        out_shape=[jax.ShapeDtypeStruct((N, D_MODEL), F32),
                   jax.ShapeDtypeStruct((N, LANES), jnp.int32),
                   jax.ShapeDtypeStruct((N, LANES), F32),
                   jax.ShapeDtypeStruct((N, LANES), jnp.int32),
                   jax.ShapeDtypeStruct((1, LANES), F32)],
        scratch_shapes=[pltpu.VMEM((1, LANES), F32)],
        compiler_params=_params(("arbitrary",)),
        name="router",
    )(x1, shift, scale, g, rw0, rw1, rb, tri)


INDEX_SLICE = ROW_MOVE_TILE * TOP_K
DISPATCH_TILE = 2 * ROW_MOVE_TILE
DISPATCH_SLICE = DISPATCH_TILE * TOP_K


def _row_copy_out(h_ref, xs_hbm, sem, r, dst):
    return pltpu.make_async_copy(h_ref.at[pl.ds(r, 1)], xs_hbm.at[pl.ds(dst, 1)], sem)


def _zero_tile_copy(zero_ref, xs_hbm, sem, start):
    return pltpu.make_async_copy(zero_ref, xs_hbm.at[pl.ds(pl.multiple_of(start, MOE_TILE), MOE_TILE)], sem)


def _dispatch_kernel(pad_end_ref, padded_ref, dest_hbm, h_ref, xs_hbm, idx_smem, zero_ref, sem_idx, sem_rows, sem_zero):
    i = pl.program_id(0)
    fetch = pltpu.make_async_copy(dest_hbm.at[pl.ds(i * DISPATCH_SLICE, DISPATCH_SLICE)], idx_smem, sem_idx)
    fetch.start()

    @pl.when(i == 0)
    def _():
        zero_ref[...] = jnp.zeros_like(zero_ref)
        for e in range(N_EXPERTS):
            @pl.when(padded_ref[e] > 0)
            def _():
                _zero_tile_copy(zero_ref, xs_hbm, sem_zero, pad_end_ref[e] - MOE_TILE).start()
        for e in range(N_EXPERTS):
            @pl.when(padded_ref[e] > 0)
            def _():
                _zero_tile_copy(zero_ref, xs_hbm, sem_zero, 0).wait()

    fetch.wait()

    def start(r, carry):
        for kk in range(TOP_K):
            _row_copy_out(h_ref, xs_hbm, sem_rows, r, idx_smem[r * TOP_K + kk]).start(priority=kk % 2)
        return carry

    lax.fori_loop(0, DISPATCH_TILE, start, 0, unroll=8)
    for kk in range(TOP_K):
        pltpu.make_async_copy(h_ref, xs_hbm.at[pl.ds(0, DISPATCH_TILE)], sem_rows).wait()


def _dispatch(pad_ends, padded, dest_flat, h2, n_rows):
    N = h2.shape[0]
    return pl.pallas_call(
        _dispatch_kernel,
        grid_spec=pltpu.PrefetchScalarGridSpec(
            num_scalar_prefetch=2,
            grid=(N // DISPATCH_TILE,),
            in_specs=[pl.BlockSpec(memory_space=pl.ANY),
                      pl.BlockSpec((DISPATCH_TILE, D_MODEL), lambda i, pe, pd: (i, 0))],
            out_specs=pl.BlockSpec(memory_space=pl.ANY),
            scratch_shapes=[pltpu.SMEM((DISPATCH_SLICE,), jnp.int32), pltpu.VMEM((MOE_TILE, D_MODEL), F32),
                            pltpu.SemaphoreType.DMA, pltpu.SemaphoreType.DMA, pltpu.SemaphoreType.DMA]),
        out_shape=jax.ShapeDtypeStruct((n_rows, D_MODEL), F32),
        compiler_params=_params(("arbitrary",)),
        name="moe_dispatch",
    )(pad_ends, padded, dest_flat, h2)


def _expert_kernel(te_ref, nu_ref, xs_ref, wg_ref, bg_ref, wl_ref, bl_ref, wd_ref, bd_ref, ys_ref):
    del te_ref

    @pl.when(pl.program_id(0) < nu_ref[0])
    def _():
        xb = xs_ref[...].astype(BF16)
        glu = jnp.minimum(_dot(xb, wg_ref[0].astype(BF16)) + bg_ref[0], SWIGLU_LIMIT)
        lin = jnp.clip(_dot(xb, wl_ref[0].astype(BF16)) + bl_ref[0], -SWIGLU_LIMIT, SWIGLU_LIMIT)
        act = glu * _sigmoid(SWIGLU_ALPHA * glu) * (lin + 1.0)
        ys_ref[...] = _dot(act.astype(BF16), wd_ref[0].astype(BF16)) + bd_ref[0]


def _experts(tile_expert, n_used, xs, wg, bg, wl, bl, wd, bd):
    n_rows = xs.shape[0]
    n_tiles = n_rows // MOE_TILE
    wspec = pl.BlockSpec((1, D_MODEL, D_MODEL), lambda i, te, nu: (te[i], 0, 0))
    bspec = pl.BlockSpec((1, 1, D_MODEL), lambda i, te, nu: (te[i], 0, 0))
    rows = pl.BlockSpec((MOE_TILE, D_MODEL), lambda i, te, nu: (jnp.minimum(i, nu[0] - 1), 0))
    return pl.pallas_call(
        _expert_kernel,
        grid_spec=pltpu.PrefetchScalarGridSpec(
            num_scalar_prefetch=2,
            grid=(n_tiles,),
            in_specs=[rows, wspec, bspec, wspec, bspec, wspec, bspec],
            out_specs=rows),
        out_shape=jax.ShapeDtypeStruct((n_rows, D_MODEL), F32),
        compiler_params=_params(("arbitrary",)),
        name="moe_experts",
    )(tile_expert, n_used, xs, wg, bg, wl, bl, wd, bd)


def _row_copy_in(ys_hbm, buf_ref, sem, src, kk, r):
    return pltpu.make_async_copy(ys_hbm.at[pl.ds(src, 1)], buf_ref.at[kk, pl.ds(r, 1)], sem)


def _combine_kernel(dest_hbm, ys_hbm, gate_ref, x_ref, g2_ref, fg_ref, o_ref, idx_smem, buf_ref, sem_idx, sem_rows):
    i = pl.program_id(0)
    fetch = pltpu.make_async_copy(dest_hbm.at[pl.ds(i * INDEX_SLICE, INDEX_SLICE)], idx_smem, sem_idx)
    fetch.start()
    fetch.wait()

    def start(r, carry):
        for kk in range(TOP_K):
            _row_copy_in(ys_hbm, buf_ref, sem_rows, idx_smem[r * TOP_K + kk], kk, r).start(priority=kk % 2)
        return carry

    lax.fori_loop(0, ROW_MOVE_TILE, start, 0, unroll=8)
    for kk in range(TOP_K):
        pltpu.make_async_copy(ys_hbm.at[pl.ds(0, ROW_MOVE_TILE)], buf_ref.at[kk], sem_rows).wait()

    gates = gate_ref[...]
    moe = gates[:, 0:1] * buf_ref[0]
    for kk in range(1, TOP_K):
        moe = moe + gates[:, kk:kk + 1] * buf_ref[kk]
    x = x_ref[...] + g2_ref[0] * moe
    ms = jnp.mean(x * x, axis=-1, keepdims=True)
    o_ref[...] = x * lax.rsqrt(ms + EPS) * fg_ref[...]


def _combine(dest_flat, ys, gates, x1, gate2, final_g, S):
    N = x1.shape[0]
    TM = ROW_MOVE_TILE
    tpb = S // TM
    row = lambda i: (i, 0)
    return pl.pallas_call(
        _combine_kernel,
        grid=(N // TM,),
        in_specs=[pl.BlockSpec(memory_space=pl.ANY),
                  pl.BlockSpec(memory_space=pl.ANY),
                  pl.BlockSpec((TM, LANES), row),
                  pl.BlockSpec((TM, D_MODEL), row),
                  pl.BlockSpec((1, 1, D_MODEL), lambda i: (i // tpb, 0, 0)),
                  pl.BlockSpec((1, D_MODEL), lambda i: (0, 0))],
        out_specs=pl.BlockSpec((TM, D_MODEL), row),
        out_shape=jax.ShapeDtypeStruct((N, D_MODEL), F32),
        scratch_shapes=[pltpu.SMEM((INDEX_SLICE,), jnp.int32), pltpu.VMEM((TOP_K, TM, D_MODEL), F32),
                        pltpu.SemaphoreType.DMA, pltpu.SemaphoreType.DMA],
        compiler_params=_params(("arbitrary",)),
        name="moe_combine",
    )(dest_flat, ys, gates, x1, gate2, final_g)


def _pad_lanes(a, offset=0):
    return jnp.pad(a, ((0, 0), (offset, LANES - offset - a.shape[1])))


def kernel(x, c, ada_w, ada_b, norm1_g, norm2_g, w_in, b_gate, conv_w, a_log, dt_bias, gdn_norm_g, w_branch_a,
           diff_lambda, diff_norm_g, w_branch_b, w_out, router_w, router_b, w_glu, b_glu, w_lin, b_lin, w_down,
           b_down, final_g):
    B, S, D = x.shape
    N = B * S
    assert D == D_MODEL and S % GROUP == 0 and ada_w.shape[0] == 1
    x2 = x.reshape(N, D)

    mod = _adaln(c, ada_w[0], ada_b[0])
    shift1, scale1, gate1, shift2, scale2, gate2 = [m.reshape(B, 1, D) for m in jnp.split(mod, 6, axis=-1)]

    wi = w_in[0]
    n_a = 4 * D
    n_small = 4 * N_HEADS
    w_big = jnp.concatenate([wi[:, :n_a], wi[:, n_a + n_small:]], axis=1).astype(BF16)
    w_small = _pad_lanes(wi[:, n_a:n_a + n_small]).astype(BF16)
    alog_row = _pad_lanes(a_log[0].reshape(1, -1), GATE_LANE0)
    dt_row = _pad_lanes(dt_bias[0].reshape(1, -1), GATE_LANE0)
    proj, beta, G, Gl, cdl = _inproj(x2, shift1, scale1, norm1_g, w_big, w_small, alog_row, dt_row, S)
    proj3 = proj.reshape(B, S, N_COL_BLOCKS * D)

    qkv = _gdn_conv(proj3, conv_w[0])
    r3 = lambda a: a.reshape(B, S, LANES)
    nC = S // GDN_CHUNK
    GT = jnp.transpose(r3(G)[:, :, GATE_LANE0:GATE_LANE0 + 2 * N_HEADS], (0, 2, 1)).reshape(B, 2 * N_HEADS, 1, S)
    cd = r3(cdl).reshape(B, nC, GDN_CHUNK, LANES)[:, :, 0, GATE_LANE0:GATE_LANE0 + 2 * N_HEADS]
    cd = jnp.transpose(cd.reshape(B, nC, 2, N_HEADS), (0, 2, 3, 1)).reshape(-1)
    u, w, qd, att, kdt = _gdn_prep(qkv, r3(beta), r3(G), r3(Gl), GT)
    oAf, oAb = _gdn_scan(cd, u, w, qd, att, kdt)

    half = DIFF_DH // 2
    inv_freq = ROPE_THETA ** (-jnp.arange(half, dtype=F32) / half)
    ang = jnp.arange(S, dtype=F32)[:, None] * inv_freq[None, :]
    cos_t = jnp.tile(jnp.cos(ang), (1, 4))
    sin_h = jnp.sin(ang)
    sin_t = jnp.tile(jnp.concatenate([-sin_h, sin_h], axis=1), (1, 2))
    qt, kr, vt = _rope(proj3, cos_t, sin_t)
    lam_a = jnp.pad(_pad_lanes(diff_lambda[0][0::2]), ((0, 6), (0, 0)))
    lam_b = jnp.pad(_pad_lanes(diff_lambda[0][1::2]), ((0, 6), (0, 0)))
    oB = _diff_attn(qt, kr, vt, lam_a, lam_b, diff_norm_g)

    x1 = _merge(oAf.reshape(N, D), oAb.reshape(N, D), proj, oB.reshape(N, D), b_gate, x2, gate1, gdn_norm_g,
                w_branch_a[0].astype(BF16), w_branch_b[0].astype(BF16), w_out[0].astype(BF16), S)

    rw = _pad_lanes(router_w[0])
    rw0 = rw.astype(BF16)
    rw1 = (rw - rw0.astype(F32)).astype(BF16)
    h2, idx, gates, rank, counts = _router(x1, shift2, scale2, norm2_g, rw0, rw1, _pad_lanes(router_b), S)
    cnt = counts[0, :N_EXPERTS].astype(jnp.int32)
    padded = (cnt + MOE_TILE - 1) // MOE_TILE * MOE_TILE
    pad_ends = jnp.cumsum(padded)
    pad_starts = pad_ends - padded
    dest = (pad_starts[idx[:, :TOP_K]] + rank[:, :TOP_K]).reshape(-1)
    n_tiles = -(-(N * TOP_K) // MOE_TILE) + N_EXPERTS
    tile_start = jnp.arange(n_tiles, dtype=jnp.int32) * MOE_TILE
    tile_expert = jnp.sum((tile_start[:, None] >= pad_ends[None, :]).astype(jnp.int32), axis=1)
    tile_expert = jnp.minimum(tile_expert, N_EXPERTS - 1)
    n_used = (pad_ends[N_EXPERTS - 1:] // MOE_TILE).astype(jnp.int32)
    xs = _dispatch(pad_ends.astype(jnp.int32), padded, dest, h2, n_tiles * MOE_TILE)
    ys = _experts(tile_expert, n_used, xs, w_glu[0], b_glu[0][:, None, :], w_lin[0], b_lin[0][:, None, :],
                  w_down[0], b_down[0][:, None, :])
    out = _combine(dest, ys, gates, x1, gate2, final_g.reshape(1, D), S)
    return out.reshape(B, S, D)
```

```python
import functools
import math

import jax
import jax.numpy as jnp
from jax import lax
from jax.experimental import pallas as pl
from jax.experimental.pallas import tpu as pltpu

F32 = jnp.float32
BF16 = jnp.bfloat16

D_MODEL = 1024
EPS = 1e-6
N_HEADS = 8
HEAD_W = 128
GDN_CHUNK = 64
CONV_WIDTH = 5
DIFF_DH = 64
ROPE_THETA = 10000.0
LAM_INIT = 0.8 - 0.6 * math.exp(-0.3 * 0)
N_EXPERTS = 32
TOP_K = 4
SWIGLU_ALPHA = 1.702
SWIGLU_LIMIT = 7.0

LANES = 128
GROUP = 256
CHUNKS_PER_GROUP = GROUP // GDN_CHUNK
CHUNK_SHIFT = GDN_CHUNK.bit_length() - 1
MOE_TILE = 512
ROW_MOVE_TILE = 1024
NEG_INF = float("-inf")

COL_QA, COL_KA, COL_VA, COL_ZA, COL_QB, COL_KB, COL_VB, COL_GA, COL_GB = range(9)
N_COL_BLOCKS = 9


def _params(sem, vmem_mb=48):
    return pltpu.CompilerParams(dimension_semantics=sem, vmem_limit_bytes=vmem_mb * 1024 * 1024)


def _dot(a, b):
    return jnp.dot(a, b, preferred_element_type=F32)


def _dot_nt(a, b):
    return lax.dot_general(a, b, (((1,), (1,)), ((), ())), preferred_element_type=F32)


def _sigmoid(x):
    return 1.0 / (1.0 + jnp.exp(-x))


def _split3(x):
    a = x.astype(BF16)
    r = x - a.astype(F32)
    b = r.astype(BF16)
    c = (r - b.astype(F32)).astype(BF16)
    return a, b, c


def _adaln_kernel(c_ref, w_ref, b_ref, o_ref):
    c = c_ref[...]
    cond = c * _sigmoid(c)
    c0, c1, c2 = _split3(cond)
    w0, w1, w2 = _split3(w_ref[...])
    acc = _dot(c0, w0) + (_dot(c0, w1) + _dot(c1, w0)) + (_dot(c0, w2) + _dot(c1, w1) + _dot(c2, w0))
    o_ref[...] = acc + b_ref[...]


def _adaln(c, ada_w, ada_b):
    B = c.shape[0]
    n = ada_w.shape[1] // D_MODEL
    return pl.pallas_call(
        _adaln_kernel,
        grid=(n,),
        in_specs=[pl.BlockSpec((B, D_MODEL), lambda j: (0, 0)),
                  pl.BlockSpec((D_MODEL, D_MODEL), lambda j: (0, j)),
                  pl.BlockSpec((1, D_MODEL), lambda j: (0, j))],
        out_specs=pl.BlockSpec((B, D_MODEL), lambda j: (0, j)),
        out_shape=jax.ShapeDtypeStruct((B, n * D_MODEL), F32),
        compiler_params=_params(("parallel",)),
        name="adaln",
    )(c, ada_w, ada_b.reshape(1, -1))


def _inproj_kernel(x_ref, sh_ref, sc_ref, g_ref, w_ref, ws_ref, alog_ref, dt_ref,
                   o_ref, beta_ref, gcum_ref, glast_ref, cd_ref, h_ref):
    @pl.when(pl.program_id(1) == 0)
    def _():
        x = x_ref[...]
        ms = jnp.mean(x * x, axis=-1, keepdims=True)
        y = x * lax.rsqrt(ms + EPS) * g_ref[...]
        h = (y * (1.0 + sc_ref[0]) + sh_ref[0]).astype(BF16)
        h_ref[...] = h
        _gdn_gates(_dot(h, ws_ref[...]), alog_ref[...], dt_ref[...], beta_ref, gcum_ref, glast_ref, cd_ref)

    col = pl.multiple_of(pl.program_id(1) * D_MODEL, D_MODEL)
    o_ref[...] = _dot(h_ref[...], w_ref[:, pl.ds(col, D_MODEL)]).astype(o_ref.dtype)


def _inproj(x2, shift, scale, g, w_big, w_small, alog_row, dt_row, S):
    N = x2.shape[0]
    TM = min(1024, S)
    tpb = S // TM
    const = lambda i, j: (0, 0)
    lanes = pl.BlockSpec((TM, LANES), lambda i, j: (i, 0))
    return pl.pallas_call(
        _inproj_kernel,
        grid=(N // TM, N_COL_BLOCKS),
        in_specs=[pl.BlockSpec((TM, D_MODEL), lambda i, j: (i, 0)),
                  pl.BlockSpec((1, 1, D_MODEL), lambda i, j: (i // tpb, 0, 0)),
                  pl.BlockSpec((1, 1, D_MODEL), lambda i, j: (i // tpb, 0, 0)),
                  pl.BlockSpec((1, D_MODEL), const),
                  pl.BlockSpec((D_MODEL, N_COL_BLOCKS * D_MODEL), const, pipeline_mode=pl.Buffered(1)),
                  pl.BlockSpec((D_MODEL, LANES), const),
                  pl.BlockSpec((1, LANES), const),
                  pl.BlockSpec((1, LANES), const)],
        out_specs=[pl.BlockSpec((TM, D_MODEL), lambda i, j: (i, j)), lanes, lanes, lanes, lanes],
        out_shape=[jax.ShapeDtypeStruct((N, N_COL_BLOCKS * D_MODEL), BF16)]
                  + [jax.ShapeDtypeStruct((N, LANES), F32)] * 4,
        scratch_shapes=[pltpu.VMEM((TM, D_MODEL), BF16)],
        compiler_params=_params(("parallel", "arbitrary")),
        name="inproj",
    )(x2, shift, scale, g, w_big, w_small, alog_row, dt_row)


HALO = 16


CONV_BLOCK = 256
CONV_PAD = (CONV_WIDTH - 1) // 2
CONV_TAPS = tuple(j for j in range(CONV_WIDTH) if j != CONV_PAD)


def _conv_shifts():
    r = jnp.arange(CONV_BLOCK)[:, None]
    c = jnp.arange(CONV_BLOCK)[None, :]
    return jnp.stack([c == r + (j - CONV_PAD) for j in CONV_TAPS]).astype(BF16)


def _conv_kernel(cur_ref, prev_ref, next_ref, w_ref, shift_ref, o_ref, ext_ref, *, TR):
    i = pl.program_id(1)
    g = pl.program_id(2)
    last = pl.num_programs(1) - 1
    ext_ref[8:8 + TR, :] = cur_ref[0].astype(F32)
    pv = prev_ref[0].astype(F32)[HALO - 8:HALO]
    nx = next_ref[0].astype(F32)[0:8]
    ext_ref[0:8, :] = jnp.where(i > 0, pv, 0.0)
    ext_ref[TR + 8:TR + 16, :] = jnp.where(i < last, nx, 0.0)
    ones = jnp.ones((HEAD_W, HEAD_W), BF16)
    qscale = jnp.where(g == 0, HEAD_W ** -0.5, 1.0)

    def edge_rows(row0):
        e = ext_ref[8 + row0 - CONV_PAD:16 + row0 - CONV_PAD, :] * w_ref[0:1, :]
        for j in range(1, CONV_WIDTH):
            e = e + ext_ref[8 + row0 - CONV_PAD + j:16 + row0 - CONV_PAD + j, :] * w_ref[j:j + 1, :]
        return e

    for blk in range(TR // CONV_BLOCK):
        r0 = blk * CONV_BLOCK
        rows = slice(r0, r0 + CONV_BLOCK)
        ub = cur_ref[0, rows, :]
        acc = ext_ref[8 + r0:8 + r0 + CONV_BLOCK, :] * w_ref[CONV_PAD:CONV_PAD + 1, :]
        for si, j in enumerate(CONV_TAPS):
            acc = acc + _dot(shift_ref[si], ub) * w_ref[j:j + 1, :]
        acc = jnp.concatenate([edge_rows(r0), acc[8:CONV_BLOCK - 8], edge_rows(r0 + CONV_BLOCK - 8)], axis=0)
        y = acc * _sigmoid(acc)

        for h in range(N_HEADS):
            cols = slice(h * HEAD_W, (h + 1) * HEAD_W)
            yh = y[:, cols]
            ss = _dot((yh * yh).astype(BF16), ones)
            scale = jnp.where(g < 2, lax.rsqrt(ss + EPS) * qscale, 1.0)
            o_ref[0, rows, cols] = (yh * scale).astype(o_ref.dtype)


def _gdn_conv(proj3, conv_w):
    B, S, _ = proj3.shape
    TR = min(1024, S)
    nT = S // TR
    rb = TR // HALO
    nH = S // HALO
    return pl.pallas_call(
        functools.partial(_conv_kernel, TR=TR),
        grid=(B, nT, 3),
        in_specs=[pl.BlockSpec((1, TR, D_MODEL), lambda b, i, g: (b, i, g)),
                  pl.BlockSpec((1, HALO, D_MODEL), lambda b, i, g: (b, jnp.maximum(i * rb - 1, 0), g)),
                  pl.BlockSpec((1, HALO, D_MODEL), lambda b, i, g: (b, jnp.minimum((i + 1) * rb, nH - 1), g)),
                  pl.BlockSpec((CONV_WIDTH, D_MODEL), lambda b, i, g: (0, g)),
                  pl.BlockSpec((len(CONV_TAPS), CONV_BLOCK, CONV_BLOCK), lambda b, i, g: (0, 0, 0))],
        out_specs=pl.BlockSpec((1, TR, D_MODEL), lambda b, i, g: (b, i, g)),
        out_shape=jax.ShapeDtypeStruct((B, S, 3 * D_MODEL), BF16),
        scratch_shapes=[pltpu.VMEM((TR + 16, D_MODEL), F32)],
        compiler_params=_params(("parallel", "parallel", "parallel")),
        name="gdn_conv",
    )(proj3, proj3, proj3, conv_w, _conv_shifts())


GATE_LANE0 = 16


def _gdn_gates(x, alog, dt, beta_ref, g_ref, gl_ref, cd_ref):
    lane = lax.broadcasted_iota(jnp.int32, (GROUP, LANES), 1)
    r = lax.broadcasted_iota(jnp.int32, (GROUP, GROUP), 0)
    c = lax.broadcasted_iota(jnp.int32, (GROUP, GROUP), 1)
    same = (r >> CHUNK_SHIFT) == (c >> CHUNK_SHIFT)
    lower = jnp.where(same & (c <= r), 1.0, 0.0).astype(BF16)
    upper = jnp.where(same & (c >= r), 1.0, 0.0).astype(BF16)
    block = jnp.where(same, 1.0, 0.0).astype(BF16)
    beta_ref[...] = _sigmoid(x)
    for gi in range(x.shape[0] // GROUP):
        rows = slice(gi * GROUP, (gi + 1) * GROUP)
        z = x[rows] + dt
        softplus = jnp.maximum(z, 0.0) + jnp.log(1.0 + jnp.exp(-jnp.abs(z)))
        gd = -jnp.exp(alog) * softplus
        gd = jnp.where((lane >= GATE_LANE0) & (lane < GATE_LANE0 + 2 * N_HEADS), gd, 0.0)
        p0, p1, p2 = _split3(gd)
        g_fwd = _dot(lower, p0) + _dot(lower, p1) + _dot(lower, p2)
        g_bwd = _dot(upper, p0) + _dot(upper, p1) + _dot(upper, p2)
        tot = _dot(block, p0) + _dot(block, p1) + _dot(block, p2)
        G = jnp.where(lane < GATE_LANE0 + N_HEADS, g_fwd, g_bwd)
        g_ref[rows, :] = G
        gl_ref[rows, :] = tot - G
        cd_ref[rows, :] = jnp.exp(tot)


PREP_HEADS = 8
(MASK_STRICT_LO, MASK_STRICT_UP, MASK_INCL_LO, MASK_INCL_UP, MASK_EYE, MASK_BLK4, MASK_OFF0) = range(7)
N_MASKS = MASK_OFF0 + (CHUNK_SHIFT - 2)


def _prep_masks():
    r = jnp.arange(GROUP)[:, None]
    c = jnp.arange(GROUP)[None, :]
    same = (r >> CHUNK_SHIFT) == (c >> CHUNK_SHIFT)
    masks = [same & (c < r), same & (c > r), same & (c <= r), same & (c >= r), r == c, (r >> 2) == (c >> 2)]
    for shift in range(2, CHUNK_SHIFT):
        masks.append(((r >> shift) != (c >> shift)) & ((r >> (shift + 1)) == (c >> (shift + 1))))
    return jnp.stack(masks).astype(F32)


def _col(x, l, lane):
    return jnp.broadcast_to(jnp.sum(jnp.where(lane == l, x, 0.0), axis=1, keepdims=True), x.shape)


def _prep_kernel(q_ref, k_ref, v_ref, beta_ref, g_ref, gl_ref, gtf_ref, gtb_ref, mask_ref, bmask_ref,
                 u_ref, w_ref, qd_ref, at_ref, kdt_ref):
    hp = pl.program_id(1)
    lane = lax.broadcasted_iota(jnp.int32, (GROUP, LANES), 1)
    wide = lambda a: jnp.concatenate([a, a], axis=1)
    chains = [(hh, d) for hh in range(PREP_HEADS) for d in range(2)]
    p, rhs = {}, {}
    for hh in range(PREP_HEADS):
        cols = slice(hh * HEAD_W, (hh + 1) * HEAD_W)
        q = q_ref[0, :, cols]
        k = k_ref[0, :, cols]
        qf = q.astype(F32)
        kf = k.astype(F32)
        vf = v_ref[0, :, cols].astype(F32)
        kk = _dot_nt(k, k)
        qk = _dot_nt(q, k)
        for d in range(2):
            lb = d * N_HEADS + hp * PREP_HEADS + hh
            beta_c = _col(beta_ref[0], lb, lane)
            g_c = _col(g_ref[0], GATE_LANE0 + lb, lane)
            eg_c = jnp.exp(g_c)
            egl_c = jnp.exp(_col(gl_ref[0], GATE_LANE0 + lb, lane))
            g_r = (gtf_ref if d == 0 else gtb_ref)[0, hh]
            dec = jnp.exp(jnp.minimum(wide(g_c) - g_r, 0.0))
            p[hh, d] = (-(kk * wide(beta_c)) * dec * mask_ref[MASK_STRICT_LO + d]).astype(BF16)
            att = qk * dec * mask_ref[MASK_INCL_LO + d]
            rhs[hh, d] = jnp.concatenate([vf * beta_c, kf * (beta_c * eg_c)], axis=1).astype(BF16)
            qd_ref[0, d, :, cols] = (qf * eg_c).astype(qd_ref.dtype)
            kdt = (kf * egl_c).T
            for ci in range(CHUNKS_PER_GROUP):
                sl = slice(ci * GDN_CHUNK, (ci + 1) * GDN_CHUNK)
                at_ref[0, d, hh, ci] = att[sl, sl].astype(at_ref.dtype)
                kdt_ref[0, d, hh, ci] = kdt[:, sl].astype(kdt_ref.dtype)
    p4 = {ch: p[ch] * bmask_ref[0] for ch in chains}
    sq = {ch: _dot(p4[ch], p4[ch]).astype(BF16) for ch in chains}
    tb = {ch: bmask_ref[N_MASKS - MASK_BLK4] + p4[ch] for ch in chains}
    tb = {ch: (tb[ch].astype(F32) + _dot(tb[ch], sq[ch])).astype(BF16) for ch in chains}
    for lvl in range(CHUNK_SHIFT - 2):
        x = {ch: _dot(tb[ch], p[ch] * bmask_ref[1 + lvl]).astype(BF16) for ch in chains}
        tb = {ch: (tb[ch].astype(F32) + _dot(x[ch], tb[ch])).astype(BF16) for ch in chains}
    uw = {ch: _dot(tb[ch], rhs[ch]) for ch in chains}
    for hh, d in chains:
        cols = slice(hh * HEAD_W, (hh + 1) * HEAD_W)
        u_ref[0, d, :, cols] = uw[hh, d][:, :HEAD_W].astype(u_ref.dtype)
        w_ref[0, d, :, cols] = uw[hh, d][:, HEAD_W:].astype(w_ref.dtype)


def _gdn_prep(qkv, beta, G, Gl, GT):
    B, S, _ = qkv.shape
    nG = S // GROUP
    nC = S // GDN_CHUNK
    PW = PREP_HEADS * HEAD_W
    nP = N_HEADS // PREP_HEADS
    sm = pl.BlockSpec((1, GROUP, LANES), lambda b, h, g: (b, g, 0))
    big = pl.BlockSpec((1, 2, GROUP, PW), lambda b, h, g: (b, 0, g, h))
    masks = _prep_masks()
    return pl.pallas_call(
        _prep_kernel,
        grid=(B, nP, nG),
        in_specs=[pl.BlockSpec((1, GROUP, PW), lambda b, h, g: (b, g, h)),
                  pl.BlockSpec((1, GROUP, PW), lambda b, h, g: (b, g, nP + h)),
                  pl.BlockSpec((1, GROUP, PW), lambda b, h, g: (b, g, 2 * nP + h)),
                  sm, sm, sm,
                  pl.BlockSpec((1, PREP_HEADS, 1, GROUP), lambda b, h, g: (b, h, 0, g)),
                  pl.BlockSpec((1, PREP_HEADS, 1, GROUP), lambda b, h, g: (b, nP + h, 0, g)),
                  pl.BlockSpec((MASK_EYE, GROUP, GROUP), lambda b, h, g: (0, 0, 0)),
                  pl.BlockSpec((N_MASKS - MASK_BLK4 + 1, GROUP, GROUP), lambda b, h, g: (0, 0, 0))],
        out_specs=[big, big, big,
                   pl.BlockSpec((1, 2, PREP_HEADS, CHUNKS_PER_GROUP, GDN_CHUNK, GDN_CHUNK),
                                lambda b, h, g: (b, 0, h, g, 0, 0)),
                   pl.BlockSpec((1, 2, PREP_HEADS, CHUNKS_PER_GROUP, HEAD_W, GDN_CHUNK),
                                lambda b, h, g: (b, 0, h, g, 0, 0))],
        out_shape=[jax.ShapeDtypeStruct((B, 2, S, D_MODEL), BF16)] * 3
                  + [jax.ShapeDtypeStruct((B, 2, N_HEADS, nC, GDN_CHUNK, GDN_CHUNK), BF16),
                     jax.ShapeDtypeStruct((B, 2, N_HEADS, nC, HEAD_W, GDN_CHUNK), BF16)],
        compiler_params=_params(("parallel", "parallel", "parallel")),
        name="gdn_prep",
    )(qkv, qkv, qkv, beta, G, Gl, GT, GT, masks[:MASK_EYE],
      jnp.concatenate([masks[MASK_BLK4:], masks[MASK_EYE:MASK_EYE + 1]]).astype(BF16))


def _scan_kernel(cd_ref, uf_ref, wf_ref, qdf_ref, atf_ref, kdtf_ref, ub_ref, wb_ref, qdb_ref, atb_ref, kdtb_ref,
                 of_ref, ob_ref, state_ref, *, nc, nC):
    b = pl.program_id(0)
    t = pl.program_id(1)
    nT = pl.num_programs(1)

    @pl.when(t == 0)
    def _():
        state_ref[...] = jnp.zeros_like(state_ref)

    dirs = ((uf_ref, wf_ref, qdf_ref, atf_ref, kdtf_ref, of_ref), (ub_ref, wb_ref, qdb_ref, atb_ref, kdtb_ref, ob_ref))

    def chunk(ci, carry):
        work = []
        for d, refs in enumerate(dirs):
            c = ci if d == 0 else nc - 1 - ci
            tt = t if d == 0 else nT - 1 - t
            row = pl.multiple_of(c * GDN_CHUNK, GDN_CHUNK)
            for h in range(N_HEADS):
                work.append((d, h, c, row, ((b * 2 + d) * N_HEADS + h) * nC + tt * nc + c, refs))
        s_old = [state_ref[d, h] for d, h, *_ in work]
        sb = [s.astype(BF16) for s in s_old]
        tile = lambda ref, row, h: ref[0, 0, pl.ds(row, GDN_CHUNK), h * HEAD_W:(h + 1) * HEAD_W]
        ws = [_dot(tile(refs[1], row, h), sb[i]) for i, (d, h, c, row, gi, refs) in enumerate(work)]
        qs = [_dot(tile(refs[2], row, h), sb[i]) for i, (d, h, c, row, gi, refs) in enumerate(work)]
        vb = [(tile(refs[0], row, h).astype(F32) - ws[i]).astype(BF16)
              for i, (d, h, c, row, gi, refs) in enumerate(work)]
        o = [qs[i] + _dot(refs[3][0, 0, h, c], vb[i]) for i, (d, h, c, row, gi, refs) in enumerate(work)]
        upd = [_dot(refs[4][0, 0, h, c], vb[i]) for i, (d, h, c, row, gi, refs) in enumerate(work)]
        for i, (d, h, c, row, gi, refs) in enumerate(work):
            state_ref[d, h] = s_old[i] * cd_ref[gi] + upd[i]
            refs[5][0, pl.ds(row, GDN_CHUNK), h * HEAD_W:(h + 1) * HEAD_W] = o[i].astype(of_ref.dtype)
        return carry

    lax.fori_loop(0, nc, chunk, 0)


def _gdn_scan(cd, u, w, qd, att, kdt):
    B, _, S, _ = u.shape
    TC = min(512, S)
    nT = S // TC
    nc = TC // GDN_CHUNK
    nC = S // GDN_CHUNK
    fwd = lambda b, t: t
    bwd = lambda b, t: nT - 1 - t

    def specs(d, tm):
        big = pl.BlockSpec((1, 1, TC, D_MODEL), lambda b, t: (b, d, tm(b, t), 0))
        return [big, big, big,
                pl.BlockSpec((1, 1, N_HEADS, nc, GDN_CHUNK, GDN_CHUNK), lambda b, t: (b, d, 0, tm(b, t), 0, 0)),
                pl.BlockSpec((1, 1, N_HEADS, nc, HEAD_W, GDN_CHUNK), lambda b, t: (b, d, 0, tm(b, t), 0, 0))]

    return pl.pallas_call(
        functools.partial(_scan_kernel, nc=nc, nC=nC),
        grid=(B, nT),
        in_specs=[pl.BlockSpec(memory_space=pltpu.SMEM)] + specs(0, fwd) + specs(1, bwd),
        out_specs=[pl.BlockSpec((1, TC, D_MODEL), lambda b, t: (b, t, 0)),
                   pl.BlockSpec((1, TC, D_MODEL), lambda b, t: (b, nT - 1 - t, 0))],
        out_shape=[jax.ShapeDtypeStruct((B, S, D_MODEL), BF16)] * 2,
        scratch_shapes=[pltpu.VMEM((2, N_HEADS, HEAD_W, HEAD_W), F32)],
        compiler_params=_params(("parallel", "arbitrary")),
        name="gdn_scan",
    )(cd, u, w, qd, att, kdt, u, w, qd, att, kdt)


ATT_TQ = 256
ATT_TK = 256
ATT_NQ = 4
ATT_VROWS = HEAD_W + 16
LOG2E = 1.4426950408889634


def _rope_kernel(q_ref, k_ref, v_ref, cos_ref, sin_ref, qt_ref, kr_ref, vt_ref, *, TR):
    cs = cos_ref[...]
    sn = sin_ref[...]
    lane = lax.broadcasted_iota(jnp.int32, cs.shape, 1)
    first_half = (lane & (DIFF_DH - 1)) < (DIFF_DH // 2)
    qscale = DIFF_DH ** -0.5 * LOG2E

    def rot(x):
        partner = jnp.where(first_half, pltpu.roll(x, HEAD_W - DIFF_DH // 2, 1), pltpu.roll(x, DIFF_DH // 2, 1))
        return x * cs + partner * sn

    for h in range(N_HEADS):
        cols = slice(h * HEAD_W, (h + 1) * HEAD_W)
        qr = rot(q_ref[0, :, cols].astype(F32)) * qscale
        kr_ref[0, :, cols] = rot(k_ref[0, :, cols].astype(F32)).astype(kr_ref.dtype)
        vf = v_ref[0, :, cols].astype(F32)
        for ci in range(TR // ATT_TK):
            rows = slice(ci * ATT_TK, (ci + 1) * ATT_TK)
            vt_ref[0, h, ci, 0:HEAD_W, :] = vf[rows].T.astype(vt_ref.dtype)
            vt_ref[0, h, ci, HEAD_W:ATT_VROWS, :] = jnp.ones((ATT_VROWS - HEAD_W, ATT_TK), vt_ref.dtype)
        for ci in range(TR // ATT_TQ):
            rows = slice(ci * ATT_TQ, (ci + 1) * ATT_TQ)
            qt_ref[0, h, ci] = qr[rows].T.astype(qt_ref.dtype)


def _rope(proj3, cos_t, sin_t):
    B, S, _ = proj3.shape
    TR = min(512, S)
    tab = pl.BlockSpec((TR, HEAD_W), lambda b, i: (i, 0))
    col = lambda cb: pl.BlockSpec((1, TR, D_MODEL), lambda b, i: (b, i, cb))
    return pl.pallas_call(
        functools.partial(_rope_kernel, TR=TR),
        grid=(B, S // TR),
        in_specs=[col(COL_QB), col(COL_KB), col(COL_VB), tab, tab],
        out_specs=[pl.BlockSpec((1, N_HEADS, TR // ATT_TQ, HEAD_W, ATT_TQ), lambda b, i: (b, 0, i, 0, 0)),
                   pl.BlockSpec((1, TR, D_MODEL), lambda b, i: (b, i, 0)),
                   pl.BlockSpec((1, N_HEADS, TR // ATT_TK, ATT_VROWS, ATT_TK), lambda b, i: (b, 0, i, 0, 0))],
        out_shape=[jax.ShapeDtypeStruct((B, N_HEADS, S // ATT_TQ, HEAD_W, ATT_TQ), BF16),
                   jax.ShapeDtypeStruct((B, S, D_MODEL), BF16),
                   jax.ShapeDtypeStruct((B, N_HEADS, S // ATT_TK, ATT_VROWS, ATT_TK), BF16)],
        compiler_params=_params(("parallel", "parallel")),
        name="rope",
    )(proj3, proj3, proj3, cos_t, sin_t)


def _attn_kernel(qt_ref, k_ref, vt_ref, la_ref, lb_ref, g_ref, o_ref, s_ref, acc_ref, *, n_chunks):
    row = lax.broadcasted_iota(jnp.int32, (HEAD_W, ATT_TQ), 0)
    qw = []
    for qb in range(ATT_NQ):
        qt = qt_ref[0, 0, qb]
        zero = jnp.zeros_like(qt)
        qw.append((jnp.where(row < DIFF_DH, qt, zero), jnp.where(row >= DIFF_DH, qt, zero)))
    chains = [(qb, comp) for qb in range(ATT_NQ) for comp in range(2)]

    def scores(j):
        kc = k_ref[0, pl.ds(pl.multiple_of(j * ATT_TK, ATT_TK), ATT_TK), :]
        return [_dot(kc, qw[qb][comp]) for qb, comp in chains]

    acc_ref[...] = jnp.zeros_like(acc_ref)
    for (qb, comp), s0 in zip(chains, scores(0)):
        s_ref[qb, comp] = s0

    def chunk(j, carry):
        s_next = scores(jnp.minimum(j + 1, n_chunks - 1))
        vt = vt_ref[0, 0, j]
        out = []
        for ci, (qb, comp) in enumerate(chains):
            m_prev = carry[ci]
            s = s_ref[qb, comp]
            m_new = jnp.maximum(m_prev, jnp.max(s, axis=0, keepdims=True))
            alpha = jnp.exp2(m_prev - m_new)
            p = jnp.exp2(s - m_new)
            out.append(m_new)
            acc_ref[qb, comp] = alpha * acc_ref[qb, comp] + _dot(vt, p.astype(BF16))
        for (qb, comp), sn in zip(chains, s_next):
            s_ref[qb, comp] = sn
        return tuple(out)

    neg = jnp.full((1, ATT_TQ), NEG_INF, F32)
    lax.fori_loop(0, n_chunks, chunk, (neg,) * len(chains), unroll=8)

    sums = jnp.sum(la_ref[...] * lb_ref[...], axis=1, keepdims=True)
    lrow = lax.broadcasted_iota(jnp.int32, sums.shape, 0)
    sign = jnp.where(lrow == 0, 1.0, jnp.where(lrow == 1, -1.0, 0.0))
    lam = jnp.sum(sign * jnp.exp(sums), axis=0, keepdims=True) + LAM_INIT
    for qb in range(ATT_NQ):
        l0 = acc_ref[qb, 0, HEAD_W:HEAD_W + 1, :]
        l1 = acc_ref[qb, 1, HEAD_W:HEAD_W + 1, :]
        ot = acc_ref[qb, 0, 0:HEAD_W, :] / l0 - lam * (acc_ref[qb, 1, 0:HEAD_W, :] / l1)
        ms = jnp.mean(ot * ot, axis=0, keepdims=True)
        y = (ot * lax.rsqrt(ms + EPS)).T * g_ref[...] * (1.0 - LAM_INIT)
        o_ref[0, qb * ATT_TQ:(qb + 1) * ATT_TQ, :] = y.astype(o_ref.dtype)


def _diff_attn(qt, kr, vt, lam_a, lam_b, norm_g):
    B, S, _ = kr.shape
    lam_spec = pl.BlockSpec((8, LANES), lambda b, h, qi: (0, 0))
    return pl.pallas_call(
        functools.partial(_attn_kernel, n_chunks=S // ATT_TK),
        grid=(B, N_HEADS, S // (ATT_NQ * ATT_TQ)),
        in_specs=[pl.BlockSpec((1, 1, ATT_NQ, HEAD_W, ATT_TQ), lambda b, h, qi: (b, h, qi, 0, 0)),
                  pl.BlockSpec((1, S, HEAD_W), lambda b, h, qi: (b, 0, h)),
                  pl.BlockSpec((1, 1, S // ATT_TK, ATT_VROWS, ATT_TK), lambda b, h, qi: (b, h, 0, 0, 0)),
                  lam_spec, lam_spec,
                  pl.BlockSpec((1, HEAD_W), lambda b, h, qi: (0, 0))],
        out_specs=pl.BlockSpec((1, ATT_NQ * ATT_TQ, HEAD_W), lambda b, h, qi: (b, qi, h)),
        out_shape=jax.ShapeDtypeStruct((B, S, D_MODEL), BF16),
        scratch_shapes=[pltpu.VMEM((ATT_NQ, 2, ATT_TK, ATT_TQ), F32), pltpu.VMEM((ATT_NQ, 2, ATT_VROWS, ATT_TQ), F32)],
        compiler_params=_params(("parallel", "parallel", "parallel")),
        name="diff_attn",
    )(qt, kr, vt, lam_a, lam_b, norm_g)


def _merge_kernel(of_ref, ob_ref, z_ref, oB_ref, ga_ref, gb_ref, bga_ref, bgb_ref, x_ref, g1_ref, gn_ref,
                  wa_ref, wb_ref, wo_ref, o_ref, ya_ref):
    oa = of_ref[...].astype(F32) + ob_ref[...].astype(F32)
    z = z_ref[...].astype(F32)
    gate = z * _sigmoid(z)
    for h in range(N_HEADS):
        cols = slice(h * HEAD_W, (h + 1) * HEAD_W)
        oh = oa[:, cols]
        ms = jnp.mean(oh * oh, axis=-1, keepdims=True)
        ya_ref[:, cols] = (oh * lax.rsqrt(ms + EPS) * gn_ref[...] * gate[:, cols]).astype(BF16)
    y_a = _dot(ya_ref[...], wa_ref[...])
    y_b = _dot(oB_ref[...], wb_ref[...])
    gate_a = _sigmoid(ga_ref[...].astype(F32) + bga_ref[...])
    gate_b = _sigmoid(gb_ref[...].astype(F32) + bgb_ref[...])
    mix = _dot((gate_a * y_a + gate_b * y_b).astype(BF16), wo_ref[...])
    o_ref[...] = x_ref[...] + g1_ref[0] * mix


def _merge(oAf, oAb, proj, oB, b_gate, x2, gate1, gn, wa, wb, wo, S):
    N = x2.shape[0]
    TM = min(512, S)
    tpb = S // TM
    row = lambda i: (i, 0)
    full = pl.BlockSpec((D_MODEL, D_MODEL), lambda i: (0, 0))
    return pl.pallas_call(
        _merge_kernel,
        grid=(N // TM,),
        in_specs=[pl.BlockSpec((TM, D_MODEL), row),
                  pl.BlockSpec((TM, D_MODEL), row),
                  pl.BlockSpec((TM, D_MODEL), lambda i: (i, COL_ZA)),
                  pl.BlockSpec((TM, D_MODEL), row),
                  pl.BlockSpec((TM, D_MODEL), lambda i: (i, COL_GA)),
                  pl.BlockSpec((TM, D_MODEL), lambda i: (i, COL_GB)),
                  pl.BlockSpec((1, D_MODEL), lambda i: (0, 0)),
                  pl.BlockSpec((1, D_MODEL), lambda i: (0, 1)),
                  pl.BlockSpec((TM, D_MODEL), row),
                  pl.BlockSpec((1, 1, D_MODEL), lambda i: (i // tpb, 0, 0)),
                  pl.BlockSpec((1, HEAD_W), lambda i: (0, 0)),
                  full, full, full],
        out_specs=pl.BlockSpec((TM, D_MODEL), row),
        out_shape=jax.ShapeDtypeStruct((N, D_MODEL), F32),
        scratch_shapes=[pltpu.VMEM((TM, D_MODEL), BF16)],
        compiler_params=_params(("parallel",)),
        name="merge",
    )(oAf, oAb, proj, oB, proj, proj, b_gate, b_gate, x2, gate1, gn, wa, wb, wo)


def _router_kernel(x_ref, sh_ref, sc_ref, g_ref, rw0_ref, rw1_ref, rb_ref, tri_ref,
                   h_ref, idx_ref, gate_ref, rank_ref, cnt_ref, base_ref):
    i = pl.program_id(0)

    @pl.when(i == 0)
    def _():
        base_ref[...] = jnp.zeros_like(base_ref)

    x = x_ref[...]
    ms = jnp.mean(x * x, axis=-1, keepdims=True)
    h = x * lax.rsqrt(ms + EPS) * g_ref[...] * (1.0 + sc_ref[0]) + sh_ref[0]
    h_ref[...] = h
    h0 = h.astype(BF16)
    h1 = (h - h0.astype(F32)).astype(BF16)
    logits = _dot(h0, rw0_ref[...]) + (_dot(h0, rw1_ref[...]) + _dot(h1, rw0_ref[...])) + rb_ref[...]
    lane = lax.broadcasted_iota(jnp.int32, logits.shape, 1)
    lane_f = lane.astype(F32)
    cur = jnp.where(lane < N_EXPERTS, logits, NEG_INF)
    vals, sel = [], []
    for _ in range(TOP_K):
        m = jnp.max(cur, axis=1, keepdims=True)
        ix = jnp.min(jnp.where(cur == m, lane_f, float(LANES)), axis=1, keepdims=True)
        hit = lane_f == ix
        vals.append(m)
        sel.append(hit)
        cur = jnp.where(hit, NEG_INF, cur)
    exps = [jnp.exp(v - vals[0]) for v in vals]
    den = exps[0] + exps[1] + exps[2] + exps[3]
    onehot = jnp.zeros(logits.shape, F32)
    for hit in sel:
        onehot = onehot + jnp.where(hit, 1.0, 0.0)
    before = _dot(tri_ref[...], onehot.astype(BF16)) + base_ref[...]
    idx_out = jnp.zeros(logits.shape, F32)
    gate_out = jnp.zeros(logits.shape, F32)
    rank_out = jnp.zeros(logits.shape, F32)
    for kk in range(TOP_K):
        slot = lane == kk
        e_id = jnp.sum(jnp.where(sel[kk], lane_f, 0.0), axis=1, keepdims=True)
        rk = jnp.sum(jnp.where(sel[kk], before, 0.0), axis=1, keepdims=True)
        idx_out = jnp.where(slot, e_id, idx_out)
        gate_out = jnp.where(slot, exps[kk] / den, gate_out)
        rank_out = jnp.where(slot, rk, rank_out)
    idx_ref[...] = idx_out.astype(jnp.int32)
    gate_ref[...] = gate_out
    rank_ref[...] = rank_out.astype(jnp.int32)
    base_ref[...] = base_ref[...] + jnp.sum(onehot, axis=0, keepdims=True)
    cnt_ref[...] = base_ref[...]


def _router(x1, shift, scale, g, rw0, rw1, rb, S):
    N = x1.shape[0]
    TM = min(512, S)
    tpb = S // TM
    r = jnp.arange(TM)
    tri = (r[None, :] < r[:, None]).astype(BF16)
    row = lambda i: (i, 0)
    const = lambda i: (0, 0)
    lanes = pl.BlockSpec((TM, LANES), row)
    return pl.pallas_call(
        _router_kernel,
        grid=(N // TM,),
        in_specs=[pl.BlockSpec((TM, D_MODEL), row),
                  pl.BlockSpec((1, 1, D_MODEL), lambda i: (i // tpb, 0, 0)),
                  pl.BlockSpec((1, 1, D_MODEL), lambda i: (i // tpb, 0, 0)),
                  pl.BlockSpec((1, D_MODEL), const),
                  pl.BlockSpec((D_MODEL, LANES), const),
                  pl.BlockSpec((D_MODEL, LANES), const),
                  pl.BlockSpec((1, LANES), const),
                  pl.BlockSpec((TM, TM), const)],
        out_specs=[pl.BlockSpec((TM, D_MODEL), row), lanes, lanes, lanes, pl.BlockSpec((1, LANES), const)],
        out_shape=[jax.ShapeDtypeStruct((N, D_MODEL), F32),
                   jax.ShapeDtypeStruct((N, LANES), jnp.int32),
                   jax.ShapeDtypeStruct((N, LANES), F32),
                   jax.ShapeDtypeStruct((N, LANES), jnp.int32),
                   jax.ShapeDtypeStruct((1, LANES), F32)],
        scratch_shapes=[pltpu.VMEM((1, LANES), F32)],
        compiler_params=_params(("arbitrary",)),
        name="router",
    )(x1, shift, scale, g, rw0, rw1, rb, tri)


INDEX_SLICE = ROW_MOVE_TILE * TOP_K
DISPATCH_TILE = 2 * ROW_MOVE_TILE
DISPATCH_SLICE = DISPATCH_TILE * TOP_K


def _row_copy_out(h_ref, xs_hbm, sem, r, dst):
    return pltpu.make_async_copy(h_ref.at[pl.ds(r, 1)], xs_hbm.at[pl.ds(dst, 1)], sem)


def _zero_tile_copy(zero_ref, xs_hbm, sem, start):
    return pltpu.make_async_copy(zero_ref, xs_hbm.at[pl.ds(pl.multiple_of(start, MOE_TILE), MOE_TILE)], sem)


def _dispatch_kernel(pad_end_ref, padded_ref, dest_hbm, h_ref, xs_hbm, idx_smem, zero_ref, sem_idx, sem_rows, sem_zero):
    i = pl.program_id(0)
    fetch = pltpu.make_async_copy(dest_hbm.at[pl.ds(i * DISPATCH_SLICE, DISPATCH_SLICE)], idx_smem, sem_idx)
    fetch.start()

    @pl.when(i == 0)
    def _():
        zero_ref[...] = jnp.zeros_like(zero_ref)
        for e in range(N_EXPERTS):
            @pl.when(padded_ref[e] > 0)
            def _():
                _zero_tile_copy(zero_ref, xs_hbm, sem_zero, pad_end_ref[e] - MOE_TILE).start()
        for e in range(N_EXPERTS):
            @pl.when(padded_ref[e] > 0)
            def _():
                _zero_tile_copy(zero_ref, xs_hbm, sem_zero, 0).wait()

        first_unused = pad_end_ref[N_EXPERTS - 1] // MOE_TILE
        n_tiles = xs_hbm.shape[0] // MOE_TILE

        def zero_start(t, carry):
            _zero_tile_copy(zero_ref, xs_hbm, sem_zero, t * MOE_TILE).start()
            return carry

        def zero_wait(t, carry):
            _zero_tile_copy(zero_ref, xs_hbm, sem_zero, 0).wait()
            return carry

        lax.fori_loop(first_unused, n_tiles, zero_start, 0)
        lax.fori_loop(first_unused, n_tiles, zero_wait, 0)

    fetch.wait()

    def start(r, carry):
        for kk in range(TOP_K):
            _row_copy_out(h_ref, xs_hbm, sem_rows, r, idx_smem[r * TOP_K + kk]).start(priority=kk % 2)
        return carry

    lax.fori_loop(0, DISPATCH_TILE, start, 0, unroll=8)
    for kk in range(TOP_K):
        pltpu.make_async_copy(h_ref, xs_hbm.at[pl.ds(0, DISPATCH_TILE)], sem_rows).wait()


def _dispatch(pad_ends, padded, dest_flat, h2, n_rows):
    N = h2.shape[0]
    return pl.pallas_call(
        _dispatch_kernel,
        grid_spec=pltpu.PrefetchScalarGridSpec(
            num_scalar_prefetch=2,
            grid=(N // DISPATCH_TILE,),
            in_specs=[pl.BlockSpec(memory_space=pl.ANY),
                      pl.BlockSpec((DISPATCH_TILE, D_MODEL), lambda i, pe, pd: (i, 0))],
            out_specs=pl.BlockSpec(memory_space=pl.ANY),
            scratch_shapes=[pltpu.SMEM((DISPATCH_SLICE,), jnp.int32), pltpu.VMEM((MOE_TILE, D_MODEL), F32),
                            pltpu.SemaphoreType.DMA, pltpu.SemaphoreType.DMA, pltpu.SemaphoreType.DMA]),
        out_shape=jax.ShapeDtypeStruct((n_rows, D_MODEL), F32),
        compiler_params=_params(("arbitrary",)),
        name="moe_dispatch",
    )(pad_ends, padded, dest_flat, h2)


def _expert_kernel(te_ref, nu_ref, xs_ref, wg_ref, bg_ref, wl_ref, bl_ref, wd_ref, bd_ref, ys_ref):
    del te_ref

    @pl.when(pl.program_id(0) >= nu_ref[0])
    def _():
        ys_ref[...] = jnp.zeros_like(ys_ref)

    @pl.when(pl.program_id(0) < nu_ref[0])
    def _():
        xb = xs_ref[...].astype(BF16)
        glu = jnp.minimum(_dot(xb, wg_ref[0].astype(BF16)) + bg_ref[0], SWIGLU_LIMIT)
        lin = jnp.clip(_dot(xb, wl_ref[0].astype(BF16)) + bl_ref[0], -SWIGLU_LIMIT, SWIGLU_LIMIT)
        act = glu * _sigmoid(SWIGLU_ALPHA * glu) * (lin + 1.0)
        ys_ref[...] = _dot(act.astype(BF16), wd_ref[0].astype(BF16)) + bd_ref[0]


def _experts(tile_expert, n_used, xs, wg, bg, wl, bl, wd, bd):
    n_rows = xs.shape[0]
    n_tiles = n_rows // MOE_TILE
    wspec = pl.BlockSpec((1, D_MODEL, D_MODEL), lambda i, te, nu: (te[i], 0, 0))
    bspec = pl.BlockSpec((1, 1, D_MODEL), lambda i, te, nu: (te[i], 0, 0))
    rows = pl.BlockSpec((MOE_TILE, D_MODEL), lambda i, te, nu: (i, 0))
    return pl.pallas_call(
        _expert_kernel,
        grid_spec=pltpu.PrefetchScalarGridSpec(
            num_scalar_prefetch=2,
            grid=(n_tiles,),
            in_specs=[rows, wspec, bspec, wspec, bspec, wspec, bspec],
            out_specs=rows),
        out_shape=jax.ShapeDtypeStruct((n_rows, D_MODEL), F32),
        compiler_params=_params(("arbitrary",)),
        name="moe_experts",
    )(tile_expert, n_used, xs, wg, bg, wl, bl, wd, bd)


def _row_copy_in(ys_hbm, buf_ref, sem, src, kk, r):
    return pltpu.make_async_copy(ys_hbm.at[pl.ds(src, 1)], buf_ref.at[kk, pl.ds(r, 1)], sem)


def _combine_kernel(dest_hbm, ys_hbm, gate_ref, x_ref, g2_ref, fg_ref, o_ref, idx_smem, buf_ref, sem_idx, sem_rows):
    i = pl.program_id(0)
    fetch = pltpu.make_async_copy(dest_hbm.at[pl.ds(i * INDEX_SLICE, INDEX_SLICE)], idx_smem, sem_idx)
    fetch.start()
    fetch.wait()

    def start(r, carry):
        for kk in range(TOP_K):
            _row_copy_in(ys_hbm, buf_ref, sem_rows, idx_smem[r * TOP_K + kk], kk, r).start(priority=kk % 2)
        return carry

    lax.fori_loop(0, ROW_MOVE_TILE, start, 0, unroll=8)
    for kk in range(TOP_K):
        pltpu.make_async_copy(ys_hbm.at[pl.ds(0, ROW_MOVE_TILE)], buf_ref.at[kk], sem_rows).wait()

    gates = gate_ref[...]
    moe = gates[:, 0:1] * buf_ref[0]
    for kk in range(1, TOP_K):
        moe = moe + gates[:, kk:kk + 1] * buf_ref[kk]
    x = x_ref[...] + g2_ref[0] * moe
    ms = jnp.mean(x * x, axis=-1, keepdims=True)
    o_ref[...] = x * lax.rsqrt(ms + EPS) * fg_ref[...]


def _combine(dest_flat, ys, gates, x1, gate2, final_g, S):
    N = x1.shape[0]
    TM = ROW_MOVE_TILE
    tpb = S // TM
    row = lambda i: (i, 0)
    return pl.pallas_call(
        _combine_kernel,
        grid=(N // TM,),
        in_specs=[pl.BlockSpec(memory_space=pl.ANY),
                  pl.BlockSpec(memory_space=pl.ANY),
                  pl.BlockSpec((TM, LANES), row),
                  pl.BlockSpec((TM, D_MODEL), row),
                  pl.BlockSpec((1, 1, D_MODEL), lambda i: (i // tpb, 0, 0)),
                  pl.BlockSpec((1, D_MODEL), lambda i: (0, 0))],
        out_specs=pl.BlockSpec((TM, D_MODEL), row),
        out_shape=jax.ShapeDtypeStruct((N, D_MODEL), F32),
        scratch_shapes=[pltpu.SMEM((INDEX_SLICE,), jnp.int32), pltpu.VMEM((TOP_K, TM, D_MODEL), F32),
                        pltpu.SemaphoreType.DMA, pltpu.SemaphoreType.DMA],
        compiler_params=_params(("arbitrary",)),
        name="moe_combine",
    )(dest_flat, ys, gates, x1, gate2, final_g)


def _pad_lanes(a, offset=0):
    return jnp.pad(a, ((0, 0), (offset, LANES - offset - a.shape[1])))


def kernel(x, c, ada_w, ada_b, norm1_g, norm2_g, w_in, b_gate, conv_w, a_log, dt_bias, gdn_norm_g, w_branch_a,
           diff_lambda, diff_norm_g, w_branch_b, w_out, router_w, router_b, w_glu, b_glu, w_lin, b_lin, w_down,
           b_down, final_g):
    B, S, D = x.shape
    N = B * S
    assert D == D_MODEL and S % GROUP == 0 and ada_w.shape[0] == 1
    x2 = x.reshape(N, D)

    mod = _adaln(c, ada_w[0], ada_b[0])
    shift1, scale1, gate1, shift2, scale2, gate2 = [m.reshape(B, 1, D) for m in jnp.split(mod, 6, axis=-1)]

    wi = w_in[0]
    n_a = 4 * D
    n_small = 4 * N_HEADS
    w_big = jnp.concatenate([wi[:, :n_a], wi[:, n_a + n_small:]], axis=1).astype(BF16)
    w_small = _pad_lanes(wi[:, n_a:n_a + n_small]).astype(BF16)
    alog_row = _pad_lanes(a_log[0].reshape(1, -1), GATE_LANE0)
    dt_row = _pad_lanes(dt_bias[0].reshape(1, -1), GATE_LANE0)
    proj, beta, G, Gl, cdl = _inproj(x2, shift1, scale1, norm1_g, w_big, w_small, alog_row, dt_row, S)
    proj3 = proj.reshape(B, S, N_COL_BLOCKS * D)

    qkv = _gdn_conv(proj3, conv_w[0])
    r3 = lambda a: a.reshape(B, S, LANES)
    nC = S // GDN_CHUNK
    GT = jnp.transpose(r3(G)[:, :, GATE_LANE0:GATE_LANE0 + 2 * N_HEADS], (0, 2, 1)).reshape(B, 2 * N_HEADS, 1, S)
    cd = r3(cdl).reshape(B, nC, GDN_CHUNK, LANES)[:, :, 0, GATE_LANE0:GATE_LANE0 + 2 * N_HEADS]
    cd = jnp.transpose(cd.reshape(B, nC, 2, N_HEADS), (0, 2, 3, 1)).reshape(-1)
    u, w, qd, att, kdt = _gdn_prep(qkv, r3(beta), r3(G), r3(Gl), GT)
    oAf, oAb = _gdn_scan(cd, u, w, qd, att, kdt)

    half = DIFF_DH // 2
    inv_freq = ROPE_THETA ** (-jnp.arange(half, dtype=F32) / half)
    ang = jnp.arange(S, dtype=F32)[:, None] * inv_freq[None, :]
    cos_t = jnp.tile(jnp.cos(ang), (1, 4))
    sin_h = jnp.sin(ang)
    sin_t = jnp.tile(jnp.concatenate([-sin_h, sin_h], axis=1), (1, 2))
    qt, kr, vt = _rope(proj3, cos_t, sin_t)
    lam_a = jnp.pad(_pad_lanes(diff_lambda[0][0::2]), ((0, 6), (0, 0)))
    lam_b = jnp.pad(_pad_lanes(diff_lambda[0][1::2]), ((0, 6), (0, 0)))
    oB = _diff_attn(qt, kr, vt, lam_a, lam_b, diff_norm_g)

    x1 = _merge(oAf.reshape(N, D), oAb.reshape(N, D), proj, oB.reshape(N, D), b_gate, x2, gate1, gdn_norm_g,
                w_branch_a[0].astype(BF16), w_branch_b[0].astype(BF16), w_out[0].astype(BF16), S)

    rw = _pad_lanes(router_w[0])
    rw0 = rw.astype(BF16)
    rw1 = (rw - rw0.astype(F32)).astype(BF16)
    h2, idx, gates, rank, counts = _router(x1, shift2, scale2, norm2_g, rw0, rw1, _pad_lanes(router_b), S)
    cnt = counts[0, :N_EXPERTS].astype(jnp.int32)
    padded = (cnt + MOE_TILE - 1) // MOE_TILE * MOE_TILE
    pad_ends = jnp.cumsum(padded)
    pad_starts = pad_ends - padded
    dest = (pad_starts[idx[:, :TOP_K]] + rank[:, :TOP_K]).reshape(-1)
    n_tiles = -(-(N * TOP_K) // MOE_TILE) + N_EXPERTS
    tile_start = jnp.arange(n_tiles, dtype=jnp.int32) * MOE_TILE
    tile_expert = jnp.sum((tile_start[:, None] >= pad_ends[None, :]).astype(jnp.int32), axis=1)
    tile_expert = jnp.minimum(tile_expert, N_EXPERTS - 1)
    n_used = (pad_ends[N_EXPERTS - 1:] // MOE_TILE).astype(jnp.int32)
    xs = _dispatch(pad_ends.astype(jnp.int32), padded, dest, h2, n_tiles * MOE_TILE)
    ys = _experts(tile_expert, n_used, xs, w_glu[0], b_glu[0][:, None, :], w_lin[0], b_lin[0][:, None, :],
                  w_down[0], b_down[0][:, None, :])
    out = _combine(dest, ys, gates, x1, gate2, final_g.reshape(1, D), S)
    return out.reshape(B, S, D)
```
